```python
import math
import jax, jax.numpy as jnp
from jax import lax
import numpy as np

D_MODEL = 1024
BATCH = 2
SEQ = 8192
DEPTH = 1

ATTN_HEADS = 16
ATTN_KV_HEADS = 4
HEAD_DIM = 64
ATTN_WIDTH = ATTN_HEADS * HEAD_DIM
KV_WIDTH = ATTN_KV_HEADS * HEAD_DIM
WINDOW = 128
BLOCK = 128
REL_BUCKETS = 32
REL_MAX_DIST = 128
SSM_EXPAND = 2
SSM_WIDTH = SSM_EXPAND * D_MODEL
SSM_HEAD_DIM = 64
SSM_HEADS = SSM_WIDTH // SSM_HEAD_DIM
SSM_GROUPS = 4
SSM_HEADS_PER_GROUP = SSM_HEADS // SSM_GROUPS
SSM_STATE = 128
CONV_WIDTH = 4
CHUNK = 128
XBC_WIDTH = SSM_WIDTH + 2 * SSM_GROUPS * SSM_STATE
DT_MIN = 0.001
DT_MAX = 0.1
N_BRANCH = 2
IN_WIDTH = ATTN_WIDTH + 2 * KV_WIDTH + ATTN_WIDTH + SSM_WIDTH + XBC_WIDTH + SSM_HEADS + N_BRANCH * D_MODEL
EPS = 1e-6

kernel_name = "hybrid_swa_sink_ssd_gated_merge"


def rms_norm(x, w, eps=EPS):
    xf = x.astype(jnp.float32)
    xf = xf * lax.rsqrt(jnp.mean(xf * xf, axis=-1, keepdims=True) + eps)
    return (xf * w.astype(jnp.float32)).astype(x.dtype)


def t5_causal_bucket(dist):
    n = jnp.maximum(dist, 0)
    max_exact = REL_BUCKETS // 2
    nf = jnp.maximum(n, 1).astype(jnp.float32)
    large = max_exact + (jnp.log(nf / max_exact) / math.log(REL_MAX_DIST / max_exact)
                         * (REL_BUCKETS - max_exact)).astype(jnp.int32)
    large = jnp.minimum(large, REL_BUCKETS - 1)
    return jnp.where(n < max_exact, n, large)


def sliding_window_attention(q, k, v, q_norm_w, k_norm_w, rel_bias, sinks):
    b, s = q.shape[0], q.shape[1]
    nb = s // BLOCK
    grp = ATTN_HEADS // ATTN_KV_HEADS
    q = rms_norm(q, q_norm_w)
    k = rms_norm(k, k_norm_w)
    qb = q.reshape(b, nb, BLOCK, ATTN_KV_HEADS, grp, HEAD_DIM)

    def banded(t):
        tb = t.reshape(b, nb, BLOCK, ATTN_KV_HEADS, HEAD_DIM)
        prev = jnp.concatenate([jnp.zeros_like(tb[:, :1]), tb[:, :-1]], axis=1)
        return jnp.concatenate([prev, tb], axis=2)

    kb, vb = banded(k), banded(v)
    scores = jnp.einsum("bnqhgd,bnkhd->bnhgqk", qb, kb).astype(jnp.float32) * (HEAD_DIM ** -0.5)

    qi = jnp.arange(BLOCK)[:, None]
    kj = jnp.arange(2 * BLOCK)[None, :]
    dist = qi + BLOCK - kj
    key_pos = jnp.arange(nb)[:, None, None] * BLOCK - BLOCK + kj[None]
    mask = (dist >= 0) & (dist < WINDOW) & (key_pos >= 0)
    bias = rel_bias.astype(jnp.float32)[t5_causal_bucket(dist)]
    bias = bias.reshape(BLOCK, 2 * BLOCK, ATTN_KV_HEADS, grp).transpose(2, 3, 0, 1)
    scores = jnp.where(mask[None, :, None, None], scores + bias[None, None], -jnp.inf)

    sink = sinks.astype(jnp.float32).reshape(ATTN_KV_HEADS, grp)[None, None, :, :, None, None]
    m = jnp.maximum(scores.max(axis=-1, keepdims=True), sink)
    p = jnp.exp(scores - m)
    probs = p / (p.sum(axis=-1, keepdims=True) + jnp.exp(sink - m))
    out = jnp.einsum("bnhgqk,bnkhd->bnqhgd", probs.astype(v.dtype), vb)
    return out.reshape(b, s, ATTN_WIDTH)


def causal_depthwise_conv(x, w, bias):
    out = lax.conv_general_dilated(
        x, w[:, None, :].astype(x.dtype), window_strides=(1,), padding=[(CONV_WIDTH - 1, 0)],
        dimension_numbers=("NWC", "WIO", "NWC"), feature_group_count=x.shape[-1])
    return out + bias


def ssd_chunked(xs, dt, a, bm, cm):
    b, s = xs.shape[0], xs.shape[1]
    nc = s // CHUNK
    G, R, P, N = SSM_GROUPS, SSM_HEADS_PER_GROUP, SSM_HEAD_DIM, SSM_STATE
    xs = xs.reshape(b, nc, CHUNK, G, R, P).astype(jnp.float32)
    dt = dt.reshape(b, nc, CHUNK, G, R)
    bm = bm.reshape(b, nc, CHUNK, G, N).astype(jnp.float32)
    cm = cm.reshape(b, nc, CHUNK, G, N).astype(jnp.float32)
    a_cum = jnp.cumsum(dt * a, axis=2)
    xdt = xs * dt[..., None]
    causal = jnp.tril(jnp.ones((CHUNK, CHUNK), dtype=bool))[:, :, None, None]
    seg = a_cum[:, :, :, None] - a_cum[:, :, None, :]
    decay = jnp.exp(jnp.where(causal, seg, -jnp.inf))
    cb = jnp.einsum("bclgn,bcsgn->bclsg", cm, bm)
    y_diag = jnp.einsum("bclsg,bclsgr,bcsgrp->bclgrp", cb, decay, xdt)
    decay_to_end = jnp.exp(a_cum[:, :, -1:] - a_cum)
    states = jnp.einsum("bclgn,bclgr,bclgrp->bcgrpn", bm, decay_to_end, xdt)
    chunk_decay = jnp.exp(a_cum[:, :, -1])

    def step(h, inp):
        st, dec = inp
        return h * dec[..., None, None] + st, h

    h0 = jnp.zeros((b, G, R, P, N), jnp.float32)
    _, h_prev = lax.scan(step, h0, (jnp.moveaxis(states, 1, 0), jnp.moveaxis(chunk_decay, 1, 0)))
    h_prev = jnp.moveaxis(h_prev, 0, 1)
    y_off = jnp.einsum("bclgn,bcgrpn,bclgr->bclgrp", cm, h_prev, jnp.exp(a_cum))
    return (y_diag + y_off).reshape(b, s, G, R, P)


def hybrid_layer(x, c, w_ada, b_ada, norm_w, w_in, q_norm_w, k_norm_w, rel_bias, sinks,
                 conv_w, conv_b, dt_bias, a_log, d_skip, ssm_norm_w,
                 w_attn_proj, w_ssm_proj, w_out):
    b, s, _ = x.shape
    G, R, P, N = SSM_GROUPS, SSM_HEADS_PER_GROUP, SSM_HEAD_DIM, SSM_STATE
    mod = jax.nn.silu(c) @ w_ada + b_ada
    shift, scale, gate = jnp.split(mod, 3, axis=-1)
    h = rms_norm(x, norm_w) * (1 + scale[:, None]) + shift[:, None]
    proj = h @ w_in
    widths = [ATTN_WIDTH, KV_WIDTH, KV_WIDTH, ATTN_WIDTH, SSM_WIDTH, XBC_WIDTH, SSM_HEADS, D_MODEL, D_MODEL]
    q, k, v, z_a, z_m, xbc, dt_raw, g_a, g_b = jnp.split(proj, list(np.cumsum(widths)[:-1]), axis=-1)

    y_a = sliding_window_attention(q.reshape(b, s, ATTN_HEADS, HEAD_DIM),
                                   k.reshape(b, s, ATTN_KV_HEADS, HEAD_DIM),
                                   v.reshape(b, s, ATTN_KV_HEADS, HEAD_DIM),
                                   q_norm_w, k_norm_w, rel_bias, sinks)
    y_a = (y_a * jax.nn.silu(z_a)) @ w_attn_proj

    xbc = jax.nn.silu(causal_depthwise_conv(xbc, conv_w, conv_b))
    xs, bm, cm = jnp.split(xbc, [SSM_WIDTH, SSM_WIDTH + G * N], axis=-1)
    xs = xs.reshape(b, s, G, R, P)
    dt = jax.nn.softplus((dt_raw + dt_bias).astype(jnp.float32)).reshape(b, s, G, R)
    a = -jnp.exp(a_log.astype(jnp.float32)).reshape(G, R)
    y = ssd_chunked(xs, dt, a, bm.reshape(b, s, G, N), cm.reshape(b, s, G, N))
    y = y + d_skip.astype(jnp.float32).reshape(G, R)[:, :, None] * xs.astype(jnp.float32)
    y = y.reshape(b, s, SSM_WIDTH).astype(x.dtype) * jax.nn.silu(z_m)
    y = rms_norm(y.reshape(b, s, G, SSM_WIDTH // G), ssm_norm_w.reshape(G, SSM_WIDTH // G))
    y_b = y.reshape(b, s, SSM_WIDTH) @ w_ssm_proj

    merged = jax.nn.sigmoid(g_a) * y_a + jax.nn.sigmoid(g_b) * y_b
    return x + gate[:, None] * (merged @ w_out)


def setup_inputs(seed: int = 0) -> dict:
    key = jax.random.key(seed)
    ks = jax.random.split(key, 20)
    nrm = jax.random.normal
    L, D = DEPTH, D_MODEL
    dt0 = jnp.exp(jax.random.uniform(ks[12], (L, SSM_HEADS)) * (math.log(DT_MAX) - math.log(DT_MIN)) + math.log(DT_MIN))
    return {
        "x": nrm(ks[0], (BATCH, SEQ, D), jnp.float32),
        "c": nrm(ks[1], (BATCH, D), jnp.float32),
        "w_ada": nrm(ks[2], (L, D, 3 * D)) * (0.5 * D ** -0.5),
        "b_ada": 0.01 * nrm(ks[3], (L, 3 * D)),
        "norm_w": 1 + 0.02 * nrm(ks[4], (L, D)),
        "w_in": nrm(ks[5], (L, D, IN_WIDTH)) * D ** -0.5,
        "q_norm_w": 1 + 0.02 * nrm(ks[6], (L, HEAD_DIM)),
        "k_norm_w": 1 + 0.02 * nrm(ks[7], (L, HEAD_DIM)),
        "rel_bias": 0.5 * nrm(ks[8], (REL_BUCKETS, ATTN_HEADS)),
        "sinks": 0.5 * nrm(ks[9], (L, ATTN_HEADS)),
        "conv_w": nrm(ks[10], (L, CONV_WIDTH, XBC_WIDTH)) * CONV_WIDTH ** -0.5,
        "conv_b": 0.01 * nrm(ks[11], (L, XBC_WIDTH)),
        "dt_bias": dt0 + jnp.log(-jnp.expm1(-dt0)),
        "a_log": jnp.log(jax.random.uniform(ks[13], (L, SSM_HEADS), minval=1.0, maxval=16.0)),
        "d_skip": 1 + 0.1 * nrm(ks[14], (L, SSM_HEADS)),
        "ssm_norm_w": 1 + 0.02 * nrm(ks[15], (L, SSM_WIDTH)),
        "w_attn_proj": nrm(ks[16], (L, ATTN_WIDTH, D)) * ATTN_WIDTH ** -0.5,
        "w_ssm_proj": nrm(ks[17], (L, SSM_WIDTH, D)) * SSM_WIDTH ** -0.5,
        "w_out": nrm(ks[18], (L, D, D)) * D ** -0.5,
    }


def reference(x, c, w_ada, b_ada, norm_w, w_in, q_norm_w, k_norm_w, rel_bias, sinks,
              conv_w, conv_b, dt_bias, a_log, d_skip, ssm_norm_w,
              w_attn_proj, w_ssm_proj, w_out):
    for i in range(DEPTH):
        x = hybrid_layer(x, c, w_ada[i], b_ada[i], norm_w[i], w_in[i], q_norm_w[i], k_norm_w[i],
                         rel_bias, sinks[i], conv_w[i], conv_b[i], dt_bias[i], a_log[i], d_skip[i],
                         ssm_norm_w[i], w_attn_proj[i], w_ssm_proj[i], w_out[i])
    return x
```

```python
import functools
import math

import numpy as np
import jax
import jax.numpy as jnp
from jax import lax
from jax.experimental import pallas as pl
from jax.experimental.pallas import tpu as pltpu

F32 = jnp.float32
BF16 = jnp.bfloat16

HEAD_DIM = 64
ATTN_HEADS = 16
ATTN_KV_HEADS = 4
BLOCK = 128
REL_BUCKETS = 32
REL_MAX_DIST = 128
SSM_HEAD_DIM = 64
SSM_GROUPS = 4
SSM_STATE = 128
CONV_WIDTH = 4
CHUNK = 128
EPS = 1e-6
NEG = -1e30
LANES = 128
VMEM_LIMIT = 56 * 1024 * 1024

ROWS_INPROJ = 512
ROWS_ATTN = 512
ROWS_SSD = 256


def _sigmoid(x):
    return 1.0 / (1.0 + jnp.exp(-x))


def _silu(x):
    return x * _sigmoid(x)


def _softplus(x):
    return jnp.maximum(x, 0.0) + jnp.log(1.0 + jnp.exp(-jnp.abs(x)))


def _dot(a, b):
    return jnp.dot(a, b, preferred_element_type=F32)


def _dot_nt(a, b):
    return lax.dot_general(a, b, (((1,), (1,)), ((), ())), preferred_element_type=F32)


def _resident(shape):
    nd = len(shape)
    return pl.BlockSpec(shape, lambda *_: (0,) * nd, pipeline_mode=pl.Buffered(1))


def _mod_kernel(c_ref, w_ref, b_ref, o_ref):
    s = _silu(c_ref[...])
    o_ref[...] = _dot(s.astype(BF16), w_ref[...].astype(BF16)) + b_ref[...]


def _mod_call(c8, w_ada, b_ada):
    d, n = w_ada.shape
    tn = 1024
    return pl.pallas_call(
        _mod_kernel,
        grid=(n // tn,),
        in_specs=[pl.BlockSpec((8, d), lambda j: (0, 0)),
                  pl.BlockSpec((d, tn), lambda j: (0, j)),
                  pl.BlockSpec((1, tn), lambda j: (0, j))],
        out_specs=pl.BlockSpec((8, tn), lambda j: (0, j)),
        out_shape=jax.ShapeDtypeStruct((8, n), F32),
        name="mod",
    )(c8, w_ada, b_ada)


def _bucket_table():
    qi = np.arange(BLOCK)[:, None]
    kj = np.arange(2 * BLOCK)[None, :]
    dist = qi + BLOCK - kj
    n = np.maximum(dist, 0)
    max_exact = REL_BUCKETS // 2
    nf = np.maximum(n, 1).astype(np.float32)
    large = max_exact + (np.log(nf / max_exact) / math.log(REL_MAX_DIST / max_exact)
                         * (REL_BUCKETS - max_exact)).astype(np.int32)
    large = np.minimum(large, REL_BUCKETS - 1)
    bucket = np.where(n < max_exact, n, large)
    valid = (dist >= 0) & (dist < BLOCK)
    return np.where(valid, bucket, -1).astype(np.int32)


def _bias_kernel(rb_ref, idx_ref, o_ref):
    h = pl.program_id(0)
    idx = idx_ref[...]
    acc = jnp.zeros(idx.shape, F32)
    for b in range(REL_BUCKETS):
        acc = jnp.where(idx == b, rb_ref[b, h], acc)
    o_ref[0] = jnp.where(idx >= 0, acc, NEG)


def _bias_call(rel_bias):
    idx = jnp.asarray(_bucket_table())
    return pl.pallas_call(
        _bias_kernel,
        grid=(ATTN_HEADS,),
        in_specs=[pl.BlockSpec(memory_space=pltpu.SMEM),
                  pl.BlockSpec((BLOCK, 2 * BLOCK), lambda h: (0, 0))],
        out_specs=pl.BlockSpec((1, BLOCK, 2 * BLOCK), lambda h: (h, 0, 0)),
        out_shape=jax.ShapeDtypeStruct((ATTN_HEADS, BLOCK, 2 * BLOCK), F32),
        name="bias",
    )(rel_bias, idx)


def _inproj_kernel(x_ref, shift_ref, scale_ref, nw_ref,
                   wqkv_ref, wza_ref, wzm_ref, wxbc_ref, wg_ref, wdt_ref, dtb_ref,
                   qkv_o, za_o, zm_o, xbc_o, g_o, dt_o):
    x = x_ref[...]
    ms = jnp.mean(x * x, axis=-1, keepdims=True)
    xn = x * lax.rsqrt(ms + EPS) * nw_ref[...]
    h = (xn * (1.0 + scale_ref[0]) + shift_ref[0]).astype(BF16)

    def project(w_ref, o_ref, act):
        n = w_ref.shape[1]
        step = 512
        for c0 in range(0, n, step):
            y = _dot(h, w_ref[:, c0:c0 + step])
            o_ref[:, c0:c0 + step] = act(y).astype(o_ref.dtype)

    project(wqkv_ref, qkv_o, lambda y: y)
    project(wza_ref, za_o, _silu)
    project(wzm_ref, zm_o, _silu)
    project(wxbc_ref, xbc_o, lambda y: y)
    project(wg_ref, g_o, _sigmoid)
    dt_o[...] = _softplus(_dot(h, wdt_ref[...]) + dtb_ref[...])


def _inproj_call(x2, shift, scale, norm_w, wqkv, wza, wzm, wxbc, wg, wdt, dtb, batch, seq):
    t, d = x2.shape
    ts = min(ROWS_INPROJ, seq)
    ns = seq // ts
    row = lambda b, i: (b * ns + i, 0)
    per_b = pl.BlockSpec((1, 1, d), lambda b, i: (b, 0, 0))
    weights = [wqkv, wza, wzm, wxbc, wg, wdt]
    outs = [(wqkv.shape[1], BF16), (wza.shape[1], BF16), (wzm.shape[1], BF16),
            (wxbc.shape[1], BF16), (wg.shape[1], BF16), (wdt.shape[1], F32)]
    return pl.pallas_call(
        _inproj_kernel,
        grid=(batch, ns),
        in_specs=[pl.BlockSpec((ts, d), row), per_b, per_b, _resident((1, d))]
                 + [_resident(w.shape) for w in weights] + [_resident(dtb.shape)],
        out_specs=[pl.BlockSpec((ts, n), row) for n, _ in outs],
        out_shape=[jax.ShapeDtypeStruct((t, n), dt) for n, dt in outs],
        compiler_params=pltpu.CompilerParams(
            dimension_semantics=("parallel", "parallel"), vmem_limit_bytes=VMEM_LIMIT),
        name="inproj",
    )(x2, shift, scale, norm_w, *weights, dtb)


def _segment_rms(x, bd, w):
    ms = _dot((x * x).astype(BF16), bd)
    return x * lax.rsqrt(ms + EPS) * w


def _attn_kernel(sinks_ref, q_ref, kvc_ref, kvp_ref, za_ref, ga_ref, bias_ref,
                 qw_ref, kw_ref, bd_ref, wap_ref, o_ref,
                 qn_s, kn_s, v_s, y_s):
    rows = q_ref.shape[0]
    nblk = rows // BLOCK
    kvw = ATTN_KV_HEADS * HEAD_DIM
    first_tile = pl.program_id(1) == 0
    bd = bd_ref[...]
    qw = qw_ref[...] * (HEAD_DIM ** -0.5)
    kw = kw_ref[...]

    for j in range(q_ref.shape[1] // LANES):
        sl = slice(j * LANES, (j + 1) * LANES)
        qn_s[:, sl] = _segment_rms(q_ref[:, sl].astype(F32), bd, qw).astype(BF16)
    for j in range(kvw // LANES):
        sl = slice(j * LANES, (j + 1) * LANES)
        kn_s[0:BLOCK, sl] = _segment_rms(kvp_ref[:, sl].astype(F32), bd, kw)
        kn_s[BLOCK:, sl] = _segment_rms(kvc_ref[:, sl].astype(F32), bd, kw)
    v_s[0:BLOCK, :] = kvp_ref[:, kvw:2 * kvw].astype(F32)
    v_s[BLOCK:, :] = kvc_ref[:, kvw:2 * kvw].astype(F32)

    lane = lax.broadcasted_iota(jnp.int32, (2 * BLOCK, LANES), 1)
    low_half = lane < HEAD_DIM
    key_col = lax.broadcasted_iota(jnp.int32, (BLOCK, 2 * BLOCK), 1)

    def block_body(qb, carry):
        r0 = pl.multiple_of(qb * BLOCK, BLOCK)
        kblk = kn_s[pl.ds(r0, 2 * BLOCK), :]
        vblk = v_s[pl.ds(r0, 2 * BLOCK), :]
        no_prev = jnp.logical_and(first_tile, qb == 0)
        extra = jnp.where(jnp.logical_and(no_prev, key_col < BLOCK), NEG, 0.0)
        for h in range(ATTN_KV_HEADS):
            j, half = divmod(h, 2)
            kp = kblk[:, j * LANES:(j + 1) * LANES]
            vp = vblk[:, j * LANES:(j + 1) * LANES]
            if half == 0:
                k_lo = jnp.where(low_half, kp, 0.0)
                v_lo = jnp.where(low_half, vp, 0.0)
                k_hi = pltpu.roll(k_lo, HEAD_DIM, axis=1)
                v_hi = pltpu.roll(v_lo, HEAD_DIM, axis=1)
            else:
                k_hi = jnp.where(low_half, 0.0, kp)
                v_hi = jnp.where(low_half, 0.0, vp)
                k_lo = pltpu.roll(k_hi, HEAD_DIM, axis=1)
                v_lo = pltpu.roll(v_hi, HEAD_DIM, axis=1)
            sides = ((k_lo.astype(BF16), v_lo.astype(BF16)), (k_hi.astype(BF16), v_hi.astype(BF16)))
            for p in range(2):
                pair = 2 * h + p
                sl = slice(pair * LANES, (pair + 1) * LANES)
                qp = qn_s[pl.ds(r0, BLOCK), sl]
                acc = None
                for side, (kk, vv) in enumerate(sides):
                    head = 2 * pair + side
                    s = _dot_nt(qp, kk) + bias_ref[head] + extra
                    sink = sinks_ref[head]
                    m = jnp.maximum(jnp.max(s, axis=-1, keepdims=True), sink)
                    e = jnp.exp(s - m)
                    den = jnp.sum(e, axis=-1, keepdims=True) + jnp.exp(sink - m)
                    probs = (e * (1.0 / den)).astype(BF16)
                    o = _dot(probs, vv)
                    acc = o if acc is None else acc + o
                za = za_ref[pl.ds(r0, BLOCK), sl].astype(F32)
                y_s[pl.ds(r0, BLOCK), sl] = (acc * za).astype(BF16)
        return carry

    lax.fori_loop(0, nblk, block_body, 0)
    ya = _dot(y_s[...], wap_ref[...])
    o_ref[...] = (ya * ga_ref[...].astype(F32)).astype(o_ref.dtype)


def _attn_call(qkv, za, g, bias, sinks, qw, kw, bd, wap, batch, seq):
    t = qkv.shape[0]
    aw = ATTN_HEADS * HEAD_DIM
    kvw2 = 2 * ATTN_KV_HEADS * HEAD_DIM
    rows = min(ROWS_ATTN, seq)
    nt = seq // rows
    bpt = rows // BLOCK
    row = lambda b, i: (b * nt + i, 0)
    return pl.pallas_call(
        _attn_kernel,
        grid=(batch, nt),
        in_specs=[pl.BlockSpec(memory_space=pltpu.SMEM),
                  pl.BlockSpec((rows, aw), row),
                  pl.BlockSpec((rows, kvw2), lambda b, i: (b * nt + i, aw // kvw2)),
                  pl.BlockSpec((BLOCK, kvw2),
                               lambda b, i: (jnp.maximum((b * nt + i) * bpt - 1, 0), aw // kvw2)),
                  pl.BlockSpec((rows, aw), row),
                  pl.BlockSpec((rows, aw), row),
                  _resident(bias.shape), _resident(qw.shape), _resident(kw.shape),
                  _resident(bd.shape), _resident(wap.shape)],
        out_specs=pl.BlockSpec((rows, wap.shape[1]), row),
        out_shape=jax.ShapeDtypeStruct((t, wap.shape[1]), BF16),
        scratch_shapes=[pltpu.VMEM((rows, aw), BF16),
                        pltpu.VMEM((rows + BLOCK, kvw2 // 2), F32),
                        pltpu.VMEM((rows + BLOCK, kvw2 // 2), F32),
                        pltpu.VMEM((rows, aw), BF16)],
        compiler_params=pltpu.CompilerParams(
            dimension_semantics=("parallel", "parallel"), vmem_limit_bytes=VMEM_LIMIT),
        name="attn",
    )(sinks, qkv, qkv, qkv, za, g, bias, qw, kw, bd, wap)


def _ssd_kernel(xbc_ref, dt_ref, zm_ref, gb_ref, ya_ref, x_ref, gate_ref,
                cw_ref, cb_ref, alog_ref, dsk_ref, nw_ref, tri_ref, wsp_ref, wout_ref,
                o_ref, xpad_s, h_s, yn_s):
    rows = xbc_ref.shape[0]
    nchunk = rows // CHUNK
    sw = zm_ref.shape[1]
    gw = sw // SSM_GROUPS
    pairs_per_group = gw // LANES
    halo = 8

    @pl.when(pl.program_id(1) == 0)
    def _():
        xpad_s[0:halo, :] = jnp.zeros((halo, xpad_s.shape[1]), F32)
        h_s[...] = jnp.zeros(h_s.shape, F32)

    xpad_s[halo:halo + rows, :] = xbc_ref[...].astype(F32)

    a_row = -jnp.exp(alog_ref[...])
    tri = tri_ref[...]
    li = lax.broadcasted_iota(jnp.int32, (CHUNK, CHUNK), 0)
    si = lax.broadcasted_iota(jnp.int32, (CHUNK, CHUNK), 1)
    causal = li >= si
    low_half = si < SSM_HEAD_DIM

    for c in range(nchunk):
        r0 = c * CHUNK
        xc = cb_ref[...] + cw_ref[0:1, :] * xpad_s[r0 + halo - 3:r0 + halo - 3 + CHUNK, :]
        for w in range(1, CONV_WIDTH):
            off = r0 + halo - (CONV_WIDTH - 1) + w
            xc = xc + cw_ref[w:w + 1, :] * xpad_s[off:off + CHUNK, :]
        xc = _silu(xc)
        bm = xc[:, sw:sw + SSM_GROUPS * SSM_STATE]
        cm = xc[:, sw + SSM_GROUPS * SSM_STATE:]

        dt = dt_ref[r0:r0 + CHUNK, :]
        dta = dt * a_row
        dta_hi = dta.astype(BF16)
        dta_lo = (dta - dta_hi.astype(F32)).astype(BF16)
        a_cum = _dot(tri, dta_hi) + _dot(tri, dta_lo)
        a_cum_t = a_cum.T
        dt_t = dt.T
        a_end_t = a_cum_t[:, CHUNK - 1:CHUNK]
        w_t = dt_t * jnp.exp(a_end_t - a_cum_t)
        e_end_t = jnp.exp(a_end_t)

        for g in range(SSM_GROUPS):
            bm_g = bm[:, g * SSM_STATE:(g + 1) * SSM_STATE]
            cm_g = cm[:, g * SSM_STATE:(g + 1) * SSM_STATE]
            cm_gb = cm_g.astype(BF16)
            cb = _dot_nt(cm_gb, bm_g.astype(BF16))
            bm_gt = bm_g.T
            gsl = slice(g * gw, (g + 1) * gw)
            hprev = h_s[g]
            y_off = _dot(cm_gb, hprev.astype(BF16))
            y_parts = []
            for p in range(pairs_per_group):
                pair = g * pairs_per_group + p
                lsl = slice(pair * LANES, (pair + 1) * LANES)
                xs_pair = xc[:, lsl]
                xs_lo = jnp.where(low_half, xs_pair, 0.0)
                xs_hi = jnp.where(low_half, 0.0, xs_pair)
                rhs = jnp.concatenate([xs_lo, xs_hi], axis=0).astype(BF16)
                m_parts, b_parts, e_cols, e_ends = [], [], [], []
                for side in range(2):
                    hh = 2 * pair + side
                    col_a = jnp.broadcast_to(a_cum[:, hh:hh + 1], (CHUNK, CHUNK))
                    row_a = a_cum_t[hh:hh + 1, :]
                    dec = jnp.exp(jnp.where(causal, col_a - row_a, NEG))
                    m_parts.append(cb * dec * dt_t[hh:hh + 1, :])
                    b_parts.append(bm_gt * w_t[hh:hh + 1, :])
                    e_cols.append(jnp.exp(col_a))
                    e_ends.append(e_end_t[hh:hh + 1, :])
                lhs = jnp.concatenate(
                    [jnp.concatenate(m_parts, axis=1), jnp.concatenate(b_parts, axis=1)],
                    axis=0).astype(BF16)
                res = _dot(lhs, rhs)
                e_col = jnp.where(low_half, e_cols[0], e_cols[1])
                y_pair = res[0:CHUNK] + e_col * y_off[:, p * LANES:(p + 1) * LANES]
                y_pair = y_pair + dsk_ref[:, lsl] * xs_pair
                y_parts.append(y_pair * zm_ref[r0:r0 + CHUNK, lsl].astype(F32))
                e_end = jnp.where(low_half, e_ends[0], e_ends[1])
                h_s[g, :, p * LANES:(p + 1) * LANES] = (
                    hprev[:, p * LANES:(p + 1) * LANES] * e_end + res[CHUNK:])
            yg = jnp.concatenate(y_parts, axis=1)
            ms = jnp.mean(yg * yg, axis=-1, keepdims=True)
            yn_s[r0:r0 + CHUNK, gsl] = (yg * lax.rsqrt(ms + EPS) * nw_ref[:, gsl]).astype(BF16)

    xpad_s[0:halo, :] = xpad_s[rows:rows + halo, :]

    yb = _dot(yn_s[...], wsp_ref[...])
    merged = ya_ref[...].astype(F32) + gb_ref[...].astype(F32) * yb
    o = _dot(merged.astype(BF16), wout_ref[...])
    o_ref[...] = x_ref[...] + gate_ref[0] * o


def _ssd_call(xbc, dt, zm, g, ya, x2, gate, cw, cb, alog, dsk, nw, tri, wsp, wout, batch, seq):
    t, d = x2.shape
    sw = zm.shape[1]
    rows = min(ROWS_SSD, seq)
    nt = seq // rows
    row = lambda b, i: (b * nt + i, 0)
    return pl.pallas_call(
        _ssd_kernel,
        grid=(batch, nt),
        in_specs=[pl.BlockSpec((rows, xbc.shape[1]), row),
                  pl.BlockSpec((rows, dt.shape[1]), row),
                  pl.BlockSpec((rows, sw), row),
                  pl.BlockSpec((rows, d), lambda b, i: (b * nt + i, 1)),
                  pl.BlockSpec((rows, d), row),
                  pl.BlockSpec((rows, d), row),
                  pl.BlockSpec((1, 1, d), lambda b, i: (b, 0, 0)),
                  _resident(cw.shape), _resident(cb.shape), _resident(alog.shape),
                  _resident(dsk.shape), _resident(nw.shape), _resident(tri.shape),
                  _resident(wsp.shape), _resident(wout.shape)],
        out_specs=pl.BlockSpec((rows, d), row),
        out_shape=jax.ShapeDtypeStruct((t, d), F32),
        scratch_shapes=[pltpu.VMEM((rows + 8, xbc.shape[1]), F32),
                        pltpu.VMEM((SSM_GROUPS, SSM_STATE, sw // SSM_GROUPS), F32),
                        pltpu.VMEM((rows, sw), BF16)],
        compiler_params=pltpu.CompilerParams(
            dimension_semantics=("arbitrary", "arbitrary"), vmem_limit_bytes=VMEM_LIMIT),
        name="ssd",
    )(xbc, dt, zm, g, ya, x2, gate, cw, cb, alog, dsk, nw, tri, wsp, wout)


def _layer(x, c, w_ada, b_ada, norm_w, w_in, q_norm_w, k_norm_w, rel_bias, sinks,
           conv_w, conv_b, dt_bias, a_log, d_skip, ssm_norm_w, w_attn_proj, w_ssm_proj, w_out):
    batch, seq, d = x.shape
    aw = ATTN_HEADS * HEAD_DIM
    kvw = ATTN_KV_HEADS * HEAD_DIM
    sw = w_ssm_proj.shape[0]
    ssm_heads = sw // SSM_HEAD_DIM
    xbc_w = sw + 2 * SSM_GROUPS * SSM_STATE
    assert seq % BLOCK == 0 and seq % CHUNK == 0 and ssm_heads <= LANES

    c8 = jnp.zeros((8, d), F32).at[:batch].set(c)
    mod = _mod_call(c8, w_ada, b_ada.reshape(1, -1))[:batch]
    shift, scale, gate = (mod[:, k * d:(k + 1) * d].reshape(batch, 1, d) for k in range(3))

    o = np.cumsum([0, aw, kvw, kvw, aw, sw, xbc_w, ssm_heads, d, d])
    wb = w_in.astype(BF16)
    wqkv, wza, wzm, wxbc = wb[:, o[0]:o[3]], wb[:, o[3]:o[4]], wb[:, o[4]:o[5]], wb[:, o[5]:o[6]]
    wdt = jnp.pad(wb[:, o[6]:o[7]], ((0, 0), (0, LANES - ssm_heads)))
    wg = wb[:, o[7]:o[9]]
    pad_heads = lambda v: jnp.pad(v.astype(F32), (0, LANES - ssm_heads)).reshape(1, LANES)

    x2 = x.reshape(batch * seq, d)
    qkv, za, zm, xbc, g, dt = _inproj_call(
        x2, shift, scale, norm_w.reshape(1, d), wqkv, wza, wzm, wxbc, wg, wdt,
        pad_heads(dt_bias), batch, seq)

    bias = _bias_call(rel_bias.astype(F32))
    seg = np.arange(LANES) // HEAD_DIM
    bd = jnp.asarray((seg[:, None] == seg[None, :]).astype(np.float32) / HEAD_DIM, dtype=BF16)
    qw = jnp.tile(q_norm_w.astype(F32), LANES // HEAD_DIM).reshape(1, LANES)
    kw = jnp.tile(k_norm_w.astype(F32), LANES // HEAD_DIM).reshape(1, LANES)
    ya = _attn_call(qkv, za, g, bias, sinks.astype(F32), qw, kw, bd,
                    w_attn_proj.astype(BF16), batch, seq)

    tri = jnp.asarray(np.tril(np.ones((CHUNK, CHUNK), np.float32)), dtype=BF16)
    out = _ssd_call(xbc, dt, zm, g, ya, x2, gate,
                    conv_w.astype(F32), conv_b.reshape(1, -1).astype(F32), pad_heads(a_log),
                    jnp.repeat(d_skip.astype(F32), SSM_HEAD_DIM).reshape(1, sw),
                    ssm_norm_w.reshape(1, sw).astype(F32), tri,
                    w_ssm_proj.astype(BF16), w_out.astype(BF16), batch, seq)
    return out.reshape(batch, seq, d)


def kernel(x, c, w_ada, b_ada, norm_w, w_in, q_norm_w, k_norm_w, rel_bias, sinks, conv_w, conv_b,
           dt_bias, a_log, d_skip, ssm_norm_w, w_attn_proj, w_ssm_proj, w_out):
    depth = w_in.shape[0]
    for i in range(depth):
        x = _layer(x, c, w_ada[i], b_ada[i], norm_w[i], w_in[i], q_norm_w[i], k_norm_w[i],
                   rel_bias, sinks[i], conv_w[i], conv_b[i], dt_bias[i], a_log[i], d_skip[i],
                   ssm_norm_w[i], w_attn_proj[i], w_ssm_proj[i], w_out[i])
    return x
```

```python
import functools
import math

import numpy as np
import jax
import jax.numpy as jnp
from jax import lax
from jax.experimental import pallas as pl
from jax.experimental.pallas import tpu as pltpu

F32 = jnp.float32
BF16 = jnp.bfloat16

HEAD_DIM = 64
ATTN_HEADS = 16
ATTN_KV_HEADS = 4
GROUP = ATTN_HEADS // ATTN_KV_HEADS
BLOCK = 128
REL_BUCKETS = 32
REL_MAX_DIST = 128
SSM_HEAD_DIM = 64
SSM_GROUPS = 4
SSM_STATE = 128
CONV_WIDTH = 4
CHUNK = 128
EPS = 1e-6
NEG = -1e30
LANES = 128
VMEM_LIMIT = 56 * 1024 * 1024

ROWS_INPROJ = 512
ROWS_ATTN = 512
ROWS_SSD = 256


def _sigmoid(x):
    return 1.0 / (1.0 + jnp.exp(-x))


def _silu(x):
    return x * _sigmoid(x)


def _softplus(x):
    return jnp.maximum(x, 0.0) + jnp.log(1.0 + jnp.exp(-jnp.abs(x)))


def _dot(a, b):
    return jnp.dot(a, b, preferred_element_type=F32)


def _dot_nt(a, b):
    return lax.dot_general(a, b, (((1,), (1,)), ((), ())), preferred_element_type=F32)


def _resident(shape):
    nd = len(shape)
    return pl.BlockSpec(shape, lambda *_: (0,) * nd, pipeline_mode=pl.Buffered(1))


def _mod_kernel(c_ref, w_ref, b_ref, o_ref):
    s = _silu(c_ref[...])
    o_ref[...] = _dot(s.astype(BF16), w_ref[...].astype(BF16)) + b_ref[...]


def _mod_call(c8, w_ada, b_ada):
    d, n = w_ada.shape
    tn = 1024
    return pl.pallas_call(
        _mod_kernel,
        grid=(n // tn,),
        in_specs=[pl.BlockSpec((8, d), lambda j: (0, 0)),
                  pl.BlockSpec((d, tn), lambda j: (0, j)),
                  pl.BlockSpec((1, tn), lambda j: (0, j))],
        out_specs=pl.BlockSpec((8, tn), lambda j: (0, j)),
        out_shape=jax.ShapeDtypeStruct((8, n), F32),
        name="mod",
    )(c8, w_ada, b_ada)


def _bucket_table():
    qi = np.arange(BLOCK)[:, None]
    kj = np.arange(2 * BLOCK)[None, :]
    dist = qi + BLOCK - kj
    n = np.maximum(dist, 0)
    max_exact = REL_BUCKETS // 2
    nf = np.maximum(n, 1).astype(np.float32)
    large = max_exact + (np.log(nf / max_exact) / math.log(REL_MAX_DIST / max_exact)
                         * (REL_BUCKETS - max_exact)).astype(np.int32)
    large = np.minimum(large, REL_BUCKETS - 1)
    bucket = np.where(n < max_exact, n, large)
    valid = (dist >= 0) & (dist < BLOCK)
    return np.where(valid, bucket, -1).astype(np.int32)


def _bias_kernel(rb_ref, idx_ref, o_ref):
    t = pl.program_id(0)
    kv = t % ATTN_KV_HEADS
    idx = idx_ref[...]
    key_row = lax.broadcasted_iota(jnp.int32, idx.shape, 0)
    keep = jnp.logical_and(idx >= 0, jnp.logical_or(t < ATTN_KV_HEADS, key_row >= BLOCK))
    for i in range(GROUP):
        acc = jnp.zeros(idx.shape, F32)
        for b in range(REL_BUCKETS):
            acc = jnp.where(idx == b, rb_ref[b, kv * GROUP + i], acc)
        o_ref[0, :, i * BLOCK:(i + 1) * BLOCK] = jnp.where(keep, acc, NEG)


def _bias_call(rel_bias):
    idx_t = jnp.asarray(np.ascontiguousarray(_bucket_table().T))
    return pl.pallas_call(
        _bias_kernel,
        grid=(2 * ATTN_KV_HEADS,),
        in_specs=[pl.BlockSpec(memory_space=pltpu.SMEM),
                  pl.BlockSpec((2 * BLOCK, BLOCK), lambda t: (0, 0))],
        out_specs=pl.BlockSpec((1, 2 * BLOCK, GROUP * BLOCK), lambda t: (t, 0, 0)),
        out_shape=jax.ShapeDtypeStruct((2 * ATTN_KV_HEADS, 2 * BLOCK, GROUP * BLOCK), F32),
        name="bias",
    )(rel_bias, idx_t)


def _inproj_kernel(x_ref, shift_ref, scale_ref, nw_ref, w_ref, dtb_ref,
                   qkv_o, za_o, zm_o, xbc_o, dt_o, g_o):
    x = x_ref[...]
    ms = jnp.mean(x * x, axis=-1, keepdims=True)
    xn = x * lax.rsqrt(ms + EPS) * nw_ref[...]
    h = (xn * (1.0 + scale_ref[0]) + shift_ref[0]).astype(BF16)

    col = 0
    for o_ref, act in ((qkv_o, None), (za_o, _silu), (zm_o, _silu), (xbc_o, None),
                       (dt_o, lambda y: _softplus(y + dtb_ref[...])), (g_o, _sigmoid)):
        n = o_ref.shape[1]
        step = min(n, 512)
        for c0 in range(0, n, step):
            y = _dot(h, w_ref[:, col + c0:col + c0 + step])
            o_ref[:, c0:c0 + step] = (y if act is None else act(y)).astype(o_ref.dtype)
        col += n


def _inproj_call(x2, shift, scale, norm_w, w_all, dtb, widths, batch, seq):
    t, d = x2.shape
    ts = min(ROWS_INPROJ, seq)
    ns = seq // ts
    row = lambda b, i: (b * ns + i, 0)
    per_b = pl.BlockSpec((1, 1, d), lambda b, i: (b, 0, 0))
    dtypes = [BF16, BF16, BF16, BF16, F32, BF16]
    return pl.pallas_call(
        _inproj_kernel,
        grid=(batch, ns),
        in_specs=[pl.BlockSpec((ts, d), row), per_b, per_b, _resident((1, d)),
                  _resident(w_all.shape), _resident(dtb.shape)],
        out_specs=[pl.BlockSpec((ts, n), row) for n in widths],
        out_shape=[jax.ShapeDtypeStruct((t, n), dt) for n, dt in zip(widths, dtypes)],
        compiler_params=pltpu.CompilerParams(
            dimension_semantics=("parallel", "parallel"), vmem_limit_bytes=VMEM_LIMIT),
        name="inproj",
    )(x2, shift, scale, norm_w, w_all, dtb)


def _segment_rms(x, bd, w):
    ms = _dot((x * x).astype(BF16), bd)
    return x * lax.rsqrt(ms + EPS) * w


def _attn_kernel(q_ref, kvc_ref, kvp_ref, za_ref, ga_ref, bias_ref, sink_ref,
                 qw_ref, kw_ref, bd_ref, wap_ref, o_ref,
                 qn_s, kn_s, v_s, y_s, yt_s):
    rows = q_ref.shape[0]
    nblk = rows // BLOCK
    kvw = ATTN_KV_HEADS * HEAD_DIM
    first_tile = pl.program_id(1) == 0
    bd = bd_ref[...]
    qw = qw_ref[...] * (HEAD_DIM ** -0.5)
    kw = kw_ref[...]

    for j in range(q_ref.shape[1] // LANES):
        sl = slice(j * LANES, (j + 1) * LANES)
        qn_s[:, sl] = _segment_rms(q_ref[:, sl].astype(F32), bd, qw).astype(BF16)
    for j in range(kvw // LANES):
        sl = slice(j * LANES, (j + 1) * LANES)
        kn_s[0:BLOCK, sl] = _segment_rms(kvp_ref[:, sl].astype(F32), bd, kw)
        kn_s[BLOCK:, sl] = _segment_rms(kvc_ref[:, sl].astype(F32), bd, kw)
    v_s[0:BLOCK, :] = kvp_ref[:, kvw:2 * kvw].astype(F32)
    v_s[BLOCK:, :] = kvc_ref[:, kvw:2 * kvw].astype(F32)

    lane = lax.broadcasted_iota(jnp.int32, (2 * BLOCK, LANES), 1)
    low_half = lane < HEAD_DIM
    pairs_per_kvpair = q_ref.shape[1] // LANES // (ATTN_KV_HEADS // 2)

    def block_body(qb, carry):
        r0 = pl.multiple_of(qb * BLOCK, BLOCK)
        kblk = kn_s[pl.ds(r0, 2 * BLOCK), :]
        v_t = v_s[pl.ds(r0, 2 * BLOCK), :].T.astype(BF16)
        no_prev = jnp.logical_and(first_tile, qb == 0)
        variant = jnp.where(no_prev, ATTN_KV_HEADS, 0)
        for j in range(ATTN_KV_HEADS // 2):
            kpair = kblk[:, j * LANES:(j + 1) * LANES]
            k_sides = (jnp.where(low_half, kpair, 0.0).astype(BF16),
                       jnp.where(low_half, 0.0, kpair).astype(BF16))
            q_stack = jnp.concatenate(
                [qn_s[pl.ds(r0, BLOCK), (j * pairs_per_kvpair + p) * LANES:(j * pairs_per_kvpair + p + 1) * LANES]
                 for p in range(pairs_per_kvpair)], axis=0)
            for side in range(2):
                kv = 2 * j + side
                s_t = _dot_nt(k_sides[side], q_stack) + bias_ref[variant + kv]
                sink = sink_ref[kv]
                m = jnp.maximum(jnp.max(s_t, axis=0, keepdims=True), sink)
                e = jnp.exp(s_t - m)
                den = jnp.sum(e, axis=0, keepdims=True) + jnp.exp(sink - m)
                probs_t = (e * (1.0 / den)).astype(BF16)
                o_t = _dot(v_t[kv * HEAD_DIM:(kv + 1) * HEAD_DIM, :], probs_t)
                for i in range(GROUP):
                    head = kv * GROUP + i
                    yt_s[head * HEAD_DIM:(head + 1) * HEAD_DIM, :] = o_t[:, i * BLOCK:(i + 1) * BLOCK]
        y = yt_s[...].T * za_ref[pl.ds(r0, BLOCK), :].astype(F32)
        y_s[pl.ds(r0, BLOCK), :] = y.astype(BF16)
        return carry

    lax.fori_loop(0, nblk, block_body, 0)
    ya = _dot(y_s[...], wap_ref[...])
    o_ref[...] = (ya * ga_ref[...].astype(F32)).astype(o_ref.dtype)


def _attn_call(qkv, za, g, bias, sink_rows, qw, kw, bd, wap, batch, seq):
    t = qkv.shape[0]
    aw = ATTN_HEADS * HEAD_DIM
    kvw2 = 2 * ATTN_KV_HEADS * HEAD_DIM
    rows = min(ROWS_ATTN, seq)
    nt = seq // rows
    bpt = rows // BLOCK
    row = lambda b, i: (b * nt + i, 0)
    return pl.pallas_call(
        _attn_kernel,
        grid=(batch, nt),
        in_specs=[pl.BlockSpec((rows, aw), row),
                  pl.BlockSpec((rows, kvw2), lambda b, i: (b * nt + i, aw // kvw2)),
                  pl.BlockSpec((BLOCK, kvw2),
                               lambda b, i: (jnp.maximum((b * nt + i) * bpt - 1, 0), aw // kvw2)),
                  pl.BlockSpec((rows, aw), row),
                  pl.BlockSpec((rows, aw), row),
                  _resident(bias.shape), _resident(sink_rows.shape),
                  _resident(qw.shape), _resident(kw.shape),
                  _resident(bd.shape), _resident(wap.shape)],
        out_specs=pl.BlockSpec((rows, wap.shape[1]), row),
        out_shape=jax.ShapeDtypeStruct((t, wap.shape[1]), BF16),
        scratch_shapes=[pltpu.VMEM((rows, aw), BF16),
                        pltpu.VMEM((rows + BLOCK, kvw2 // 2), F32),
                        pltpu.VMEM((rows + BLOCK, kvw2 // 2), F32),
                        pltpu.VMEM((rows, aw), BF16),
                        pltpu.VMEM((aw, BLOCK), F32)],
        compiler_params=pltpu.CompilerParams(
            dimension_semantics=("parallel", "parallel"), vmem_limit_bytes=VMEM_LIMIT),
        name="attn",
    )(qkv, qkv, qkv, za, g, bias, sink_rows, qw, kw, bd, wap)


def _ssd_kernel(xbc_ref, dt_ref, zm_ref, gb_ref, ya_ref, x_ref, gate_ref,
                cw_ref, cb_ref, alog_ref, dsk_ref, nw_ref, tri_ref, wsp_ref, wout_ref,
                o_ref, xpad_s, h_s, yn_s):
    rows = xbc_ref.shape[0]
    nchunk = rows // CHUNK
    sw = zm_ref.shape[1]
    gw = sw // SSM_GROUPS
    pairs_per_group = gw // LANES
    halo = 8

    @pl.when(pl.program_id(1) == 0)
    def _():
        xpad_s[0:halo, :] = jnp.zeros((halo, xpad_s.shape[1]), F32)
        h_s[...] = jnp.zeros(h_s.shape, F32)

    xpad_s[halo:halo + rows, :] = xbc_ref[...].astype(F32)

    a_row = -jnp.exp(alog_ref[...])
    tri = tri_ref[...]
    li = lax.broadcasted_iota(jnp.int32, (CHUNK, CHUNK), 0)
    si = lax.broadcasted_iota(jnp.int32, (CHUNK, CHUNK), 1)
    causal = li >= si
    low_half = si < SSM_HEAD_DIM

    for c in range(nchunk):
        r0 = c * CHUNK
        xc = cb_ref[...] + cw_ref[0:1, :] * xpad_s[r0 + halo - 3:r0 + halo - 3 + CHUNK, :]
        for w in range(1, CONV_WIDTH):
            off = r0 + halo - (CONV_WIDTH - 1) + w
            xc = xc + cw_ref[w:w + 1, :] * xpad_s[off:off + CHUNK, :]
        xc = _silu(xc)
        bm = xc[:, sw:sw + SSM_GROUPS * SSM_STATE]
        cm = xc[:, sw + SSM_GROUPS * SSM_STATE:]

        dt = dt_ref[r0:r0 + CHUNK, :]
        dta = dt * a_row
        dta_hi = dta.astype(BF16)
        dta_lo = (dta - dta_hi.astype(F32)).astype(BF16)
        a_cum = _dot(tri, dta_hi) + _dot(tri, dta_lo)
        a_cum_t = a_cum.T
        dt_t = dt.T
        a_end_t = a_cum_t[:, CHUNK - 1:CHUNK]
        w_t = dt_t * jnp.exp(a_end_t - a_cum_t)
        e_end_t = jnp.exp(a_end_t)

        for g in range(SSM_GROUPS):
            bm_g = bm[:, g * SSM_STATE:(g + 1) * SSM_STATE]
            cm_g = cm[:, g * SSM_STATE:(g + 1) * SSM_STATE]
            cm_gb = cm_g.astype(BF16)
            cb = _dot_nt(cm_gb, bm_g.astype(BF16))
            bm_gt = bm_g.T
            gsl = slice(g * gw, (g + 1) * gw)
            hprev = h_s[g]
            y_off = _dot(cm_gb, hprev.astype(BF16))
            y_parts = []
            for p in range(pairs_per_group):
                pair = g * pairs_per_group + p
                lsl = slice(pair * LANES, (pair + 1) * LANES)
                xs_pair = xc[:, lsl]
                xs_lo = jnp.where(low_half, xs_pair, 0.0)
                xs_hi = jnp.where(low_half, 0.0, xs_pair)
                rhs = jnp.concatenate([xs_lo, xs_hi], axis=0).astype(BF16)
                m_parts, b_parts, e_cols, e_ends = [], [], [], []
                for side in range(2):
                    hh = 2 * pair + side
                    col_a = jnp.broadcast_to(a_cum[:, hh:hh + 1], (CHUNK, CHUNK))
                    row_a = a_cum_t[hh:hh + 1, :]
                    dec = jnp.exp(jnp.where(causal, col_a - row_a, NEG))
                    m_parts.append(cb * dec * dt_t[hh:hh + 1, :])
                    b_parts.append(bm_gt * w_t[hh:hh + 1, :])
                    e_cols.append(jnp.exp(col_a))
                    e_ends.append(e_end_t[hh:hh + 1, :])
                lhs = jnp.concatenate(
                    [jnp.concatenate(m_parts, axis=1), jnp.concatenate(b_parts, axis=1)],
                    axis=0).astype(BF16)
                res = _dot(lhs, rhs)
                e_col = jnp.where(low_half, e_cols[0], e_cols[1])
                y_pair = res[0:CHUNK] + e_col * y_off[:, p * LANES:(p + 1) * LANES]
                y_pair = y_pair + dsk_ref[:, lsl] * xs_pair
                y_parts.append(y_pair * zm_ref[r0:r0 + CHUNK, lsl].astype(F32))
                e_end = jnp.where(low_half, e_ends[0], e_ends[1])
                h_s[g, :, p * LANES:(p + 1) * LANES] = (
                    hprev[:, p * LANES:(p + 1) * LANES] * e_end + res[CHUNK:])
            yg = jnp.concatenate(y_parts, axis=1)
            ms = jnp.mean(yg * yg, axis=-1, keepdims=True)
            yn_s[r0:r0 + CHUNK, gsl] = (yg * lax.rsqrt(ms + EPS) * nw_ref[:, gsl]).astype(BF16)

    xpad_s[0:halo, :] = xpad_s[rows:rows + halo, :]

    yb = _dot(yn_s[...], wsp_ref[...])
    merged = ya_ref[...].astype(F32) + gb_ref[...].astype(F32) * yb
    o = _dot(merged.astype(BF16), wout_ref[...])
    o_ref[...] = x_ref[...] + gate_ref[0] * o


def _ssd_call(xbc, dt, zm, g, ya, x2, gate, cw, cb, alog, dsk, nw, tri, wsp, wout, batch, seq):
    t, d = x2.shape
    sw = zm.shape[1]
    rows = min(ROWS_SSD, seq)
    nt = seq // rows
    row = lambda b, i: (b * nt + i, 0)
    return pl.pallas_call(
        _ssd_kernel,
        grid=(batch, nt),
        in_specs=[pl.BlockSpec((rows, xbc.shape[1]), row),
                  pl.BlockSpec((rows, dt.shape[1]), row),
                  pl.BlockSpec((rows, sw), row),
                  pl.BlockSpec((rows, d), lambda b, i: (b * nt + i, 1)),
                  pl.BlockSpec((rows, d), row),
                  pl.BlockSpec((rows, d), row),
                  pl.BlockSpec((1, 1, d), lambda b, i: (b, 0, 0)),
                  _resident(cw.shape), _resident(cb.shape), _resident(alog.shape),
                  _resident(dsk.shape), _resident(nw.shape), _resident(tri.shape),
                  _resident(wsp.shape), _resident(wout.shape)],
        out_specs=pl.BlockSpec((rows, d), row),
        out_shape=jax.ShapeDtypeStruct((t, d), F32),
        scratch_shapes=[pltpu.VMEM((rows + 8, xbc.shape[1]), F32),
                        pltpu.VMEM((SSM_GROUPS, SSM_STATE, sw // SSM_GROUPS), F32),
                        pltpu.VMEM((rows, sw), BF16)],
        compiler_params=pltpu.CompilerParams(
            dimension_semantics=("arbitrary", "arbitrary"), vmem_limit_bytes=VMEM_LIMIT),
        name="ssd",
    )(xbc, dt, zm, g, ya, x2, gate, cw, cb, alog, dsk, nw, tri, wsp, wout)


def _layer(x, c, w_ada, b_ada, norm_w, w_in, q_norm_w, k_norm_w, rel_bias, sinks,
           conv_w, conv_b, dt_bias, a_log, d_skip, ssm_norm_w, w_attn_proj, w_ssm_proj, w_out):
    batch, seq, d = x.shape
    aw = ATTN_HEADS * HEAD_DIM
    kvw = ATTN_KV_HEADS * HEAD_DIM
    sw = w_ssm_proj.shape[0]
    ssm_heads = sw // SSM_HEAD_DIM
    xbc_w = sw + 2 * SSM_GROUPS * SSM_STATE
    assert seq % BLOCK == 0 and seq % CHUNK == 0 and ssm_heads <= LANES

    c8 = jnp.zeros((8, d), F32).at[:batch].set(c)
    mod = _mod_call(c8, w_ada, b_ada.reshape(1, -1))[:batch]
    shift, scale, gate = (mod[:, k * d:(k + 1) * d].reshape(batch, 1, d) for k in range(3))

    o = np.cumsum([0, aw, kvw, kvw, aw, sw, xbc_w, ssm_heads, d, d])
    q_heads = [hd for j in range(ATTN_KV_HEADS // 2) for i in range(GROUP)
               for hd in (2 * j * GROUP + i, (2 * j + 1) * GROUP + i)]
    w_all = jnp.concatenate(
        [w_in[:, hd * HEAD_DIM:(hd + 1) * HEAD_DIM] for hd in q_heads]
        + [w_in[:, o[1]:o[6]], jnp.pad(w_in[:, o[6]:o[7]], ((0, 0), (0, LANES - ssm_heads))), w_in[:, o[7]:o[9]]],
        axis=1).astype(BF16)
    widths = [aw + 2 * kvw, aw, sw, xbc_w, LANES, 2 * d]
    pad_heads = lambda v: jnp.pad(v.astype(F32), (0, LANES - ssm_heads)).reshape(1, LANES)

    x2 = x.reshape(batch * seq, d)
    qkv, za, zm, xbc, dt, g = _inproj_call(
        x2, shift, scale, norm_w.reshape(1, d), w_all, pad_heads(dt_bias), widths, batch, seq)

    bias = _bias_call(rel_bias.astype(F32))
    seg = np.arange(LANES) // HEAD_DIM
    bd = jnp.asarray((seg[:, None] == seg[None, :]).astype(np.float32) / HEAD_DIM, dtype=BF16)
    qw = jnp.tile(q_norm_w.astype(F32), LANES // HEAD_DIM).reshape(1, LANES)
    kw = jnp.tile(k_norm_w.astype(F32), LANES // HEAD_DIM).reshape(1, LANES)
    sink_rows = jnp.repeat(sinks.astype(F32), BLOCK).reshape(ATTN_KV_HEADS, 1, GROUP * BLOCK)
    ya = _attn_call(qkv, za, g, bias, sink_rows, qw, kw, bd,
                    w_attn_proj.astype(BF16), batch, seq)

    tri = jnp.asarray(np.tril(np.ones((CHUNK, CHUNK), np.float32)), dtype=BF16)
    out = _ssd_call(xbc, dt, zm, g, ya, x2, gate,
                    conv_w.astype(F32), conv_b.reshape(1, -1).astype(F32), pad_heads(a_log),
                    jnp.repeat(d_skip.astype(F32), SSM_HEAD_DIM).reshape(1, sw),
                    ssm_norm_w.reshape(1, sw).astype(F32), tri,
                    w_ssm_proj.astype(BF16), w_out.astype(BF16), batch, seq)
    return out.reshape(batch, seq, d)


def kernel(x, c, w_ada, b_ada, norm_w, w_in, q_norm_w, k_norm_w, rel_bias, sinks, conv_w, conv_b,
           dt_bias, a_log, d_skip, ssm_norm_w, w_attn_proj, w_ssm_proj, w_out):
    depth = w_in.shape[0]
    for i in range(depth):
        x = _layer(x, c, w_ada[i], b_ada[i], norm_w[i], w_in[i], q_norm_w[i], k_norm_w[i],
                   rel_bias, sinks[i], conv_w[i], conv_b[i], dt_bias[i], a_log[i], d_skip[i],
                   ssm_norm_w[i], w_attn_proj[i], w_ssm_proj[i], w_out[i])
    return x
```

```python
import functools
import math

import numpy as np
import jax
import jax.numpy as jnp
from jax import lax
from jax.experimental import pallas as pl
from jax.experimental.pallas import tpu as pltpu

F32 = jnp.float32
BF16 = jnp.bfloat16

HEAD_DIM = 64
ATTN_HEADS = 16
ATTN_KV_HEADS = 4
GROUP = ATTN_HEADS // ATTN_KV_HEADS
BLOCK = 128
REL_BUCKETS = 32
REL_MAX_DIST = 128
SSM_HEAD_DIM = 64
SSM_GROUPS = 4
SSM_STATE = 128
CONV_WIDTH = 4
CHUNK = 128
EPS = 1e-6
NEG = -1e30
LOG2E = 1.4426950408889634
LANES = 128
VMEM_LIMIT = 56 * 1024 * 1024

ROWS_INPROJ = 512
ROWS_ATTN = 512
ROWS_SSD = 256


def _sigmoid(x):
    return 1.0 / (1.0 + jnp.exp(-x))


def _silu(x):
    return x * _sigmoid(x)


def _softplus(x):
    return jnp.maximum(x, 0.0) + jnp.log(1.0 + jnp.exp(-jnp.abs(x)))


def _dot(a, b):
    return jnp.dot(a, b, preferred_element_type=F32)


def _dot_nt(a, b):
    return lax.dot_general(a, b, (((1,), (1,)), ((), ())), preferred_element_type=F32)


def _resident(shape):
    nd = len(shape)
    return pl.BlockSpec(shape, lambda *_: (0,) * nd, pipeline_mode=pl.Buffered(1))


def _mod_kernel(c_ref, w_ref, b_ref, o_ref):
    s = _silu(c_ref[...])
    o_ref[...] = _dot(s.astype(BF16), w_ref[...].astype(BF16)) + b_ref[...]


def _mod_call(c8, w_ada, b_ada):
    d, n = w_ada.shape
    tn = 1024
    return pl.pallas_call(
        _mod_kernel,
        grid=(n // tn,),
        in_specs=[pl.BlockSpec((8, d), lambda j: (0, 0)),
                  pl.BlockSpec((d, tn), lambda j: (0, j)),
                  pl.BlockSpec((1, tn), lambda j: (0, j))],
        out_specs=pl.BlockSpec((8, tn), lambda j: (0, j)),
        out_shape=jax.ShapeDtypeStruct((8, n), F32),
        name="mod",
    )(c8, w_ada, b_ada)


def _bucket_table():
    qi = np.arange(BLOCK)[:, None]
    kj = np.arange(2 * BLOCK)[None, :]
    dist = qi + BLOCK - kj
    n = np.maximum(dist, 0)
    max_exact = REL_BUCKETS // 2
    nf = np.maximum(n, 1).astype(np.float32)
    large = max_exact + (np.log(nf / max_exact) / math.log(REL_MAX_DIST / max_exact)
                         * (REL_BUCKETS - max_exact)).astype(np.int32)
    large = np.minimum(large, REL_BUCKETS - 1)
    bucket = np.where(n < max_exact, n, large)
    valid = (dist >= 0) & (dist < BLOCK)
    return np.where(valid, bucket, -1).astype(np.int32)


def _bias_kernel(rb_ref, idx_ref, o_ref):
    t = pl.program_id(0)
    kv = t % ATTN_KV_HEADS
    idx = idx_ref[...]
    key_row = lax.broadcasted_iota(jnp.int32, idx.shape, 0)
    keep = jnp.logical_and(idx >= 0, jnp.logical_or(t < ATTN_KV_HEADS, key_row >= BLOCK))
    for s in range(2):
        for c in range(2):
            acc = jnp.zeros(idx.shape, F32)
            for b in range(REL_BUCKETS):
                acc = jnp.where(idx == b, rb_ref[b, kv * GROUP + 2 * c + s], acc)
            o_ref[0, s * 2 * BLOCK:(s + 1) * 2 * BLOCK, c * BLOCK:(c + 1) * BLOCK] = (
                jnp.where(keep, acc * LOG2E, NEG))


def _bias_call(rel_bias):
    idx_t = jnp.asarray(np.ascontiguousarray(_bucket_table().T))
    return pl.pallas_call(
        _bias_kernel,
        grid=(2 * ATTN_KV_HEADS,),
        in_specs=[pl.BlockSpec(memory_space=pltpu.SMEM),
                  pl.BlockSpec((2 * BLOCK, BLOCK), lambda t: (0, 0))],
        out_specs=pl.BlockSpec((1, 4 * BLOCK, 2 * BLOCK), lambda t: (t, 0, 0)),
        out_shape=jax.ShapeDtypeStruct((2 * ATTN_KV_HEADS, 4 * BLOCK, 2 * BLOCK), F32),
        name="bias",
    )(rel_bias, idx_t)


def _inproj_kernel(x_ref, shift_ref, scale_ref, nw_ref, w_ref, dtb_ref,
                   qkv_o, za_o, zm_o, xbc_o, dt_o, g_o):
    x = x_ref[...]
    ms = jnp.mean(x * x, axis=-1, keepdims=True)
    xn = x * lax.rsqrt(ms + EPS) * nw_ref[...]
    h = (xn * (1.0 + scale_ref[0]) + shift_ref[0]).astype(BF16)

    col = 0
    for o_ref, act in ((qkv_o, None), (za_o, _silu), (zm_o, _silu), (xbc_o, None),
                       (dt_o, lambda y: _softplus(y + dtb_ref[...])), (g_o, _sigmoid)):
        n = o_ref.shape[1]
        step = min(n, 512)
        for c0 in range(0, n, step):
            y = _dot(h, w_ref[:, col + c0:col + c0 + step])
            o_ref[:, c0:c0 + step] = (y if act is None else act(y)).astype(o_ref.dtype)
        col += n


def _inproj_call(x2, shift, scale, norm_w, w_all, dtb, widths, batch, seq):
    t, d = x2.shape
    ts = min(ROWS_INPROJ, seq)
    ns = seq // ts
    row = lambda b, i: (b * ns + i, 0)
    per_b = pl.BlockSpec((1, 1, d), lambda b, i: (b, 0, 0))
    dtypes = [BF16, BF16, BF16, BF16, F32, BF16]
    return pl.pallas_call(
        _inproj_kernel,
        grid=(batch, ns),
        in_specs=[pl.BlockSpec((ts, d), row), per_b, per_b, _resident((1, d)),
                  _resident(w_all.shape), _resident(dtb.shape)],
        out_specs=[pl.BlockSpec((ts, n), row) for n in widths],
        out_shape=[jax.ShapeDtypeStruct((t, n), dt) for n, dt in zip(widths, dtypes)],
        compiler_params=pltpu.CompilerParams(
            dimension_semantics=("parallel", "parallel"), vmem_limit_bytes=VMEM_LIMIT),
        name="inproj",
    )(x2, shift, scale, norm_w, w_all, dtb)


def _segment_rms(x, bd, w):
    ms = _dot((x * x).astype(BF16), bd)
    return x * lax.rsqrt(ms + EPS) * w


def _attn_kernel(q_ref, kvc_ref, kvp_ref, za_ref, ga_ref, bias_ref, sink_ref,
                 qw_ref, kw_ref, bd_ref, wap_ref, o_ref,
                 qn_s, klo_s, khi_s, vt_s, s_s, p_s, yt_s):
    rows = q_ref.shape[0]
    nblk = rows // BLOCK
    kvw = ATTN_KV_HEADS * HEAD_DIM
    first_tile = pl.program_id(1) == 0
    bd = bd_ref[...]
    qw = qw_ref[...] * (HEAD_DIM ** -0.5 * LOG2E)
    kw = kw_ref[...]
    low_half = lax.broadcasted_iota(jnp.int32, (1, LANES), 1) < HEAD_DIM

    for j in range(q_ref.shape[1] // LANES):
        sl = slice(j * LANES, (j + 1) * LANES)
        qn_s[:, sl] = _segment_rms(q_ref[:, sl].astype(F32), bd, qw).astype(BF16)
    for j in range(kvw // LANES):
        sl = slice(j * LANES, (j + 1) * LANES)
        even = slice(2 * j * LANES, (2 * j + 1) * LANES)
        odd = slice((2 * j + 1) * LANES, (2 * j + 2) * LANES)
        for dst, src in ((slice(0, BLOCK), kvp_ref), (slice(BLOCK, BLOCK + rows), kvc_ref)):
            kn = _segment_rms(src[:, sl].astype(F32), bd, kw)
            lo = jnp.where(low_half, kn, 0.0)
            hi = jnp.where(low_half, 0.0, kn)
            klo_s[dst, even] = lo.astype(BF16)
            khi_s[dst, odd] = hi.astype(BF16)
            khi_s[dst, even] = pltpu.roll(lo, HEAD_DIM, axis=1).astype(BF16)
            klo_s[dst, odd] = pltpu.roll(hi, HEAD_DIM, axis=1).astype(BF16)
    vt_s[:, 0:BLOCK] = kvp_ref[:, kvw:2 * kvw].astype(F32).T.astype(BF16)
    vt_s[:, BLOCK:] = kvc_ref[:, kvw:2 * kvw].astype(F32).T.astype(BF16)

    units = [(qb, h) for qb in range(nblk) for h in range(ATTN_KV_HEADS)]

    def scores(u):
        qb, h = units[u]
        r0 = qb * BLOCK
        hl = slice(h * LANES, (h + 1) * LANES)
        k2 = jnp.concatenate([klo_s[r0:r0 + 2 * BLOCK, hl], khi_s[r0:r0 + 2 * BLOCK, hl]], axis=0)
        q2 = jnp.concatenate([qn_s[r0:r0 + BLOCK, (2 * h + c) * LANES:(2 * h + c + 1) * LANES]
                              for c in range(2)], axis=0)
        entry = jnp.where(first_tile, ATTN_KV_HEADS, 0) + h if qb == 0 else h
        s_s[u % 2] = _dot_nt(k2, q2) + bias_ref[entry]

    def finish(u):
        qb, h = units[u]
        r0 = qb * BLOCK
        slot = u % 2
        inv = []
        for s in range(2):
            for c in range(2):
                blk = s_s[slot, s * 2 * BLOCK:(s + 1) * 2 * BLOCK, c * BLOCK:(c + 1) * BLOCK]
                sink = sink_ref[h, s:s + 1, c * BLOCK:(c + 1) * BLOCK] * LOG2E
                m = jnp.maximum(jnp.max(blk, axis=0, keepdims=True), sink)
                e = jnp.exp2(blk - m)
                den = jnp.sum(e, axis=0, keepdims=True) + jnp.exp2(sink - m)
                p_s[slot, :, (2 * s + c) * BLOCK:(2 * s + c + 1) * BLOCK] = e.astype(BF16)
                inv.append(1.0 / den)
        o_t = _dot(vt_s[h * HEAD_DIM:(h + 1) * HEAD_DIM, r0:r0 + 2 * BLOCK], p_s[slot])
        for s in range(2):
            for c in range(2):
                head = GROUP * h + 2 * c + s
                k = 2 * s + c
                yt_s[head * HEAD_DIM:(head + 1) * HEAD_DIM, r0:r0 + BLOCK] = (
                    o_t[:, k * BLOCK:(k + 1) * BLOCK] * inv[k])

    scores(0)
    for u in range(len(units)):
        if u + 1 < len(units):
            scores(u + 1)
        finish(u)

    y = (yt_s[...].T * za_ref[...].astype(F32)).astype(BF16)
    ya = _dot(y, wap_ref[...])
    o_ref[...] = (ya * ga_ref[...].astype(F32)).astype(o_ref.dtype)


def _attn_call(qkv, za, g, bias, sink_rows, qw, kw, bd, wap, batch, seq):
    t = qkv.shape[0]
    aw = ATTN_HEADS * HEAD_DIM
    kvw2 = 2 * ATTN_KV_HEADS * HEAD_DIM
    rows = min(ROWS_ATTN, seq)
    nt = seq // rows
    bpt = rows // BLOCK
    row = lambda b, i: (b * nt + i, 0)
    return pl.pallas_call(
        _attn_kernel,
        grid=(batch, nt),
        in_specs=[pl.BlockSpec((rows, aw), row),
                  pl.BlockSpec((rows, kvw2), lambda b, i: (b * nt + i, aw // kvw2)),
                  pl.BlockSpec((BLOCK, kvw2),
                               lambda b, i: (jnp.maximum((b * nt + i) * bpt - 1, 0), aw // kvw2)),
                  pl.BlockSpec((rows, aw), row),
                  pl.BlockSpec((rows, aw), row),
                  _resident(bias.shape), _resident(sink_rows.shape),
                  _resident(qw.shape), _resident(kw.shape),
                  _resident(bd.shape), _resident(wap.shape)],
        out_specs=pl.BlockSpec((rows, wap.shape[1]), row),
        out_shape=jax.ShapeDtypeStruct((t, wap.shape[1]), BF16),
        scratch_shapes=[pltpu.VMEM((rows, aw), BF16),
                        pltpu.VMEM((rows + BLOCK, ATTN_KV_HEADS * LANES), BF16),
                        pltpu.VMEM((rows + BLOCK, ATTN_KV_HEADS * LANES), BF16),
                        pltpu.VMEM((kvw2 // 2, rows + BLOCK), BF16),
                        pltpu.VMEM((2, 4 * BLOCK, 2 * BLOCK), F32),
                        pltpu.VMEM((2, 2 * BLOCK, 4 * BLOCK), BF16),
                        pltpu.VMEM((aw, rows), F32)],
        compiler_params=pltpu.CompilerParams(
            dimension_semantics=("parallel", "parallel"), vmem_limit_bytes=VMEM_LIMIT),
        name="attn",
    )(qkv, qkv, qkv, za, g, bias, sink_rows, qw, kw, bd, wap)


def _ssd_kernel(xbc_ref, dt_ref, zm_ref, gb_ref, ya_ref, x_ref, gate_ref,
                cw_ref, cb_ref, alog_ref, dsk_ref, nw_ref, tri_ref, wsp_ref, wout_ref,
                o_ref, xpad_s, h_s, yn_s):
    rows = xbc_ref.shape[0]
    nchunk = rows // CHUNK
    sw = zm_ref.shape[1]
    gw = sw // SSM_GROUPS
    pairs_per_group = gw // LANES
    halo = 8

    @pl.when(pl.program_id(1) == 0)
    def _():
        xpad_s[0:halo, :] = jnp.zeros((halo, xpad_s.shape[1]), F32)
        h_s[...] = jnp.zeros(h_s.shape, F32)

    xpad_s[halo:halo + rows, :] = xbc_ref[...].astype(F32)

    a_row = -jnp.exp(alog_ref[...])
    tri = tri_ref[...]
    li = lax.broadcasted_iota(jnp.int32, (CHUNK, CHUNK), 0)
    si = lax.broadcasted_iota(jnp.int32, (CHUNK, CHUNK), 1)
    causal = li >= si
    low_half = si < SSM_HEAD_DIM

    for c in range(nchunk):
        r0 = c * CHUNK
        xc = cb_ref[...] + cw_ref[0:1, :] * xpad_s[r0 + halo - 3:r0 + halo - 3 + CHUNK, :]
        for w in range(1, CONV_WIDTH):
            off = r0 + halo - (CONV_WIDTH - 1) + w
            xc = xc + cw_ref[w:w + 1, :] * xpad_s[off:off + CHUNK, :]
        xc = _silu(xc)
        bm = xc[:, sw:sw + SSM_GROUPS * SSM_STATE]
        cm = xc[:, sw + SSM_GROUPS * SSM_STATE:]

        dt = dt_ref[r0:r0 + CHUNK, :]
        dta = dt * a_row
        dta_hi = dta.astype(BF16)
        dta_lo = (dta - dta_hi.astype(F32)).astype(BF16)
        a_cum = _dot(tri, dta_hi) + _dot(tri, dta_lo)
        a_cum_t = a_cum.T
        dt_t = dt.T
        a_end_t = a_cum_t[:, CHUNK - 1:CHUNK]
        w_t = dt_t * jnp.exp(a_end_t - a_cum_t)
        e_end_t = jnp.exp(a_end_t)

        for g in range(SSM_GROUPS):
            bm_g = bm[:, g * SSM_STATE:(g + 1) * SSM_STATE]
            cm_g = cm[:, g * SSM_STATE:(g + 1) * SSM_STATE]
            cm_gb = cm_g.astype(BF16)
            cb = _dot_nt(cm_gb, bm_g.astype(BF16))
            bm_gt = bm_g.T
            gsl = slice(g * gw, (g + 1) * gw)
            hprev = h_s[g]
            y_off = _dot(cm_gb, hprev.astype(BF16))
            y_parts = []
            for p in range(pairs_per_group):
                pair = g * pairs_per_group + p
                lsl = slice(pair * LANES, (pair + 1) * LANES)
                xs_pair = xc[:, lsl]
                xs_lo = jnp.where(low_half, xs_pair, 0.0)
                xs_hi = jnp.where(low_half, 0.0, xs_pair)
                rhs = jnp.concatenate([xs_lo, xs_hi], axis=0).astype(BF16)
                m_parts, b_parts, e_cols, e_ends = [], [], [], []
                for side in range(2):
                    hh = 2 * pair + side
                    col_a = jnp.broadcast_to(a_cum[:, hh:hh + 1], (CHUNK, CHUNK))
                    row_a = a_cum_t[hh:hh + 1, :]
                    dec = jnp.exp(jnp.where(causal, col_a - row_a, NEG))
                    m_parts.append(cb * dec * dt_t[hh:hh + 1, :])
                    b_parts.append(bm_gt * w_t[hh:hh + 1, :])
                    e_cols.append(jnp.exp(col_a))
                    e_ends.append(e_end_t[hh:hh + 1, :])
                lhs = jnp.concatenate(
                    [jnp.concatenate(m_parts, axis=1), jnp.concatenate(b_parts, axis=1)],
                    axis=0).astype(BF16)
                res = _dot(lhs, rhs)
                e_col = jnp.where(low_half, e_cols[0], e_cols[1])
                y_pair = res[0:CHUNK] + e_col * y_off[:, p * LANES:(p + 1) * LANES]
                y_pair = y_pair + dsk_ref[:, lsl] * xs_pair
                y_parts.append(y_pair * zm_ref[r0:r0 + CHUNK, lsl].astype(F32))
                e_end = jnp.where(low_half, e_ends[0], e_ends[1])
                h_s[g, :, p * LANES:(p + 1) * LANES] = (
                    hprev[:, p * LANES:(p + 1) * LANES] * e_end + res[CHUNK:])
            yg = jnp.concatenate(y_parts, axis=1)
            ms = jnp.mean(yg * yg, axis=-1, keepdims=True)
            yn_s[r0:r0 + CHUNK, gsl] = (yg * lax.rsqrt(ms + EPS) * nw_ref[:, gsl]).astype(BF16)

    xpad_s[0:halo, :] = xpad_s[rows:rows + halo, :]

    yb = _dot(yn_s[...], wsp_ref[...])
    merged = ya_ref[...].astype(F32) + gb_ref[...].astype(F32) * yb
    o = _dot(merged.astype(BF16), wout_ref[...])
    o_ref[...] = x_ref[...] + gate_ref[0] * o


def _ssd_call(xbc, dt, zm, g, ya, x2, gate, cw, cb, alog, dsk, nw, tri, wsp, wout, batch, seq):
    t, d = x2.shape
    sw = zm.shape[1]
    rows = min(ROWS_SSD, seq)
    nt = seq // rows
    row = lambda b, i: (b * nt + i, 0)
    return pl.pallas_call(
        _ssd_kernel,
        grid=(batch, nt),
        in_specs=[pl.BlockSpec((rows, xbc.shape[1]), row),
                  pl.BlockSpec((rows, dt.shape[1]), row),
                  pl.BlockSpec((rows, sw), row),
                  pl.BlockSpec((rows, d), lambda b, i: (b * nt + i, 1)),
                  pl.BlockSpec((rows, d), row),
                  pl.BlockSpec((rows, d), row),
                  pl.BlockSpec((1, 1, d), lambda b, i: (b, 0, 0)),
                  _resident(cw.shape), _resident(cb.shape), _resident(alog.shape),
                  _resident(dsk.shape), _resident(nw.shape), _resident(tri.shape),
                  _resident(wsp.shape), _resident(wout.shape)],
        out_specs=pl.BlockSpec((rows, d), row),
        out_shape=jax.ShapeDtypeStruct((t, d), F32),
        scratch_shapes=[pltpu.VMEM((rows + 8, xbc.shape[1]), F32),
                        pltpu.VMEM((SSM_GROUPS, SSM_STATE, sw // SSM_GROUPS), F32),
                        pltpu.VMEM((rows, sw), BF16)],
        compiler_params=pltpu.CompilerParams(
            dimension_semantics=("arbitrary", "arbitrary"), vmem_limit_bytes=VMEM_LIMIT),
        name="ssd",
    )(xbc, dt, zm, g, ya, x2, gate, cw, cb, alog, dsk, nw, tri, wsp, wout)


def _layer(x, c, w_ada, b_ada, norm_w, w_in, q_norm_w, k_norm_w, rel_bias, sinks,
           conv_w, conv_b, dt_bias, a_log, d_skip, ssm_norm_w, w_attn_proj, w_ssm_proj, w_out):
    batch, seq, d = x.shape
    aw = ATTN_HEADS * HEAD_DIM
    kvw = ATTN_KV_HEADS * HEAD_DIM
    sw = w_ssm_proj.shape[0]
    ssm_heads = sw // SSM_HEAD_DIM
    xbc_w = sw + 2 * SSM_GROUPS * SSM_STATE
    assert seq % BLOCK == 0 and seq % CHUNK == 0 and ssm_heads <= LANES

    c8 = jnp.zeros((8, d), F32).at[:batch].set(c)
    mod = _mod_call(c8, w_ada, b_ada.reshape(1, -1))[:batch]
    shift, scale, gate = (mod[:, k * d:(k + 1) * d].reshape(batch, 1, d) for k in range(3))

    o = np.cumsum([0, aw, kvw, kvw, aw, sw, xbc_w, ssm_heads, d, d])
    w_all = jnp.concatenate(
        [w_in[:, o[0]:o[6]], jnp.pad(w_in[:, o[6]:o[7]], ((0, 0), (0, LANES - ssm_heads))), w_in[:, o[7]:o[9]]],
        axis=1).astype(BF16)
    widths = [aw + 2 * kvw, aw, sw, xbc_w, LANES, 2 * d]
    pad_heads = lambda v: jnp.pad(v.astype(F32), (0, LANES - ssm_heads)).reshape(1, LANES)

    x2 = x.reshape(batch * seq, d)
    qkv, za, zm, xbc, dt, g = _inproj_call(
        x2, shift, scale, norm_w.reshape(1, d), w_all, pad_heads(dt_bias), widths, batch, seq)

    bias = _bias_call(rel_bias.astype(F32))
    seg = np.arange(LANES) // HEAD_DIM
    bd = jnp.asarray((seg[:, None] == seg[None, :]).astype(np.float32) / HEAD_DIM, dtype=BF16)
    qw = jnp.tile(q_norm_w.astype(F32), LANES // HEAD_DIM).reshape(1, LANES)
    kw = jnp.tile(k_norm_w.astype(F32), LANES // HEAD_DIM).reshape(1, LANES)
    sink_rows = jnp.repeat(sinks.astype(F32).reshape(ATTN_KV_HEADS, 2, 2).transpose(0, 2, 1), BLOCK, axis=2)
    ya = _attn_call(qkv, za, g, bias, sink_rows, qw, kw, bd,
                    w_attn_proj.astype(BF16), batch, seq)

    tri = jnp.asarray(np.tril(np.ones((CHUNK, CHUNK), np.float32)), dtype=BF16)
    out = _ssd_call(xbc, dt, zm, g, ya, x2, gate,
                    conv_w.astype(F32), conv_b.reshape(1, -1).astype(F32), pad_heads(a_log),
                    jnp.repeat(d_skip.astype(F32), SSM_HEAD_DIM).reshape(1, sw),
                    ssm_norm_w.reshape(1, sw).astype(F32), tri,
                    w_ssm_proj.astype(BF16), w_out.astype(BF16), batch, seq)
    return out.reshape(batch, seq, d)


def kernel(x, c, w_ada, b_ada, norm_w, w_in, q_norm_w, k_norm_w, rel_bias, sinks, conv_w, conv_b,
           dt_bias, a_log, d_skip, ssm_norm_w, w_attn_proj, w_ssm_proj, w_out):
    depth = w_in.shape[0]
    for i in range(depth):
        x = _layer(x, c, w_ada[i], b_ada[i], norm_w[i], w_in[i], q_norm_w[i], k_norm_w[i],
                   rel_bias, sinks[i], conv_w[i], conv_b[i], dt_bias[i], a_log[i], d_skip[i],
                   ssm_norm_w[i], w_attn_proj[i], w_ssm_proj[i], w_out[i])
    return x
```

```python
import functools
import math

import numpy as np
import jax
import jax.numpy as jnp
from jax import lax
from jax.experimental import pallas as pl
from jax.experimental.pallas import tpu as pltpu

F32 = jnp.float32
BF16 = jnp.bfloat16

HEAD_DIM = 64
ATTN_HEADS = 16
ATTN_KV_HEADS = 4
GROUP = ATTN_HEADS // ATTN_KV_HEADS
BLOCK = 128
REL_BUCKETS = 32
REL_MAX_DIST = 128
SSM_HEAD_DIM = 64
SSM_GROUPS = 4
SSM_STATE = 128
CONV_WIDTH = 4
CHUNK = 128
EPS = 1e-6
NEG = -1e30
LOG2E = 1.4426950408889634
LANES = 128
VMEM_LIMIT = 56 * 1024 * 1024

ROWS_INPROJ = 512
ROWS_ATTN = 512
ROWS_SSD = 256


def _sigmoid(x):
    return 1.0 / (1.0 + jnp.exp(-x))


def _silu(x):
    return x * _sigmoid(x)


def _softplus(x):
    return jnp.maximum(x, 0.0) + jnp.log(1.0 + jnp.exp(-jnp.abs(x)))


def _dot(a, b):
    return jnp.dot(a, b, preferred_element_type=F32)


def _dot_nt(a, b):
    return lax.dot_general(a, b, (((1,), (1,)), ((), ())), preferred_element_type=F32)


def _resident(shape):
    nd = len(shape)
    return pl.BlockSpec(shape, lambda *_: (0,) * nd, pipeline_mode=pl.Buffered(1))


def _mod_kernel(c_ref, w_ref, b_ref, o_ref):
    s = _silu(c_ref[...])
    o_ref[...] = _dot(s.astype(BF16), w_ref[...].astype(BF16)) + b_ref[...]


def _mod_call(c8, w_ada, b_ada):
    d, n = w_ada.shape
    tn = 1024
    return pl.pallas_call(
        _mod_kernel,
        grid=(n // tn,),
        in_specs=[pl.BlockSpec((8, d), lambda j: (0, 0)),
                  pl.BlockSpec((d, tn), lambda j: (0, j)),
                  pl.BlockSpec((1, tn), lambda j: (0, j))],
        out_specs=pl.BlockSpec((8, tn), lambda j: (0, j)),
        out_shape=jax.ShapeDtypeStruct((8, n), F32),
        name="mod",
    )(c8, w_ada, b_ada)


def _bucket_table():
    qi = np.arange(BLOCK)[:, None]
    kj = np.arange(2 * BLOCK)[None, :]
    dist = qi + BLOCK - kj
    n = np.maximum(dist, 0)
    max_exact = REL_BUCKETS // 2
    nf = np.maximum(n, 1).astype(np.float32)
    large = max_exact + (np.log(nf / max_exact) / math.log(REL_MAX_DIST / max_exact)
                         * (REL_BUCKETS - max_exact)).astype(np.int32)
    large = np.minimum(large, REL_BUCKETS - 1)
    bucket = np.where(n < max_exact, n, large)
    valid = (dist >= 0) & (dist < BLOCK)
    return np.where(valid, bucket, -1).astype(np.int32)


def _bias_kernel(rb_ref, idx_ref, o_ref):
    t = pl.program_id(0)
    kv = t % ATTN_KV_HEADS
    idx = idx_ref[...]
    key_row = lax.broadcasted_iota(jnp.int32, idx.shape, 0)
    keep = jnp.logical_and(idx >= 0, jnp.logical_or(t < ATTN_KV_HEADS, key_row >= BLOCK))
    for s in range(2):
        for c in range(2):
            acc = jnp.zeros(idx.shape, F32)
            for b in range(REL_BUCKETS):
                acc = jnp.where(idx == b, rb_ref[b, kv * GROUP + 2 * c + s], acc)
            o_ref[0, s * 2 * BLOCK:(s + 1) * 2 * BLOCK, c * BLOCK:(c + 1) * BLOCK] = (
                jnp.where(keep, acc * LOG2E, NEG))


def _bias_call(rel_bias):
    idx_t = jnp.asarray(np.ascontiguousarray(_bucket_table().T))
    return pl.pallas_call(
        _bias_kernel,
        grid=(2 * ATTN_KV_HEADS,),
        in_specs=[pl.BlockSpec(memory_space=pltpu.SMEM),
                  pl.BlockSpec((2 * BLOCK, BLOCK), lambda t: (0, 0))],
        out_specs=pl.BlockSpec((1, 4 * BLOCK, 2 * BLOCK), lambda t: (t, 0, 0)),
        out_shape=jax.ShapeDtypeStruct((2 * ATTN_KV_HEADS, 4 * BLOCK, 2 * BLOCK), F32),
        name="bias",
    )(rel_bias, idx_t)


def _inproj_kernel(x_ref, shift_ref, scale_ref, nw_ref, w_ref, dtb_ref,
                   qkv_o, za_o, zm_o, xbc_o, dt_o, g_o):
    x = x_ref[...]
    ms = jnp.mean(x * x, axis=-1, keepdims=True)
    xn = x * lax.rsqrt(ms + EPS) * nw_ref[...]
    h = (xn * (1.0 + scale_ref[0]) + shift_ref[0]).astype(BF16)

    col = 0
    for o_ref, act in ((qkv_o, None), (za_o, _silu), (zm_o, _silu), (xbc_o, None),
                       (dt_o, lambda y: _softplus(y + dtb_ref[...])), (g_o, _sigmoid)):
        n = o_ref.shape[1]
        step = min(n, 512)
        for c0 in range(0, n, step):
            y = _dot(h, w_ref[:, col + c0:col + c0 + step])
            o_ref[:, c0:c0 + step] = (y if act is None else act(y)).astype(o_ref.dtype)
        col += n


def _inproj_call(x2, shift, scale, norm_w, w_all, dtb, widths, batch, seq):
    t, d = x2.shape
    ts = min(ROWS_INPROJ, seq)
    ns = seq // ts
    row = lambda b, i: (b * ns + i, 0)
    per_b = pl.BlockSpec((1, 1, d), lambda b, i: (b, 0, 0))
    dtypes = [BF16, BF16, BF16, BF16, F32, BF16]
    return pl.pallas_call(
        _inproj_kernel,
        grid=(batch, ns),
        in_specs=[pl.BlockSpec((ts, d), row), per_b, per_b, _resident((1, d)),
                  _resident(w_all.shape), _resident(dtb.shape)],
        out_specs=[pl.BlockSpec((ts, n), row) for n in widths],
        out_shape=[jax.ShapeDtypeStruct((t, n), dt) for n, dt in zip(widths, dtypes)],
        compiler_params=pltpu.CompilerParams(
            dimension_semantics=("parallel", "parallel"), vmem_limit_bytes=VMEM_LIMIT),
        name="inproj",
    )(x2, shift, scale, norm_w, w_all, dtb)


def _segment_rms(x, bd, w):
    ms = _dot((x * x).astype(BF16), bd)
    return x * lax.rsqrt(ms + EPS) * w


def _attn_kernel(q_ref, kvc_ref, kvp_ref, za_ref, ga_ref, bias_ref, sink_ref,
                 qw_ref, kw_ref, bd_ref, wap_ref, o_ref,
                 qn_s, klo_s, khi_s, vt_s, s_s, p_s, yt_s):
    rows = q_ref.shape[0]
    nblk = rows // BLOCK
    kvw = ATTN_KV_HEADS * HEAD_DIM
    first_tile = pl.program_id(1) == 0
    bd = bd_ref[...]
    qw = qw_ref[...] * (HEAD_DIM ** -0.5 * LOG2E)
    kw = kw_ref[...]
    low_half = lax.broadcasted_iota(jnp.int32, (1, LANES), 1) < HEAD_DIM

    for j in range(q_ref.shape[1] // LANES):
        sl = slice(j * LANES, (j + 1) * LANES)
        qn_s[:, sl] = _segment_rms(q_ref[:, sl].astype(F32), bd, qw).astype(BF16)
    for j in range(kvw // LANES):
        sl = slice(j * LANES, (j + 1) * LANES)
        even = slice(2 * j * LANES, (2 * j + 1) * LANES)
        odd = slice((2 * j + 1) * LANES, (2 * j + 2) * LANES)
        for dst, src in ((slice(0, BLOCK), kvp_ref), (slice(BLOCK, BLOCK + rows), kvc_ref)):
            kn = _segment_rms(src[:, sl].astype(F32), bd, kw)
            lo = jnp.where(low_half, kn, 0.0)
            hi = jnp.where(low_half, 0.0, kn)
            klo_s[dst, even] = lo.astype(BF16)
            khi_s[dst, odd] = hi.astype(BF16)
            khi_s[dst, even] = pltpu.roll(lo, HEAD_DIM, axis=1).astype(BF16)
            klo_s[dst, odd] = pltpu.roll(hi, HEAD_DIM, axis=1).astype(BF16)
    vt_s[:, 0:BLOCK] = kvp_ref[:, kvw:2 * kvw].astype(F32).T.astype(BF16)
    vt_s[:, BLOCK:] = kvc_ref[:, kvw:2 * kvw].astype(F32).T.astype(BF16)

    units = [(qb, h) for qb in range(nblk) for h in range(ATTN_KV_HEADS)]

    def scores(u):
        qb, h = units[u]
        r0 = qb * BLOCK
        hl = slice(h * LANES, (h + 1) * LANES)
        k2 = jnp.concatenate([klo_s[r0:r0 + 2 * BLOCK, hl], khi_s[r0:r0 + 2 * BLOCK, hl]], axis=0)
        q2 = jnp.concatenate([qn_s[r0:r0 + BLOCK, (2 * h + c) * LANES:(2 * h + c + 1) * LANES]
                              for c in range(2)], axis=0)
        entry = jnp.where(first_tile, ATTN_KV_HEADS, 0) + h if qb == 0 else h
        s_s[u % 2] = _dot_nt(k2, q2) + bias_ref[entry]

    def finish(u):
        qb, h = units[u]
        r0 = qb * BLOCK
        slot = u % 2
        inv = []
        for s in range(2):
            for c in range(2):
                blk = s_s[slot, s * 2 * BLOCK:(s + 1) * 2 * BLOCK, c * BLOCK:(c + 1) * BLOCK]
                sink = sink_ref[h, s:s + 1, c * BLOCK:(c + 1) * BLOCK] * LOG2E
                m = jnp.maximum(jnp.max(blk, axis=0, keepdims=True), sink)
                e = jnp.exp2(blk - m)
                den = jnp.sum(e, axis=0, keepdims=True) + jnp.exp2(sink - m)
                p_s[slot, :, (2 * s + c) * BLOCK:(2 * s + c + 1) * BLOCK] = e.astype(BF16)
                inv.append(1.0 / den)
        o_t = _dot(vt_s[h * HEAD_DIM:(h + 1) * HEAD_DIM, r0:r0 + 2 * BLOCK], p_s[slot])
        for s in range(2):
            for c in range(2):
                head = GROUP * h + 2 * c + s
                k = 2 * s + c
                yt_s[head * HEAD_DIM:(head + 1) * HEAD_DIM, r0:r0 + BLOCK] = (
                    o_t[:, k * BLOCK:(k + 1) * BLOCK] * inv[k])

    scores(0)
    for u in range(len(units)):
        if u + 1 < len(units):
            scores(u + 1)
        finish(u)

    y = (yt_s[...].T * za_ref[...].astype(F32)).astype(BF16)
    ya = _dot(y, wap_ref[...])
    o_ref[...] = (ya * ga_ref[...].astype(F32)).astype(o_ref.dtype)


def _attn_call(qkv, za, g, bias, sink_rows, qw, kw, bd, wap, batch, seq):
    t = qkv.shape[0]
    aw = ATTN_HEADS * HEAD_DIM
    kvw2 = 2 * ATTN_KV_HEADS * HEAD_DIM
    rows = min(ROWS_ATTN, seq)
    nt = seq // rows
    bpt = rows // BLOCK
    row = lambda b, i: (b * nt + i, 0)
    return pl.pallas_call(
        _attn_kernel,
        grid=(batch, nt),
        in_specs=[pl.BlockSpec((rows, aw), row),
                  pl.BlockSpec((rows, kvw2), lambda b, i: (b * nt + i, aw // kvw2)),
                  pl.BlockSpec((BLOCK, kvw2),
                               lambda b, i: (jnp.maximum((b * nt + i) * bpt - 1, 0), aw // kvw2)),
                  pl.BlockSpec((rows, aw), row),
                  pl.BlockSpec((rows, aw), row),
                  _resident(bias.shape), _resident(sink_rows.shape),
                  _resident(qw.shape), _resident(kw.shape),
                  _resident(bd.shape), _resident(wap.shape)],
        out_specs=pl.BlockSpec((rows, wap.shape[1]), row),
        out_shape=jax.ShapeDtypeStruct((t, wap.shape[1]), BF16),
        scratch_shapes=[pltpu.VMEM((rows, aw), BF16),
                        pltpu.VMEM((rows + BLOCK, ATTN_KV_HEADS * LANES), BF16),
                        pltpu.VMEM((rows + BLOCK, ATTN_KV_HEADS * LANES), BF16),
                        pltpu.VMEM((kvw2 // 2, rows + BLOCK), BF16),
                        pltpu.VMEM((2, 4 * BLOCK, 2 * BLOCK), F32),
                        pltpu.VMEM((2, 2 * BLOCK, 4 * BLOCK), BF16),
                        pltpu.VMEM((aw, rows), F32)],
        compiler_params=pltpu.CompilerParams(
            dimension_semantics=("parallel", "parallel"), vmem_limit_bytes=VMEM_LIMIT),
        name="attn",
    )(qkv, qkv, qkv, za, g, bias, sink_rows, qw, kw, bd, wap)


def _ssd_kernel(xbc_ref, dt_ref, zm_ref, gb_ref, ya_ref, x_ref, gate_ref,
                cw_ref, cb_ref, alog_ref, dsk_ref, nw_ref, tri_ref, wsp_ref, wout_ref,
                o_ref, xpad_s, h_s, yn_s):
    rows = xbc_ref.shape[0]
    nchunk = rows // CHUNK
    sw = zm_ref.shape[1]
    gw = sw // SSM_GROUPS
    pairs_per_group = gw // LANES
    n_xs = sw // LANES
    nblocks = xbc_ref.shape[1] // LANES
    halo = 8

    @pl.when(pl.program_id(1) == 0)
    def _():
        xpad_s[:, 0:halo, :] = jnp.zeros((nblocks, halo, LANES), F32)
        h_s[...] = jnp.zeros(h_s.shape, F32)

    for j in range(nblocks):
        xpad_s[j, halo:halo + rows, :] = xbc_ref[:, j * LANES:(j + 1) * LANES].astype(F32)

    def conv_silu(j, r0):
        jl = slice(j * LANES, (j + 1) * LANES)
        acc = cb_ref[:, jl]
        for w in range(CONV_WIDTH):
            off = r0 + halo - (CONV_WIDTH - 1) + w
            acc = acc + cw_ref[w:w + 1, jl] * xpad_s[j, off:off + CHUNK, :]
        return acc * (1.0 / (1.0 + jnp.exp2(acc * (-LOG2E))))

    a_row = -jnp.exp(alog_ref[...])
    tri = tri_ref[...]
    li = lax.broadcasted_iota(jnp.int32, (CHUNK, CHUNK), 0)
    si = lax.broadcasted_iota(jnp.int32, (CHUNK, CHUNK), 1)
    causal = li >= si
    low_half = si < SSM_HEAD_DIM

    for c in range(nchunk):
        r0 = c * CHUNK
        dt = dt_ref[r0:r0 + CHUNK, :]
        dta = dt * a_row
        dta_hi = dta.astype(BF16)
        dta_lo = (dta - dta_hi.astype(F32)).astype(BF16)
        a2 = (_dot(tri, dta_hi) + _dot(tri, dta_lo)) * LOG2E
        a2_t = a2.T
        dt_t = dt.T
        row2_t = a2_t - jnp.log2(dt_t)
        a_end2_t = a2_t[:, CHUNK - 1:CHUNK]
        w_t = dt_t * jnp.exp2(a_end2_t - a2_t)
        e_end_t = jnp.exp2(a_end2_t)

        for g in range(SSM_GROUPS):
            bm_g = conv_silu(n_xs + g, r0)
            cm_gb = conv_silu(n_xs + SSM_GROUPS + g, r0).astype(BF16)
            cb = _dot_nt(cm_gb, bm_g.astype(BF16))
            bm_gt = bm_g.T
            gsl = slice(g * gw, (g + 1) * gw)
            hprev = h_s[g]
            y_off = _dot(cm_gb, hprev.astype(BF16))
            y_parts = []
            for p in range(pairs_per_group):
                pair = g * pairs_per_group + p
                lsl = slice(pair * LANES, (pair + 1) * LANES)
                xs_pair = conv_silu(pair, r0)
                xs_lo = jnp.where(low_half, xs_pair, 0.0)
                xs_hi = jnp.where(low_half, 0.0, xs_pair)
                rhs = jnp.concatenate([xs_lo, xs_hi], axis=0).astype(BF16)
                m_parts, b_parts, cols, e_ends = [], [], [], []
                for side in range(2):
                    hh = 2 * pair + side
                    col2 = jnp.broadcast_to(a2[:, hh:hh + 1], (CHUNK, CHUNK))
                    dec_dt = jnp.exp2(jnp.where(causal, col2 - row2_t[hh:hh + 1, :], NEG))
                    m_parts.append(cb * dec_dt)
                    b_parts.append(bm_gt * w_t[hh:hh + 1, :])
                    cols.append(col2)
                    e_ends.append(e_end_t[hh:hh + 1, :])
                lhs = jnp.concatenate(
                    [jnp.concatenate(m_parts, axis=1), jnp.concatenate(b_parts, axis=1)],
                    axis=0).astype(BF16)
                res = _dot(lhs, rhs)
                e_col = jnp.exp2(jnp.where(low_half, cols[0], cols[1]))
                y_pair = res[0:CHUNK] + e_col * y_off[:, p * LANES:(p + 1) * LANES]
                y_pair = y_pair + dsk_ref[:, lsl] * xs_pair
                y_parts.append(y_pair * zm_ref[r0:r0 + CHUNK, lsl].astype(F32))
                e_end = jnp.where(low_half, e_ends[0], e_ends[1])
                h_s[g, :, p * LANES:(p + 1) * LANES] = (
                    hprev[:, p * LANES:(p + 1) * LANES] * e_end + res[CHUNK:])
            yg = jnp.concatenate(y_parts, axis=1)
            ms = jnp.mean(yg * yg, axis=-1, keepdims=True)
            yn_s[r0:r0 + CHUNK, gsl] = (yg * lax.rsqrt(ms + EPS) * nw_ref[:, gsl]).astype(BF16)

    xpad_s[:, 0:halo, :] = xpad_s[:, rows:rows + halo, :]

    yb = _dot(yn_s[...], wsp_ref[...])
    merged = ya_ref[...].astype(F32) + gb_ref[...].astype(F32) * yb
    o = _dot(merged.astype(BF16), wout_ref[...])
    o_ref[...] = x_ref[...] + gate_ref[0] * o


def _ssd_call(xbc, dt, zm, g, ya, x2, gate, cw, cb, alog, dsk, nw, tri, wsp, wout, batch, seq):
    t, d = x2.shape
    sw = zm.shape[1]
    rows = min(ROWS_SSD, seq)
    nt = seq // rows
    row = lambda b, i: (b * nt + i, 0)
    return pl.pallas_call(
        _ssd_kernel,
        grid=(batch, nt),
        in_specs=[pl.BlockSpec((rows, xbc.shape[1]), row),
                  pl.BlockSpec((rows, dt.shape[1]), row),
                  pl.BlockSpec((rows, sw), row),
                  pl.BlockSpec((rows, d), lambda b, i: (b * nt + i, 1)),
                  pl.BlockSpec((rows, d), row),
                  pl.BlockSpec((rows, d), row),
                  pl.BlockSpec((1, 1, d), lambda b, i: (b, 0, 0)),
                  _resident(cw.shape), _resident(cb.shape), _resident(alog.shape),
                  _resident(dsk.shape), _resident(nw.shape), _resident(tri.shape),
                  _resident(wsp.shape), _resident(wout.shape)],
        out_specs=pl.BlockSpec((rows, d), row),
        out_shape=jax.ShapeDtypeStruct((t, d), F32),
        scratch_shapes=[pltpu.VMEM((xbc.shape[1] // LANES, rows + 8, LANES), F32),
                        pltpu.VMEM((SSM_GROUPS, SSM_STATE, sw // SSM_GROUPS), F32),
                        pltpu.VMEM((rows, sw), BF16)],
        compiler_params=pltpu.CompilerParams(
            dimension_semantics=("arbitrary", "arbitrary"), vmem_limit_bytes=VMEM_LIMIT),
        name="ssd",
    )(xbc, dt, zm, g, ya, x2, gate, cw, cb, alog, dsk, nw, tri, wsp, wout)


def _layer(x, c, w_ada, b_ada, norm_w, w_in, q_norm_w, k_norm_w, rel_bias, sinks,
           conv_w, conv_b, dt_bias, a_log, d_skip, ssm_norm_w, w_attn_proj, w_ssm_proj, w_out):
    batch, seq, d = x.shape
    aw = ATTN_HEADS * HEAD_DIM
    kvw = ATTN_KV_HEADS * HEAD_DIM
    sw = w_ssm_proj.shape[0]
    ssm_heads = sw // SSM_HEAD_DIM
    xbc_w = sw + 2 * SSM_GROUPS * SSM_STATE
    assert seq % BLOCK == 0 and seq % CHUNK == 0 and ssm_heads <= LANES

    c8 = jnp.zeros((8, d), F32).at[:batch].set(c)
    mod = _mod_call(c8, w_ada, b_ada.reshape(1, -1))[:batch]
    shift, scale, gate = (mod[:, k * d:(k + 1) * d].reshape(batch, 1, d) for k in range(3))

    o = np.cumsum([0, aw, kvw, kvw, aw, sw, xbc_w, ssm_heads, d, d])
    w_all = jnp.concatenate(
        [w_in[:, o[0]:o[6]], jnp.pad(w_in[:, o[6]:o[7]], ((0, 0), (0, LANES - ssm_heads))), w_in[:, o[7]:o[9]]],
        axis=1).astype(BF16)
    widths = [aw + 2 * kvw, aw, sw, xbc_w, LANES, 2 * d]
    pad_heads = lambda v: jnp.pad(v.astype(F32), (0, LANES - ssm_heads)).reshape(1, LANES)

    x2 = x.reshape(batch * seq, d)
    qkv, za, zm, xbc, dt, g = _inproj_call(
        x2, shift, scale, norm_w.reshape(1, d), w_all, pad_heads(dt_bias), widths, batch, seq)

    bias = _bias_call(rel_bias.astype(F32))
    seg = np.arange(LANES) // HEAD_DIM
    bd = jnp.asarray((seg[:, None] == seg[None, :]).astype(np.float32) / HEAD_DIM, dtype=BF16)
    qw = jnp.tile(q_norm_w.astype(F32), LANES // HEAD_DIM).reshape(1, LANES)
    kw = jnp.tile(k_norm_w.astype(F32), LANES // HEAD_DIM).reshape(1, LANES)
    sink_rows = jnp.repeat(sinks.astype(F32).reshape(ATTN_KV_HEADS, 2, 2).transpose(0, 2, 1), BLOCK, axis=2)
    ya = _attn_call(qkv, za, g, bias, sink_rows, qw, kw, bd,
                    w_attn_proj.astype(BF16), batch, seq)

    tri = jnp.asarray(np.tril(np.ones((CHUNK, CHUNK), np.float32)), dtype=BF16)
    out = _ssd_call(xbc, dt, zm, g, ya, x2, gate,
                    conv_w.astype(F32), conv_b.reshape(1, -1).astype(F32), pad_heads(a_log),
                    jnp.repeat(d_skip.astype(F32), SSM_HEAD_DIM).reshape(1, sw),
                    ssm_norm_w.reshape(1, sw).astype(F32), tri,
                    w_ssm_proj.astype(BF16), w_out.astype(BF16), batch, seq)
    return out.reshape(batch, seq, d)


def kernel(x, c, w_ada, b_ada, norm_w, w_in, q_norm_w, k_norm_w, rel_bias, sinks, conv_w, conv_b,
           dt_bias, a_log, d_skip, ssm_norm_w, w_attn_proj, w_ssm_proj, w_out):
    depth = w_in.shape[0]
    for i in range(depth):
        x = _layer(x, c, w_ada[i], b_ada[i], norm_w[i], w_in[i], q_norm_w[i], k_norm_w[i],
                   rel_bias, sinks[i], conv_w[i], conv_b[i], dt_bias[i], a_log[i], d_skip[i],
                   ssm_norm_w[i], w_attn_proj[i], w_ssm_proj[i], w_out[i])
    return x
```

```python
import functools
import math

import numpy as np
import jax
import jax.numpy as jnp
from jax import lax
from jax.experimental import pallas as pl
from jax.experimental.pallas import tpu as pltpu

F32 = jnp.float32
BF16 = jnp.bfloat16

HEAD_DIM = 64
ATTN_HEADS = 16
ATTN_KV_HEADS = 4
GROUP = ATTN_HEADS // ATTN_KV_HEADS
BLOCK = 128
REL_BUCKETS = 32
REL_MAX_DIST = 128
SSM_HEAD_DIM = 64
SSM_GROUPS = 4
SSM_STATE = 128
CONV_WIDTH = 4
CHUNK = 128
EPS = 1e-6
NEG = -1e30
LOG2E = 1.4426950408889634
LANES = 128
VMEM_LIMIT = 56 * 1024 * 1024

ROWS_INPROJ = 512
ROWS_ATTN = 512
ROWS_SSD = 256


def _sigmoid(x):
    return 1.0 / (1.0 + jnp.exp(-x))


def _silu(x):
    return x * _sigmoid(x)


def _softplus(x):
    return jnp.maximum(x, 0.0) + jnp.log(1.0 + jnp.exp(-jnp.abs(x)))


def _dot(a, b):
    return jnp.dot(a, b, preferred_element_type=F32)


def _dot_nt(a, b):
    return lax.dot_general(a, b, (((1,), (1,)), ((), ())), preferred_element_type=F32)


def _resident(shape):
    nd = len(shape)
    return pl.BlockSpec(shape, lambda *_: (0,) * nd, pipeline_mode=pl.Buffered(1))


def _mod_kernel(c_ref, w_ref, b_ref, o_ref):
    s = _silu(c_ref[...])
    o_ref[...] = _dot(s.astype(BF16), w_ref[...].astype(BF16)) + b_ref[...]


def _mod_call(c8, w_ada, b_ada):
    d, n = w_ada.shape
    tn = 1024
    return pl.pallas_call(
        _mod_kernel,
        grid=(n // tn,),
        in_specs=[pl.BlockSpec((8, d), lambda j: (0, 0)),
                  pl.BlockSpec((d, tn), lambda j: (0, j)),
                  pl.BlockSpec((1, tn), lambda j: (0, j))],
        out_specs=pl.BlockSpec((8, tn), lambda j: (0, j)),
        out_shape=jax.ShapeDtypeStruct((8, n), F32),
        name="mod",
    )(c8, w_ada, b_ada)


def _bucket_table():
    qi = np.arange(BLOCK)[:, None]
    kj = np.arange(2 * BLOCK)[None, :]
    dist = qi + BLOCK - kj
    n = np.maximum(dist, 0)
    max_exact = REL_BUCKETS // 2
    nf = np.maximum(n, 1).astype(np.float32)
    large = max_exact + (np.log(nf / max_exact) / math.log(REL_MAX_DIST / max_exact)
                         * (REL_BUCKETS - max_exact)).astype(np.int32)
    large = np.minimum(large, REL_BUCKETS - 1)
    bucket = np.where(n < max_exact, n, large)
    valid = (dist >= 0) & (dist < BLOCK)
    return np.where(valid, bucket, -1).astype(np.int32)


def _bias_kernel(rb_ref, idx_ref, o_ref):
    t = pl.program_id(0)
    kv = t % ATTN_KV_HEADS
    idx = idx_ref[...]
    key_row = lax.broadcasted_iota(jnp.int32, idx.shape, 0)
    keep = jnp.logical_and(idx >= 0, jnp.logical_or(t < ATTN_KV_HEADS, key_row >= BLOCK))
    for s in range(2):
        for c in range(2):
            acc = jnp.zeros(idx.shape, F32)
            for b in range(REL_BUCKETS):
                acc = jnp.where(idx == b, rb_ref[b, kv * GROUP + 2 * c + s], acc)
            o_ref[0, s * 2 * BLOCK:(s + 1) * 2 * BLOCK, c * BLOCK:(c + 1) * BLOCK] = (
                jnp.where(keep, acc * LOG2E, NEG))


def _bias_call(rel_bias):
    idx_t = jnp.asarray(np.ascontiguousarray(_bucket_table().T))
    return pl.pallas_call(
        _bias_kernel,
        grid=(2 * ATTN_KV_HEADS,),
        in_specs=[pl.BlockSpec(memory_space=pltpu.SMEM),
                  pl.BlockSpec((2 * BLOCK, BLOCK), lambda t: (0, 0))],
        out_specs=pl.BlockSpec((1, 4 * BLOCK, 2 * BLOCK), lambda t: (t, 0, 0)),
        out_shape=jax.ShapeDtypeStruct((2 * ATTN_KV_HEADS, 4 * BLOCK, 2 * BLOCK), F32),
        name="bias",
    )(rel_bias, idx_t)


def _wprep_kernel(wt_ref, o_ref):
    o_ref[...] = wt_ref[...].T.astype(BF16)


def _wprep_call(w_t, dt_start, dt_width):
    n, k = w_t.shape
    tn = 512
    head_tiles, tail_tiles = dt_start // tn, (n - dt_start - dt_width) // tn
    assert head_tiles * tn == dt_start and tail_tiles * tn == n - dt_start - dt_width
    tail_start = dt_start + dt_width

    assert tail_start % 8 == 0

    def rows(i):
        start = jnp.where(i < head_tiles, i * tn, tail_start + (i - head_tiles) * tn)
        return (pl.multiple_of(start, 8), 0)

    return pl.pallas_call(
        _wprep_kernel,
        grid=(head_tiles + tail_tiles,),
        in_specs=[pl.BlockSpec((pl.Element(tn), pl.Element(k)), rows)],
        out_specs=pl.BlockSpec((k, tn), lambda i: (0, i)),
        out_shape=jax.ShapeDtypeStruct((k, n - dt_width), BF16),
        compiler_params=pltpu.CompilerParams(dimension_semantics=("parallel",)),
        name="wprep",
    )(w_t)


def _inproj_kernel(x_ref, shift_ref, scale_ref, nw_ref, w_ref, wdt_ref, dtb_ref,
                   qkv_o, za_o, zm_o, xbc_o, g_o, dt_o):
    x = x_ref[...]
    ms = jnp.mean(x * x, axis=-1, keepdims=True)
    xn = x * lax.rsqrt(ms + EPS) * nw_ref[...]
    h = (xn * (1.0 + scale_ref[0]) + shift_ref[0]).astype(BF16)

    col = 0
    for o_ref, act in ((qkv_o, None), (za_o, _silu), (zm_o, _silu), (xbc_o, None), (g_o, _sigmoid)):
        n = o_ref.shape[1]
        step = min(n, 512)
        for c0 in range(0, n, step):
            y = _dot(h, w_ref[:, col + c0:col + c0 + step])
            o_ref[:, c0:c0 + step] = (y if act is None else act(y)).astype(o_ref.dtype)
        col += n
    dt_o[...] = _softplus(_dot_nt(h, wdt_ref[...]) + dtb_ref[...])


def _inproj_call(x2, shift, scale, norm_w, w_all, wdt, dtb, widths, batch, seq):
    t, d = x2.shape
    ts = min(ROWS_INPROJ, seq)
    ns = seq // ts
    row = lambda b, i: (b * ns + i, 0)
    per_b = pl.BlockSpec((1, 1, d), lambda b, i: (b, 0, 0))
    widths = list(widths) + [wdt.shape[0]]
    dtypes = [BF16] * (len(widths) - 1) + [F32]
    return pl.pallas_call(
        _inproj_kernel,
        grid=(batch, ns),
        in_specs=[pl.BlockSpec((ts, d), row), per_b, per_b, _resident((1, d)),
                  _resident(w_all.shape), _resident(wdt.shape), _resident(dtb.shape)],
        out_specs=[pl.BlockSpec((ts, n), row) for n in widths],
        out_shape=[jax.ShapeDtypeStruct((t, n), dt) for n, dt in zip(widths, dtypes)],
        compiler_params=pltpu.CompilerParams(
            dimension_semantics=("parallel", "parallel"), vmem_limit_bytes=VMEM_LIMIT),
        name="inproj",
    )(x2, shift, scale, norm_w, w_all, wdt, dtb)


def _segment_rms(x, bd, w):
    ms = _dot((x * x).astype(BF16), bd)
    return x * lax.rsqrt(ms + EPS) * w


def _attn_kernel(q_ref, kvc_ref, kvp_ref, za_ref, ga_ref, bias_ref, sink_ref,
                 qw_ref, kw_ref, bd_ref, wap_ref, o_ref,
                 qn_s, klo_s, khi_s, vt_s, s_s, p_s, yt_s):
    rows = q_ref.shape[0]
    nblk = rows // BLOCK
    kvw = ATTN_KV_HEADS * HEAD_DIM
    first_tile = pl.program_id(1) == 0
    bd = bd_ref[...]
    qw = qw_ref[...] * (HEAD_DIM ** -0.5 * LOG2E)
    kw = kw_ref[...]
    low_half = lax.broadcasted_iota(jnp.int32, (1, LANES), 1) < HEAD_DIM

    for j in range(q_ref.shape[1] // LANES):
        sl = slice(j * LANES, (j + 1) * LANES)
        qn_s[:, sl] = _segment_rms(q_ref[:, sl].astype(F32), bd, qw).astype(BF16)
    for j in range(kvw // LANES):
        sl = slice(j * LANES, (j + 1) * LANES)
        even = slice(2 * j * LANES, (2 * j + 1) * LANES)
        odd = slice((2 * j + 1) * LANES, (2 * j + 2) * LANES)
        for dst, src in ((slice(0, BLOCK), kvp_ref), (slice(BLOCK, BLOCK + rows), kvc_ref)):
            kn = _segment_rms(src[:, sl].astype(F32), bd, kw)
            lo = jnp.where(low_half, kn, 0.0)
            hi = jnp.where(low_half, 0.0, kn)
            klo_s[dst, even] = lo.astype(BF16)
            khi_s[dst, odd] = hi.astype(BF16)
            khi_s[dst, even] = pltpu.roll(lo, HEAD_DIM, axis=1).astype(BF16)
            klo_s[dst, odd] = pltpu.roll(hi, HEAD_DIM, axis=1).astype(BF16)
    vt_s[:, 0:BLOCK] = kvp_ref[:, kvw:2 * kvw].astype(F32).T.astype(BF16)
    vt_s[:, BLOCK:] = kvc_ref[:, kvw:2 * kvw].astype(F32).T.astype(BF16)

    units = [(qb, h) for qb in range(nblk) for h in range(ATTN_KV_HEADS)]

    def scores(u):
        qb, h = units[u]
        r0 = qb * BLOCK
        hl = slice(h * LANES, (h + 1) * LANES)
        k2 = jnp.concatenate([klo_s[r0:r0 + 2 * BLOCK, hl], khi_s[r0:r0 + 2 * BLOCK, hl]], axis=0)
        q2 = jnp.concatenate([qn_s[r0:r0 + BLOCK, (2 * h + c) * LANES:(2 * h + c + 1) * LANES]
                              for c in range(2)], axis=0)
        entry = jnp.where(first_tile, ATTN_KV_HEADS, 0) + h if qb == 0 else h
        s_s[u % 2] = _dot_nt(k2, q2) + bias_ref[entry]

    def finish(u):
        qb, h = units[u]
        r0 = qb * BLOCK
        slot = u % 2
        inv = []
        for s in range(2):
            for c in range(2):
                blk = s_s[slot, s * 2 * BLOCK:(s + 1) * 2 * BLOCK, c * BLOCK:(c + 1) * BLOCK]
                sink = sink_ref[h, s:s + 1, c * BLOCK:(c + 1) * BLOCK] * LOG2E
                m = jnp.maximum(jnp.max(blk, axis=0, keepdims=True), sink)
                e = jnp.exp2(blk - m)
                den = jnp.sum(e, axis=0, keepdims=True) + jnp.exp2(sink - m)
                p_s[slot, :, (2 * s + c) * BLOCK:(2 * s + c + 1) * BLOCK] = e.astype(BF16)
                inv.append(1.0 / den)
        o_t = _dot(vt_s[h * HEAD_DIM:(h + 1) * HEAD_DIM, r0:r0 + 2 * BLOCK], p_s[slot])
        for s in range(2):
            for c in range(2):
                head = GROUP * h + 2 * c + s
                k = 2 * s + c
                yt_s[head * HEAD_DIM:(head + 1) * HEAD_DIM, r0:r0 + BLOCK] = (
                    o_t[:, k * BLOCK:(k + 1) * BLOCK] * inv[k])

    scores(0)
    for u in range(len(units)):
        if u + 1 < len(units):
            scores(u + 1)
        finish(u)

    y = (yt_s[...].T * za_ref[...].astype(F32)).astype(BF16)
    ya = _dot(y, wap_ref[...])
    o_ref[...] = (ya * ga_ref[...].astype(F32)).astype(o_ref.dtype)


def _attn_call(qkv, za, g, bias, sink_rows, qw, kw, bd, wap, batch, seq):
    t = qkv.shape[0]
    aw = ATTN_HEADS * HEAD_DIM
    kvw2 = 2 * ATTN_KV_HEADS * HEAD_DIM
    rows = min(ROWS_ATTN, seq)
    nt = seq // rows
    bpt = rows // BLOCK
    row = lambda b, i: (b * nt + i, 0)
    return pl.pallas_call(
        _attn_kernel,
        grid=(batch, nt),
        in_specs=[pl.BlockSpec((rows, aw), row),
                  pl.BlockSpec((rows, kvw2), lambda b, i: (b * nt + i, aw // kvw2)),
                  pl.BlockSpec((BLOCK, kvw2),
                               lambda b, i: (jnp.maximum((b * nt + i) * bpt - 1, 0), aw // kvw2)),
                  pl.BlockSpec((rows, aw), row),
                  pl.BlockSpec((rows, aw), row),
                  _resident(bias.shape), _resident(sink_rows.shape),
                  _resident(qw.shape), _resident(kw.shape),
                  _resident(bd.shape), _resident(wap.shape)],
        out_specs=pl.BlockSpec((rows, wap.shape[1]), row),
        out_shape=jax.ShapeDtypeStruct((t, wap.shape[1]), BF16),
        scratch_shapes=[pltpu.VMEM((rows, aw), BF16),
                        pltpu.VMEM((rows + BLOCK, ATTN_KV_HEADS * LANES), BF16),
                        pltpu.VMEM((rows + BLOCK, ATTN_KV_HEADS * LANES), BF16),
                        pltpu.VMEM((kvw2 // 2, rows + BLOCK), BF16),
                        pltpu.VMEM((2, 4 * BLOCK, 2 * BLOCK), F32),
                        pltpu.VMEM((2, 2 * BLOCK, 4 * BLOCK), BF16),
                        pltpu.VMEM((aw, rows), F32)],
        compiler_params=pltpu.CompilerParams(
            dimension_semantics=("parallel", "parallel"), vmem_limit_bytes=VMEM_LIMIT),
        name="attn",
    )(qkv, qkv, qkv, za, g, bias, sink_rows, qw, kw, bd, wap)


def _ssd_kernel(xbc_ref, dt_ref, zm_ref, gb_ref, ya_ref, x_ref, gate_ref,
                cw_ref, cb_ref, alog_ref, dsk_ref, nw_ref, tri_ref, wsp_ref, wout_ref,
                o_ref, xpad_s, h_s, yn_s):
    rows = xbc_ref.shape[0]
    nchunk = rows // CHUNK
    sw = zm_ref.shape[1]
    gw = sw // SSM_GROUPS
    pairs_per_group = gw // LANES
    n_xs = sw // LANES
    nblocks = xbc_ref.shape[1] // LANES
    halo = 8

    @pl.when(pl.program_id(1) == 0)
    def _():
        xpad_s[:, 0:halo, :] = jnp.zeros((nblocks, halo, LANES), F32)
        h_s[...] = jnp.zeros(h_s.shape, F32)

    for j in range(nblocks):
        xpad_s[j, halo:halo + rows, :] = xbc_ref[:, j * LANES:(j + 1) * LANES].astype(F32)

    def conv_silu(j, r0):
        jl = slice(j * LANES, (j + 1) * LANES)
        acc = cb_ref[:, jl]
        for w in range(CONV_WIDTH):
            off = r0 + halo - (CONV_WIDTH - 1) + w
            acc = acc + cw_ref[w:w + 1, jl] * xpad_s[j, off:off + CHUNK, :]
        return acc * (1.0 / (1.0 + jnp.exp2(acc * (-LOG2E))))

    a_row = -jnp.exp(alog_ref[...])
    tri = tri_ref[...]
    li = lax.broadcasted_iota(jnp.int32, (CHUNK, CHUNK), 0)
    si = lax.broadcasted_iota(jnp.int32, (CHUNK, CHUNK), 1)
    causal = li >= si
    low_half = si < SSM_HEAD_DIM

    for c in range(nchunk):
        r0 = c * CHUNK
        dt = dt_ref[r0:r0 + CHUNK, :]
        dta = dt * a_row
        dta_hi = dta.astype(BF16)
        dta_lo = (dta - dta_hi.astype(F32)).astype(BF16)
        a2 = (_dot(tri, dta_hi) + _dot(tri, dta_lo)) * LOG2E
        a2_t = a2.T
        dt_t = dt.T
        row2_t = a2_t - jnp.log2(dt_t)
        a_end2_t = a2_t[:, CHUNK - 1:CHUNK]
        w_t = dt_t * jnp.exp2(a_end2_t - a2_t)
        e_end_t = jnp.exp2(a_end2_t)

        for g in range(SSM_GROUPS):
            bm_g = conv_silu(n_xs + g, r0)
            cm_gb = conv_silu(n_xs + SSM_GROUPS + g, r0).astype(BF16)
            cb = _dot_nt(cm_gb, bm_g.astype(BF16))
            bm_gt = bm_g.T
            gsl = slice(g * gw, (g + 1) * gw)
            hprev = h_s[g]
            y_off = _dot(cm_gb, hprev.astype(BF16))
            y_parts = []
            for p in range(pairs_per_group):
                pair = g * pairs_per_group + p
                lsl = slice(pair * LANES, (pair + 1) * LANES)
                xs_pair = conv_silu(pair, r0)
                xs_lo = jnp.where(low_half, xs_pair, 0.0)
                xs_hi = jnp.where(low_half, 0.0, xs_pair)
                rhs = jnp.concatenate([xs_lo, xs_hi], axis=0).astype(BF16)
                m_parts, b_parts, cols, e_ends = [], [], [], []
                for side in range(2):
                    hh = 2 * pair + side
                    col2 = jnp.broadcast_to(a2[:, hh:hh + 1], (CHUNK, CHUNK))
                    dec_dt = jnp.exp2(jnp.where(causal, col2 - row2_t[hh:hh + 1, :], NEG))
                    m_parts.append(cb * dec_dt)
                    b_parts.append(bm_gt * w_t[hh:hh + 1, :])
                    cols.append(col2)
                    e_ends.append(e_end_t[hh:hh + 1, :])
                lhs = jnp.concatenate(
                    [jnp.concatenate(m_parts, axis=1), jnp.concatenate(b_parts, axis=1)],
                    axis=0).astype(BF16)
                res = _dot(lhs, rhs)
                e_col = jnp.exp2(jnp.where(low_half, cols[0], cols[1]))
                y_pair = res[0:CHUNK] + e_col * y_off[:, p * LANES:(p + 1) * LANES]
                y_pair = y_pair + dsk_ref[:, lsl] * xs_pair
                y_parts.append(y_pair * zm_ref[r0:r0 + CHUNK, lsl].astype(F32))
                e_end = jnp.where(low_half, e_ends[0], e_ends[1])
                h_s[g, :, p * LANES:(p + 1) * LANES] = (
                    hprev[:, p * LANES:(p + 1) * LANES] * e_end + res[CHUNK:])
            yg = jnp.concatenate(y_parts, axis=1)
            ms = jnp.mean(yg * yg, axis=-1, keepdims=True)
            yn_s[r0:r0 + CHUNK, gsl] = (yg * lax.rsqrt(ms + EPS) * nw_ref[:, gsl]).astype(BF16)

    xpad_s[:, 0:halo, :] = xpad_s[:, rows:rows + halo, :]

    yb = _dot(yn_s[...], wsp_ref[...])
    merged = ya_ref[...].astype(F32) + gb_ref[...].astype(F32) * yb
    o = _dot(merged.astype(BF16), wout_ref[...])
    o_ref[...] = x_ref[...] + gate_ref[0] * o


def _ssd_call(xbc, dt, zm, g, ya, x2, gate, cw, cb, alog, dsk, nw, tri, wsp, wout, batch, seq):
    t, d = x2.shape
    sw = zm.shape[1]
    rows = min(ROWS_SSD, seq)
    nt = seq // rows
    row = lambda b, i: (b * nt + i, 0)
    return pl.pallas_call(
        _ssd_kernel,
        grid=(batch, nt),
        in_specs=[pl.BlockSpec((rows, xbc.shape[1]), row),
                  pl.BlockSpec((rows, dt.shape[1]), row),
                  pl.BlockSpec((rows, sw), row),
                  pl.BlockSpec((rows, d), lambda b, i: (b * nt + i, 1)),
                  pl.BlockSpec((rows, d), row),
                  pl.BlockSpec((rows, d), row),
                  pl.BlockSpec((1, 1, d), lambda b, i: (b, 0, 0)),
                  _resident(cw.shape), _resident(cb.shape), _resident(alog.shape),
                  _resident(dsk.shape), _resident(nw.shape), _resident(tri.shape),
                  _resident(wsp.shape), _resident(wout.shape)],
        out_specs=pl.BlockSpec((rows, d), row),
        out_shape=jax.ShapeDtypeStruct((t, d), F32),
        scratch_shapes=[pltpu.VMEM((xbc.shape[1] // LANES, rows + 8, LANES), F32),
                        pltpu.VMEM((SSM_GROUPS, SSM_STATE, sw // SSM_GROUPS), F32),
                        pltpu.VMEM((rows, sw), BF16)],
        compiler_params=pltpu.CompilerParams(
            dimension_semantics=("arbitrary", "arbitrary"), vmem_limit_bytes=VMEM_LIMIT),
        name="ssd",
    )(xbc, dt, zm, g, ya, x2, gate, cw, cb, alog, dsk, nw, tri, wsp, wout)


def _layer(x, c, w_ada, b_ada, norm_w, w_in, q_norm_w, k_norm_w, rel_bias, sinks,
           conv_w, conv_b, dt_bias, a_log, d_skip, ssm_norm_w, w_attn_proj, w_ssm_proj, w_out):
    batch, seq, d = x.shape
    aw = ATTN_HEADS * HEAD_DIM
    kvw = ATTN_KV_HEADS * HEAD_DIM
    sw = w_ssm_proj.shape[0]
    ssm_heads = sw // SSM_HEAD_DIM
    xbc_w = sw + 2 * SSM_GROUPS * SSM_STATE
    assert seq % BLOCK == 0 and seq % CHUNK == 0 and ssm_heads <= LANES

    c8 = jnp.zeros((8, d), F32).at[:batch].set(c)
    mod = _mod_call(c8, w_ada, b_ada.reshape(1, -1))[:batch]
    shift, scale, gate = (mod[:, k * d:(k + 1) * d].reshape(batch, 1, d) for k in range(3))

    o = np.cumsum([0, aw, kvw, kvw, aw, sw, xbc_w, ssm_heads, d, d])
    w_t = w_in.T
    w_all = _wprep_call(w_t, int(o[6]), ssm_heads)
    wdt = jnp.pad(w_t[o[6]:o[7]], ((0, LANES - ssm_heads), (0, 0))).astype(BF16)
    widths = [aw + 2 * kvw, aw, sw, xbc_w, 2 * d]
    pad_heads = lambda v: jnp.pad(v.astype(F32), (0, LANES - ssm_heads)).reshape(1, LANES)

    x2 = x.reshape(batch * seq, d)
    qkv, za, zm, xbc, g, dt = _inproj_call(
        x2, shift, scale, norm_w.reshape(1, d), w_all, wdt, pad_heads(dt_bias), widths, batch, seq)

    bias = _bias_call(rel_bias.astype(F32))
    seg = np.arange(LANES) // HEAD_DIM
    bd = jnp.asarray((seg[:, None] == seg[None, :]).astype(np.float32) / HEAD_DIM, dtype=BF16)
    qw = jnp.tile(q_norm_w.astype(F32), LANES // HEAD_DIM).reshape(1, LANES)
    kw = jnp.tile(k_norm_w.astype(F32), LANES // HEAD_DIM).reshape(1, LANES)
    sink_rows = jnp.repeat(sinks.astype(F32).reshape(ATTN_KV_HEADS, 2, 2).transpose(0, 2, 1), BLOCK, axis=2)
    ya = _attn_call(qkv, za, g, bias, sink_rows, qw, kw, bd,
                    w_attn_proj.astype(BF16), batch, seq)

    tri = jnp.asarray(np.tril(np.ones((CHUNK, CHUNK), np.float32)), dtype=BF16)
    out = _ssd_call(xbc, dt, zm, g, ya, x2, gate,
                    conv_w.astype(F32), conv_b.reshape(1, -1).astype(F32), pad_heads(a_log),
                    jnp.repeat(d_skip.astype(F32), SSM_HEAD_DIM).reshape(1, sw),
                    ssm_norm_w.reshape(1, sw).astype(F32), tri,
                    w_ssm_proj.astype(BF16), w_out.astype(BF16), batch, seq)
    return out.reshape(batch, seq, d)


def kernel(x, c, w_ada, b_ada, norm_w, w_in, q_norm_w, k_norm_w, rel_bias, sinks, conv_w, conv_b,
           dt_bias, a_log, d_skip, ssm_norm_w, w_attn_proj, w_ssm_proj, w_out):
    depth = w_in.shape[0]
    for i in range(depth):
        x = _layer(x, c, w_ada[i], b_ada[i], norm_w[i], w_in[i], q_norm_w[i], k_norm_w[i],
                   rel_bias, sinks[i], conv_w[i], conv_b[i], dt_bias[i], a_log[i], d_skip[i],
                   ssm_norm_w[i], w_attn_proj[i], w_ssm_proj[i], w_out[i])
    return x
```

```python
import functools
import math

import numpy as np
import jax
import jax.numpy as jnp
from jax import lax
from jax.experimental import pallas as pl
from jax.experimental.pallas import tpu as pltpu

F32 = jnp.float32
BF16 = jnp.bfloat16

HEAD_DIM = 64
ATTN_HEADS = 16
ATTN_KV_HEADS = 4
GROUP = ATTN_HEADS // ATTN_KV_HEADS
BLOCK = 128
REL_BUCKETS = 32
REL_MAX_DIST = 128
SSM_HEAD_DIM = 64
SSM_GROUPS = 4
SSM_STATE = 128
CONV_WIDTH = 4
CHUNK = 128
EPS = 1e-6
NEG = -1e30
LOG2E = 1.4426950408889634
LANES = 128
VMEM_LIMIT = 56 * 1024 * 1024

ROWS_INPROJ = 512
ROWS_ATTN = 512
ROWS_SSD = 256


def _sigmoid(x):
    return 1.0 / (1.0 + jnp.exp(-x))


def _silu(x):
    return x * _sigmoid(x)


def _softplus(x):
    return jnp.maximum(x, 0.0) + jnp.log(1.0 + jnp.exp(-jnp.abs(x)))


def _dot(a, b):
    return jnp.dot(a, b, preferred_element_type=F32)


def _dot_nt(a, b):
    return lax.dot_general(a, b, (((1,), (1,)), ((), ())), preferred_element_type=F32)


def _resident(shape):
    nd = len(shape)
    return pl.BlockSpec(shape, lambda *_: (0,) * nd, pipeline_mode=pl.Buffered(1))


def _mod_kernel(c_ref, w_ref, b_ref, o_ref):
    s = _silu(c_ref[...])
    o_ref[...] = _dot(s.astype(BF16), w_ref[...].astype(BF16)) + b_ref[...]


def _mod_call(c8, w_ada, b_ada):
    d, n = w_ada.shape
    tn = 1024
    return pl.pallas_call(
        _mod_kernel,
        grid=(n // tn,),
        in_specs=[pl.BlockSpec((8, d), lambda j: (0, 0)),
                  pl.BlockSpec((d, tn), lambda j: (0, j)),
                  pl.BlockSpec((1, tn), lambda j: (0, j))],
        out_specs=pl.BlockSpec((8, tn), lambda j: (0, j)),
        out_shape=jax.ShapeDtypeStruct((8, n), F32),
        name="mod",
    )(c8, w_ada, b_ada)


def _bucket_table():
    qi = np.arange(BLOCK)[:, None]
    kj = np.arange(2 * BLOCK)[None, :]
    dist = qi + BLOCK - kj
    n = np.maximum(dist, 0)
    max_exact = REL_BUCKETS // 2
    nf = np.maximum(n, 1).astype(np.float32)
    large = max_exact + (np.log(nf / max_exact) / math.log(REL_MAX_DIST / max_exact)
                         * (REL_BUCKETS - max_exact)).astype(np.int32)
    large = np.minimum(large, REL_BUCKETS - 1)
    bucket = np.where(n < max_exact, n, large)
    valid = (dist >= 0) & (dist < BLOCK)
    return np.where(valid, bucket, -1).astype(np.int32)


def _bias_kernel(rb_ref, idx_ref, o_ref):
    t = pl.program_id(0)
    kv = t % ATTN_KV_HEADS
    idx = idx_ref[...]
    key_row = lax.broadcasted_iota(jnp.int32, idx.shape, 0)
    keep = jnp.logical_and(idx >= 0, jnp.logical_or(t < ATTN_KV_HEADS, key_row >= BLOCK))
    for s in range(2):
        for c in range(2):
            acc = jnp.zeros(idx.shape, F32)
            for b in range(REL_BUCKETS):
                acc = jnp.where(idx == b, rb_ref[b, kv * GROUP + 2 * c + s], acc)
            o_ref[0, s * 2 * BLOCK:(s + 1) * 2 * BLOCK, c * BLOCK:(c + 1) * BLOCK] = (
                jnp.where(keep, acc * LOG2E, NEG))


def _bias_call(rel_bias):
    idx_t = jnp.asarray(np.ascontiguousarray(_bucket_table().T))
    return pl.pallas_call(
        _bias_kernel,
        grid=(2 * ATTN_KV_HEADS,),
        in_specs=[pl.BlockSpec(memory_space=pltpu.SMEM),
                  pl.BlockSpec((2 * BLOCK, BLOCK), lambda t: (0, 0))],
        out_specs=pl.BlockSpec((1, 4 * BLOCK, 2 * BLOCK), lambda t: (t, 0, 0)),
        out_shape=jax.ShapeDtypeStruct((2 * ATTN_KV_HEADS, 4 * BLOCK, 2 * BLOCK), F32),
        name="bias",
    )(rel_bias, idx_t)


def _wprep_kernel(wt_ref, o_ref):
    o_ref[...] = wt_ref[...].T.astype(BF16)


def _wprep_call(w_t, dt_start, dt_width):
    n, k = w_t.shape
    tn = 512
    head_tiles, tail_tiles = dt_start // tn, (n - dt_start - dt_width) // tn
    assert head_tiles * tn == dt_start and tail_tiles * tn == n - dt_start - dt_width
    tail_start = dt_start + dt_width

    assert tail_start % 8 == 0

    def rows(i):
        start = jnp.where(i < head_tiles, i * tn, tail_start + (i - head_tiles) * tn)
        return (pl.multiple_of(start, 8), 0)

    return pl.pallas_call(
        _wprep_kernel,
        grid=(head_tiles + tail_tiles,),
        in_specs=[pl.BlockSpec((pl.Element(tn), pl.Element(k)), rows)],
        out_specs=pl.BlockSpec((k, tn), lambda i: (0, i)),
        out_shape=jax.ShapeDtypeStruct((k, n - dt_width), BF16),
        compiler_params=pltpu.CompilerParams(dimension_semantics=("parallel",)),
        name="wprep",
    )(w_t)


def _inproj_kernel(x_ref, shift_ref, scale_ref, nw_ref, w_ref, wdt_ref, dtb_ref, cw_ref, cb_ref,
                   qkv_o, za_o, zm_o, xbc_o, g_o, dt_o, xraw_s, carry_s):
    ts = x_ref.shape[0]
    halo = carry_s.shape[1]

    @pl.when(pl.program_id(1) == 0)
    def _():
        carry_s[...] = jnp.zeros(carry_s.shape, F32)

    x = x_ref[...]
    ms = jnp.mean(x * x, axis=-1, keepdims=True)
    xn = x * lax.rsqrt(ms + EPS) * nw_ref[...]
    h = (xn * (1.0 + scale_ref[0]) + shift_ref[0]).astype(BF16)

    def conv_silu_store(y, c0, slot):
        for jj in range(y.shape[1] // LANES):
            j = c0 // LANES + jj
            jl = slice(j * LANES, (j + 1) * LANES)
            yj = y[:, jj * LANES:(jj + 1) * LANES]
            xraw_s[slot, jj, 0:halo, :] = carry_s[j]
            xraw_s[slot, jj, halo:halo + ts, :] = yj
            carry_s[j] = yj[ts - halo:ts, :]
            acc = cb_ref[:, jl]
            for w in range(CONV_WIDTH):
                off = halo - (CONV_WIDTH - 1) + w
                acc = acc + cw_ref[w:w + 1, jl] * xraw_s[slot, jj, off:off + ts, :]
            xbc_o[:, jl] = (acc * (1.0 / (1.0 + jnp.exp2(acc * (-LOG2E))))).astype(BF16)

    step = xraw_s.shape[1] * LANES
    light, heavy, col = [], [], 0
    for o_ref, act in ((qkv_o, None), (za_o, _silu), (zm_o, _silu), (xbc_o, "conv"), (g_o, _sigmoid)):
        for c0 in range(0, o_ref.shape[1], step):
            (heavy if act == "conv" else light).append((o_ref, act, col + c0, c0))
        col += o_ref.shape[1]
    order = []
    while light or heavy:
        if light:
            order.append(light.pop(0))
        if heavy:
            order.append(heavy.pop(0))
    slot = 0
    for o_ref, act, wc, c0 in order:
        y = _dot(h, w_ref[:, wc:wc + step])
        if act == "conv":
            conv_silu_store(y, c0, slot)
            slot = 1 - slot
        else:
            o_ref[:, c0:c0 + step] = (y if act is None else act(y)).astype(o_ref.dtype)
    dt_o[...] = _softplus(_dot_nt(h, wdt_ref[...]) + dtb_ref[...])


def _inproj_call(x2, shift, scale, norm_w, w_all, wdt, dtb, cw, cb, widths, batch, seq):
    t, d = x2.shape
    ts = min(ROWS_INPROJ, seq)
    ns = seq // ts
    row = lambda b, i: (b * ns + i, 0)
    per_b = pl.BlockSpec((1, 1, d), lambda b, i: (b, 0, 0))
    widths = list(widths) + [wdt.shape[0]]
    dtypes = [BF16] * (len(widths) - 1) + [F32]
    piece_blocks = 4
    return pl.pallas_call(
        _inproj_kernel,
        grid=(batch, ns),
        in_specs=[pl.BlockSpec((ts, d), row), per_b, per_b, _resident((1, d)),
                  _resident(w_all.shape), _resident(wdt.shape), _resident(dtb.shape),
                  _resident(cw.shape), _resident(cb.shape)],
        out_specs=[pl.BlockSpec((ts, n), row) for n in widths],
        out_shape=[jax.ShapeDtypeStruct((t, n), dt) for n, dt in zip(widths, dtypes)],
        scratch_shapes=[pltpu.VMEM((2, piece_blocks, ts + 8, LANES), F32),
                        pltpu.VMEM((cw.shape[1] // LANES, 8, LANES), F32)],
        compiler_params=pltpu.CompilerParams(
            dimension_semantics=("arbitrary", "arbitrary"), vmem_limit_bytes=VMEM_LIMIT),
        name="inproj",
    )(x2, shift, scale, norm_w, w_all, wdt, dtb, cw, cb)


def _segment_rms(x, bd, w):
    ms = _dot((x * x).astype(BF16), bd)
    return x * lax.rsqrt(ms + EPS) * w


def _attn_kernel(q_ref, kvc_ref, kvp_ref, za_ref, ga_ref, bias_ref, sink_ref,
                 qw_ref, kw_ref, bd_ref, wap_ref, o_ref,
                 qn_s, klo_s, khi_s, vt_s, s_s, p_s, yt_s):
    rows = q_ref.shape[0]
    nblk = rows // BLOCK
    kvw = ATTN_KV_HEADS * HEAD_DIM
    first_tile = pl.program_id(1) == 0
    bd = bd_ref[...]
    qw = qw_ref[...] * (HEAD_DIM ** -0.5 * LOG2E)
    kw = kw_ref[...]
    low_half = lax.broadcasted_iota(jnp.int32, (1, LANES), 1) < HEAD_DIM

    for j in range(q_ref.shape[1] // LANES):
        sl = slice(j * LANES, (j + 1) * LANES)
        qn_s[:, sl] = _segment_rms(q_ref[:, sl].astype(F32), bd, qw).astype(BF16)
    for j in range(kvw // LANES):
        sl = slice(j * LANES, (j + 1) * LANES)
        even = slice(2 * j * LANES, (2 * j + 1) * LANES)
        odd = slice((2 * j + 1) * LANES, (2 * j + 2) * LANES)
        for dst, src in ((slice(0, BLOCK), kvp_ref), (slice(BLOCK, BLOCK + rows), kvc_ref)):
            kn = _segment_rms(src[:, sl].astype(F32), bd, kw)
            lo = jnp.where(low_half, kn, 0.0)
            hi = jnp.where(low_half, 0.0, kn)
            klo_s[dst, even] = lo.astype(BF16)
            khi_s[dst, odd] = hi.astype(BF16)
            khi_s[dst, even] = pltpu.roll(lo, HEAD_DIM, axis=1).astype(BF16)
            klo_s[dst, odd] = pltpu.roll(hi, HEAD_DIM, axis=1).astype(BF16)
    vt_s[:, 0:BLOCK] = kvp_ref[:, kvw:2 * kvw].astype(F32).T.astype(BF16)
    vt_s[:, BLOCK:] = kvc_ref[:, kvw:2 * kvw].astype(F32).T.astype(BF16)

    units = [(qb, h) for qb in range(nblk) for h in range(ATTN_KV_HEADS)]

    def scores(u):
        qb, h = units[u]
        r0 = qb * BLOCK
        hl = slice(h * LANES, (h + 1) * LANES)
        k2 = jnp.concatenate([klo_s[r0:r0 + 2 * BLOCK, hl], khi_s[r0:r0 + 2 * BLOCK, hl]], axis=0)
        q2 = jnp.concatenate([qn_s[r0:r0 + BLOCK, (2 * h + c) * LANES:(2 * h + c + 1) * LANES]
                              for c in range(2)], axis=0)
        entry = jnp.where(first_tile, ATTN_KV_HEADS, 0) + h if qb == 0 else h
        s_s[u % 2] = _dot_nt(k2, q2) + bias_ref[entry]

    def finish(u):
        qb, h = units[u]
        r0 = qb * BLOCK
        slot = u % 2
        inv = []
        for s in range(2):
            for c in range(2):
                blk = s_s[slot, s * 2 * BLOCK:(s + 1) * 2 * BLOCK, c * BLOCK:(c + 1) * BLOCK]
                sink = sink_ref[h, s:s + 1, c * BLOCK:(c + 1) * BLOCK] * LOG2E
                m = jnp.maximum(jnp.max(blk, axis=0, keepdims=True), sink)
                e = jnp.exp2(blk - m)
                den = jnp.sum(e, axis=0, keepdims=True) + jnp.exp2(sink - m)
                p_s[slot, :, (2 * s + c) * BLOCK:(2 * s + c + 1) * BLOCK] = e.astype(BF16)
                inv.append(1.0 / den)
        o_t = _dot(vt_s[h * HEAD_DIM:(h + 1) * HEAD_DIM, r0:r0 + 2 * BLOCK], p_s[slot])
        for s in range(2):
            for c in range(2):
                head = GROUP * h + 2 * c + s
                k = 2 * s + c
                yt_s[head * HEAD_DIM:(head + 1) * HEAD_DIM, r0:r0 + BLOCK] = (
                    o_t[:, k * BLOCK:(k + 1) * BLOCK] * inv[k])

    scores(0)
    for u in range(len(units)):
        if u + 1 < len(units):
            scores(u + 1)
        finish(u)

    y = (yt_s[...].T * za_ref[...].astype(F32)).astype(BF16)
    ya = _dot(y, wap_ref[...])
    o_ref[...] = (ya * ga_ref[...].astype(F32)).astype(o_ref.dtype)


def _attn_call(qkv, za, g, bias, sink_rows, qw, kw, bd, wap, batch, seq):
    t = qkv.shape[0]
    aw = ATTN_HEADS * HEAD_DIM
    kvw2 = 2 * ATTN_KV_HEADS * HEAD_DIM
    rows = min(ROWS_ATTN, seq)
    nt = seq // rows
    bpt = rows // BLOCK
    row = lambda b, i: (b * nt + i, 0)
    return pl.pallas_call(
        _attn_kernel,
        grid=(batch, nt),
        in_specs=[pl.BlockSpec((rows, aw), row),
                  pl.BlockSpec((rows, kvw2), lambda b, i: (b * nt + i, aw // kvw2)),
                  pl.BlockSpec((BLOCK, kvw2),
                               lambda b, i: (jnp.maximum((b * nt + i) * bpt - 1, 0), aw // kvw2)),
                  pl.BlockSpec((rows, aw), row),
                  pl.BlockSpec((rows, aw), row),
                  _resident(bias.shape), _resident(sink_rows.shape),
                  _resident(qw.shape), _resident(kw.shape),
                  _resident(bd.shape), _resident(wap.shape)],
        out_specs=pl.BlockSpec((rows, wap.shape[1]), row),
        out_shape=jax.ShapeDtypeStruct((t, wap.shape[1]), BF16),
        scratch_shapes=[pltpu.VMEM((rows, aw), BF16),
                        pltpu.VMEM((rows + BLOCK, ATTN_KV_HEADS * LANES), BF16),
                        pltpu.VMEM((rows + BLOCK, ATTN_KV_HEADS * LANES), BF16),
                        pltpu.VMEM((kvw2 // 2, rows + BLOCK), BF16),
                        pltpu.VMEM((2, 4 * BLOCK, 2 * BLOCK), F32),
                        pltpu.VMEM((2, 2 * BLOCK, 4 * BLOCK), BF16),
                        pltpu.VMEM((aw, rows), F32)],
        compiler_params=pltpu.CompilerParams(
            dimension_semantics=("parallel", "parallel"), vmem_limit_bytes=VMEM_LIMIT),
        name="attn",
    )(qkv, qkv, qkv, za, g, bias, sink_rows, qw, kw, bd, wap)


def _ssd_kernel(xbc_ref, dt_ref, zm_ref, gb_ref, ya_ref, x_ref, gate_ref,
                alog_ref, dsk_ref, nw_ref, tri_ref, wsp_ref, wout_ref,
                o_ref, h_s, yn_s):
    rows = xbc_ref.shape[0]
    nchunk = rows // CHUNK
    sw = zm_ref.shape[1]
    gw = sw // SSM_GROUPS
    pairs_per_group = gw // LANES
    n_xs = sw // LANES

    @pl.when(pl.program_id(1) == 0)
    def _():
        h_s[...] = jnp.zeros(h_s.shape, F32)

    def lane_block(j, r0):
        return xbc_ref[r0:r0 + CHUNK, j * LANES:(j + 1) * LANES]

    a_row = -jnp.exp(alog_ref[...])
    tri = tri_ref[...]
    li = lax.broadcasted_iota(jnp.int32, (CHUNK, CHUNK), 0)
    si = lax.broadcasted_iota(jnp.int32, (CHUNK, CHUNK), 1)
    causal = li >= si
    low_half = si < SSM_HEAD_DIM
    lo_mask = (lax.broadcasted_iota(jnp.int32, (1, LANES), 1) < SSM_HEAD_DIM).astype(BF16)
    hi_mask = 1 - lo_mask

    for c in range(nchunk):
        r0 = c * CHUNK
        dt = dt_ref[r0:r0 + CHUNK, :]
        dta = dt * a_row
        dta_hi = dta.astype(BF16)
        dta_lo = (dta - dta_hi.astype(F32)).astype(BF16)
        a2 = (_dot(tri, dta_hi) + _dot(tri, dta_lo)) * LOG2E
        a2_t = a2.T
        dt_t = dt.T
        row2_t = a2_t - jnp.log2(dt_t)
        a_end2_t = a2_t[:, CHUNK - 1:CHUNK]
        w_t = dt_t * jnp.exp2(a_end2_t - a2_t)
        e_end_t = jnp.exp2(a_end2_t)

        for g in range(SSM_GROUPS):
            bm_gb = lane_block(n_xs + g, r0)
            cm_gb = lane_block(n_xs + SSM_GROUPS + g, r0)
            cb = _dot_nt(cm_gb, bm_gb)
            bm_gt = bm_gb.astype(F32).T
            gsl = slice(g * gw, (g + 1) * gw)
            hprev = h_s[g]
            y_off = _dot(cm_gb, hprev.astype(BF16))
            y_parts = []
            for p in range(pairs_per_group):
                pair = g * pairs_per_group + p
                lsl = slice(pair * LANES, (pair + 1) * LANES)
                xs_b = lane_block(pair, r0)
                xs_pair = xs_b.astype(F32)
                rhs = jnp.concatenate([xs_b * lo_mask, xs_b * hi_mask], axis=0)
                m_parts, b_parts, cols, e_ends = [], [], [], []
                for side in range(2):
                    hh = 2 * pair + side
                    col2 = jnp.broadcast_to(a2[:, hh:hh + 1], (CHUNK, CHUNK))
                    dec_dt = jnp.exp2(jnp.where(causal, col2 - row2_t[hh:hh + 1, :], NEG))
                    m_parts.append(cb * dec_dt)
                    b_parts.append(bm_gt * w_t[hh:hh + 1, :])
                    cols.append(col2)
                    e_ends.append(e_end_t[hh:hh + 1, :])
                lhs = jnp.concatenate(
                    [jnp.concatenate(m_parts, axis=1), jnp.concatenate(b_parts, axis=1)],
                    axis=0).astype(BF16)
                res = _dot(lhs, rhs)
                e_col = jnp.exp2(jnp.where(low_half, cols[0], cols[1]))
                y_pair = res[0:CHUNK] + e_col * y_off[:, p * LANES:(p + 1) * LANES]
                y_pair = y_pair + dsk_ref[:, lsl] * xs_pair
                y_parts.append(y_pair * zm_ref[r0:r0 + CHUNK, lsl].astype(F32))
                e_end = jnp.where(low_half, e_ends[0], e_ends[1])
                h_s[g, :, p * LANES:(p + 1) * LANES] = (
                    hprev[:, p * LANES:(p + 1) * LANES] * e_end + res[CHUNK:])
            yg = jnp.concatenate(y_parts, axis=1)
            ms = jnp.mean(yg * yg, axis=-1, keepdims=True)
            yn_s[r0:r0 + CHUNK, gsl] = (yg * lax.rsqrt(ms + EPS) * nw_ref[:, gsl]).astype(BF16)

    yb =_dot(yn_s[...], wsp_ref[...])
    merged = ya_ref[...].astype(F32) + gb_ref[...].astype(F32) * yb
    o = _dot(merged.astype(BF16), wout_ref[...])
    o_ref[...] = x_ref[...] + gate_ref[0] * o


def _ssd_call(xbc, dt, zm, g, ya, x2, gate, alog, dsk, nw, tri, wsp, wout, batch, seq):
    t, d = x2.shape
    sw = zm.shape[1]
    rows = min(ROWS_SSD, seq)
    nt = seq // rows
    row = lambda b, i: (b * nt + i, 0)
    return pl.pallas_call(
        _ssd_kernel,
        grid=(batch, nt),
        in_specs=[pl.BlockSpec((rows, xbc.shape[1]), row),
                  pl.BlockSpec((rows, dt.shape[1]), row),
                  pl.BlockSpec((rows, sw), row),
                  pl.BlockSpec((rows, d), lambda b, i: (b * nt + i, 1)),
                  pl.BlockSpec((rows, d), row),
                  pl.BlockSpec((rows, d), row),
                  pl.BlockSpec((1, 1, d), lambda b, i: (b, 0, 0)),
                  _resident(alog.shape),
                  _resident(dsk.shape), _resident(nw.shape), _resident(tri.shape),
                  _resident(wsp.shape), _resident(wout.shape)],
        out_specs=pl.BlockSpec((rows, d), row),
        out_shape=jax.ShapeDtypeStruct((t, d), F32),
        scratch_shapes=[pltpu.VMEM((SSM_GROUPS, SSM_STATE, sw // SSM_GROUPS), F32),
                        pltpu.VMEM((rows, sw), BF16)],
        compiler_params=pltpu.CompilerParams(
            dimension_semantics=("arbitrary", "arbitrary"), vmem_limit_bytes=VMEM_LIMIT),
        name="ssd",
    )(xbc, dt, zm, g, ya, x2, gate, alog, dsk, nw, tri, wsp, wout)


def _layer(x, c, w_ada, b_ada, norm_w, w_in, q_norm_w, k_norm_w, rel_bias, sinks,
           conv_w, conv_b, dt_bias, a_log, d_skip, ssm_norm_w, w_attn_proj, w_ssm_proj, w_out):
    batch, seq, d = x.shape
    aw = ATTN_HEADS * HEAD_DIM
    kvw = ATTN_KV_HEADS * HEAD_DIM
    sw = w_ssm_proj.shape[0]
    ssm_heads = sw // SSM_HEAD_DIM
    xbc_w = sw + 2 * SSM_GROUPS * SSM_STATE
    assert seq % BLOCK == 0 and seq % CHUNK == 0 and ssm_heads <= LANES

    c8 = jnp.zeros((8, d), F32).at[:batch].set(c)
    mod = _mod_call(c8, w_ada, b_ada.reshape(1, -1))[:batch]
    shift, scale, gate = (mod[:, k * d:(k + 1) * d].reshape(batch, 1, d) for k in range(3))

    o = np.cumsum([0, aw, kvw, kvw, aw, sw, xbc_w, ssm_heads, d, d])
    w_t = w_in.T
    w_all = _wprep_call(w_t, int(o[6]), ssm_heads)
    wdt = jnp.pad(w_t[o[6]:o[7]], ((0, LANES - ssm_heads), (0, 0))).astype(BF16)
    widths = [aw + 2 * kvw, aw, sw, xbc_w, 2 * d]
    pad_heads = lambda v: jnp.pad(v.astype(F32), (0, LANES - ssm_heads)).reshape(1, LANES)

    x2 = x.reshape(batch * seq, d)
    qkv, za, zm, xbc, g, dt = _inproj_call(
        x2, shift, scale, norm_w.reshape(1, d), w_all, wdt, pad_heads(dt_bias),
        conv_w.astype(F32), conv_b.reshape(1, -1).astype(F32), widths, batch, seq)

    bias = _bias_call(rel_bias.astype(F32))
    seg = np.arange(LANES) // HEAD_DIM
    bd = jnp.asarray((seg[:, None] == seg[None, :]).astype(np.float32) / HEAD_DIM, dtype=BF16)
    qw = jnp.tile(q_norm_w.astype(F32), LANES // HEAD_DIM).reshape(1, LANES)
    kw = jnp.tile(k_norm_w.astype(F32), LANES // HEAD_DIM).reshape(1, LANES)
    sink_rows = jnp.repeat(sinks.astype(F32).reshape(ATTN_KV_HEADS, 2, 2).transpose(0, 2, 1), BLOCK, axis=2)
    ya = _attn_call(qkv, za, g, bias, sink_rows, qw, kw, bd,
                    w_attn_proj.astype(BF16), batch, seq)

    tri = jnp.asarray(np.tril(np.ones((CHUNK, CHUNK), np.float32)), dtype=BF16)
    out = _ssd_call(xbc, dt, zm, g, ya, x2, gate, pad_heads(a_log),
                    jnp.repeat(d_skip.astype(F32), SSM_HEAD_DIM).reshape(1, sw),
                    ssm_norm_w.reshape(1, sw).astype(F32), tri,
                    w_ssm_proj.astype(BF16), w_out.astype(BF16), batch, seq)
    return out.reshape(batch, seq, d)


def kernel(x, c, w_ada, b_ada, norm_w, w_in, q_norm_w, k_norm_w, rel_bias, sinks, conv_w, conv_b,
           dt_bias, a_log, d_skip, ssm_norm_w, w_attn_proj, w_ssm_proj, w_out):
    depth = w_in.shape[0]
    for i in range(depth):
        x = _layer(x, c, w_ada[i], b_ada[i], norm_w[i], w_in[i], q_norm_w[i], k_norm_w[i],
                   rel_bias, sinks[i], conv_w[i], conv_b[i], dt_bias[i], a_log[i], d_skip[i],
                   ssm_norm_w[i], w_attn_proj[i], w_ssm_proj[i], w_out[i])
    return x
```

```python
import functools
import math

import numpy as np
import jax
import jax.numpy as jnp
from jax import lax
from jax.experimental import pallas as pl
from jax.experimental.pallas import tpu as pltpu

F32 = jnp.float32
BF16 = jnp.bfloat16

HEAD_DIM = 64
ATTN_HEADS = 16
ATTN_KV_HEADS = 4
GROUP = ATTN_HEADS // ATTN_KV_HEADS
BLOCK = 128
REL_BUCKETS = 32
REL_MAX_DIST = 128
SSM_HEAD_DIM = 64
SSM_GROUPS = 4
SSM_STATE = 128
CONV_WIDTH = 4
CHUNK = 128
EPS = 1e-6
NEG = -1e30
LOG2E = 1.4426950408889634
LANES = 128
VMEM_LIMIT = 56 * 1024 * 1024

ROWS_INPROJ = 512
ROWS_ATTN = 512
ROWS_SSD = 256


def _sigmoid(x):
    return 1.0 / (1.0 + jnp.exp(-x))


def _silu(x):
    return x * _sigmoid(x)


def _softplus(x):
    return jnp.maximum(x, 0.0) + jnp.log(1.0 + jnp.exp(-jnp.abs(x)))


def _dot(a, b):
    return jnp.dot(a, b, preferred_element_type=F32)


def _dot_nt(a, b):
    return lax.dot_general(a, b, (((1,), (1,)), ((), ())), preferred_element_type=F32)


def _resident(shape):
    nd = len(shape)
    return pl.BlockSpec(shape, lambda *_: (0,) * nd, pipeline_mode=pl.Buffered(1))


def _mod_kernel(c_ref, w_ref, b_ref, o_ref):
    s = _silu(c_ref[...])
    o_ref[...] = _dot(s.astype(BF16), w_ref[...].astype(BF16)) + b_ref[...]


def _mod_call(c8, w_ada, b_ada):
    d, n = w_ada.shape
    tn = 1024
    return pl.pallas_call(
        _mod_kernel,
        grid=(n // tn,),
        in_specs=[pl.BlockSpec((8, d), lambda j: (0, 0)),
                  pl.BlockSpec((d, tn), lambda j: (0, j)),
                  pl.BlockSpec((1, tn), lambda j: (0, j))],
        out_specs=pl.BlockSpec((8, tn), lambda j: (0, j)),
        out_shape=jax.ShapeDtypeStruct((8, n), F32),
        name="mod",
    )(c8, w_ada, b_ada)


def _bucket_table():
    qi = np.arange(BLOCK)[:, None]
    kj = np.arange(2 * BLOCK)[None, :]
    dist = qi + BLOCK - kj
    n = np.maximum(dist, 0)
    max_exact = REL_BUCKETS // 2
    nf = np.maximum(n, 1).astype(np.float32)
    large = max_exact + (np.log(nf / max_exact) / math.log(REL_MAX_DIST / max_exact)
                         * (REL_BUCKETS - max_exact)).astype(np.int32)
    large = np.minimum(large, REL_BUCKETS - 1)
    bucket = np.where(n < max_exact, n, large)
    valid = (dist >= 0) & (dist < BLOCK)
    return np.where(valid, bucket, -1).astype(np.int32)


def _bias_kernel(rb_ref, idx_ref, o_ref):
    t = pl.program_id(0)
    kv = t % ATTN_KV_HEADS
    idx = idx_ref[...]
    key_row = lax.broadcasted_iota(jnp.int32, idx.shape, 0)
    keep = jnp.logical_and(idx >= 0, jnp.logical_or(t < ATTN_KV_HEADS, key_row >= BLOCK))
    for s in range(2):
        for c in range(2):
            acc = jnp.zeros(idx.shape, F32)
            for b in range(REL_BUCKETS):
                acc = jnp.where(idx == b, rb_ref[b, kv * GROUP + 2 * c + s], acc)
            o_ref[0, s * 2 * BLOCK:(s + 1) * 2 * BLOCK, c * BLOCK:(c + 1) * BLOCK] = (
                jnp.where(keep, acc * LOG2E, NEG))


def _bias_call(rel_bias):
    idx_t = jnp.asarray(np.ascontiguousarray(_bucket_table().T))
    return pl.pallas_call(
        _bias_kernel,
        grid=(2 * ATTN_KV_HEADS,),
        in_specs=[pl.BlockSpec(memory_space=pltpu.SMEM),
                  pl.BlockSpec((2 * BLOCK, BLOCK), lambda t: (0, 0))],
        out_specs=pl.BlockSpec((1, 4 * BLOCK, 2 * BLOCK), lambda t: (t, 0, 0)),
        out_shape=jax.ShapeDtypeStruct((2 * ATTN_KV_HEADS, 4 * BLOCK, 2 * BLOCK), F32),
        name="bias",
    )(rel_bias, idx_t)


def _wprep_kernel(wt_ref, o_ref):
    o_ref[...] = wt_ref[...].T.astype(BF16)


def _wprep_call(w_t, dt_start, dt_width):
    n, k = w_t.shape
    tn = 512
    head_tiles, tail_tiles = dt_start // tn, (n - dt_start - dt_width) // tn
    assert head_tiles * tn == dt_start and tail_tiles * tn == n - dt_start - dt_width
    tail_start = dt_start + dt_width

    assert tail_start % 8 == 0

    def rows(i):
        start = jnp.where(i < head_tiles, i * tn, tail_start + (i - head_tiles) * tn)
        return (pl.multiple_of(start, 8), 0)

    return pl.pallas_call(
        _wprep_kernel,
        grid=(head_tiles + tail_tiles,),
        in_specs=[pl.BlockSpec((pl.Element(tn), pl.Element(k)), rows)],
        out_specs=pl.BlockSpec((k, tn), lambda i: (0, i)),
        out_shape=jax.ShapeDtypeStruct((k, n - dt_width), BF16),
        compiler_params=pltpu.CompilerParams(dimension_semantics=("parallel",)),
        name="wprep",
    )(w_t)


def _inproj_kernel(x_ref, shift_ref, scale_ref, nw_ref, w_ref, wdt_ref, dtb_ref, cw_ref, cb_ref,
                   qkv_o, za_o, zm_o, xbc_o, g_o, dt_o, xraw_s, carry_s):
    ts = x_ref.shape[0]
    halo = carry_s.shape[1]

    @pl.when(pl.program_id(1) == 0)
    def _():
        carry_s[...] = jnp.zeros(carry_s.shape, F32)

    x = x_ref[...]
    ms = jnp.mean(x * x, axis=-1, keepdims=True)
    xn = x * lax.rsqrt(ms + EPS) * nw_ref[...]
    h = (xn * (1.0 + scale_ref[0]) + shift_ref[0]).astype(BF16)

    def conv_silu_store(y, c0, slot):
        for jj in range(y.shape[1] // LANES):
            j = c0 // LANES + jj
            jl = slice(j * LANES, (j + 1) * LANES)
            yj = y[:, jj * LANES:(jj + 1) * LANES]
            xraw_s[slot, jj, 0:halo, :] = carry_s[j]
            xraw_s[slot, jj, halo:halo + ts, :] = yj
            carry_s[j] = yj[ts - halo:ts, :]
            acc = cb_ref[:, jl]
            for w in range(CONV_WIDTH):
                off = halo - (CONV_WIDTH - 1) + w
                acc = acc + cw_ref[w:w + 1, jl] * xraw_s[slot, jj, off:off + ts, :]
            xbc_o[:, jl] = (acc * (1.0 / (1.0 + jnp.exp2(acc * (-LOG2E))))).astype(BF16)

    step = xraw_s.shape[1] * LANES
    light, heavy, col = [], [], 0
    for o_ref, act in ((qkv_o, None), (za_o, _silu), (zm_o, _silu), (xbc_o, "conv"), (g_o, _sigmoid)):
        for c0 in range(0, o_ref.shape[1], step):
            (heavy if act == "conv" else light).append((o_ref, act, col + c0, c0))
        col += o_ref.shape[1]
    order = []
    while light or heavy:
        if light:
            order.append(light.pop(0))
        if heavy:
            order.append(heavy.pop(0))
    slot = 0
    for o_ref, act, wc, c0 in order:
        y = _dot(h, w_ref[:, wc:wc + step])
        if act == "conv":
            conv_silu_store(y, c0, slot)
            slot = 1 - slot
        else:
            o_ref[:, c0:c0 + step] = (y if act is None else act(y)).astype(o_ref.dtype)
    dt_o[...] = _softplus(_dot_nt(h, wdt_ref[...]) + dtb_ref[...])


def _inproj_call(x2, shift, scale, norm_w, w_all, wdt, dtb, cw, cb, widths, batch, seq):
    t, d = x2.shape
    ts = min(ROWS_INPROJ, seq)
    ns = seq // ts
    row = lambda b, i: (b * ns + i, 0)
    per_b = pl.BlockSpec((1, 1, d), lambda b, i: (b, 0, 0))
    widths = list(widths) + [wdt.shape[0]]
    dtypes = [BF16] * (len(widths) - 1) + [F32]
    piece_blocks = 4
    return pl.pallas_call(
        _inproj_kernel,
        grid=(batch, ns),
        in_specs=[pl.BlockSpec((ts, d), row), per_b, per_b, _resident((1, d)),
                  _resident(w_all.shape), _resident(wdt.shape), _resident(dtb.shape),
                  _resident(cw.shape), _resident(cb.shape)],
        out_specs=[pl.BlockSpec((ts, n), row) for n in widths],
        out_shape=[jax.ShapeDtypeStruct((t, n), dt) for n, dt in zip(widths, dtypes)],
        scratch_shapes=[pltpu.VMEM((2, piece_blocks, ts + 8, LANES), F32),
                        pltpu.VMEM((cw.shape[1] // LANES, 8, LANES), F32)],
        compiler_params=pltpu.CompilerParams(
            dimension_semantics=("arbitrary", "arbitrary"), vmem_limit_bytes=VMEM_LIMIT),
        name="inproj",
    )(x2, shift, scale, norm_w, w_all, wdt, dtb, cw, cb)


def _segment_rms(x, bd, w):
    ms = _dot((x * x).astype(BF16), bd)
    return x * lax.rsqrt(ms + EPS) * w


def _attn_kernel(q_ref, kvc_ref, kvp_ref, za_ref, ga_ref, bias_ref, sink_ref,
                 qw_ref, kw_ref, bd_ref, wap_ref, o_ref,
                 qn_s, klo_s, khi_s, vt_s, s_s, p_s, yt_s):
    rows = q_ref.shape[0]
    nblk = rows // BLOCK
    kvw = ATTN_KV_HEADS * HEAD_DIM
    first_tile = pl.program_id(1) == 0
    bd = bd_ref[...]
    qw = qw_ref[...] * (HEAD_DIM ** -0.5 * LOG2E)
    kw = kw_ref[...]
    low_half = lax.broadcasted_iota(jnp.int32, (1, LANES), 1) < HEAD_DIM

    for j in range(q_ref.shape[1] // LANES):
        sl = slice(j * LANES, (j + 1) * LANES)
        qn_s[:, sl] = _segment_rms(q_ref[:, sl].astype(F32), bd, qw).astype(BF16)
    for j in range(kvw // LANES):
        sl = slice(j * LANES, (j + 1) * LANES)
        even = slice(2 * j * LANES, (2 * j + 1) * LANES)
        odd = slice((2 * j + 1) * LANES, (2 * j + 2) * LANES)
        for dst, src in ((slice(0, BLOCK), kvp_ref), (slice(BLOCK, BLOCK + rows), kvc_ref)):
            kn = _segment_rms(src[:, sl].astype(F32), bd, kw)
            lo = jnp.where(low_half, kn, 0.0)
            hi = jnp.where(low_half, 0.0, kn)
            klo_s[dst, even] = lo.astype(BF16)
            khi_s[dst, odd] = hi.astype(BF16)
            khi_s[dst, even] = pltpu.roll(lo, HEAD_DIM, axis=1).astype(BF16)
            klo_s[dst, odd] = pltpu.roll(hi, HEAD_DIM, axis=1).astype(BF16)
    vt_s[:, 0:BLOCK] = kvp_ref[:, kvw:2 * kvw].astype(F32).T.astype(BF16)
    vt_s[:, BLOCK:] = kvc_ref[:, kvw:2 * kvw].astype(F32).T.astype(BF16)

    units = [(qb, h) for qb in range(nblk) for h in range(ATTN_KV_HEADS)]

    def scores(u):
        qb, h = units[u]
        r0 = qb * BLOCK
        hl = slice(h * LANES, (h + 1) * LANES)
        k2 = jnp.concatenate([klo_s[r0:r0 + 2 * BLOCK, hl], khi_s[r0:r0 + 2 * BLOCK, hl]], axis=0)
        q2 = jnp.concatenate([qn_s[r0:r0 + BLOCK, (2 * h + c) * LANES:(2 * h + c + 1) * LANES]
                              for c in range(2)], axis=0)
        entry = jnp.where(first_tile, ATTN_KV_HEADS, 0) + h if qb == 0 else h
        s_s[u % 2] = _dot_nt(k2, q2) + bias_ref[entry]

    def finish(u):
        qb, h = units[u]
        r0 = qb * BLOCK
        slot = u % 2
        inv = []
        for s in range(2):
            for c in range(2):
                blk = s_s[slot, s * 2 * BLOCK:(s + 1) * 2 * BLOCK, c * BLOCK:(c + 1) * BLOCK]
                sink = sink_ref[h, s:s + 1, c * BLOCK:(c + 1) * BLOCK] * LOG2E
                m = jnp.maximum(jnp.max(blk, axis=0, keepdims=True), sink)
                e = jnp.exp2(blk - m)
                den = jnp.sum(e, axis=0, keepdims=True) + jnp.exp2(sink - m)
                p_s[slot, :, (2 * s + c) * BLOCK:(2 * s + c + 1) * BLOCK] = e.astype(BF16)
                inv.append(1.0 / den)
        o_t = _dot(vt_s[h * HEAD_DIM:(h + 1) * HEAD_DIM, r0:r0 + 2 * BLOCK], p_s[slot])
        for s in range(2):
            for c in range(2):
                head = GROUP * h + 2 * c + s
                k = 2 * s + c
                yt_s[head * HEAD_DIM:(head + 1) * HEAD_DIM, r0:r0 + BLOCK] = (
                    o_t[:, k * BLOCK:(k + 1) * BLOCK] * inv[k])

    scores(0)
    for u in range(len(units)):
        if u + 1 < len(units):
            scores(u + 1)
        finish(u)

    y = (yt_s[...].T * za_ref[...].astype(F32)).astype(BF16)
    ya = _dot(y, wap_ref[...])
    o_ref[...] = (ya * ga_ref[...].astype(F32)).astype(o_ref.dtype)


def _attn_call(qkv, za, g, bias, sink_rows, qw, kw, bd, wap, batch, seq):
    t = qkv.shape[0]
    aw = ATTN_HEADS * HEAD_DIM
    kvw2 = 2 * ATTN_KV_HEADS * HEAD_DIM
    rows = min(ROWS_ATTN, seq)
    nt = seq // rows
    bpt = rows // BLOCK
    row = lambda b, i: (b * nt + i, 0)
    return pl.pallas_call(
        _attn_kernel,
        grid=(batch, nt),
        in_specs=[pl.BlockSpec((rows, aw), row),
                  pl.BlockSpec((rows, kvw2), lambda b, i: (b * nt + i, aw // kvw2)),
                  pl.BlockSpec((BLOCK, kvw2),
                               lambda b, i: (jnp.maximum((b * nt + i) * bpt - 1, 0), aw // kvw2)),
                  pl.BlockSpec((rows, aw), row),
                  pl.BlockSpec((rows, aw), row),
                  _resident(bias.shape), _resident(sink_rows.shape),
                  _resident(qw.shape), _resident(kw.shape),
                  _resident(bd.shape), _resident(wap.shape)],
        out_specs=pl.BlockSpec((rows, wap.shape[1]), row),
        out_shape=jax.ShapeDtypeStruct((t, wap.shape[1]), BF16),
        scratch_shapes=[pltpu.VMEM((rows, aw), BF16),
                        pltpu.VMEM((rows + BLOCK, ATTN_KV_HEADS * LANES), BF16),
                        pltpu.VMEM((rows + BLOCK, ATTN_KV_HEADS * LANES), BF16),
                        pltpu.VMEM((kvw2 // 2, rows + BLOCK), BF16),
                        pltpu.VMEM((2, 4 * BLOCK, 2 * BLOCK), F32),
                        pltpu.VMEM((2, 2 * BLOCK, 4 * BLOCK), BF16),
                        pltpu.VMEM((aw, rows), F32)],
        compiler_params=pltpu.CompilerParams(
            dimension_semantics=("parallel", "parallel"), vmem_limit_bytes=VMEM_LIMIT),
        name="attn",
    )(qkv, qkv, qkv, za, g, bias, sink_rows, qw, kw, bd, wap)


def _ssd_kernel(xbc_ref, dt_ref, zm_ref, gb_ref, ya_ref, x_ref, gate_ref,
                alog_ref, dsk_ref, nw_ref, tri_ref, wsp_ref, wout_ref,
                o_ref, h_s, yn_s, yprev_s, merged_s):
    rows = xbc_ref.shape[0]
    nchunk = rows // CHUNK
    sw = zm_ref.shape[1]
    gw = sw // SSM_GROUPS
    pairs_per_group = gw // LANES
    n_xs = sw // LANES

    @pl.when(pl.program_id(1) == 0)
    def _():
        h_s[...] = jnp.zeros(h_s.shape, F32)
        yprev_s[...] = jnp.zeros(yprev_s.shape, BF16)

    d_out = o_ref.shape[1]
    piece_w = 2 * LANES

    def merge_piece(c0):
        cs = slice(c0, c0 + piece_w)
        yb = _dot(yprev_s[...], wsp_ref[:, cs])
        merged = ya_ref[:, cs].astype(F32) + gb_ref[:, cs].astype(F32) * yb
        merged_s[:, cs] = merged.astype(BF16)

    def out_piece(c0):
        cs = slice(c0, c0 + piece_w)
        o = _dot(merged_s[...], wout_ref[:, cs])
        o_ref[:, cs] = x_ref[:, cs] + gate_ref[0][:, cs] * o

    finish = ([functools.partial(merge_piece, c0) for c0 in range(0, d_out, piece_w)]
              + [functools.partial(out_piece, c0) for c0 in range(0, d_out, piece_w)])

    def lane_block(j, r0):
        return xbc_ref[r0:r0 + CHUNK, j * LANES:(j + 1) * LANES]

    a_row = -jnp.exp(alog_ref[...])
    tri = tri_ref[...]
    li = lax.broadcasted_iota(jnp.int32, (CHUNK, CHUNK), 0)
    si = lax.broadcasted_iota(jnp.int32, (CHUNK, CHUNK), 1)
    causal = li >= si
    low_half = si < SSM_HEAD_DIM
    lo_mask = (lax.broadcasted_iota(jnp.int32, (1, LANES), 1) < SSM_HEAD_DIM).astype(BF16)
    hi_mask = 1 - lo_mask

    for c in range(nchunk):
        r0 = c * CHUNK
        dt = dt_ref[r0:r0 + CHUNK, :]
        dta = dt * a_row
        dta_hi = dta.astype(BF16)
        dta_lo = (dta - dta_hi.astype(F32)).astype(BF16)
        a2 = (_dot(tri, dta_hi) + _dot(tri, dta_lo)) * LOG2E
        a2_t = a2.T
        dt_t = dt.T
        row2_t = a2_t - jnp.log2(dt_t)
        a_end2_t = a2_t[:, CHUNK - 1:CHUNK]
        w_t = dt_t * jnp.exp2(a_end2_t - a2_t)
        e_end_t = jnp.exp2(a_end2_t)

        for g in range(SSM_GROUPS):
            if finish:
                finish.pop(0)()
            bm_gb = lane_block(n_xs + g, r0)
            cm_gb = lane_block(n_xs + SSM_GROUPS + g, r0)
            cb = _dot_nt(cm_gb, bm_gb)
            bm_gt = bm_gb.astype(F32).T
            gsl = slice(g * gw, (g + 1) * gw)
            hprev = h_s[g]
            y_off = _dot(cm_gb, hprev.astype(BF16))
            y_parts = []
            for p in range(pairs_per_group):
                pair = g * pairs_per_group + p
                lsl = slice(pair * LANES, (pair + 1) * LANES)
                xs_b = lane_block(pair, r0)
                xs_pair = xs_b.astype(F32)
                rhs = jnp.concatenate([xs_b * lo_mask, xs_b * hi_mask], axis=0)
                m_parts, b_parts, cols, e_ends = [], [], [], []
                for side in range(2):
                    hh = 2 * pair + side
                    col2 = jnp.broadcast_to(a2[:, hh:hh + 1], (CHUNK, CHUNK))
                    dec_dt = jnp.exp2(jnp.where(causal, col2 - row2_t[hh:hh + 1, :], NEG))
                    m_parts.append(cb * dec_dt)
                    b_parts.append(bm_gt * w_t[hh:hh + 1, :])
                    cols.append(col2)
                    e_ends.append(e_end_t[hh:hh + 1, :])
                lhs = jnp.concatenate(
                    [jnp.concatenate(m_parts, axis=1), jnp.concatenate(b_parts, axis=1)],
                    axis=0).astype(BF16)
                res = _dot(lhs, rhs)
                e_col = jnp.exp2(jnp.where(low_half, cols[0], cols[1]))
                y_pair = res[0:CHUNK] + e_col * y_off[:, p * LANES:(p + 1) * LANES]
                y_pair = y_pair + dsk_ref[:, lsl] * xs_pair
                y_parts.append(y_pair * zm_ref[r0:r0 + CHUNK, lsl].astype(F32))
                e_end = jnp.where(low_half, e_ends[0], e_ends[1])
                h_s[g, :, p * LANES:(p + 1) * LANES] = (
                    hprev[:, p * LANES:(p + 1) * LANES] * e_end + res[CHUNK:])
            yg = jnp.concatenate(y_parts, axis=1)
            ms = jnp.mean(yg * yg, axis=-1, keepdims=True)
            yn_s[r0:r0 + CHUNK, gsl] = (yg * lax.rsqrt(ms + EPS) * nw_ref[:, gsl]).astype(BF16)

    while finish:
        finish.pop(0)()
    yprev_s[...] = yn_s[...]


def _ssd_call(xbc, dt, zm, g, ya, x2, gate, alog, dsk, nw, tri, wsp, wout, batch, seq):
    t, d = x2.shape
    sw = zm.shape[1]
    rows = min(ROWS_SSD, seq)
    nt = seq // rows
    scan = lambda b, i: (b * nt + jnp.minimum(i, nt - 1), 0)
    done = lambda b, i: (b * nt + jnp.maximum(i - 1, 0), 0)
    return pl.pallas_call(
        _ssd_kernel,
        grid=(batch, nt + 1),
        in_specs=[pl.BlockSpec((rows, xbc.shape[1]), scan),
                  pl.BlockSpec((rows, dt.shape[1]), scan),
                  pl.BlockSpec((rows, sw), scan),
                  pl.BlockSpec((rows, d), lambda b, i: (b * nt + jnp.maximum(i - 1, 0), 1)),
                  pl.BlockSpec((rows, d), done),
                  pl.BlockSpec((rows, d), done),
                  pl.BlockSpec((1, 1, d), lambda b, i: (b, 0, 0)),
                  _resident(alog.shape),
                  _resident(dsk.shape), _resident(nw.shape), _resident(tri.shape),
                  _resident(wsp.shape), _resident(wout.shape)],
        out_specs=pl.BlockSpec((rows, d), done),
        out_shape=jax.ShapeDtypeStruct((t, d), F32),
        scratch_shapes=[pltpu.VMEM((SSM_GROUPS, SSM_STATE, sw // SSM_GROUPS), F32),
                        pltpu.VMEM((rows, sw), BF16),
                        pltpu.VMEM((rows, sw), BF16),
                        pltpu.VMEM((rows, d), BF16)],
        compiler_params=pltpu.CompilerParams(
            dimension_semantics=("arbitrary", "arbitrary"), vmem_limit_bytes=VMEM_LIMIT),
        name="ssd",
    )(xbc, dt, zm, g, ya, x2, gate, alog, dsk, nw, tri, wsp, wout)


def _layer(x, c, w_ada, b_ada, norm_w, w_in, q_norm_w, k_norm_w, rel_bias, sinks,
           conv_w, conv_b, dt_bias, a_log, d_skip, ssm_norm_w, w_attn_proj, w_ssm_proj, w_out):
    batch, seq, d = x.shape
    aw = ATTN_HEADS * HEAD_DIM
    kvw = ATTN_KV_HEADS * HEAD_DIM
    sw = w_ssm_proj.shape[0]
    ssm_heads = sw // SSM_HEAD_DIM
    xbc_w = sw + 2 * SSM_GROUPS * SSM_STATE
    assert seq % BLOCK == 0 and seq % CHUNK == 0 and ssm_heads <= LANES

    c8 = jnp.zeros((8, d), F32).at[:batch].set(c)
    mod = _mod_call(c8, w_ada, b_ada.reshape(1, -1))[:batch]
    shift, scale, gate = (mod[:, k * d:(k + 1) * d].reshape(batch, 1, d) for k in range(3))

    o = np.cumsum([0, aw, kvw, kvw, aw, sw, xbc_w, ssm_heads, d, d])
    w_t = w_in.T
    w_all = _wprep_call(w_t, int(o[6]), ssm_heads)
    wdt = jnp.pad(w_t[o[6]:o[7]], ((0, LANES - ssm_heads), (0, 0))).astype(BF16)
    widths = [aw + 2 * kvw, aw, sw, xbc_w, 2 * d]
    pad_heads = lambda v: jnp.pad(v.astype(F32), (0, LANES - ssm_heads)).reshape(1, LANES)

    x2 = x.reshape(batch * seq, d)
    qkv, za, zm, xbc, g, dt = _inproj_call(
        x2, shift, scale, norm_w.reshape(1, d), w_all, wdt, pad_heads(dt_bias),
        conv_w.astype(F32), conv_b.reshape(1, -1).astype(F32), widths, batch, seq)

    bias = _bias_call(rel_bias.astype(F32))
    seg = np.arange(LANES) // HEAD_DIM
    bd = jnp.asarray((seg[:, None] == seg[None, :]).astype(np.float32) / HEAD_DIM, dtype=BF16)
    qw = jnp.tile(q_norm_w.astype(F32), LANES // HEAD_DIM).reshape(1, LANES)
    kw = jnp.tile(k_norm_w.astype(F32), LANES // HEAD_DIM).reshape(1, LANES)
    sink_rows = jnp.repeat(sinks.astype(F32).reshape(ATTN_KV_HEADS, 2, 2).transpose(0, 2, 1), BLOCK, axis=2)
    ya = _attn_call(qkv, za, g, bias, sink_rows, qw, kw, bd,
                    w_attn_proj.astype(BF16), batch, seq)

    tri = jnp.asarray(np.tril(np.ones((CHUNK, CHUNK), np.float32)), dtype=BF16)
    out = _ssd_call(xbc, dt, zm, g, ya, x2, gate, pad_heads(a_log),
                    jnp.repeat(d_skip.astype(F32), SSM_HEAD_DIM).reshape(1, sw),
                    ssm_norm_w.reshape(1, sw).astype(F32), tri,
                    w_ssm_proj.astype(BF16), w_out.astype(BF16), batch, seq)
    return out.reshape(batch, seq, d)


def kernel(x, c, w_ada, b_ada, norm_w, w_in, q_norm_w, k_norm_w, rel_bias, sinks, conv_w, conv_b,
           dt_bias, a_log, d_skip, ssm_norm_w, w_attn_proj, w_ssm_proj, w_out):
    depth = w_in.shape[0]
    for i in range(depth):
        x = _layer(x, c, w_ada[i], b_ada[i], norm_w[i], w_in[i], q_norm_w[i], k_norm_w[i],
                   rel_bias, sinks[i], conv_w[i], conv_b[i], dt_bias[i], a_log[i], d_skip[i],
                   ssm_norm_w[i], w_attn_proj[i], w_ssm_proj[i], w_out[i])
    return x
```

```python
import functools
import math

import numpy as np
import jax
import jax.numpy as jnp
from jax import lax
from jax.experimental import pallas as pl
from jax.experimental.pallas import tpu as pltpu

F32 = jnp.float32
BF16 = jnp.bfloat16

HEAD_DIM = 64
ATTN_HEADS = 16
ATTN_KV_HEADS = 4
GROUP = ATTN_HEADS // ATTN_KV_HEADS
BLOCK = 128
REL_BUCKETS = 32
REL_MAX_DIST = 128
SSM_HEAD_DIM = 64
SSM_GROUPS = 4
SSM_STATE = 128
CONV_WIDTH = 4
CHUNK = 128
EPS = 1e-6
NEG = -1e30
LOG2E = 1.4426950408889634
LANES = 128
VMEM_LIMIT = 56 * 1024 * 1024

ROWS_INPROJ = 512
ROWS_ATTN = 512
ROWS_SSD = 256


def _sigmoid(x):
    return 1.0 / (1.0 + jnp.exp(-x))


def _silu(x):
    return x * _sigmoid(x)


def _softplus(x):
    return jnp.maximum(x, 0.0) + jnp.log(1.0 + jnp.exp(-jnp.abs(x)))


def _dot(a, b):
    return jnp.dot(a, b, preferred_element_type=F32)


def _dot_nt(a, b):
    return lax.dot_general(a, b, (((1,), (1,)), ((), ())), preferred_element_type=F32)


def _resident(shape):
    nd = len(shape)
    return pl.BlockSpec(shape, lambda *_: (0,) * nd, pipeline_mode=pl.Buffered(1))


def _mod_kernel(c_ref, w_ref, b_ref, o_ref):
    s = _silu(c_ref[...])
    o_ref[...] = _dot(s.astype(BF16), w_ref[...].astype(BF16)) + b_ref[...]


def _mod_call(c8, w_ada, b_ada):
    d, n = w_ada.shape
    tn = 1024
    return pl.pallas_call(
        _mod_kernel,
        grid=(n // tn,),
        in_specs=[pl.BlockSpec((8, d), lambda j: (0, 0)),
                  pl.BlockSpec((d, tn), lambda j: (0, j)),
                  pl.BlockSpec((1, tn), lambda j: (0, j))],
        out_specs=pl.BlockSpec((8, tn), lambda j: (0, j)),
        out_shape=jax.ShapeDtypeStruct((8, n), F32),
        name="mod",
    )(c8, w_ada, b_ada)


def _bucket_table():
    qi = np.arange(BLOCK)[:, None]
    kj = np.arange(2 * BLOCK)[None, :]
    dist = qi + BLOCK - kj
    n = np.maximum(dist, 0)
    max_exact = REL_BUCKETS // 2
    nf = np.maximum(n, 1).astype(np.float32)
    large = max_exact + (np.log(nf / max_exact) / math.log(REL_MAX_DIST / max_exact)
                         * (REL_BUCKETS - max_exact)).astype(np.int32)
    large = np.minimum(large, REL_BUCKETS - 1)
    bucket = np.where(n < max_exact, n, large)
    valid = (dist >= 0) & (dist < BLOCK)
    return np.where(valid, bucket, -1).astype(np.int32)


def _bias_kernel(rb_ref, idx_ref, o_ref):
    kv = pl.program_id(0)
    idx = idx_ref[...]
    key_row = lax.broadcasted_iota(jnp.int32, idx.shape, 0)
    keep = idx >= 0
    keep_first = jnp.logical_and(keep, key_row >= BLOCK)
    for s in range(2):
        for c in range(2):
            acc = jnp.zeros(idx.shape, F32)
            for b in range(REL_BUCKETS):
                acc = jnp.where(idx == b, rb_ref[b, kv * GROUP + 2 * c + s], acc)
            acc = acc * LOG2E
            rows, cols = slice(s * 2 * BLOCK, (s + 1) * 2 * BLOCK), slice(c * BLOCK, (c + 1) * BLOCK)
            o_ref[0, 0, rows, cols] = jnp.where(keep, acc, NEG)
            o_ref[1, 0, rows, cols] = jnp.where(keep_first, acc, NEG)


def _bias_call(rel_bias):
    idx_t = jnp.asarray(np.ascontiguousarray(_bucket_table().T))
    table = pl.pallas_call(
        _bias_kernel,
        grid=(ATTN_KV_HEADS,),
        in_specs=[pl.BlockSpec(memory_space=pltpu.SMEM),
                  pl.BlockSpec((2 * BLOCK, BLOCK), lambda h: (0, 0))],
        out_specs=pl.BlockSpec((2, 1, 4 * BLOCK, 2 * BLOCK), lambda h: (0, h, 0, 0)),
        out_shape=jax.ShapeDtypeStruct((2, ATTN_KV_HEADS, 4 * BLOCK, 2 * BLOCK), F32),
        name="bias",
    )(rel_bias, idx_t)
    return table.reshape(2 * ATTN_KV_HEADS, 4 * BLOCK, 2 * BLOCK)


def _wprep_kernel(wt_ref, o_ref):
    o_ref[...] = wt_ref[...].T.astype(BF16)


def _wprep_call(w_t, dt_start, dt_width):
    n, k = w_t.shape
    tn = 512
    head_tiles, tail_tiles = dt_start // tn, (n - dt_start - dt_width) // tn
    assert head_tiles * tn == dt_start and tail_tiles * tn == n - dt_start - dt_width
    tail_start = dt_start + dt_width

    assert tail_start % 8 == 0

    def rows(i):
        start = jnp.where(i < head_tiles, i * tn, tail_start + (i - head_tiles) * tn)
        return (pl.multiple_of(start, 8), 0)

    return pl.pallas_call(
        _wprep_kernel,
        grid=(head_tiles + tail_tiles,),
        in_specs=[pl.BlockSpec((pl.Element(tn), pl.Element(k)), rows)],
        out_specs=pl.BlockSpec((k, tn), lambda i: (0, i)),
        out_shape=jax.ShapeDtypeStruct((k, n - dt_width), BF16),
        compiler_params=pltpu.CompilerParams(dimension_semantics=("parallel",)),
        name="wprep",
    )(w_t)


def _inproj_kernel(x_ref, shift_ref, scale_ref, nw_ref, w_ref, wdt_ref, dtb_ref, cw_ref, cb_ref,
                   qkv_o, za_o, zm_o, xbc_o, g_o, dt_o, xraw_s, carry_s):
    ts = x_ref.shape[0]
    halo = carry_s.shape[1]

    @pl.when(pl.program_id(1) == 0)
    def _():
        carry_s[...] = jnp.zeros(carry_s.shape, F32)

    x = x_ref[...]
    ms = jnp.mean(x * x, axis=-1, keepdims=True)
    xn = x * lax.rsqrt(ms + EPS) * nw_ref[...]
    h = (xn * (1.0 + scale_ref[0]) + shift_ref[0]).astype(BF16)

    def conv_silu_store(y, c0, slot):
        for jj in range(y.shape[1] // LANES):
            j = c0 // LANES + jj
            jl = slice(j * LANES, (j + 1) * LANES)
            yj = y[:, jj * LANES:(jj + 1) * LANES]
            xraw_s[slot, jj, 0:halo, :] = carry_s[j]
            xraw_s[slot, jj, halo:halo + ts, :] = yj
            carry_s[j] = yj[ts - halo:ts, :]
            acc = cb_ref[:, jl]
            for w in range(CONV_WIDTH):
                off = halo - (CONV_WIDTH - 1) + w
                acc = acc + cw_ref[w:w + 1, jl] * xraw_s[slot, jj, off:off + ts, :]
            xbc_o[:, jl] = (acc * (1.0 / (1.0 + jnp.exp2(acc * (-LOG2E))))).astype(BF16)

    step = xraw_s.shape[1] * LANES
    light, heavy, col = [], [], 0
    for o_ref, act in ((qkv_o, None), (za_o, _silu), (zm_o, _silu), (xbc_o, "conv"), (g_o, _sigmoid)):
        for c0 in range(0, o_ref.shape[1], step):
            (heavy if act == "conv" else light).append((o_ref, act, col + c0, c0))
        col += o_ref.shape[1]
    order = []
    while light or heavy:
        if light:
            order.append(light.pop(0))
        if heavy:
            order.append(heavy.pop(0))
    slot = 0
    for o_ref, act, wc, c0 in order:
        y = _dot(h, w_ref[:, wc:wc + step])
        if act == "conv":
            conv_silu_store(y, c0, slot)
            slot = 1 - slot
        else:
            o_ref[:, c0:c0 + step] = (y if act is None else act(y)).astype(o_ref.dtype)
    dt_o[...] = _softplus(_dot_nt(h, wdt_ref[...]) + dtb_ref[...])


def _inproj_call(x2, shift, scale, norm_w, w_all, wdt, dtb, cw, cb, widths, batch, seq):
    t, d = x2.shape
    ts = min(ROWS_INPROJ, seq)
    ns = seq // ts
    row = lambda b, i: (b * ns + i, 0)
    per_b = pl.BlockSpec((1, 1, d), lambda b, i: (b, 0, 0))
    widths = list(widths) + [wdt.shape[0]]
    dtypes = [BF16] * (len(widths) - 1) + [F32]
    piece_blocks = 4
    return pl.pallas_call(
        _inproj_kernel,
        grid=(batch, ns),
        in_specs=[pl.BlockSpec((ts, d), row), per_b, per_b, _resident((1, d)),
                  _resident(w_all.shape), _resident(wdt.shape), _resident(dtb.shape),
                  _resident(cw.shape), _resident(cb.shape)],
        out_specs=[pl.BlockSpec((ts, n), row) for n in widths],
        out_shape=[jax.ShapeDtypeStruct((t, n), dt) for n, dt in zip(widths, dtypes)],
        scratch_shapes=[pltpu.VMEM((2, piece_blocks, ts + 8, LANES), F32),
                        pltpu.VMEM((cw.shape[1] // LANES, 8, LANES), F32)],
        compiler_params=pltpu.CompilerParams(
            dimension_semantics=("arbitrary", "arbitrary"), vmem_limit_bytes=VMEM_LIMIT),
        name="inproj",
    )(x2, shift, scale, norm_w, w_all, wdt, dtb, cw, cb)


def _segment_rms(x, bd, w):
    ms = _dot((x * x).astype(BF16), bd)
    return x * lax.rsqrt(ms + EPS) * w


def _attn_kernel(q_ref, kvc_ref, kvp_ref, za_ref, ga_ref, bias_ref, sink_ref,
                 qw_ref, kw_ref, bd_ref, wap_ref, o_ref,
                 qn_s, klo_s, khi_s, vt_s, s_s, p_s, yt_s):
    rows = q_ref.shape[0]
    nblk = rows // BLOCK
    kvw = ATTN_KV_HEADS * HEAD_DIM
    first_tile = pl.program_id(1) == 0
    bd = bd_ref[...]
    qw = qw_ref[...] * (HEAD_DIM ** -0.5 * LOG2E)
    kw = kw_ref[...]
    low_half = lax.broadcasted_iota(jnp.int32, (1, LANES), 1) < HEAD_DIM

    for j in range(q_ref.shape[1] // LANES):
        sl = slice(j * LANES, (j + 1) * LANES)
        qn_s[:, sl] = _segment_rms(q_ref[:, sl].astype(F32), bd, qw).astype(BF16)
    for j in range(kvw // LANES):
        sl = slice(j * LANES, (j + 1) * LANES)
        even = slice(2 * j * LANES, (2 * j + 1) * LANES)
        odd = slice((2 * j + 1) * LANES, (2 * j + 2) * LANES)
        for dst, src in ((slice(0, BLOCK), kvp_ref), (slice(BLOCK, BLOCK + rows), kvc_ref)):
            kn = _segment_rms(src[:, sl].astype(F32), bd, kw)
            lo = jnp.where(low_half, kn, 0.0)
            hi = jnp.where(low_half, 0.0, kn)
            klo_s[dst, even] = lo.astype(BF16)
            khi_s[dst, odd] = hi.astype(BF16)
            khi_s[dst, even] = pltpu.roll(lo, HEAD_DIM, axis=1).astype(BF16)
            klo_s[dst, odd] = pltpu.roll(hi, HEAD_DIM, axis=1).astype(BF16)
    vt_s[:, 0:BLOCK] = kvp_ref[:, kvw:2 * kvw].astype(F32).T.astype(BF16)
    vt_s[:, BLOCK:] = kvc_ref[:, kvw:2 * kvw].astype(F32).T.astype(BF16)

    units = [(qb, h) for qb in range(nblk) for h in range(ATTN_KV_HEADS)]

    def scores(u):
        qb, h = units[u]
        r0 = qb * BLOCK
        hl = slice(h * LANES, (h + 1) * LANES)
        k2 = jnp.concatenate([klo_s[r0:r0 + 2 * BLOCK, hl], khi_s[r0:r0 + 2 * BLOCK, hl]], axis=0)
        q2 = jnp.concatenate([qn_s[r0:r0 + BLOCK, (2 * h + c) * LANES:(2 * h + c + 1) * LANES]
                              for c in range(2)], axis=0)
        entry = jnp.where(first_tile, ATTN_KV_HEADS, 0) + h if qb == 0 else h
        s_s[u % 2] = _dot_nt(k2, q2) + bias_ref[entry]

    def finish(u):
        qb, h = units[u]
        r0 = qb * BLOCK
        slot = u % 2
        inv = []
        for s in range(2):
            for c in range(2):
                blk = s_s[slot, s * 2 * BLOCK:(s + 1) * 2 * BLOCK, c * BLOCK:(c + 1) * BLOCK]
                sink = sink_ref[h, s:s + 1, c * BLOCK:(c + 1) * BLOCK] * LOG2E
                m = jnp.maximum(jnp.max(blk, axis=0, keepdims=True), sink)
                e = jnp.exp2(blk - m)
                den = jnp.sum(e, axis=0, keepdims=True) + jnp.exp2(sink - m)
                p_s[slot, :, (2 * s + c) * BLOCK:(2 * s + c + 1) * BLOCK] = e.astype(BF16)
                inv.append(1.0 / den)
        o_t = _dot(vt_s[h * HEAD_DIM:(h + 1) * HEAD_DIM, r0:r0 + 2 * BLOCK], p_s[slot])
        for s in range(2):
            for c in range(2):
                head = GROUP * h + 2 * c + s
                k = 2 * s + c
                yt_s[head * HEAD_DIM:(head + 1) * HEAD_DIM, r0:r0 + BLOCK] = (
                    o_t[:, k * BLOCK:(k + 1) * BLOCK] * inv[k])

    scores(0)
    for u in range(len(units)):
        if u + 1 < len(units):
            scores(u + 1)
        finish(u)

    y = (yt_s[...].T * za_ref[...].astype(F32)).astype(BF16)
    ya = _dot(y, wap_ref[...])
    o_ref[...] = (ya * ga_ref[...].astype(F32)).astype(o_ref.dtype)


def _attn_call(qkv, za, g, bias, sink_rows, qw, kw, bd, wap, batch, seq):
    t = qkv.shape[0]
    aw = ATTN_HEADS * HEAD_DIM
    kvw2 = 2 * ATTN_KV_HEADS * HEAD_DIM
    rows = min(ROWS_ATTN, seq)
    nt = seq // rows
    bpt = rows // BLOCK
    row = lambda b, i: (b * nt + i, 0)
    return pl.pallas_call(
        _attn_kernel,
        grid=(batch, nt),
        in_specs=[pl.BlockSpec((rows, aw), row),
                  pl.BlockSpec((rows, kvw2), lambda b, i: (b * nt + i, aw // kvw2)),
                  pl.BlockSpec((BLOCK, kvw2),
                               lambda b, i: (jnp.maximum((b * nt + i) * bpt - 1, 0), aw // kvw2)),
                  pl.BlockSpec((rows, aw), row),
                  pl.BlockSpec((rows, aw), row),
                  _resident(bias.shape), _resident(sink_rows.shape),
                  _resident(qw.shape), _resident(kw.shape),
                  _resident(bd.shape), _resident(wap.shape)],
        out_specs=pl.BlockSpec((rows, wap.shape[1]), row),
        out_shape=jax.ShapeDtypeStruct((t, wap.shape[1]), BF16),
        scratch_shapes=[pltpu.VMEM((rows, aw), BF16),
                        pltpu.VMEM((rows + BLOCK, ATTN_KV_HEADS * LANES), BF16),
                        pltpu.VMEM((rows + BLOCK, ATTN_KV_HEADS * LANES), BF16),
                        pltpu.VMEM((kvw2 // 2, rows + BLOCK), BF16),
                        pltpu.VMEM((2, 4 * BLOCK, 2 * BLOCK), F32),
                        pltpu.VMEM((2, 2 * BLOCK, 4 * BLOCK), BF16),
                        pltpu.VMEM((aw, rows), F32)],
        compiler_params=pltpu.CompilerParams(
            dimension_semantics=("parallel", "parallel"), vmem_limit_bytes=VMEM_LIMIT),
        name="attn",
    )(qkv, qkv, qkv, za, g, bias, sink_rows, qw, kw, bd, wap)


def _ssd_kernel(xbc_ref, dt_ref, zm_ref, gb_ref, ya_ref, x_ref, gate_ref,
                alog_ref, dsk_ref, nw_ref, tri_ref, wsp_ref, wout_ref,
                o_ref, h_s, yn_s, yprev_s, merged_s):
    rows = xbc_ref.shape[0]
    nchunk = rows // CHUNK
    sw = zm_ref.shape[1]
    gw = sw // SSM_GROUPS
    pairs_per_group = gw // LANES
    n_xs = sw // LANES

    @pl.when(pl.program_id(1) == 0)
    def _():
        h_s[...] = jnp.zeros(h_s.shape, F32)
        yprev_s[...] = jnp.zeros(yprev_s.shape, BF16)

    d_out = o_ref.shape[1]
    piece_w = 2 * LANES

    def merge_piece(c0):
        cs = slice(c0, c0 + piece_w)
        yb = _dot(yprev_s[...], wsp_ref[:, cs])
        merged = ya_ref[:, cs].astype(F32) + gb_ref[:, cs].astype(F32) * yb
        merged_s[:, cs] = merged.astype(BF16)

    def out_piece(c0):
        cs = slice(c0, c0 + piece_w)
        o = _dot(merged_s[...], wout_ref[:, cs])
        o_ref[:, cs] = x_ref[:, cs] + gate_ref[0][:, cs] * o

    finish = ([functools.partial(merge_piece, c0) for c0 in range(0, d_out, piece_w)]
              + [functools.partial(out_piece, c0) for c0 in range(0, d_out, piece_w)])

    def lane_block(j, r0):
        return xbc_ref[r0:r0 + CHUNK, j * LANES:(j + 1) * LANES]

    a_row = -jnp.exp(alog_ref[...])
    tri = tri_ref[...]
    li = lax.broadcasted_iota(jnp.int32, (CHUNK, CHUNK), 0)
    si = lax.broadcasted_iota(jnp.int32, (CHUNK, CHUNK), 1)
    causal = li >= si
    low_half = si < SSM_HEAD_DIM
    lo_mask = (lax.broadcasted_iota(jnp.int32, (1, LANES), 1) < SSM_HEAD_DIM).astype(BF16)
    hi_mask = 1 - lo_mask

    for c in range(nchunk):
        r0 = c * CHUNK
        dt = dt_ref[r0:r0 + CHUNK, :]
        dta = dt * a_row
        dta_hi = dta.astype(BF16)
        dta_lo = (dta - dta_hi.astype(F32)).astype(BF16)
        a2 = (_dot(tri, dta_hi) + _dot(tri, dta_lo)) * LOG2E
        a2_t = a2.T
        dt_t = dt.T
        row2_t = a2_t - jnp.log2(dt_t)
        a_end2_t = a2_t[:, CHUNK - 1:CHUNK]
        w_t = dt_t * jnp.exp2(a_end2_t - a2_t)
        e_end_t = jnp.exp2(a_end2_t)

        for g in range(SSM_GROUPS):
            if finish:
                finish.pop(0)()
            bm_gb = lane_block(n_xs + g, r0)
            cm_gb = lane_block(n_xs + SSM_GROUPS + g, r0)
            cb = _dot_nt(cm_gb, bm_gb)
            bm_gt = bm_gb.astype(F32).T
            gsl = slice(g * gw, (g + 1) * gw)
            hprev = h_s[g]
            y_off = _dot(cm_gb, hprev.astype(BF16))
            y_parts = []
            for p in range(pairs_per_group):
                pair = g * pairs_per_group + p
                lsl = slice(pair * LANES, (pair + 1) * LANES)
                xs_b = lane_block(pair, r0)
                xs_pair = xs_b.astype(F32)
                rhs = jnp.concatenate([xs_b * lo_mask, xs_b * hi_mask], axis=0)
                m_parts, b_parts, cols, e_ends = [], [], [], []
                for side in range(2):
                    hh = 2 * pair + side
                    col2 = jnp.broadcast_to(a2[:, hh:hh + 1], (CHUNK, CHUNK))
                    dec_dt = jnp.exp2(jnp.where(causal, col2 - row2_t[hh:hh + 1, :], NEG))
                    m_parts.append(cb * dec_dt)
                    b_parts.append(bm_gt * w_t[hh:hh + 1, :])
                    cols.append(col2)
                    e_ends.append(e_end_t[hh:hh + 1, :])
                lhs = jnp.concatenate(
                    [jnp.concatenate(m_parts, axis=1), jnp.concatenate(b_parts, axis=1)],
                    axis=0).astype(BF16)
                res = _dot(lhs, rhs)
                e_col = jnp.exp2(jnp.where(low_half, cols[0], cols[1]))
                y_pair = res[0:CHUNK] + e_col * y_off[:, p * LANES:(p + 1) * LANES]
                y_pair = y_pair + dsk_ref[:, lsl] * xs_pair
                y_parts.append(y_pair * zm_ref[r0:r0 + CHUNK, lsl].astype(F32))
                e_end = jnp.where(low_half, e_ends[0], e_ends[1])
                h_s[g, :, p * LANES:(p + 1) * LANES] = (
                    hprev[:, p * LANES:(p + 1) * LANES] * e_end + res[CHUNK:])
            yg = jnp.concatenate(y_parts, axis=1)
            ms = jnp.mean(yg * yg, axis=-1, keepdims=True)
            yn_s[r0:r0 + CHUNK, gsl] = (yg * lax.rsqrt(ms + EPS) * nw_ref[:, gsl]).astype(BF16)

    while finish:
        finish.pop(0)()
    yprev_s[...] = yn_s[...]


def _ssd_call(xbc, dt, zm, g, ya, x2, gate, alog, dsk, nw, tri, wsp, wout, batch, seq):
    t, d = x2.shape
    sw = zm.shape[1]
    rows = min(ROWS_SSD, seq)
    nt = seq // rows
    scan = lambda b, i: (b * nt + jnp.minimum(i, nt - 1), 0)
    done = lambda b, i: (b * nt + jnp.maximum(i - 1, 0), 0)
    return pl.pallas_call(
        _ssd_kernel,
        grid=(batch, nt + 1),
        in_specs=[pl.BlockSpec((rows, xbc.shape[1]), scan),
                  pl.BlockSpec((rows, dt.shape[1]), scan),
                  pl.BlockSpec((rows, sw), scan),
                  pl.BlockSpec((rows, d), lambda b, i: (b * nt + jnp.maximum(i - 1, 0), 1)),
                  pl.BlockSpec((rows, d), done),
                  pl.BlockSpec((rows, d), done),
                  pl.BlockSpec((1, 1, d), lambda b, i: (b, 0, 0)),
                  _resident(alog.shape),
                  _resident(dsk.shape), _resident(nw.shape), _resident(tri.shape),
                  _resident(wsp.shape), _resident(wout.shape)],
        out_specs=pl.BlockSpec((rows, d), done),
        out_shape=jax.ShapeDtypeStruct((t, d), F32),
        scratch_shapes=[pltpu.VMEM((SSM_GROUPS, SSM_STATE, sw // SSM_GROUPS), F32),
                        pltpu.VMEM((rows, sw), BF16),
                        pltpu.VMEM((rows, sw), BF16),
                        pltpu.VMEM((rows, d), BF16)],
        compiler_params=pltpu.CompilerParams(
            dimension_semantics=("arbitrary", "arbitrary"), vmem_limit_bytes=VMEM_LIMIT),
        name="ssd",
    )(xbc, dt, zm, g, ya, x2, gate, alog, dsk, nw, tri, wsp, wout)


def _layer(x, c, w_ada, b_ada, norm_w, w_in, q_norm_w, k_norm_w, rel_bias, sinks,
           conv_w, conv_b, dt_bias, a_log, d_skip, ssm_norm_w, w_attn_proj, w_ssm_proj, w_out):
    batch, seq, d = x.shape
    aw = ATTN_HEADS * HEAD_DIM
    kvw = ATTN_KV_HEADS * HEAD_DIM
    sw = w_ssm_proj.shape[0]
    ssm_heads = sw // SSM_HEAD_DIM
    xbc_w = sw + 2 * SSM_GROUPS * SSM_STATE
    assert seq % BLOCK == 0 and seq % CHUNK == 0 and ssm_heads <= LANES

    c8 = jnp.zeros((8, d), F32).at[:batch].set(c)
    mod = _mod_call(c8, w_ada, b_ada.reshape(1, -1))[:batch]
    shift, scale, gate = (mod[:, k * d:(k + 1) * d].reshape(batch, 1, d) for k in range(3))

    o = np.cumsum([0, aw, kvw, kvw, aw, sw, xbc_w, ssm_heads, d, d])
    w_t = w_in.T
    w_all = _wprep_call(w_t, int(o[6]), ssm_heads)
    wdt = jnp.pad(w_t[o[6]:o[7]], ((0, LANES - ssm_heads), (0, 0))).astype(BF16)
    widths = [aw + 2 * kvw, aw, sw, xbc_w, 2 * d]
    pad_heads = lambda v: jnp.pad(v.astype(F32), (0, LANES - ssm_heads)).reshape(1, LANES)

    x2 = x.reshape(batch * seq, d)
    qkv, za, zm, xbc, g, dt = _inproj_call(
        x2, shift, scale, norm_w.reshape(1, d), w_all, wdt, pad_heads(dt_bias),
        conv_w.astype(F32), conv_b.reshape(1, -1).astype(F32), widths, batch, seq)

    bias = _bias_call(rel_bias.astype(F32))
    seg = np.arange(LANES) // HEAD_DIM
    bd = jnp.asarray((seg[:, None] == seg[None, :]).astype(np.float32) / HEAD_DIM, dtype=BF16)
    qw = jnp.tile(q_norm_w.astype(F32), LANES // HEAD_DIM).reshape(1, LANES)
    kw = jnp.tile(k_norm_w.astype(F32), LANES // HEAD_DIM).reshape(1, LANES)
    sink_rows = jnp.repeat(sinks.astype(F32).reshape(ATTN_KV_HEADS, 2, 2).transpose(0, 2, 1), BLOCK, axis=2)
    ya = _attn_call(qkv, za, g, bias, sink_rows, qw, kw, bd,
                    w_attn_proj.astype(BF16), batch, seq)

    tri = jnp.asarray(np.tril(np.ones((CHUNK, CHUNK), np.float32)), dtype=BF16)
    out = _ssd_call(xbc, dt, zm, g, ya, x2, gate, pad_heads(a_log),
                    jnp.repeat(d_skip.astype(F32), SSM_HEAD_DIM).reshape(1, sw),
                    ssm_norm_w.reshape(1, sw).astype(F32), tri,
                    w_ssm_proj.astype(BF16), w_out.astype(BF16), batch, seq)
    return out.reshape(batch, seq, d)


def kernel(x, c, w_ada, b_ada, norm_w, w_in, q_norm_w, k_norm_w, rel_bias, sinks, conv_w, conv_b,
           dt_bias, a_log, d_skip, ssm_norm_w, w_attn_proj, w_ssm_proj, w_out):
    depth = w_in.shape[0]
    for i in range(depth):
        x = _layer(x, c, w_ada[i], b_ada[i], norm_w[i], w_in[i], q_norm_w[i], k_norm_w[i],
                   rel_bias, sinks[i], conv_w[i], conv_b[i], dt_bias[i], a_log[i], d_skip[i],
                   ssm_norm_w[i], w_attn_proj[i], w_ssm_proj[i], w_out[i])
    return x
```

```python
import functools
import math

import numpy as np
import jax
import jax.numpy as jnp
from jax import lax
from jax.experimental import pallas as pl
from jax.experimental.pallas import tpu as pltpu

F32 = jnp.float32
BF16 = jnp.bfloat16

HEAD_DIM = 64
ATTN_HEADS = 16
ATTN_KV_HEADS = 4
GROUP = ATTN_HEADS // ATTN_KV_HEADS
BLOCK = 128
REL_BUCKETS = 32
REL_MAX_DIST = 128
SSM_HEAD_DIM = 64
SSM_GROUPS = 4
SSM_STATE = 128
CONV_WIDTH = 4
CHUNK = 128
EPS = 1e-6
NEG = -1e30
LOG2E = 1.4426950408889634
LANES = 128
VMEM_LIMIT = 56 * 1024 * 1024

ROWS_INPROJ = 512
ROWS_ATTN = 512
ROWS_SSD = 256


def _sigmoid(x):
    return 1.0 / (1.0 + jnp.exp(-x))


def _silu(x):
    return x * _sigmoid(x)


def _softplus(x):
    return jnp.maximum(x, 0.0) + jnp.log(1.0 + jnp.exp(-jnp.abs(x)))


def _dot(a, b):
    return jnp.dot(a, b, preferred_element_type=F32)


def _dot_nt(a, b):
    return lax.dot_general(a, b, (((1,), (1,)), ((), ())), preferred_element_type=F32)


def _resident(shape):
    nd = len(shape)
    return pl.BlockSpec(shape, lambda *_: (0,) * nd, pipeline_mode=pl.Buffered(1))


def _mod_kernel(c_ref, w_ref, b_ref, o_ref):
    s = _silu(c_ref[...])
    o_ref[...] = _dot(s.astype(BF16), w_ref[...].astype(BF16)) + b_ref[...]


def _mod_call(c8, w_ada, b_ada):
    d, n = w_ada.shape
    tn = 1024
    return pl.pallas_call(
        _mod_kernel,
        grid=(n // tn,),
        in_specs=[pl.BlockSpec((8, d), lambda j: (0, 0)),
                  pl.BlockSpec((d, tn), lambda j: (0, j)),
                  pl.BlockSpec((1, tn), lambda j: (0, j))],
        out_specs=pl.BlockSpec((8, tn), lambda j: (0, j)),
        out_shape=jax.ShapeDtypeStruct((8, n), F32),
        name="mod",
    )(c8, w_ada, b_ada)


def _bucket_table():
    qi = np.arange(BLOCK)[:, None]
    kj = np.arange(2 * BLOCK)[None, :]
    dist = qi + BLOCK - kj
    n = np.maximum(dist, 0)
    max_exact = REL_BUCKETS // 2
    nf = np.maximum(n, 1).astype(np.float32)
    large = max_exact + (np.log(nf / max_exact) / math.log(REL_MAX_DIST / max_exact)
                         * (REL_BUCKETS - max_exact)).astype(np.int32)
    large = np.minimum(large, REL_BUCKETS - 1)
    bucket = np.where(n < max_exact, n, large)
    valid = (dist >= 0) & (dist < BLOCK)
    return np.where(valid, bucket, -1).astype(np.int32)


def _bias_kernel(rb_ref, idx_ref, o_ref):
    kv = pl.program_id(0)
    idx = idx_ref[...]
    key_row = lax.broadcasted_iota(jnp.int32, idx.shape, 0)
    keep = idx >= 0
    keep_first = jnp.logical_and(keep, key_row >= BLOCK)
    for s in range(2):
        for c in range(2):
            acc = jnp.zeros(idx.shape, F32)
            for b in range(REL_BUCKETS):
                acc = jnp.where(idx == b, rb_ref[b, kv * GROUP + 2 * c + s], acc)
            acc = acc * LOG2E
            rows, cols = slice(s * 2 * BLOCK, (s + 1) * 2 * BLOCK), slice(c * BLOCK, (c + 1) * BLOCK)
            o_ref[0, 0, rows, cols] = jnp.where(keep, acc, NEG)
            o_ref[1, 0, rows, cols] = jnp.where(keep_first, acc, NEG)


def _bias_call(rel_bias):
    idx_t = jnp.asarray(np.ascontiguousarray(_bucket_table().T))
    table = pl.pallas_call(
        _bias_kernel,
        grid=(ATTN_KV_HEADS,),
        in_specs=[pl.BlockSpec(memory_space=pltpu.SMEM),
                  pl.BlockSpec((2 * BLOCK, BLOCK), lambda h: (0, 0))],
        out_specs=pl.BlockSpec((2, 1, 4 * BLOCK, 2 * BLOCK), lambda h: (0, h, 0, 0)),
        out_shape=jax.ShapeDtypeStruct((2, ATTN_KV_HEADS, 4 * BLOCK, 2 * BLOCK), F32),
        name="bias",
    )(rel_bias, idx_t)
    return table.reshape(2 * ATTN_KV_HEADS, 4 * BLOCK, 2 * BLOCK)


def _wprep_kernel(wt_ref, o_ref):
    o_ref[...] = wt_ref[...].T.astype(BF16)


def _wprep_call(w_t, dt_start, dt_width):
    n, k = w_t.shape
    tn = 512
    head_tiles, tail_tiles = dt_start // tn, (n - dt_start - dt_width) // tn
    assert head_tiles * tn == dt_start and tail_tiles * tn == n - dt_start - dt_width
    tail_start = dt_start + dt_width

    assert tail_start % 8 == 0

    def rows(i):
        start = jnp.where(i < head_tiles, i * tn, tail_start + (i - head_tiles) * tn)
        return (pl.multiple_of(start, 8), 0)

    return pl.pallas_call(
        _wprep_kernel,
        grid=(head_tiles + tail_tiles,),
        in_specs=[pl.BlockSpec((pl.Element(tn), pl.Element(k)), rows)],
        out_specs=pl.BlockSpec((k, tn), lambda i: (0, i)),
        out_shape=jax.ShapeDtypeStruct((k, n - dt_width), BF16),
        compiler_params=pltpu.CompilerParams(dimension_semantics=("parallel",)),
        name="wprep",
    )(w_t)


def _inproj_kernel(x_ref, shift_ref, scale_ref, nw_ref, w_ref, wdt_ref, dtb_ref, cw_ref, cb_ref,
                   qkv_o, za_o, zm_o, xbc_o, g_o, dt_o, xraw_s, carry_s):
    ts = x_ref.shape[0]
    halo = carry_s.shape[1]

    @pl.when(pl.program_id(1) == 0)
    def _():
        carry_s[...] = jnp.zeros(carry_s.shape, F32)

    x = x_ref[...]
    ms = jnp.mean(x * x, axis=-1, keepdims=True)
    xn = x * lax.rsqrt(ms + EPS) * nw_ref[...]
    h = (xn * (1.0 + scale_ref[0]) + shift_ref[0]).astype(BF16)

    def conv_silu_store(y, c0, slot):
        for jj in range(y.shape[1] // LANES):
            j = c0 // LANES + jj
            jl = slice(j * LANES, (j + 1) * LANES)
            yj = y[:, jj * LANES:(jj + 1) * LANES]
            xraw_s[slot, jj, 0:halo, :] = carry_s[j]
            xraw_s[slot, jj, halo:halo + ts, :] = yj
            carry_s[j] = yj[ts - halo:ts, :]
            acc = cb_ref[:, jl]
            for w in range(CONV_WIDTH):
                off = halo - (CONV_WIDTH - 1) + w
                acc = acc + cw_ref[w:w + 1, jl] * xraw_s[slot, jj, off:off + ts, :]
            xbc_o[:, jl] = (acc * (1.0 / (1.0 + jnp.exp2(acc * (-LOG2E))))).astype(BF16)

    step = xraw_s.shape[1] * LANES
    light, heavy, col = [], [], 0
    for o_ref, act in ((qkv_o, None), (za_o, _silu), (zm_o, _silu), (xbc_o, "conv"), (g_o, _sigmoid)):
        for c0 in range(0, o_ref.shape[1], step):
            (heavy if act == "conv" else light).append((o_ref, act, col + c0, c0))
        col += o_ref.shape[1]
    order = []
    while light or heavy:
        if light:
            order.append(light.pop(0))
        if heavy:
            order.append(heavy.pop(0))
    slot = 0
    for o_ref, act, wc, c0 in order:
        y = _dot(h, w_ref[:, wc:wc + step])
        if act == "conv":
            conv_silu_store(y, c0, slot)
            slot = 1 - slot
        else:
            o_ref[:, c0:c0 + step] = (y if act is None else act(y)).astype(o_ref.dtype)
    dt_o[...] = _softplus(_dot_nt(h, wdt_ref[...]) + dtb_ref[...])


def _inproj_call(x2, shift, scale, norm_w, w_all, wdt, dtb, cw, cb, widths, batch, seq):
    t, d = x2.shape
    ts = min(ROWS_INPROJ, seq)
    ns = seq // ts
    row = lambda b, i: (b * ns + i, 0)
    per_b = pl.BlockSpec((1, 1, d), lambda b, i: (b, 0, 0))
    widths = list(widths) + [wdt.shape[0]]
    dtypes = [BF16] * (len(widths) - 1) + [F32]
    piece_blocks = 4
    return pl.pallas_call(
        _inproj_kernel,
        grid=(batch, ns),
        in_specs=[pl.BlockSpec((ts, d), row), per_b, per_b, _resident((1, d)),
                  _resident(w_all.shape), _resident(wdt.shape), _resident(dtb.shape),
                  _resident(cw.shape), _resident(cb.shape)],
        out_specs=[pl.BlockSpec((ts, n), row) for n in widths],
        out_shape=[jax.ShapeDtypeStruct((t, n), dt) for n, dt in zip(widths, dtypes)],
        scratch_shapes=[pltpu.VMEM((2, piece_blocks, ts + 8, LANES), F32),
                        pltpu.VMEM((cw.shape[1] // LANES, 8, LANES), F32)],
        compiler_params=pltpu.CompilerParams(
            dimension_semantics=("arbitrary", "arbitrary"), vmem_limit_bytes=VMEM_LIMIT),
        name="inproj",
    )(x2, shift, scale, norm_w, w_all, wdt, dtb, cw, cb)


def _segment_rms(x, bd, w):
    ms = _dot((x * x).astype(BF16), bd)
    return x * lax.rsqrt(ms + EPS) * w


def _attn_kernel(q_ref, kvc_ref, kvp_ref, za_ref, ga_ref, bias_ref, sink_ref,
                 qw_ref, kw_ref, bd_ref, wap_ref, o_ref,
                 qn_s, klo_s, khi_s, vt_s, s_s, p_s, yt_s):
    rows = q_ref.shape[0]
    nblk = rows // BLOCK
    kvw = ATTN_KV_HEADS * HEAD_DIM
    first_tile = pl.program_id(1) == 0
    bd = bd_ref[...]
    qw = qw_ref[...] * (HEAD_DIM ** -0.5 * LOG2E)
    kw = kw_ref[...]
    low_half = lax.broadcasted_iota(jnp.int32, (1, LANES), 1) < HEAD_DIM

    for j in range(q_ref.shape[1] // LANES):
        sl = slice(j * LANES, (j + 1) * LANES)
        qn_s[:, sl] = _segment_rms(q_ref[:, sl].astype(F32), bd, qw).astype(BF16)
    for j in range(kvw // LANES):
        sl = slice(j * LANES, (j + 1) * LANES)
        even = slice(2 * j * LANES, (2 * j + 1) * LANES)
        odd = slice((2 * j + 1) * LANES, (2 * j + 2) * LANES)
        for dst, src in ((slice(0, BLOCK), kvp_ref), (slice(BLOCK, BLOCK + rows), kvc_ref)):
            kn = _segment_rms(src[:, sl].astype(F32), bd, kw)
            lo = jnp.where(low_half, kn, 0.0)
            hi = jnp.where(low_half, 0.0, kn)
            klo_s[dst, even] = lo.astype(BF16)
            khi_s[dst, odd] = hi.astype(BF16)
            khi_s[dst, even] = pltpu.roll(lo, HEAD_DIM, axis=1).astype(BF16)
            klo_s[dst, odd] = pltpu.roll(hi, HEAD_DIM, axis=1).astype(BF16)
    vt_s[:, 0:BLOCK] = kvp_ref[:, kvw:2 * kvw].astype(F32).T.astype(BF16)
    vt_s[:, BLOCK:] = kvc_ref[:, kvw:2 * kvw].astype(F32).T.astype(BF16)

    units = [(qb, h) for qb in range(nblk) for h in range(ATTN_KV_HEADS)]

    def scores(u):
        qb, h = units[u]
        r0 = qb * BLOCK
        hl = slice(h * LANES, (h + 1) * LANES)
        k2 = jnp.concatenate([klo_s[r0:r0 + 2 * BLOCK, hl], khi_s[r0:r0 + 2 * BLOCK, hl]], axis=0)
        q2 = jnp.concatenate([qn_s[r0:r0 + BLOCK, (2 * h + c) * LANES:(2 * h + c + 1) * LANES]
                              for c in range(2)], axis=0)
        entry = jnp.where(first_tile, ATTN_KV_HEADS, 0) + h if qb == 0 else h
        s_s[u % 2] = _dot_nt(k2, q2) + bias_ref[entry]

    def finish(u):
        qb, h = units[u]
        r0 = qb * BLOCK
        slot = u % 2
        inv = []
        for s in range(2):
            for c in range(2):
                blk = s_s[slot, s * 2 * BLOCK:(s + 1) * 2 * BLOCK, c * BLOCK:(c + 1) * BLOCK]
                sink = sink_ref[h, s:s + 1, c * BLOCK:(c + 1) * BLOCK] * LOG2E
                m = jnp.maximum(jnp.max(blk, axis=0, keepdims=True), sink)
                e = jnp.exp2(blk - m)
                den = jnp.sum(e, axis=0, keepdims=True) + jnp.exp2(sink - m)
                p_s[slot, :, (2 * s + c) * BLOCK:(2 * s + c + 1) * BLOCK] = e.astype(BF16)
                inv.append(1.0 / den)
        o_t = _dot(vt_s[h * HEAD_DIM:(h + 1) * HEAD_DIM, r0:r0 + 2 * BLOCK], p_s[slot])
        for s in range(2):
            for c in range(2):
                head = GROUP * h + 2 * c + s
                k = 2 * s + c
                yt_s[head * HEAD_DIM:(head + 1) * HEAD_DIM, r0:r0 + BLOCK] = (
                    o_t[:, k * BLOCK:(k + 1) * BLOCK] * inv[k])

    scores(0)
    for u in range(len(units)):
        if u + 1 < len(units):
            scores(u + 1)
        finish(u)

    y = (yt_s[...].T * za_ref[...].astype(F32)).astype(BF16)
    ya = _dot(y, wap_ref[...])
    o_ref[...] = (ya * ga_ref[...].astype(F32)).astype(o_ref.dtype)


def _attn_call(qkv, za, g, bias, sink_rows, qw, kw, bd, wap, batch, seq):
    t = qkv.shape[0]
    aw = ATTN_HEADS * HEAD_DIM
    kvw2 = 2 * ATTN_KV_HEADS * HEAD_DIM
    rows = min(ROWS_ATTN, seq)
    nt = seq // rows
    bpt = rows // BLOCK
    row = lambda b, i: (b * nt + i, 0)
    return pl.pallas_call(
        _attn_kernel,
        grid=(batch, nt),
        in_specs=[pl.BlockSpec((rows, aw), row),
                  pl.BlockSpec((rows, kvw2), lambda b, i: (b * nt + i, aw // kvw2)),
                  pl.BlockSpec((BLOCK, kvw2),
                               lambda b, i: (jnp.maximum((b * nt + i) * bpt - 1, 0), aw // kvw2)),
                  pl.BlockSpec((rows, aw), row),
                  pl.BlockSpec((rows, aw), row),
                  _resident(bias.shape), _resident(sink_rows.shape),
                  _resident(qw.shape), _resident(kw.shape),
                  _resident(bd.shape), _resident(wap.shape)],
        out_specs=pl.BlockSpec((rows, wap.shape[1]), row),
        out_shape=jax.ShapeDtypeStruct((t, wap.shape[1]), BF16),
        scratch_shapes=[pltpu.VMEM((rows, aw), BF16),
                        pltpu.VMEM((rows + BLOCK, ATTN_KV_HEADS * LANES), BF16),
                        pltpu.VMEM((rows + BLOCK, ATTN_KV_HEADS * LANES), BF16),
                        pltpu.VMEM((kvw2 // 2, rows + BLOCK), BF16),
                        pltpu.VMEM((2, 4 * BLOCK, 2 * BLOCK), F32),
                        pltpu.VMEM((2, 2 * BLOCK, 4 * BLOCK), BF16),
                        pltpu.VMEM((aw, rows), F32)],
        compiler_params=pltpu.CompilerParams(
            dimension_semantics=("parallel", "parallel"), vmem_limit_bytes=VMEM_LIMIT),
        name="attn",
    )(qkv, qkv, qkv, za, g, bias, sink_rows, qw, kw, bd, wap)


def _ssd_kernel(xbc_ref, dt_ref, zm_ref, gb_ref, ya_ref, x_ref, gate_ref,
                alog_ref, dsk_ref, nw_ref, tri_ref, wsp_ref, wout_ref,
                o_ref, h_s, yn_s, yprev_s, merged_s):
    rows = xbc_ref.shape[0]
    nchunk = rows // CHUNK
    sw = zm_ref.shape[1]
    gw = sw // SSM_GROUPS
    pairs_per_group = gw // LANES
    n_xs = sw // LANES

    @pl.when(pl.program_id(1) == 0)
    def _():
        h_s[...] = jnp.zeros(h_s.shape, F32)
        yprev_s[...] = jnp.zeros(yprev_s.shape, BF16)

    d_out = o_ref.shape[1]
    piece_w = 2 * LANES

    def merge_piece(c0):
        cs = slice(c0, c0 + piece_w)
        yb = _dot(yprev_s[...], wsp_ref[:, cs])
        merged_s[:, cs] = ya_ref[:, cs] + gb_ref[:, cs] * yb.astype(BF16)

    def out_piece(c0):
        cs = slice(c0, c0 + piece_w)
        o = _dot(merged_s[...], wout_ref[:, cs])
        o_ref[:, cs] = x_ref[:, cs] + gate_ref[0][:, cs] * o

    finish = ([functools.partial(merge_piece, c0) for c0 in range(0, d_out, piece_w)]
              + [functools.partial(out_piece, c0) for c0 in range(0, d_out, piece_w)])

    def lane_block(j, r0):
        return xbc_ref[r0:r0 + CHUNK, j * LANES:(j + 1) * LANES]

    a_row = -jnp.exp(alog_ref[...])
    tri = tri_ref[...]
    li = lax.broadcasted_iota(jnp.int32, (CHUNK, CHUNK), 0)
    si = lax.broadcasted_iota(jnp.int32, (CHUNK, CHUNK), 1)
    causal = li >= si
    low_half = si < SSM_HEAD_DIM
    lo_mask = (lax.broadcasted_iota(jnp.int32, (1, LANES), 1) < SSM_HEAD_DIM).astype(BF16)
    hi_mask = 1 - lo_mask

    for c in range(nchunk):
        r0 = c * CHUNK
        dt = dt_ref[r0:r0 + CHUNK, :]
        dta = dt * a_row
        dta_hi = dta.astype(BF16)
        dta_lo = (dta - dta_hi.astype(F32)).astype(BF16)
        a2 = (_dot(tri, dta_hi) + _dot(tri, dta_lo)) * LOG2E
        a2_t = a2.T
        dt_t = dt.T
        row2_t = a2_t - jnp.log2(dt_t)
        a_end2_t = a2_t[:, CHUNK - 1:CHUNK]
        w_t = dt_t * jnp.exp2(a_end2_t - a2_t)
        e_end_t = jnp.exp2(a_end2_t)

        for g in range(SSM_GROUPS):
            if finish:
                finish.pop(0)()
            bm_gb = lane_block(n_xs + g, r0)
            cm_gb = lane_block(n_xs + SSM_GROUPS + g, r0)
            cb = _dot_nt(cm_gb, bm_gb)
            bm_gt = bm_gb.astype(F32).T
            gsl = slice(g * gw, (g + 1) * gw)
            hprev = h_s[g]
            y_off = _dot(cm_gb, hprev.astype(BF16))
            y_parts = []
            for p in range(pairs_per_group):
                pair = g * pairs_per_group + p
                lsl = slice(pair * LANES, (pair + 1) * LANES)
                xs_b = lane_block(pair, r0)
                xs_pair = xs_b.astype(F32)
                rhs = jnp.concatenate([xs_b * lo_mask, xs_b * hi_mask], axis=0)
                m_parts, b_parts, cols, e_ends = [], [], [], []
                for side in range(2):
                    hh = 2 * pair + side
                    col2 = jnp.broadcast_to(a2[:, hh:hh + 1], (CHUNK, CHUNK))
                    dec_dt = jnp.exp2(jnp.where(causal, col2 - row2_t[hh:hh + 1, :], NEG))
                    m_parts.append(cb * dec_dt)
                    b_parts.append(bm_gt * w_t[hh:hh + 1, :])
                    cols.append(col2)
                    e_ends.append(e_end_t[hh:hh + 1, :])
                lhs = jnp.concatenate(
                    [jnp.concatenate(m_parts, axis=1), jnp.concatenate(b_parts, axis=1)],
                    axis=0).astype(BF16)
                res = _dot(lhs, rhs)
                e_col = jnp.exp2(jnp.where(low_half, cols[0], cols[1]))
                y_pair = res[0:CHUNK] + e_col * y_off[:, p * LANES:(p + 1) * LANES]
                y_pair = y_pair + dsk_ref[:, lsl] * xs_pair
                y_parts.append(y_pair * zm_ref[r0:r0 + CHUNK, lsl].astype(F32))
                e_end = jnp.where(low_half, e_ends[0], e_ends[1])
                h_s[g, :, p * LANES:(p + 1) * LANES] = (
                    hprev[:, p * LANES:(p + 1) * LANES] * e_end + res[CHUNK:])
            yg = jnp.concatenate(y_parts, axis=1)
            ms = jnp.mean(yg * yg, axis=-1, keepdims=True)
            yn_s[r0:r0 + CHUNK, gsl] = (yg * lax.rsqrt(ms + EPS) * nw_ref[:, gsl]).astype(BF16)

    while finish:
        finish.pop(0)()
    yprev_s[...] = yn_s[...]


def _ssd_call(xbc, dt, zm, g, ya, x2, gate, alog, dsk, nw, tri, wsp, wout, batch, seq):
    t, d = x2.shape
    sw = zm.shape[1]
    rows = min(ROWS_SSD, seq)
    nt = seq // rows
    scan = lambda b, i: (b * nt + jnp.minimum(i, nt - 1), 0)
    done = lambda b, i: (b * nt + jnp.maximum(i - 1, 0), 0)
    return pl.pallas_call(
        _ssd_kernel,
        grid=(batch, nt + 1),
        in_specs=[pl.BlockSpec((rows, xbc.shape[1]), scan),
                  pl.BlockSpec((rows, dt.shape[1]), scan),
                  pl.BlockSpec((rows, sw), scan),
                  pl.BlockSpec((rows, d), lambda b, i: (b * nt + jnp.maximum(i - 1, 0), 1)),
                  pl.BlockSpec((rows, d), done),
                  pl.BlockSpec((rows, d), done),
                  pl.BlockSpec((1, 1, d), lambda b, i: (b, 0, 0)),
                  _resident(alog.shape),
                  _resident(dsk.shape), _resident(nw.shape), _resident(tri.shape),
                  _resident(wsp.shape), _resident(wout.shape)],
        out_specs=pl.BlockSpec((rows, d), done),
        out_shape=jax.ShapeDtypeStruct((t, d), F32),
        scratch_shapes=[pltpu.VMEM((SSM_GROUPS, SSM_STATE, sw // SSM_GROUPS), F32),
                        pltpu.VMEM((rows, sw), BF16),
                        pltpu.VMEM((rows, sw), BF16),
                        pltpu.VMEM((rows, d), BF16)],
        compiler_params=pltpu.CompilerParams(
            dimension_semantics=("arbitrary", "arbitrary"), vmem_limit_bytes=VMEM_LIMIT),
        name="ssd",
    )(xbc, dt, zm, g, ya, x2, gate, alog, dsk, nw, tri, wsp, wout)


def _layer(x, c, w_ada, b_ada, norm_w, w_in, q_norm_w, k_norm_w, rel_bias, sinks,
           conv_w, conv_b, dt_bias, a_log, d_skip, ssm_norm_w, w_attn_proj, w_ssm_proj, w_out):
    batch, seq, d = x.shape
    aw = ATTN_HEADS * HEAD_DIM
    kvw = ATTN_KV_HEADS * HEAD_DIM
    sw = w_ssm_proj.shape[0]
    ssm_heads = sw // SSM_HEAD_DIM
    xbc_w = sw + 2 * SSM_GROUPS * SSM_STATE
    assert seq % BLOCK == 0 and seq % CHUNK == 0 and ssm_heads <= LANES

    c8 = jnp.zeros((8, d), F32).at[:batch].set(c)
    mod = _mod_call(c8, w_ada, b_ada.reshape(1, -1))[:batch]
    shift, scale, gate = (mod[:, k * d:(k + 1) * d].reshape(batch, 1, d) for k in range(3))

    o = np.cumsum([0, aw, kvw, kvw, aw, sw, xbc_w, ssm_heads, d, d])
    w_t = w_in.T
    w_all = _wprep_call(w_t, int(o[6]), ssm_heads)
    wdt = jnp.pad(w_t[o[6]:o[7]], ((0, LANES - ssm_heads), (0, 0))).astype(BF16)
    widths = [aw + 2 * kvw, aw, sw, xbc_w, 2 * d]
    pad_heads = lambda v: jnp.pad(v.astype(F32), (0, LANES - ssm_heads)).reshape(1, LANES)

    x2 = x.reshape(batch * seq, d)
    qkv, za, zm, xbc, g, dt = _inproj_call(
        x2, shift, scale, norm_w.reshape(1, d), w_all, wdt, pad_heads(dt_bias),
        conv_w.astype(F32), conv_b.reshape(1, -1).astype(F32), widths, batch, seq)

    bias = _bias_call(rel_bias.astype(F32))
    seg = np.arange(LANES) // HEAD_DIM
    bd = jnp.asarray((seg[:, None] == seg[None, :]).astype(np.float32) / HEAD_DIM, dtype=BF16)
    qw = jnp.tile(q_norm_w.astype(F32), LANES // HEAD_DIM).reshape(1, LANES)
    kw = jnp.tile(k_norm_w.astype(F32), LANES // HEAD_DIM).reshape(1, LANES)
    sink_rows = jnp.repeat(sinks.astype(F32).reshape(ATTN_KV_HEADS, 2, 2).transpose(0, 2, 1), BLOCK, axis=2)
    ya = _attn_call(qkv, za, g, bias, sink_rows, qw, kw, bd,
                    w_attn_proj.astype(BF16), batch, seq)

    tri = jnp.asarray(np.tril(np.ones((CHUNK, CHUNK), np.float32)), dtype=BF16)
    out = _ssd_call(xbc, dt, zm, g, ya, x2, gate, pad_heads(a_log),
                    jnp.repeat(d_skip.astype(F32), SSM_HEAD_DIM).reshape(1, sw),
                    ssm_norm_w.reshape(1, sw).astype(F32), tri,
                    w_ssm_proj.astype(BF16), w_out.astype(BF16), batch, seq)
    return out.reshape(batch, seq, d)


def kernel(x, c, w_ada, b_ada, norm_w, w_in, q_norm_w, k_norm_w, rel_bias, sinks, conv_w, conv_b,
           dt_bias, a_log, d_skip, ssm_norm_w, w_attn_proj, w_ssm_proj, w_out):
    depth = w_in.shape[0]
    for i in range(depth):
        x = _layer(x, c, w_ada[i], b_ada[i], norm_w[i], w_in[i], q_norm_w[i], k_norm_w[i],
                   rel_bias, sinks[i], conv_w[i], conv_b[i], dt_bias[i], a_log[i], d_skip[i],
                   ssm_norm_w[i], w_attn_proj[i], w_ssm_proj[i], w_out[i])
    return x
```

```python
import functools
import math

import numpy as np
import jax
import jax.numpy as jnp
from jax import lax
from jax.experimental import pallas as pl
from jax.experimental.pallas import tpu as pltpu

F32 = jnp.float32
BF16 = jnp.bfloat16

HEAD_DIM = 64
ATTN_HEADS = 16
ATTN_KV_HEADS = 4
GROUP = ATTN_HEADS // ATTN_KV_HEADS
BLOCK = 128
REL_BUCKETS = 32
REL_MAX_DIST = 128
SSM_HEAD_DIM = 64
SSM_GROUPS = 4
SSM_STATE = 128
CONV_WIDTH = 4
CHUNK = 128
EPS = 1e-6
NEG = -1e30
LOG2E = 1.4426950408889634
LANES = 128
VMEM_LIMIT = 56 * 1024 * 1024

ROWS_INPROJ = 512
ROWS_BRANCHES = 256


def _sigmoid(x):
    return 1.0 / (1.0 + jnp.exp(-x))


def _silu(x):
    return x * _sigmoid(x)


def _softplus(x):
    return jnp.maximum(x, 0.0) + jnp.log(1.0 + jnp.exp(-jnp.abs(x)))


def _dot(a, b):
    return jnp.dot(a, b, preferred_element_type=F32)


def _dot_nt(a, b):
    return lax.dot_general(a, b, (((1,), (1,)), ((), ())), preferred_element_type=F32)


def _resident(shape):
    nd = len(shape)
    return pl.BlockSpec(shape, lambda *_: (0,) * nd, pipeline_mode=pl.Buffered(1))


def _mod_kernel(c_ref, w_ref, b_ref, o_ref):
    s = _silu(c_ref[...])
    o_ref[...] = _dot(s.astype(BF16), w_ref[...].astype(BF16)) + b_ref[...]


def _mod_call(c8, w_ada, b_ada):
    d, n = w_ada.shape
    tn = 1024
    return pl.pallas_call(
        _mod_kernel,
        grid=(n // tn,),
        in_specs=[pl.BlockSpec((8, d), lambda j: (0, 0)),
                  pl.BlockSpec((d, tn), lambda j: (0, j)),
                  pl.BlockSpec((1, tn), lambda j: (0, j))],
        out_specs=pl.BlockSpec((8, tn), lambda j: (0, j)),
        out_shape=jax.ShapeDtypeStruct((8, n), F32),
        name="mod",
    )(c8, w_ada, b_ada)


def _bucket_table():
    qi = np.arange(BLOCK)[:, None]
    kj = np.arange(2 * BLOCK)[None, :]
    dist = qi + BLOCK - kj
    n = np.maximum(dist, 0)
    max_exact = REL_BUCKETS // 2
    nf = np.maximum(n, 1).astype(np.float32)
    large = max_exact + (np.log(nf / max_exact) / math.log(REL_MAX_DIST / max_exact)
                         * (REL_BUCKETS - max_exact)).astype(np.int32)
    large = np.minimum(large, REL_BUCKETS - 1)
    bucket = np.where(n < max_exact, n, large)
    valid = (dist >= 0) & (dist < BLOCK)
    return np.where(valid, bucket, -1).astype(np.int32)


def _bias_kernel(rb_ref, idx_ref, o_ref):
    kv = pl.program_id(0)
    idx = idx_ref[...]
    key_row = lax.broadcasted_iota(jnp.int32, idx.shape, 0)
    keep = idx >= 0
    keep_first = jnp.logical_and(keep, key_row >= BLOCK)
    for s in range(2):
        for c in range(2):
            acc = jnp.zeros(idx.shape, F32)
            for b in range(REL_BUCKETS):
                acc = jnp.where(idx == b, rb_ref[b, kv * GROUP + 2 * c + s], acc)
            acc = acc * LOG2E
            rows, cols = slice(s * 2 * BLOCK, (s + 1) * 2 * BLOCK), slice(c * BLOCK, (c + 1) * BLOCK)
            o_ref[0, 0, rows, cols] = jnp.where(keep, acc, NEG)
            o_ref[1, 0, rows, cols] = jnp.where(keep_first, acc, NEG)


def _bias_call(rel_bias):
    idx_t = jnp.asarray(np.ascontiguousarray(_bucket_table().T))
    table = pl.pallas_call(
        _bias_kernel,
        grid=(ATTN_KV_HEADS,),
        in_specs=[pl.BlockSpec(memory_space=pltpu.SMEM),
                  pl.BlockSpec((2 * BLOCK, BLOCK), lambda h: (0, 0))],
        out_specs=pl.BlockSpec((2, 1, 4 * BLOCK, 2 * BLOCK), lambda h: (0, h, 0, 0)),
        out_shape=jax.ShapeDtypeStruct((2, ATTN_KV_HEADS, 4 * BLOCK, 2 * BLOCK), F32),
        name="bias",
    )(rel_bias, idx_t)
    return table.reshape(2 * ATTN_KV_HEADS, 4 * BLOCK, 2 * BLOCK)


def _wprep_kernel(wt_ref, o_ref):
    o_ref[...] = wt_ref[...].T.astype(BF16)


def _wprep_call(w_t, dt_start, dt_width):
    n, k = w_t.shape
    tn = 512
    head_tiles, tail_tiles = dt_start // tn, (n - dt_start - dt_width) // tn
    assert head_tiles * tn == dt_start and tail_tiles * tn == n - dt_start - dt_width
    tail_start = dt_start + dt_width

    assert tail_start % 8 == 0

    def rows(i):
        start = jnp.where(i < head_tiles, i * tn, tail_start + (i - head_tiles) * tn)
        return (pl.multiple_of(start, 8), 0)

    return pl.pallas_call(
        _wprep_kernel,
        grid=(head_tiles + tail_tiles,),
        in_specs=[pl.BlockSpec((pl.Element(tn), pl.Element(k)), rows)],
        out_specs=pl.BlockSpec((k, tn), lambda i: (0, i)),
        out_shape=jax.ShapeDtypeStruct((k, n - dt_width), BF16),
        compiler_params=pltpu.CompilerParams(dimension_semantics=("parallel",)),
        name="wprep",
    )(w_t)


def _inproj_kernel(x_ref, shift_ref, scale_ref, nw_ref, w_ref, wdt_ref, dtb_ref, cw_ref, cb_ref,
                   qkv_o, za_o, zm_o, xbc_o, g_o, dt_o, xraw_s, carry_s):
    ts = x_ref.shape[0]
    halo = carry_s.shape[1]

    @pl.when(pl.program_id(1) == 0)
    def _():
        carry_s[...] = jnp.zeros(carry_s.shape, F32)

    x = x_ref[...]
    ms = jnp.mean(x * x, axis=-1, keepdims=True)
    xn = x * lax.rsqrt(ms + EPS) * nw_ref[...]
    h = (xn * (1.0 + scale_ref[0]) + shift_ref[0]).astype(BF16)

    def conv_silu_store(y, c0, slot):
        for jj in range(y.shape[1] // LANES):
            j = c0 // LANES + jj
            jl = slice(j * LANES, (j + 1) * LANES)
            yj = y[:, jj * LANES:(jj + 1) * LANES]
            xraw_s[slot, jj, 0:halo, :] = carry_s[j]
            xraw_s[slot, jj, halo:halo + ts, :] = yj
            carry_s[j] = yj[ts - halo:ts, :]
            acc = cb_ref[:, jl]
            for w in range(CONV_WIDTH):
                off = halo - (CONV_WIDTH - 1) + w
                acc = acc + cw_ref[w:w + 1, jl] * xraw_s[slot, jj, off:off + ts, :]
            xbc_o[:, jl] = (acc * (1.0 / (1.0 + jnp.exp2(acc * (-LOG2E))))).astype(BF16)

    step = xraw_s.shape[1] * LANES
    light, heavy, col = [], [], 0
    for o_ref, act in ((qkv_o, None), (za_o, _silu), (zm_o, _silu), (xbc_o, "conv"), (g_o, _sigmoid)):
        for c0 in range(0, o_ref.shape[1], step):
            (heavy if act == "conv" else light).append((o_ref, act, col + c0, c0))
        col += o_ref.shape[1]
    order = []
    while light or heavy:
        if light:
            order.append(light.pop(0))
        if heavy:
            order.append(heavy.pop(0))
    slot = 0
    for o_ref, act, wc, c0 in order:
        y = _dot(h, w_ref[:, wc:wc + step])
        if act == "conv":
            conv_silu_store(y, c0, slot)
            slot = 1 - slot
        else:
            o_ref[:, c0:c0 + step] = (y if act is None else act(y)).astype(o_ref.dtype)
    dt_o[...] = _softplus(_dot_nt(h, wdt_ref[...]) + dtb_ref[...])


def _inproj_call(x2, shift, scale, norm_w, w_all, wdt, dtb, cw, cb, widths, batch, seq):
    t, d = x2.shape
    ts = min(ROWS_INPROJ, seq)
    ns = seq // ts
    row = lambda b, i: (b * ns + i, 0)
    per_b = pl.BlockSpec((1, 1, d), lambda b, i: (b, 0, 0))
    widths = list(widths) + [wdt.shape[0]]
    dtypes = [BF16] * (len(widths) - 1) + [F32]
    piece_blocks = 4
    return pl.pallas_call(
        _inproj_kernel,
        grid=(batch, ns),
        in_specs=[pl.BlockSpec((ts, d), row), per_b, per_b, _resident((1, d)),
                  _resident(w_all.shape), _resident(wdt.shape), _resident(dtb.shape),
                  _resident(cw.shape), _resident(cb.shape)],
        out_specs=[pl.BlockSpec((ts, n), row) for n in widths],
        out_shape=[jax.ShapeDtypeStruct((t, n), dt) for n, dt in zip(widths, dtypes)],
        scratch_shapes=[pltpu.VMEM((2, piece_blocks, ts + 8, LANES), F32),
                        pltpu.VMEM((cw.shape[1] // LANES, 8, LANES), F32)],
        compiler_params=pltpu.CompilerParams(
            dimension_semantics=("arbitrary", "arbitrary"), vmem_limit_bytes=VMEM_LIMIT),
        name="inproj",
    )(x2, shift, scale, norm_w, w_all, wdt, dtb, cw, cb)


def _segment_rms(x, bd, w):
    ms = _dot((x * x).astype(BF16), bd)
    return x * lax.rsqrt(ms + EPS) * w


def _branches_kernel(q_ref, kvc_ref, kvp_ref, za_ref, ga_ref, bias_ref, sink_ref,
                     qw_ref, kw_ref, bd_ref, wap_ref,
                     xbc_ref, dt_ref, zm_ref, alog_ref, dsk_ref, nw_ref, tri_ref,
                     gb_ref, x_ref, gate_ref, wsp_ref, wout_ref,
                     o_ref,
                     qn_s, klo_s, khi_s, vt_s, s_s, p_s, yt_s, ya_s, h_s, yn_s, yprev_s, merged_s):
    rows = q_ref.shape[0]
    nblk = rows // BLOCK
    nchunk = rows // CHUNK
    kvw = ATTN_KV_HEADS * HEAD_DIM
    sw = zm_ref.shape[1]
    gw = sw // SSM_GROUPS
    pairs_per_group = gw // LANES
    n_xs = sw // LANES
    first_tile = pl.program_id(1) == 0

    @pl.when(first_tile)
    def _():
        h_s[...] = jnp.zeros(h_s.shape, F32)
        yprev_s[...] = jnp.zeros(yprev_s.shape, BF16)
        ya_s[...] = jnp.zeros(ya_s.shape, BF16)

    d_out = o_ref.shape[1]
    piece_w = 2 * LANES

    def merge_piece(c0):
        cs = slice(c0, c0 + piece_w)
        yb = _dot(yprev_s[...], wsp_ref[:, cs])
        merged_s[:, cs] = ya_s[:, cs] + gb_ref[:, cs] * yb.astype(BF16)

    def out_piece(c0):
        cs = slice(c0, c0 + piece_w)
        o = _dot(merged_s[...], wout_ref[:, cs])
        o_ref[:, cs] = x_ref[:, cs] + gate_ref[0][:, cs] * o

    finish = ([functools.partial(merge_piece, c0) for c0 in range(0, d_out, piece_w)]
              + [functools.partial(out_piece, c0) for c0 in range(0, d_out, piece_w)])

    bd = bd_ref[...]
    qw = qw_ref[...] * (HEAD_DIM ** -0.5 * LOG2E)
    kw = kw_ref[...]
    low_lanes = lax.broadcasted_iota(jnp.int32, (1, LANES), 1) < HEAD_DIM

    for j in range(q_ref.shape[1] // LANES):
        sl = slice(j * LANES, (j + 1) * LANES)
        qn_s[:, sl] = _segment_rms(q_ref[:, sl].astype(F32), bd, qw).astype(BF16)
    for j in range(kvw // LANES):
        sl = slice(j * LANES, (j + 1) * LANES)
        even = slice(2 * j * LANES, (2 * j + 1) * LANES)
        odd = slice((2 * j + 1) * LANES, (2 * j + 2) * LANES)
        for dst, src in ((slice(0, BLOCK), kvp_ref), (slice(BLOCK, BLOCK + rows), kvc_ref)):
            kn = _segment_rms(src[:, sl].astype(F32), bd, kw)
            lo = jnp.where(low_lanes, kn, 0.0)
            hi = jnp.where(low_lanes, 0.0, kn)
            klo_s[dst, even] = lo.astype(BF16)
            khi_s[dst, odd] = hi.astype(BF16)
            khi_s[dst, even] = pltpu.roll(lo, HEAD_DIM, axis=1).astype(BF16)
            klo_s[dst, odd] = pltpu.roll(hi, HEAD_DIM, axis=1).astype(BF16)
    vt_s[:, 0:BLOCK] = kvp_ref[:, kvw:2 * kvw].astype(F32).T.astype(BF16)
    vt_s[:, BLOCK:] = kvc_ref[:, kvw:2 * kvw].astype(F32).T.astype(BF16)

    units = [(qb, h) for qb in range(nblk) for h in range(ATTN_KV_HEADS)]

    def scores(u):
        qb, h = units[u]
        r0 = qb * BLOCK
        hl = slice(h * LANES, (h + 1) * LANES)
        k2 = jnp.concatenate([klo_s[r0:r0 + 2 * BLOCK, hl], khi_s[r0:r0 + 2 * BLOCK, hl]], axis=0)
        q2 = jnp.concatenate([qn_s[r0:r0 + BLOCK, (2 * h + c) * LANES:(2 * h + c + 1) * LANES]
                              for c in range(2)], axis=0)
        entry = jnp.where(first_tile, ATTN_KV_HEADS, 0) + h if qb == 0 else h
        s_s[u % 2] = _dot_nt(k2, q2) + bias_ref[entry]

    def softmax_pv(u):
        qb, h = units[u]
        r0 = qb * BLOCK
        slot = u % 2
        inv = []
        for s in range(2):
            for c in range(2):
                blk = s_s[slot, s * 2 * BLOCK:(s + 1) * 2 * BLOCK, c * BLOCK:(c + 1) * BLOCK]
                sink = sink_ref[h, s:s + 1, c * BLOCK:(c + 1) * BLOCK] * LOG2E
                m = jnp.maximum(jnp.max(blk, axis=0, keepdims=True), sink)
                e = jnp.exp2(blk - m)
                den = jnp.sum(e, axis=0, keepdims=True) + jnp.exp2(sink - m)
                p_s[slot, :, (2 * s + c) * BLOCK:(2 * s + c + 1) * BLOCK] = e.astype(BF16)
                inv.append(1.0 / den)
        o_t = _dot(vt_s[h * HEAD_DIM:(h + 1) * HEAD_DIM, r0:r0 + 2 * BLOCK], p_s[slot])
        for s in range(2):
            for c in range(2):
                head = GROUP * h + 2 * c + s
                k = 2 * s + c
                yt_s[head * HEAD_DIM:(head + 1) * HEAD_DIM, r0:r0 + BLOCK] = (
                    o_t[:, k * BLOCK:(k + 1) * BLOCK] * inv[k])

    scores(0)
    attention = []
    for u in range(len(units)):
        def unit(u=u):
            if u + 1 < len(units):
                scores(u + 1)
            softmax_pv(u)
        attention.append(unit)

    def lane_block(j, r0):
        return xbc_ref[r0:r0 + CHUNK, j * LANES:(j + 1) * LANES]

    a_row = -jnp.exp(alog_ref[...])
    tri = tri_ref[...]
    li = lax.broadcasted_iota(jnp.int32, (CHUNK, CHUNK), 0)
    si = lax.broadcasted_iota(jnp.int32, (CHUNK, CHUNK), 1)
    causal = li >= si
    low_half = si < SSM_HEAD_DIM
    lo_mask = (lax.broadcasted_iota(jnp.int32, (1, LANES), 1) < SSM_HEAD_DIM).astype(BF16)
    hi_mask = 1 - lo_mask

    for c in range(nchunk):
        r0 = c * CHUNK
        dt = dt_ref[r0:r0 + CHUNK, :]
        dta = dt * a_row
        dta_hi = dta.astype(BF16)
        dta_lo = (dta - dta_hi.astype(F32)).astype(BF16)
        a2 = (_dot(tri, dta_hi) + _dot(tri, dta_lo)) * LOG2E
        a2_t = a2.T
        dt_t = dt.T
        row2_t = a2_t - jnp.log2(dt_t)
        a_end2_t = a2_t[:, CHUNK - 1:CHUNK]
        w_t = dt_t * jnp.exp2(a_end2_t - a2_t)
        e_end_t = jnp.exp2(a_end2_t)

        for g in range(SSM_GROUPS):
            if finish:
                finish.pop(0)()
            if attention:
                attention.pop(0)()
            bm_gb = lane_block(n_xs + g, r0)
            cm_gb = lane_block(n_xs + SSM_GROUPS + g, r0)
            cb = _dot_nt(cm_gb, bm_gb)
            bm_gt = bm_gb.astype(F32).T
            gsl = slice(g * gw, (g + 1) * gw)
            hprev = h_s[g]
            y_off = _dot(cm_gb, hprev.astype(BF16))
            y_parts = []
            for p in range(pairs_per_group):
                pair = g * pairs_per_group + p
                lsl = slice(pair * LANES, (pair + 1) * LANES)
                xs_b = lane_block(pair, r0)
                xs_pair = xs_b.astype(F32)
                rhs = jnp.concatenate([xs_b * lo_mask, xs_b * hi_mask], axis=0)
                m_parts, b_parts, cols, e_ends = [], [], [], []
                for side in range(2):
                    hh = 2 * pair + side
                    col2 = jnp.broadcast_to(a2[:, hh:hh + 1], (CHUNK, CHUNK))
                    dec_dt = jnp.exp2(jnp.where(causal, col2 - row2_t[hh:hh + 1, :], NEG))
                    m_parts.append(cb * dec_dt)
                    b_parts.append(bm_gt * w_t[hh:hh + 1, :])
                    cols.append(col2)
                    e_ends.append(e_end_t[hh:hh + 1, :])
                lhs = jnp.concatenate(
                    [jnp.concatenate(m_parts, axis=1), jnp.concatenate(b_parts, axis=1)],
                    axis=0).astype(BF16)
                res = _dot(lhs, rhs)
                e_col = jnp.exp2(jnp.where(low_half, cols[0], cols[1]))
                y_pair = res[0:CHUNK] + e_col * y_off[:, p * LANES:(p + 1) * LANES]
                y_pair = y_pair + dsk_ref[:, lsl] * xs_pair
                y_parts.append(y_pair * zm_ref[r0:r0 + CHUNK, lsl].astype(F32))
                e_end = jnp.where(low_half, e_ends[0], e_ends[1])
                h_s[g, :, p * LANES:(p + 1) * LANES] = (
                    hprev[:, p * LANES:(p + 1) * LANES] * e_end + res[CHUNK:])
            yg = jnp.concatenate(y_parts, axis=1)
            ms = jnp.mean(yg * yg, axis=-1, keepdims=True)
            yn_s[r0:r0 + CHUNK, gsl] = (yg * lax.rsqrt(ms + EPS) * nw_ref[:, gsl]).astype(BF16)

    while finish:
        finish.pop(0)()
    while attention:
        attention.pop(0)()
    yprev_s[...] = yn_s[...]

    y = (yt_s[...].T * za_ref[...].astype(F32)).astype(BF16)
    ya = _dot(y, wap_ref[...])
    ya_s[...] = (ya * ga_ref[...].astype(F32)).astype(BF16)


def _branches_call(qkv, za, g, bias, sink_rows, qw, kw, bd, wap,
                   xbc, dt, zm, alog, dsk, nw, tri, x2, gate, wsp, wout, batch, seq):
    t, d = x2.shape
    aw = ATTN_HEADS * HEAD_DIM
    kvw2 = 2 * ATTN_KV_HEADS * HEAD_DIM
    sw = zm.shape[1]
    rows = min(ROWS_BRANCHES, seq)
    nt = seq // rows
    bpt = rows // BLOCK
    tile = lambda b, i: b * nt + jnp.minimum(i, nt - 1)
    scan = lambda b, i: (tile(b, i), 0)
    done = lambda b, i: (b * nt + jnp.maximum(i - 1, 0), 0)
    return pl.pallas_call(
        _branches_kernel,
        grid=(batch, nt + 1),
        in_specs=[pl.BlockSpec((rows, aw), scan),
                  pl.BlockSpec((rows, kvw2), lambda b, i: (tile(b, i), aw // kvw2)),
                  pl.BlockSpec((BLOCK, kvw2),
                               lambda b, i: (jnp.maximum(tile(b, i) * bpt - 1, 0), aw // kvw2)),
                  pl.BlockSpec((rows, aw), scan),
                  pl.BlockSpec((rows, aw), scan),
                  _resident(bias.shape), _resident(sink_rows.shape),
                  _resident(qw.shape), _resident(kw.shape),
                  _resident(bd.shape), _resident(wap.shape),
                  pl.BlockSpec((rows, xbc.shape[1]), scan),
                  pl.BlockSpec((rows, dt.shape[1]), scan),
                  pl.BlockSpec((rows, sw), scan),
                  _resident(alog.shape), _resident(dsk.shape), _resident(nw.shape),
                  _resident(tri.shape),
                  pl.BlockSpec((rows, d), lambda b, i: (b * nt + jnp.maximum(i - 1, 0), 1)),
                  pl.BlockSpec((rows, d), done),
                  pl.BlockSpec((1, 1, d), lambda b, i: (b, 0, 0)),
                  _resident(wsp.shape), _resident(wout.shape)],
        out_specs=pl.BlockSpec((rows, d), done),
        out_shape=jax.ShapeDtypeStruct((t, d), F32),
        scratch_shapes=[pltpu.VMEM((rows, aw), BF16),
                        pltpu.VMEM((rows + BLOCK, ATTN_KV_HEADS * LANES), BF16),
                        pltpu.VMEM((rows + BLOCK, ATTN_KV_HEADS * LANES), BF16),
                        pltpu.VMEM((kvw2 // 2, rows + BLOCK), BF16),
                        pltpu.VMEM((2, 4 * BLOCK, 2 * BLOCK), F32),
                        pltpu.VMEM((2, 2 * BLOCK, 4 * BLOCK), BF16),
                        pltpu.VMEM((aw, rows), F32),
                        pltpu.VMEM((rows, d), BF16),
                        pltpu.VMEM((SSM_GROUPS, SSM_STATE, sw // SSM_GROUPS), F32),
                        pltpu.VMEM((rows, sw), BF16),
                        pltpu.VMEM((rows, sw), BF16),
                        pltpu.VMEM((rows, d), BF16)],
        compiler_params=pltpu.CompilerParams(
            dimension_semantics=("arbitrary", "arbitrary"), vmem_limit_bytes=VMEM_LIMIT),
        name="branches",
    )(qkv, qkv, qkv, za, g, bias, sink_rows, qw, kw, bd, wap,
      xbc, dt, zm, alog, dsk, nw, tri, g, x2, gate, wsp, wout)


def _layer(x, c, w_ada, b_ada, norm_w, w_in, q_norm_w, k_norm_w, rel_bias, sinks,
           conv_w, conv_b, dt_bias, a_log, d_skip, ssm_norm_w, w_attn_proj, w_ssm_proj, w_out):
    batch, seq, d = x.shape
    aw = ATTN_HEADS * HEAD_DIM
    kvw = ATTN_KV_HEADS * HEAD_DIM
    sw = w_ssm_proj.shape[0]
    ssm_heads = sw // SSM_HEAD_DIM
    xbc_w = sw + 2 * SSM_GROUPS * SSM_STATE
    assert seq % BLOCK == 0 and seq % CHUNK == 0 and ssm_heads <= LANES

    c8 = jnp.zeros((8, d), F32).at[:batch].set(c)
    mod = _mod_call(c8, w_ada, b_ada.reshape(1, -1))[:batch]
    shift, scale, gate = (mod[:, k * d:(k + 1) * d].reshape(batch, 1, d) for k in range(3))

    o = np.cumsum([0, aw, kvw, kvw, aw, sw, xbc_w, ssm_heads, d, d])
    w_t = w_in.T
    w_all = _wprep_call(w_t, int(o[6]), ssm_heads)
    wdt = jnp.pad(w_t[o[6]:o[7]], ((0, LANES - ssm_heads), (0, 0))).astype(BF16)
    widths = [aw + 2 * kvw, aw, sw, xbc_w, 2 * d]
    pad_heads = lambda v: jnp.pad(v.astype(F32), (0, LANES - ssm_heads)).reshape(1, LANES)

    x2 = x.reshape(batch * seq, d)
    qkv, za, zm, xbc, g, dt = _inproj_call(
        x2, shift, scale, norm_w.reshape(1, d), w_all, wdt, pad_heads(dt_bias),
        conv_w.astype(F32), conv_b.reshape(1, -1).astype(F32), widths, batch, seq)

    bias = _bias_call(rel_bias.astype(F32))
    seg = np.arange(LANES) // HEAD_DIM
    bd = jnp.asarray((seg[:, None] == seg[None, :]).astype(np.float32) / HEAD_DIM, dtype=BF16)
    qw = jnp.tile(q_norm_w.astype(F32), LANES // HEAD_DIM).reshape(1, LANES)
    kw = jnp.tile(k_norm_w.astype(F32), LANES // HEAD_DIM).reshape(1, LANES)
    sink_rows = jnp.repeat(sinks.astype(F32).reshape(ATTN_KV_HEADS, 2, 2).transpose(0, 2, 1), BLOCK, axis=2)
    tri = jnp.asarray(np.tril(np.ones((CHUNK, CHUNK), np.float32)), dtype=BF16)
    out = _branches_call(qkv, za, g, bias, sink_rows, qw, kw, bd, w_attn_proj.astype(BF16),
                         xbc, dt, zm, pad_heads(a_log),
                         jnp.repeat(d_skip.astype(F32), SSM_HEAD_DIM).reshape(1, sw),
                         ssm_norm_w.reshape(1, sw).astype(F32), tri,
                         x2, gate, w_ssm_proj.astype(BF16), w_out.astype(BF16), batch, seq)
    return out.reshape(batch, seq, d)


def kernel(x, c, w_ada, b_ada, norm_w, w_in, q_norm_w, k_norm_w, rel_bias, sinks, conv_w, conv_b,
           dt_bias, a_log, d_skip, ssm_norm_w, w_attn_proj, w_ssm_proj, w_out):
    depth = w_in.shape[0]
    for i in range(depth):
        x = _layer(x, c, w_ada[i], b_ada[i], norm_w[i], w_in[i], q_norm_w[i], k_norm_w[i],
                   rel_bias, sinks[i], conv_w[i], conv_b[i], dt_bias[i], a_log[i], d_skip[i],
                   ssm_norm_w[i], w_attn_proj[i], w_ssm_proj[i], w_out[i])
    return x
```

```python
import functools
import math

import numpy as np
import jax
import jax.numpy as jnp
from jax import lax
from jax.experimental import pallas as pl
from jax.experimental.pallas import tpu as pltpu

F32 = jnp.float32
BF16 = jnp.bfloat16

HEAD_DIM = 64
ATTN_HEADS = 16
ATTN_KV_HEADS = 4
GROUP = ATTN_HEADS // ATTN_KV_HEADS
BLOCK = 128
REL_BUCKETS = 32
REL_MAX_DIST = 128
SSM_HEAD_DIM = 64
SSM_GROUPS = 4
SSM_STATE = 128
CONV_WIDTH = 4
CHUNK = 128
EPS = 1e-6
NEG = -1e30
LOG2E = 1.4426950408889634
LANES = 128
VMEM_LIMIT = 56 * 1024 * 1024

ROWS_INPROJ = 512
ROWS_BRANCHES = 512


def _sigmoid(x):
    return 1.0 / (1.0 + jnp.exp(-x))


def _silu(x):
    return x * _sigmoid(x)


def _softplus(x):
    return jnp.maximum(x, 0.0) + jnp.log(1.0 + jnp.exp(-jnp.abs(x)))


def _dot(a, b):
    return jnp.dot(a, b, preferred_element_type=F32)


def _dot_nt(a, b):
    return lax.dot_general(a, b, (((1,), (1,)), ((), ())), preferred_element_type=F32)


def _resident(shape):
    nd = len(shape)
    return pl.BlockSpec(shape, lambda *_: (0,) * nd, pipeline_mode=pl.Buffered(1))


def _mod_kernel(c_ref, w_ref, b_ref, o_ref):
    s = _silu(c_ref[...])
    o_ref[...] = _dot(s.astype(BF16), w_ref[...].astype(BF16)) + b_ref[...]


def _mod_call(c8, w_ada, b_ada):
    d, n = w_ada.shape
    tn = 1024
    return pl.pallas_call(
        _mod_kernel,
        grid=(n // tn,),
        in_specs=[pl.BlockSpec((8, d), lambda j: (0, 0)),
                  pl.BlockSpec((d, tn), lambda j: (0, j)),
                  pl.BlockSpec((1, tn), lambda j: (0, j))],
        out_specs=pl.BlockSpec((8, tn), lambda j: (0, j)),
        out_shape=jax.ShapeDtypeStruct((8, n), F32),
        name="mod",
    )(c8, w_ada, b_ada)


def _bucket_table():
    qi = np.arange(BLOCK)[:, None]
    kj = np.arange(2 * BLOCK)[None, :]
    dist = qi + BLOCK - kj
    n = np.maximum(dist, 0)
    max_exact = REL_BUCKETS // 2
    nf = np.maximum(n, 1).astype(np.float32)
    large = max_exact + (np.log(nf / max_exact) / math.log(REL_MAX_DIST / max_exact)
                         * (REL_BUCKETS - max_exact)).astype(np.int32)
    large = np.minimum(large, REL_BUCKETS - 1)
    bucket = np.where(n < max_exact, n, large)
    valid = (dist >= 0) & (dist < BLOCK)
    return np.where(valid, bucket, -1).astype(np.int32)


def _bias_kernel(rb_ref, idx_ref, o_ref):
    kv = pl.program_id(0)
    idx = idx_ref[...]
    key_row = lax.broadcasted_iota(jnp.int32, idx.shape, 0)
    keep = idx >= 0
    keep_first = jnp.logical_and(keep, key_row >= BLOCK)
    for s in range(2):
        for c in range(2):
            acc = jnp.zeros(idx.shape, F32)
            for b in range(REL_BUCKETS):
                acc = jnp.where(idx == b, rb_ref[b, kv * GROUP + 2 * c + s], acc)
            acc = acc * LOG2E
            rows, cols = slice(s * 2 * BLOCK, (s + 1) * 2 * BLOCK), slice(c * BLOCK, (c + 1) * BLOCK)
            o_ref[0, 0, rows, cols] = jnp.where(keep, acc, NEG)
            o_ref[1, 0, rows, cols] = jnp.where(keep_first, acc, NEG)


def _bias_call(rel_bias):
    idx_t = jnp.asarray(np.ascontiguousarray(_bucket_table().T))
    table = pl.pallas_call(
        _bias_kernel,
        grid=(ATTN_KV_HEADS,),
        in_specs=[pl.BlockSpec(memory_space=pltpu.SMEM),
                  pl.BlockSpec((2 * BLOCK, BLOCK), lambda h: (0, 0))],
        out_specs=pl.BlockSpec((2, 1, 4 * BLOCK, 2 * BLOCK), lambda h: (0, h, 0, 0)),
        out_shape=jax.ShapeDtypeStruct((2, ATTN_KV_HEADS, 4 * BLOCK, 2 * BLOCK), F32),
        name="bias",
    )(rel_bias, idx_t)
    return table.reshape(2 * ATTN_KV_HEADS, 4 * BLOCK, 2 * BLOCK)


def _wprep_kernel(wt_ref, o_ref):
    o_ref[...] = wt_ref[...].T.astype(BF16)


def _wprep_call(w_t, dt_start, dt_width):
    n, k = w_t.shape
    tn = 512
    head_tiles, tail_tiles = dt_start // tn, (n - dt_start - dt_width) // tn
    assert head_tiles * tn == dt_start and tail_tiles * tn == n - dt_start - dt_width
    tail_start = dt_start + dt_width

    assert tail_start % 8 == 0

    def rows(i):
        start = jnp.where(i < head_tiles, i * tn, tail_start + (i - head_tiles) * tn)
        return (pl.multiple_of(start, 8), 0)

    return pl.pallas_call(
        _wprep_kernel,
        grid=(head_tiles + tail_tiles,),
        in_specs=[pl.BlockSpec((pl.Element(tn), pl.Element(k)), rows)],
        out_specs=pl.BlockSpec((k, tn), lambda i: (0, i)),
        out_shape=jax.ShapeDtypeStruct((k, n - dt_width), BF16),
        compiler_params=pltpu.CompilerParams(dimension_semantics=("parallel",)),
        name="wprep",
    )(w_t)


def _inproj_kernel(x_ref, shift_ref, scale_ref, nw_ref, w_ref, wdt_ref, dtb_ref, cw_ref, cb_ref,
                   qkv_o, za_o, zm_o, xbc_o, g_o, dt_o, xraw_s, carry_s):
    ts = x_ref.shape[0]
    halo = carry_s.shape[1]

    @pl.when(pl.program_id(1) == 0)
    def _():
        carry_s[...] = jnp.zeros(carry_s.shape, F32)

    x = x_ref[...]
    ms = jnp.mean(x * x, axis=-1, keepdims=True)
    xn = x * lax.rsqrt(ms + EPS) * nw_ref[...]
    h = (xn * (1.0 + scale_ref[0]) + shift_ref[0]).astype(BF16)

    def conv_silu_store(y, c0, slot):
        for jj in range(y.shape[1] // LANES):
            j = c0 // LANES + jj
            jl = slice(j * LANES, (j + 1) * LANES)
            yj = y[:, jj * LANES:(jj + 1) * LANES]
            xraw_s[slot, jj, 0:halo, :] = carry_s[j]
            xraw_s[slot, jj, halo:halo + ts, :] = yj
            carry_s[j] = yj[ts - halo:ts, :]
            acc = cb_ref[:, jl]
            for w in range(CONV_WIDTH):
                off = halo - (CONV_WIDTH - 1) + w
                acc = acc + cw_ref[w:w + 1, jl] * xraw_s[slot, jj, off:off + ts, :]
            xbc_o[:, jl] = (acc * (1.0 / (1.0 + jnp.exp2(acc * (-LOG2E))))).astype(BF16)

    step = xraw_s.shape[1] * LANES
    light, heavy, col = [], [], 0
    for o_ref, act in ((qkv_o, None), (za_o, _silu), (zm_o, _silu), (xbc_o, "conv"), (g_o, _sigmoid)):
        for c0 in range(0, o_ref.shape[1], step):
            (heavy if act == "conv" else light).append((o_ref, act, col + c0, c0))
        col += o_ref.shape[1]
    order = []
    while light or heavy:
        if light:
            order.append(light.pop(0))
        if heavy:
            order.append(heavy.pop(0))
    slot = 0
    for o_ref, act, wc, c0 in order:
        y = _dot(h, w_ref[:, wc:wc + step])
        if act == "conv":
            conv_silu_store(y, c0, slot)
            slot = 1 - slot
        else:
            o_ref[:, c0:c0 + step] = (y if act is None else act(y)).astype(o_ref.dtype)
    dt_o[...] = _softplus(_dot_nt(h, wdt_ref[...]) + dtb_ref[...])


def _inproj_call(x2, shift, scale, norm_w, w_all, wdt, dtb, cw, cb, widths, batch, seq):
    t, d = x2.shape
    ts = min(ROWS_INPROJ, seq)
    ns = seq // ts
    row = lambda b, i: (b * ns + i, 0)
    per_b = pl.BlockSpec((1, 1, d), lambda b, i: (b, 0, 0))
    widths = list(widths) + [wdt.shape[0]]
    dtypes = [BF16] * (len(widths) - 1) + [F32]
    piece_blocks = 4
    return pl.pallas_call(
        _inproj_kernel,
        grid=(batch, ns),
        in_specs=[pl.BlockSpec((ts, d), row), per_b, per_b, _resident((1, d)),
                  _resident(w_all.shape), _resident(wdt.shape), _resident(dtb.shape),
                  _resident(cw.shape), _resident(cb.shape)],
        out_specs=[pl.BlockSpec((ts, n), row) for n in widths],
        out_shape=[jax.ShapeDtypeStruct((t, n), dt) for n, dt in zip(widths, dtypes)],
        scratch_shapes=[pltpu.VMEM((2, piece_blocks, ts + 8, LANES), F32),
                        pltpu.VMEM((cw.shape[1] // LANES, 8, LANES), F32)],
        compiler_params=pltpu.CompilerParams(
            dimension_semantics=("arbitrary", "arbitrary"), vmem_limit_bytes=VMEM_LIMIT),
        name="inproj",
    )(x2, shift, scale, norm_w, w_all, wdt, dtb, cw, cb)


def _segment_rms(x, bd, w):
    ms = _dot((x * x).astype(BF16), bd)
    return x * lax.rsqrt(ms + EPS) * w


def _branches_kernel(q_ref, kvc_ref, kvp_ref, za_ref, ga_ref, bias_ref, sink_ref,
                     qw_ref, kw_ref, bd_ref, wap_ref,
                     xbc_ref, dt_ref, zm_ref, alog_ref, dsk_ref, nw_ref, tri_ref,
                     gb_ref, x_ref, gate_ref, wsp_ref, wout_ref,
                     o_ref,
                     qn_s, klo_s, khi_s, vt_s, s_s, p_s, yt_s, ya_s, h_s, yn_s, yprev_s, merged_s):
    rows = q_ref.shape[0]
    nblk = rows // BLOCK
    nchunk = rows // CHUNK
    kvw = ATTN_KV_HEADS * HEAD_DIM
    sw = zm_ref.shape[1]
    gw = sw // SSM_GROUPS
    pairs_per_group = gw // LANES
    n_xs = sw // LANES
    first_tile = pl.program_id(1) == 0

    @pl.when(first_tile)
    def _():
        h_s[...] = jnp.zeros(h_s.shape, F32)
        yprev_s[...] = jnp.zeros(yprev_s.shape, BF16)
        ya_s[...] = jnp.zeros(ya_s.shape, BF16)

    d_out = o_ref.shape[1]
    piece_w = 2 * LANES

    def merge_piece(c0):
        cs = slice(c0, c0 + piece_w)
        yb = _dot(yprev_s[...], wsp_ref[:, cs])
        merged_s[:, cs] = ya_s[:, cs] + gb_ref[:, cs] * yb.astype(BF16)

    def out_piece(c0):
        cs = slice(c0, c0 + piece_w)
        o = _dot(merged_s[...], wout_ref[:, cs])
        o_ref[:, cs] = x_ref[:, cs] + gate_ref[0][:, cs] * o

    finish = ([functools.partial(merge_piece, c0) for c0 in range(0, d_out, piece_w)]
              + [functools.partial(out_piece, c0) for c0 in range(0, d_out, piece_w)])

    bd = bd_ref[...]
    qw = qw_ref[...] * (HEAD_DIM ** -0.5 * LOG2E)
    kw = kw_ref[...]
    low_lanes = lax.broadcasted_iota(jnp.int32, (1, LANES), 1) < HEAD_DIM

    for j in range(q_ref.shape[1] // LANES):
        sl = slice(j * LANES, (j + 1) * LANES)
        qn_s[:, sl] = _segment_rms(q_ref[:, sl].astype(F32), bd, qw).astype(BF16)
    for j in range(kvw // LANES):
        sl = slice(j * LANES, (j + 1) * LANES)
        even = slice(2 * j * LANES, (2 * j + 1) * LANES)
        odd = slice((2 * j + 1) * LANES, (2 * j + 2) * LANES)
        for dst, src in ((slice(0, BLOCK), kvp_ref), (slice(BLOCK, BLOCK + rows), kvc_ref)):
            kn = _segment_rms(src[:, sl].astype(F32), bd, kw)
            lo = jnp.where(low_lanes, kn, 0.0)
            hi = jnp.where(low_lanes, 0.0, kn)
            klo_s[dst, even] = lo.astype(BF16)
            khi_s[dst, odd] = hi.astype(BF16)
            khi_s[dst, even] = pltpu.roll(lo, HEAD_DIM, axis=1).astype(BF16)
            klo_s[dst, odd] = pltpu.roll(hi, HEAD_DIM, axis=1).astype(BF16)
    vt_s[:, 0:BLOCK] = kvp_ref[:, kvw:2 * kvw].astype(F32).T.astype(BF16)
    vt_s[:, BLOCK:] = kvc_ref[:, kvw:2 * kvw].astype(F32).T.astype(BF16)

    units = [(qb, h) for qb in range(nblk) for h in range(ATTN_KV_HEADS)]

    def scores(u):
        qb, h = units[u]
        r0 = qb * BLOCK
        hl = slice(h * LANES, (h + 1) * LANES)
        k2 = jnp.concatenate([klo_s[r0:r0 + 2 * BLOCK, hl], khi_s[r0:r0 + 2 * BLOCK, hl]], axis=0)
        q2 = jnp.concatenate([qn_s[r0:r0 + BLOCK, (2 * h + c) * LANES:(2 * h + c + 1) * LANES]
                              for c in range(2)], axis=0)
        entry = jnp.where(first_tile, ATTN_KV_HEADS, 0) + h if qb == 0 else h
        s_s[u % 2] = _dot_nt(k2, q2) + bias_ref[entry]

    def softmax_pv(u):
        qb, h = units[u]
        r0 = qb * BLOCK
        slot = u % 2
        inv = []
        for s in range(2):
            for c in range(2):
                blk = s_s[slot, s * 2 * BLOCK:(s + 1) * 2 * BLOCK, c * BLOCK:(c + 1) * BLOCK]
                sink = sink_ref[h, s:s + 1, c * BLOCK:(c + 1) * BLOCK] * LOG2E
                m = jnp.maximum(jnp.max(blk, axis=0, keepdims=True), sink)
                e = jnp.exp2(blk - m)
                den = jnp.sum(e, axis=0, keepdims=True) + jnp.exp2(sink - m)
                p_s[slot, :, (2 * s + c) * BLOCK:(2 * s + c + 1) * BLOCK] = e.astype(BF16)
                inv.append(1.0 / den)
        o_t = _dot(vt_s[h * HEAD_DIM:(h + 1) * HEAD_DIM, r0:r0 + 2 * BLOCK], p_s[slot])
        for s in range(2):
            for c in range(2):
                head = GROUP * h + 2 * c + s
                k = 2 * s + c
                yt_s[head * HEAD_DIM:(head + 1) * HEAD_DIM, r0:r0 + BLOCK] = (
                    o_t[:, k * BLOCK:(k + 1) * BLOCK] * inv[k])

    scores(0)
    attention = []
    for u in range(len(units)):
        def unit(u=u):
            if u + 1 < len(units):
                scores(u + 1)
            softmax_pv(u)
        attention.append(unit)

    def lane_block(j, r0):
        return xbc_ref[r0:r0 + CHUNK, j * LANES:(j + 1) * LANES]

    a_row = -jnp.exp(alog_ref[...])
    tri = tri_ref[...]
    li = lax.broadcasted_iota(jnp.int32, (CHUNK, CHUNK), 0)
    si = lax.broadcasted_iota(jnp.int32, (CHUNK, CHUNK), 1)
    causal = li >= si
    low_half = si < SSM_HEAD_DIM
    lo_mask = (lax.broadcasted_iota(jnp.int32, (1, LANES), 1) < SSM_HEAD_DIM).astype(BF16)
    hi_mask = 1 - lo_mask

    for c in range(nchunk):
        r0 = c * CHUNK
        dt = dt_ref[r0:r0 + CHUNK, :]
        dta = dt * a_row
        dta_hi = dta.astype(BF16)
        dta_lo = (dta - dta_hi.astype(F32)).astype(BF16)
        a2 = (_dot(tri, dta_hi) + _dot(tri, dta_lo)) * LOG2E
        a2_t = a2.T
        dt_t = dt.T
        row2_t = a2_t - jnp.log2(dt_t)
        a_end2_t = a2_t[:, CHUNK - 1:CHUNK]
        w_t = dt_t * jnp.exp2(a_end2_t - a2_t)
        e_end_t = jnp.exp2(a_end2_t)

        for g in range(SSM_GROUPS):
            if finish:
                finish.pop(0)()
            if attention:
                attention.pop(0)()
            bm_gb = lane_block(n_xs + g, r0)
            cm_gb = lane_block(n_xs + SSM_GROUPS + g, r0)
            cb = _dot_nt(cm_gb, bm_gb)
            bm_gt = bm_gb.astype(F32).T
            gsl = slice(g * gw, (g + 1) * gw)
            hprev = h_s[g]
            y_off = _dot(cm_gb, hprev.astype(BF16))
            y_parts = []
            for p in range(pairs_per_group):
                pair = g * pairs_per_group + p
                lsl = slice(pair * LANES, (pair + 1) * LANES)
                xs_b = lane_block(pair, r0)
                xs_pair = xs_b.astype(F32)
                rhs = jnp.concatenate([xs_b * lo_mask, xs_b * hi_mask], axis=0)
                m_parts, b_parts, cols, e_ends = [], [], [], []
                for side in range(2):
                    hh = 2 * pair + side
                    col2 = jnp.broadcast_to(a2[:, hh:hh + 1], (CHUNK, CHUNK))
                    dec_dt = jnp.exp2(jnp.where(causal, col2 - row2_t[hh:hh + 1, :], NEG))
                    m_parts.append(cb * dec_dt)
                    b_parts.append(bm_gt * w_t[hh:hh + 1, :])
                    cols.append(col2)
                    e_ends.append(e_end_t[hh:hh + 1, :])
                lhs = jnp.concatenate(
                    [jnp.concatenate(m_parts, axis=1), jnp.concatenate(b_parts, axis=1)],
                    axis=0).astype(BF16)
                res = _dot(lhs, rhs)
                e_col = jnp.exp2(jnp.where(low_half, cols[0], cols[1]))
                y_pair = res[0:CHUNK] + e_col * y_off[:, p * LANES:(p + 1) * LANES]
                y_pair = y_pair + dsk_ref[:, lsl] * xs_pair
                y_parts.append(y_pair * zm_ref[r0:r0 + CHUNK, lsl].astype(F32))
                e_end = jnp.where(low_half, e_ends[0], e_ends[1])
                h_s[g, :, p * LANES:(p + 1) * LANES] = (
                    hprev[:, p * LANES:(p + 1) * LANES] * e_end + res[CHUNK:])
            yg = jnp.concatenate(y_parts, axis=1)
            ms = jnp.mean(yg * yg, axis=-1, keepdims=True)
            yn_s[r0:r0 + CHUNK, gsl] = (yg * lax.rsqrt(ms + EPS) * nw_ref[:, gsl]).astype(BF16)

    while finish:
        finish.pop(0)()
    while attention:
        attention.pop(0)()
    yprev_s[...] = yn_s[...]

    y = (yt_s[...].T * za_ref[...].astype(F32)).astype(BF16)
    ya = _dot(y, wap_ref[...])
    ya_s[...] = (ya * ga_ref[...].astype(F32)).astype(BF16)


def _branches_call(qkv, za, g, bias, sink_rows, qw, kw, bd, wap,
                   xbc, dt, zm, alog, dsk, nw, tri, x2, gate, wsp, wout, batch, seq):
    t, d = x2.shape
    aw = ATTN_HEADS * HEAD_DIM
    kvw2 = 2 * ATTN_KV_HEADS * HEAD_DIM
    sw = zm.shape[1]
    rows = min(ROWS_BRANCHES, seq)
    nt = seq // rows
    bpt = rows // BLOCK
    tile = lambda b, i: b * nt + jnp.minimum(i, nt - 1)
    scan = lambda b, i: (tile(b, i), 0)
    done = lambda b, i: (b * nt + jnp.maximum(i - 1, 0), 0)
    return pl.pallas_call(
        _branches_kernel,
        grid=(batch, nt + 1),
        in_specs=[pl.BlockSpec((rows, aw), scan),
                  pl.BlockSpec((rows, kvw2), lambda b, i: (tile(b, i), aw // kvw2)),
                  pl.BlockSpec((BLOCK, kvw2),
                               lambda b, i: (jnp.maximum(tile(b, i) * bpt - 1, 0), aw // kvw2)),
                  pl.BlockSpec((rows, aw), scan),
                  pl.BlockSpec((rows, aw), scan),
                  _resident(bias.shape), _resident(sink_rows.shape),
                  _resident(qw.shape), _resident(kw.shape),
                  _resident(bd.shape), _resident(wap.shape),
                  pl.BlockSpec((rows, xbc.shape[1]), scan),
                  pl.BlockSpec((rows, dt.shape[1]), scan),
                  pl.BlockSpec((rows, sw), scan),
                  _resident(alog.shape), _resident(dsk.shape), _resident(nw.shape),
                  _resident(tri.shape),
                  pl.BlockSpec((rows, d), lambda b, i: (b * nt + jnp.maximum(i - 1, 0), 1)),
                  pl.BlockSpec((rows, d), done),
                  pl.BlockSpec((1, 1, d), lambda b, i: (b, 0, 0)),
                  _resident(wsp.shape), _resident(wout.shape)],
        out_specs=pl.BlockSpec((rows, d), done),
        out_shape=jax.ShapeDtypeStruct((t, d), F32),
        scratch_shapes=[pltpu.VMEM((rows, aw), BF16),
                        pltpu.VMEM((rows + BLOCK, ATTN_KV_HEADS * LANES), BF16),
                        pltpu.VMEM((rows + BLOCK, ATTN_KV_HEADS * LANES), BF16),
                        pltpu.VMEM((kvw2 // 2, rows + BLOCK), BF16),
                        pltpu.VMEM((2, 4 * BLOCK, 2 * BLOCK), F32),
                        pltpu.VMEM((2, 2 * BLOCK, 4 * BLOCK), BF16),
                        pltpu.VMEM((aw, rows), F32),
                        pltpu.VMEM((rows, d), BF16),
                        pltpu.VMEM((SSM_GROUPS, SSM_STATE, sw // SSM_GROUPS), F32),
                        pltpu.VMEM((rows, sw), BF16),
                        pltpu.VMEM((rows, sw), BF16),
                        pltpu.VMEM((rows, d), BF16)],
        compiler_params=pltpu.CompilerParams(
            dimension_semantics=("arbitrary", "arbitrary"), vmem_limit_bytes=VMEM_LIMIT),
        name="branches",
    )(qkv, qkv, qkv, za, g, bias, sink_rows, qw, kw, bd, wap,
      xbc, dt, zm, alog, dsk, nw, tri, g, x2, gate, wsp, wout)


def _layer(x, c, w_ada, b_ada, norm_w, w_in, q_norm_w, k_norm_w, rel_bias, sinks,
           conv_w, conv_b, dt_bias, a_log, d_skip, ssm_norm_w, w_attn_proj, w_ssm_proj, w_out):
    batch, seq, d = x.shape
    aw = ATTN_HEADS * HEAD_DIM
    kvw = ATTN_KV_HEADS * HEAD_DIM
    sw = w_ssm_proj.shape[0]
    ssm_heads = sw // SSM_HEAD_DIM
    xbc_w = sw + 2 * SSM_GROUPS * SSM_STATE
    assert seq % BLOCK == 0 and seq % CHUNK == 0 and ssm_heads <= LANES

    c8 = jnp.zeros((8, d), F32).at[:batch].set(c)
    mod = _mod_call(c8, w_ada, b_ada.reshape(1, -1))[:batch]
    shift, scale, gate = (mod[:, k * d:(k + 1) * d].reshape(batch, 1, d) for k in range(3))

    o = np.cumsum([0, aw, kvw, kvw, aw, sw, xbc_w, ssm_heads, d, d])
    w_t = w_in.T
    w_all = _wprep_call(w_t, int(o[6]), ssm_heads)
    wdt = jnp.pad(w_t[o[6]:o[7]], ((0, LANES - ssm_heads), (0, 0))).astype(BF16)
    widths = [aw + 2 * kvw, aw, sw, xbc_w, 2 * d]
    pad_heads = lambda v: jnp.pad(v.astype(F32), (0, LANES - ssm_heads)).reshape(1, LANES)

    x2 = x.reshape(batch * seq, d)
    qkv, za, zm, xbc, g, dt = _inproj_call(
        x2, shift, scale, norm_w.reshape(1, d), w_all, wdt, pad_heads(dt_bias),
        conv_w.astype(F32), conv_b.reshape(1, -1).astype(F32), widths, batch, seq)

    bias = _bias_call(rel_bias.astype(F32))
    seg = np.arange(LANES) // HEAD_DIM
    bd = jnp.asarray((seg[:, None] == seg[None, :]).astype(np.float32) / HEAD_DIM, dtype=BF16)
    qw = jnp.tile(q_norm_w.astype(F32), LANES // HEAD_DIM).reshape(1, LANES)
    kw = jnp.tile(k_norm_w.astype(F32), LANES // HEAD_DIM).reshape(1, LANES)
    sink_rows = jnp.repeat(sinks.astype(F32).reshape(ATTN_KV_HEADS, 2, 2).transpose(0, 2, 1), BLOCK, axis=2)
    tri = jnp.asarray(np.tril(np.ones((CHUNK, CHUNK), np.float32)), dtype=BF16)
    out = _branches_call(qkv, za, g, bias, sink_rows, qw, kw, bd, w_attn_proj.astype(BF16),
                         xbc, dt, zm, pad_heads(a_log),
                         jnp.repeat(d_skip.astype(F32), SSM_HEAD_DIM).reshape(1, sw),
                         ssm_norm_w.reshape(1, sw).astype(F32), tri,
                         x2, gate, w_ssm_proj.astype(BF16), w_out.astype(BF16), batch, seq)
    return out.reshape(batch, seq, d)


def kernel(x, c, w_ada, b_ada, norm_w, w_in, q_norm_w, k_norm_w, rel_bias, sinks, conv_w, conv_b,
           dt_bias, a_log, d_skip, ssm_norm_w, w_attn_proj, w_ssm_proj, w_out):
    depth = w_in.shape[0]
    for i in range(depth):
        x = _layer(x, c, w_ada[i], b_ada[i], norm_w[i], w_in[i], q_norm_w[i], k_norm_w[i],
                   rel_bias, sinks[i], conv_w[i], conv_b[i], dt_bias[i], a_log[i], d_skip[i],
                   ssm_norm_w[i], w_attn_proj[i], w_ssm_proj[i], w_out[i])
    return x
```

```python
import functools
import math

import numpy as np
import jax
import jax.numpy as jnp
from jax import lax
from jax.experimental import pallas as pl
from jax.experimental.pallas import tpu as pltpu

F32 = jnp.float32
BF16 = jnp.bfloat16

HEAD_DIM = 64
ATTN_HEADS = 16
ATTN_KV_HEADS = 4
GROUP = ATTN_HEADS // ATTN_KV_HEADS
BLOCK = 128
REL_BUCKETS = 32
REL_MAX_DIST = 128
SSM_HEAD_DIM = 64
SSM_GROUPS = 4
SSM_STATE = 128
CONV_WIDTH = 4
CHUNK = 128
EPS = 1e-6
NEG = -1e30
LOG2E = 1.4426950408889634
LANES = 128
VMEM_LIMIT = 56 * 1024 * 1024

ROWS_INPROJ = 512
ROWS_BRANCHES = 256


def _sigmoid(x):
    return 1.0 / (1.0 + jnp.exp(-x))


def _silu(x):
    return x * _sigmoid(x)


def _softplus(x):
    return jnp.maximum(x, 0.0) + jnp.log(1.0 + jnp.exp(-jnp.abs(x)))


def _dot(a, b):
    return jnp.dot(a, b, preferred_element_type=F32)


def _dot_nt(a, b):
    return lax.dot_general(a, b, (((1,), (1,)), ((), ())), preferred_element_type=F32)


def _resident(shape):
    nd = len(shape)
    return pl.BlockSpec(shape, lambda *_: (0,) * nd, pipeline_mode=pl.Buffered(1))


def _mod_kernel(c_ref, w_ref, b_ref, o_ref):
    s = _silu(c_ref[...])
    o_ref[...] = _dot(s.astype(BF16), w_ref[...].astype(BF16)) + b_ref[...]


def _mod_call(c8, w_ada, b_ada):
    d, n = w_ada.shape
    tn = 1024
    return pl.pallas_call(
        _mod_kernel,
        grid=(n // tn,),
        in_specs=[pl.BlockSpec((8, d), lambda j: (0, 0)),
                  pl.BlockSpec((d, tn), lambda j: (0, j)),
                  pl.BlockSpec((1, tn), lambda j: (0, j))],
        out_specs=pl.BlockSpec((8, tn), lambda j: (0, j)),
        out_shape=jax.ShapeDtypeStruct((8, n), F32),
        name="mod",
    )(c8, w_ada, b_ada)


def _bucket_table():
    qi = np.arange(BLOCK)[:, None]
    kj = np.arange(2 * BLOCK)[None, :]
    dist = qi + BLOCK - kj
    n = np.maximum(dist, 0)
    max_exact = REL_BUCKETS // 2
    nf = np.maximum(n, 1).astype(np.float32)
    large = max_exact + (np.log(nf / max_exact) / math.log(REL_MAX_DIST / max_exact)
                         * (REL_BUCKETS - max_exact)).astype(np.int32)
    large = np.minimum(large, REL_BUCKETS - 1)
    bucket = np.where(n < max_exact, n, large)
    valid = (dist >= 0) & (dist < BLOCK)
    return np.where(valid, bucket, -1).astype(np.int32)


def _bias_kernel(rb_ref, idx_ref, o_ref):
    kv = pl.program_id(0)
    idx = idx_ref[...]
    key_row = lax.broadcasted_iota(jnp.int32, idx.shape, 0)
    keep = idx >= 0
    keep_first = jnp.logical_and(keep, key_row >= BLOCK)
    for s in range(2):
        for c in range(2):
            acc = jnp.zeros(idx.shape, F32)
            for b in range(REL_BUCKETS):
                acc = jnp.where(idx == b, rb_ref[b, kv * GROUP + 2 * c + s], acc)
            acc = acc * LOG2E
            rows, cols = slice(s * 2 * BLOCK, (s + 1) * 2 * BLOCK), slice(c * BLOCK, (c + 1) * BLOCK)
            o_ref[0, 0, rows, cols] = jnp.where(keep, acc, NEG)
            o_ref[1, 0, rows, cols] = jnp.where(keep_first, acc, NEG)


def _bias_call(rel_bias):
    idx_t = jnp.asarray(np.ascontiguousarray(_bucket_table().T))
    table = pl.pallas_call(
        _bias_kernel,
        grid=(ATTN_KV_HEADS,),
        in_specs=[pl.BlockSpec(memory_space=pltpu.SMEM),
                  pl.BlockSpec((2 * BLOCK, BLOCK), lambda h: (0, 0))],
        out_specs=pl.BlockSpec((2, 1, 4 * BLOCK, 2 * BLOCK), lambda h: (0, h, 0, 0)),
        out_shape=jax.ShapeDtypeStruct((2, ATTN_KV_HEADS, 4 * BLOCK, 2 * BLOCK), F32),
        name="bias",
    )(rel_bias, idx_t)
    return table.reshape(2 * ATTN_KV_HEADS, 4 * BLOCK, 2 * BLOCK)


def _wprep_kernel(wt_ref, o_ref):
    o_ref[...] = wt_ref[...].T.astype(BF16)


def _wprep_call(w_t, dt_start, dt_width):
    n, k = w_t.shape
    tn = 512
    head_tiles, tail_tiles = dt_start // tn, (n - dt_start - dt_width) // tn
    assert head_tiles * tn == dt_start and tail_tiles * tn == n - dt_start - dt_width
    tail_start = dt_start + dt_width

    assert tail_start % 8 == 0

    def rows(i):
        start = jnp.where(i < head_tiles, i * tn, tail_start + (i - head_tiles) * tn)
        return (pl.multiple_of(start, 8), 0)

    return pl.pallas_call(
        _wprep_kernel,
        grid=(head_tiles + tail_tiles,),
        in_specs=[pl.BlockSpec((pl.Element(tn), pl.Element(k)), rows)],
        out_specs=pl.BlockSpec((k, tn), lambda i: (0, i)),
        out_shape=jax.ShapeDtypeStruct((k, n - dt_width), BF16),
        compiler_params=pltpu.CompilerParams(dimension_semantics=("parallel",)),
        name="wprep",
    )(w_t)


def _inproj_kernel(x_ref, shift_ref, scale_ref, nw_ref, w_ref, wdt_ref, dtb_ref, cw_ref, cb_ref,
                   qkv_o, za_o, zm_o, xbc_o, g_o, dt_o, xraw_s, carry_s):
    ts = x_ref.shape[0]
    halo = carry_s.shape[1]

    @pl.when(pl.program_id(1) == 0)
    def _():
        carry_s[...] = jnp.zeros(carry_s.shape, F32)

    x = x_ref[...]
    ms = jnp.mean(x * x, axis=-1, keepdims=True)
    xn = x * lax.rsqrt(ms + EPS) * nw_ref[...]
    h = (xn * (1.0 + scale_ref[0]) + shift_ref[0]).astype(BF16)

    def conv_silu_store(y, c0, slot):
        for jj in range(y.shape[1] // LANES):
            j = c0 // LANES + jj
            jl = slice(j * LANES, (j + 1) * LANES)
            yj = y[:, jj * LANES:(jj + 1) * LANES]
            xraw_s[slot, jj, 0:halo, :] = carry_s[j]
            xraw_s[slot, jj, halo:halo + ts, :] = yj
            carry_s[j] = yj[ts - halo:ts, :]
            acc = cb_ref[:, jl]
            for w in range(CONV_WIDTH):
                off = halo - (CONV_WIDTH - 1) + w
                acc = acc + cw_ref[w:w + 1, jl] * xraw_s[slot, jj, off:off + ts, :]
            xbc_o[:, jl] = (acc * (1.0 / (1.0 + jnp.exp2(acc * (-LOG2E))))).astype(BF16)

    step = xraw_s.shape[1] * LANES
    light, heavy, col = [], [], 0
    for o_ref, act in ((qkv_o, None), (za_o, _silu), (zm_o, _silu), (xbc_o, "conv"), (g_o, _sigmoid)):
        for c0 in range(0, o_ref.shape[1], step):
            (heavy if act == "conv" else light).append((o_ref, act, col + c0, c0))
        col += o_ref.shape[1]
    order = []
    while light or heavy:
        if light:
            order.append(light.pop(0))
        if heavy:
            order.append(heavy.pop(0))
    slot = 0
    for o_ref, act, wc, c0 in order:
        y = _dot(h, w_ref[:, wc:wc + step])
        if act == "conv":
            conv_silu_store(y, c0, slot)
            slot = 1 - slot
        else:
            o_ref[:, c0:c0 + step] = (y if act is None else act(y)).astype(o_ref.dtype)
    dt_o[...] = _softplus(_dot_nt(h, wdt_ref[...]) + dtb_ref[...])


def _inproj_call(x2, shift, scale, norm_w, w_all, wdt, dtb, cw, cb, widths, batch, seq):
    t, d = x2.shape
    ts = min(ROWS_INPROJ, seq)
    ns = seq // ts
    row = lambda b, i: (b * ns + i, 0)
    per_b = pl.BlockSpec((1, 1, d), lambda b, i: (b, 0, 0))
    widths = list(widths) + [wdt.shape[0]]
    dtypes = [BF16] * (len(widths) - 1) + [F32]
    piece_blocks = 4
    return pl.pallas_call(
        _inproj_kernel,
        grid=(batch, ns),
        in_specs=[pl.BlockSpec((ts, d), row), per_b, per_b, _resident((1, d)),
                  _resident(w_all.shape), _resident(wdt.shape), _resident(dtb.shape),
                  _resident(cw.shape), _resident(cb.shape)],
        out_specs=[pl.BlockSpec((ts, n), row) for n in widths],
        out_shape=[jax.ShapeDtypeStruct((t, n), dt) for n, dt in zip(widths, dtypes)],
        scratch_shapes=[pltpu.VMEM((2, piece_blocks, ts + 8, LANES), F32),
                        pltpu.VMEM((cw.shape[1] // LANES, 8, LANES), F32)],
        compiler_params=pltpu.CompilerParams(
            dimension_semantics=("arbitrary", "arbitrary"), vmem_limit_bytes=VMEM_LIMIT),
        name="inproj",
    )(x2, shift, scale, norm_w, w_all, wdt, dtb, cw, cb)


def _segment_rms(x, bd, w):
    ms = _dot((x * x).astype(BF16), bd)
    return x * lax.rsqrt(ms + EPS) * w


def _branches_kernel(q_ref, kvc_ref, kvp_ref, za_ref, ga_ref, bias_ref, sink_ref,
                     qw_ref, kw_ref, bd_ref, wap_ref,
                     xbc_ref, dt_ref, zm_ref, alog_ref, dsk_ref, nw_ref, tri_ref,
                     gb_ref, x_ref, gate_ref, wsp_ref, wout_ref,
                     o_ref,
                     qn_s, klo_s, khi_s, vt_s, s_s, p_s, yt_s, y_s, ya_s, h_s, yn_s, yprev_s, merged_s):
    rows = q_ref.shape[0]
    nblk = rows // BLOCK
    nchunk = rows // CHUNK
    kvw = ATTN_KV_HEADS * HEAD_DIM
    sw = zm_ref.shape[1]
    gw = sw // SSM_GROUPS
    pairs_per_group = gw // LANES
    n_xs = sw // LANES
    first_tile = pl.program_id(1) == 0

    @pl.when(first_tile)
    def _():
        h_s[...] = jnp.zeros(h_s.shape, F32)
        yprev_s[...] = jnp.zeros(yprev_s.shape, BF16)
        ya_s[...] = jnp.zeros(ya_s.shape, BF16)

    d_out = o_ref.shape[1]
    piece_w = 2 * LANES

    def merge_piece(c0):
        cs = slice(c0, c0 + piece_w)
        yb = _dot(yprev_s[...], wsp_ref[:, cs])
        merged_s[:, cs] = ya_s[:, cs] + gb_ref[:, cs] * yb.astype(BF16)

    def out_piece(c0):
        cs = slice(c0, c0 + piece_w)
        o = _dot(merged_s[...], wout_ref[:, cs])
        o_ref[:, cs] = x_ref[:, cs] + gate_ref[0][:, cs] * o

    finish = ([functools.partial(merge_piece, c0) for c0 in range(0, d_out, piece_w)]
              + [functools.partial(out_piece, c0) for c0 in range(0, d_out, piece_w)])

    bd = bd_ref[...]
    qw = qw_ref[...] * (HEAD_DIM ** -0.5 * LOG2E)
    kw = kw_ref[...]
    low_lanes = lax.broadcasted_iota(jnp.int32, (1, LANES), 1) < HEAD_DIM

    for j in range(q_ref.shape[1] // LANES):
        sl = slice(j * LANES, (j + 1) * LANES)
        qn_s[:, sl] = _segment_rms(q_ref[:, sl].astype(F32), bd, qw).astype(BF16)
    for j in range(kvw // LANES):
        sl = slice(j * LANES, (j + 1) * LANES)
        even = slice(2 * j * LANES, (2 * j + 1) * LANES)
        odd = slice((2 * j + 1) * LANES, (2 * j + 2) * LANES)
        for dst, src in ((slice(0, BLOCK), kvp_ref), (slice(BLOCK, BLOCK + rows), kvc_ref)):
            kn = _segment_rms(src[:, sl].astype(F32), bd, kw)
            lo = jnp.where(low_lanes, kn, 0.0)
            hi = jnp.where(low_lanes, 0.0, kn)
            klo_s[dst, even] = lo.astype(BF16)
            khi_s[dst, odd] = hi.astype(BF16)
            khi_s[dst, even] = pltpu.roll(lo, HEAD_DIM, axis=1).astype(BF16)
            klo_s[dst, odd] = pltpu.roll(hi, HEAD_DIM, axis=1).astype(BF16)
    vt_s[:, 0:BLOCK] = kvp_ref[:, kvw:2 * kvw].astype(F32).T.astype(BF16)
    vt_s[:, BLOCK:] = kvc_ref[:, kvw:2 * kvw].astype(F32).T.astype(BF16)

    units = [(qb, h) for qb in range(nblk) for h in range(ATTN_KV_HEADS)]

    def scores(u):
        qb, h = units[u]
        r0 = qb * BLOCK
        hl = slice(h * LANES, (h + 1) * LANES)
        k2 = jnp.concatenate([klo_s[r0:r0 + 2 * BLOCK, hl], khi_s[r0:r0 + 2 * BLOCK, hl]], axis=0)
        q2 = jnp.concatenate([qn_s[r0:r0 + BLOCK, (2 * h + c) * LANES:(2 * h + c + 1) * LANES]
                              for c in range(2)], axis=0)
        entry = jnp.where(first_tile, ATTN_KV_HEADS, 0) + h if qb == 0 else h
        s_s[u % 2] = _dot_nt(k2, q2) + bias_ref[entry]

    def softmax_pv(u):
        qb, h = units[u]
        r0 = qb * BLOCK
        slot = u % 2
        inv = []
        for s in range(2):
            for c in range(2):
                blk = s_s[slot, s * 2 * BLOCK:(s + 1) * 2 * BLOCK, c * BLOCK:(c + 1) * BLOCK]
                sink = sink_ref[h, s:s + 1, c * BLOCK:(c + 1) * BLOCK] * LOG2E
                m = jnp.maximum(jnp.max(blk, axis=0, keepdims=True), sink)
                e = jnp.exp2(blk - m)
                den = jnp.sum(e, axis=0, keepdims=True) + jnp.exp2(sink - m)
                p_s[slot, :, (2 * s + c) * BLOCK:(2 * s + c + 1) * BLOCK] = e.astype(BF16)
                inv.append(1.0 / den)
        o_t = _dot(vt_s[h * HEAD_DIM:(h + 1) * HEAD_DIM, r0:r0 + 2 * BLOCK], p_s[slot])
        for s in range(2):
            for c in range(2):
                head = GROUP * h + 2 * c + s
                k = 2 * s + c
                yt_s[head * HEAD_DIM:(head + 1) * HEAD_DIM, r0:r0 + BLOCK] = (
                    o_t[:, k * BLOCK:(k + 1) * BLOCK] * inv[k])

    scores(0)
    attention = []
    for u in range(len(units)):
        def unit(u=u):
            if u + 1 < len(units):
                scores(u + 1)
            softmax_pv(u)
        attention.append(unit)

    def gate_y():
        y_s[...] = (yt_s[...].T * za_ref[...].astype(F32)).astype(BF16)

    def proj_piece(c0):
        cs = slice(c0, c0 + piece_w)
        ya = _dot(y_s[...], wap_ref[:, cs])
        ya_s[:, cs] = (ya * ga_ref[:, cs].astype(F32)).astype(BF16)

    first_proj = len(attention) + 1
    attention.append(gate_y)
    attention += [functools.partial(proj_piece, c0) for c0 in range(0, d_out, piece_w)]
    n_groups = nchunk * SSM_GROUPS
    base, extra = divmod(len(attention), n_groups)
    attn_counts = [base + (k < extra) for k in range(n_groups)]
    first_proj_group = next(k for k in range(n_groups) if sum(attn_counts[:k + 1]) > first_proj)
    assert len(finish) <= n_groups and first_proj_group >= len(finish) // 2 - 1

    def lane_block(j, r0):
        return xbc_ref[r0:r0 + CHUNK, j * LANES:(j + 1) * LANES]

    a_row = -jnp.exp(alog_ref[...])
    tri = tri_ref[...]
    li = lax.broadcasted_iota(jnp.int32, (CHUNK, CHUNK), 0)
    si = lax.broadcasted_iota(jnp.int32, (CHUNK, CHUNK), 1)
    causal = li >= si
    low_half = si < SSM_HEAD_DIM
    lo_mask = (lax.broadcasted_iota(jnp.int32, (1, LANES), 1) < SSM_HEAD_DIM).astype(BF16)
    hi_mask = 1 - lo_mask

    for c in range(nchunk):
        r0 = c * CHUNK
        dt = dt_ref[r0:r0 + CHUNK, :]
        dta = dt * a_row
        dta_hi = dta.astype(BF16)
        dta_lo = (dta - dta_hi.astype(F32)).astype(BF16)
        a2 = (_dot(tri, dta_hi) + _dot(tri, dta_lo)) * LOG2E
        a2_t = a2.T
        dt_t = dt.T
        row2_t = a2_t - jnp.log2(dt_t)
        a_end2_t = a2_t[:, CHUNK - 1:CHUNK]
        w_t = dt_t * jnp.exp2(a_end2_t - a2_t)
        e_end_t = jnp.exp2(a_end2_t)

        for g in range(SSM_GROUPS):
            if finish:
                finish.pop(0)()
            for _ in range(attn_counts[c * SSM_GROUPS + g]):
                attention.pop(0)()
            bm_gb = lane_block(n_xs + g, r0)
            cm_gb = lane_block(n_xs + SSM_GROUPS + g, r0)
            cb = _dot_nt(cm_gb, bm_gb)
            bm_gt = bm_gb.astype(F32).T
            gsl = slice(g * gw, (g + 1) * gw)
            hprev = h_s[g]
            y_off = _dot(cm_gb, hprev.astype(BF16))
            y_parts = []
            for p in range(pairs_per_group):
                pair = g * pairs_per_group + p
                lsl = slice(pair * LANES, (pair + 1) * LANES)
                xs_b = lane_block(pair, r0)
                xs_pair = xs_b.astype(F32)
                rhs = jnp.concatenate([xs_b * lo_mask, xs_b * hi_mask], axis=0)
                m_parts, b_parts, cols, e_ends = [], [], [], []
                for side in range(2):
                    hh = 2 * pair + side
                    col2 = jnp.broadcast_to(a2[:, hh:hh + 1], (CHUNK, CHUNK))
                    dec_dt = jnp.exp2(jnp.where(causal, col2 - row2_t[hh:hh + 1, :], NEG))
                    m_parts.append(cb * dec_dt)
                    b_parts.append(bm_gt * w_t[hh:hh + 1, :])
                    cols.append(col2)
                    e_ends.append(e_end_t[hh:hh + 1, :])
                lhs = jnp.concatenate(
                    [jnp.concatenate(m_parts, axis=1), jnp.concatenate(b_parts, axis=1)],
                    axis=0).astype(BF16)
                res = _dot(lhs, rhs)
                e_col = jnp.exp2(jnp.where(low_half, cols[0], cols[1]))
                y_pair = res[0:CHUNK] + e_col * y_off[:, p * LANES:(p + 1) * LANES]
                y_pair = y_pair + dsk_ref[:, lsl] * xs_pair
                y_parts.append(y_pair * zm_ref[r0:r0 + CHUNK, lsl].astype(F32))
                e_end = jnp.where(low_half, e_ends[0], e_ends[1])
                h_s[g, :, p * LANES:(p + 1) * LANES] = (
                    hprev[:, p * LANES:(p + 1) * LANES] * e_end + res[CHUNK:])
            yg = jnp.concatenate(y_parts, axis=1)
            ms = jnp.mean(yg * yg, axis=-1, keepdims=True)
            yn_s[r0:r0 + CHUNK, gsl] = (yg * lax.rsqrt(ms + EPS) * nw_ref[:, gsl]).astype(BF16)

    assert not finish and not attention
    yprev_s[...] = yn_s[...]


def _branches_call(qkv, za, g, bias, sink_rows, qw, kw, bd, wap,
                   xbc, dt, zm, alog, dsk, nw, tri, x2, gate, wsp, wout, batch, seq):
    t, d = x2.shape
    aw = ATTN_HEADS * HEAD_DIM
    kvw2 = 2 * ATTN_KV_HEADS * HEAD_DIM
    sw = zm.shape[1]
    rows = min(ROWS_BRANCHES, seq)
    nt = seq // rows
    bpt = rows // BLOCK
    tile = lambda b, i: b * nt + jnp.minimum(i, nt - 1)
    scan = lambda b, i: (tile(b, i), 0)
    done = lambda b, i: (b * nt + jnp.maximum(i - 1, 0), 0)
    return pl.pallas_call(
        _branches_kernel,
        grid=(batch, nt + 1),
        in_specs=[pl.BlockSpec((rows, aw), scan),
                  pl.BlockSpec((rows, kvw2), lambda b, i: (tile(b, i), aw // kvw2)),
                  pl.BlockSpec((BLOCK, kvw2),
                               lambda b, i: (jnp.maximum(tile(b, i) * bpt - 1, 0), aw // kvw2)),
                  pl.BlockSpec((rows, aw), scan),
                  pl.BlockSpec((rows, aw), scan),
                  _resident(bias.shape), _resident(sink_rows.shape),
                  _resident(qw.shape), _resident(kw.shape),
                  _resident(bd.shape), _resident(wap.shape),
                  pl.BlockSpec((rows, xbc.shape[1]), scan),
                  pl.BlockSpec((rows, dt.shape[1]), scan),
                  pl.BlockSpec((rows, sw), scan),
                  _resident(alog.shape), _resident(dsk.shape), _resident(nw.shape),
                  _resident(tri.shape),
                  pl.BlockSpec((rows, d), lambda b, i: (b * nt + jnp.maximum(i - 1, 0), 1)),
                  pl.BlockSpec((rows, d), done),
                  pl.BlockSpec((1, 1, d), lambda b, i: (b, 0, 0)),
                  _resident(wsp.shape), _resident(wout.shape)],
        out_specs=pl.BlockSpec((rows, d), done),
        out_shape=jax.ShapeDtypeStruct((t, d), F32),
        scratch_shapes=[pltpu.VMEM((rows, aw), BF16),
                        pltpu.VMEM((rows + BLOCK, ATTN_KV_HEADS * LANES), BF16),
                        pltpu.VMEM((rows + BLOCK, ATTN_KV_HEADS * LANES), BF16),
                        pltpu.VMEM((kvw2 // 2, rows + BLOCK), BF16),
                        pltpu.VMEM((2, 4 * BLOCK, 2 * BLOCK), F32),
                        pltpu.VMEM((2, 2 * BLOCK, 4 * BLOCK), BF16),
                        pltpu.VMEM((aw, rows), F32),
                        pltpu.VMEM((rows, aw), BF16),
                        pltpu.VMEM((rows, d), BF16),
                        pltpu.VMEM((SSM_GROUPS, SSM_STATE, sw // SSM_GROUPS), F32),
                        pltpu.VMEM((rows, sw), BF16),
                        pltpu.VMEM((rows, sw), BF16),
                        pltpu.VMEM((rows, d), BF16)],
        compiler_params=pltpu.CompilerParams(
            dimension_semantics=("arbitrary", "arbitrary"), vmem_limit_bytes=VMEM_LIMIT),
        name="branches",
    )(qkv, qkv, qkv, za, g, bias, sink_rows, qw, kw, bd, wap,
      xbc, dt, zm, alog, dsk, nw, tri, g, x2, gate, wsp, wout)


def _layer(x, c, w_ada, b_ada, norm_w, w_in, q_norm_w, k_norm_w, rel_bias, sinks,
           conv_w, conv_b, dt_bias, a_log, d_skip, ssm_norm_w, w_attn_proj, w_ssm_proj, w_out):
    batch, seq, d = x.shape
    aw = ATTN_HEADS * HEAD_DIM
    kvw = ATTN_KV_HEADS * HEAD_DIM
    sw = w_ssm_proj.shape[0]
    ssm_heads = sw // SSM_HEAD_DIM
    xbc_w = sw + 2 * SSM_GROUPS * SSM_STATE
    assert seq % BLOCK == 0 and seq % CHUNK == 0 and ssm_heads <= LANES

    c8 = jnp.zeros((8, d), F32).at[:batch].set(c)
    mod = _mod_call(c8, w_ada, b_ada.reshape(1, -1))[:batch]
    shift, scale, gate = (mod[:, k * d:(k + 1) * d].reshape(batch, 1, d) for k in range(3))

    o = np.cumsum([0, aw, kvw, kvw, aw, sw, xbc_w, ssm_heads, d, d])
    w_t = w_in.T
    w_all = _wprep_call(w_t, int(o[6]), ssm_heads)
    wdt = jnp.pad(w_t[o[6]:o[7]], ((0, LANES - ssm_heads), (0, 0))).astype(BF16)
    widths = [aw + 2 * kvw, aw, sw, xbc_w, 2 * d]
    pad_heads = lambda v: jnp.pad(v.astype(F32), (0, LANES - ssm_heads)).reshape(1, LANES)

    x2 = x.reshape(batch * seq, d)
    qkv, za, zm, xbc, g, dt = _inproj_call(
        x2, shift, scale, norm_w.reshape(1, d), w_all, wdt, pad_heads(dt_bias),
        conv_w.astype(F32), conv_b.reshape(1, -1).astype(F32), widths, batch, seq)

    bias = _bias_call(rel_bias.astype(F32))
    seg = np.arange(LANES) // HEAD_DIM
    bd = jnp.asarray((seg[:, None] == seg[None, :]).astype(np.float32) / HEAD_DIM, dtype=BF16)
    qw = jnp.tile(q_norm_w.astype(F32), LANES // HEAD_DIM).reshape(1, LANES)
    kw = jnp.tile(k_norm_w.astype(F32), LANES // HEAD_DIM).reshape(1, LANES)
    sink_rows = jnp.repeat(sinks.astype(F32).reshape(ATTN_KV_HEADS, 2, 2).transpose(0, 2, 1), BLOCK, axis=2)
    tri = jnp.asarray(np.tril(np.ones((CHUNK, CHUNK), np.float32)), dtype=BF16)
    out = _branches_call(qkv, za, g, bias, sink_rows, qw, kw, bd, w_attn_proj.astype(BF16),
                         xbc, dt, zm, pad_heads(a_log),
                         jnp.repeat(d_skip.astype(F32), SSM_HEAD_DIM).reshape(1, sw),
                         ssm_norm_w.reshape(1, sw).astype(F32), tri,
                         x2, gate, w_ssm_proj.astype(BF16), w_out.astype(BF16), batch, seq)
    return out.reshape(batch, seq, d)


def kernel(x, c, w_ada, b_ada, norm_w, w_in, q_norm_w, k_norm_w, rel_bias, sinks, conv_w, conv_b,
           dt_bias, a_log, d_skip, ssm_norm_w, w_attn_proj, w_ssm_proj, w_out):
    depth = w_in.shape[0]
    for i in range(depth):
        x = _layer(x, c, w_ada[i], b_ada[i], norm_w[i], w_in[i], q_norm_w[i], k_norm_w[i],
                   rel_bias, sinks[i], conv_w[i], conv_b[i], dt_bias[i], a_log[i], d_skip[i],
                   ssm_norm_w[i], w_attn_proj[i], w_ssm_proj[i], w_out[i])
    return x
```

```python
import functools
import math

import numpy as np
import jax
import jax.numpy as jnp
from jax import lax
from jax.experimental import pallas as pl
from jax.experimental.pallas import tpu as pltpu

F32 = jnp.float32
BF16 = jnp.bfloat16

HEAD_DIM = 64
ATTN_HEADS = 16
ATTN_KV_HEADS = 4
GROUP = ATTN_HEADS // ATTN_KV_HEADS
BLOCK = 128
REL_BUCKETS = 32
REL_MAX_DIST = 128
SSM_HEAD_DIM = 64
SSM_GROUPS = 4
SSM_STATE = 128
CONV_WIDTH = 4
CHUNK = 128
EPS = 1e-6
NEG = -1e30
LOG2E = 1.4426950408889634
LANES = 128
VMEM_LIMIT = 56 * 1024 * 1024

ROWS_INPROJ = 512
ROWS_BRANCHES = 256


def _sigmoid(x):
    return 1.0 / (1.0 + jnp.exp2(x * (-LOG2E)))


def _silu(x):
    return x * _sigmoid(x)


def _softplus(x):
    return jnp.maximum(x, 0.0) + jnp.log(1.0 + jnp.exp(-jnp.abs(x)))


def _dot(a, b):
    return jnp.dot(a, b, preferred_element_type=F32)


def _dot_nt(a, b):
    return lax.dot_general(a, b, (((1,), (1,)), ((), ())), preferred_element_type=F32)


def _resident(shape):
    nd = len(shape)
    return pl.BlockSpec(shape, lambda *_: (0,) * nd, pipeline_mode=pl.Buffered(1))


def _mod_kernel(c_ref, w_ref, b_ref, o_ref):
    s = _silu(c_ref[...])
    o_ref[...] = _dot(s.astype(BF16), w_ref[...].astype(BF16)) + b_ref[...]


def _mod_call(c8, w_ada, b_ada):
    d, n = w_ada.shape
    tn = 1024
    return pl.pallas_call(
        _mod_kernel,
        grid=(n // tn,),
        in_specs=[pl.BlockSpec((8, d), lambda j: (0, 0)),
                  pl.BlockSpec((d, tn), lambda j: (0, j)),
                  pl.BlockSpec((1, tn), lambda j: (0, j))],
        out_specs=pl.BlockSpec((8, tn), lambda j: (0, j)),
        out_shape=jax.ShapeDtypeStruct((8, n), F32),
        name="mod",
    )(c8, w_ada, b_ada)


def _bucket_table():
    qi = np.arange(BLOCK)[:, None]
    kj = np.arange(2 * BLOCK)[None, :]
    dist = qi + BLOCK - kj
    n = np.maximum(dist, 0)
    max_exact = REL_BUCKETS // 2
    nf = np.maximum(n, 1).astype(np.float32)
    large = max_exact + (np.log(nf / max_exact) / math.log(REL_MAX_DIST / max_exact)
                         * (REL_BUCKETS - max_exact)).astype(np.int32)
    large = np.minimum(large, REL_BUCKETS - 1)
    bucket = np.where(n < max_exact, n, large)
    valid = (dist >= 0) & (dist < BLOCK)
    return np.where(valid, bucket, -1).astype(np.int32)


def _bias_kernel(rb_ref, idx_ref, o_ref):
    kv = pl.program_id(0)
    idx = idx_ref[...]
    key_row = lax.broadcasted_iota(jnp.int32, idx.shape, 0)
    keep = idx >= 0
    keep_first = jnp.logical_and(keep, key_row >= BLOCK)
    for s in range(2):
        for c in range(2):
            acc = jnp.zeros(idx.shape, F32)
            for b in range(REL_BUCKETS):
                acc = jnp.where(idx == b, rb_ref[b, kv * GROUP + 2 * c + s], acc)
            acc = acc * LOG2E
            rows, cols = slice(s * 2 * BLOCK, (s + 1) * 2 * BLOCK), slice(c * BLOCK, (c + 1) * BLOCK)
            o_ref[0, 0, rows, cols] = jnp.where(keep, acc, NEG)
            o_ref[1, 0, rows, cols] = jnp.where(keep_first, acc, NEG)


def _bias_call(rel_bias):
    idx_t = jnp.asarray(np.ascontiguousarray(_bucket_table().T))
    table = pl.pallas_call(
        _bias_kernel,
        grid=(ATTN_KV_HEADS,),
        in_specs=[pl.BlockSpec(memory_space=pltpu.SMEM),
                  pl.BlockSpec((2 * BLOCK, BLOCK), lambda h: (0, 0))],
        out_specs=pl.BlockSpec((2, 1, 4 * BLOCK, 2 * BLOCK), lambda h: (0, h, 0, 0)),
        out_shape=jax.ShapeDtypeStruct((2, ATTN_KV_HEADS, 4 * BLOCK, 2 * BLOCK), F32),
        name="bias",
    )(rel_bias, idx_t)
    return table.reshape(2 * ATTN_KV_HEADS, 4 * BLOCK, 2 * BLOCK)


def _wprep_kernel(wt_ref, o_ref):
    o_ref[...] = wt_ref[...].T.astype(BF16)


def _wprep_call(w_t, dt_start, dt_width):
    n, k = w_t.shape
    tn = 512
    head_tiles, tail_tiles = dt_start // tn, (n - dt_start - dt_width) // tn
    assert head_tiles * tn == dt_start and tail_tiles * tn == n - dt_start - dt_width
    tail_start = dt_start + dt_width

    assert tail_start % 8 == 0

    def rows(i):
        start = jnp.where(i < head_tiles, i * tn, tail_start + (i - head_tiles) * tn)
        return (pl.multiple_of(start, 8), 0)

    return pl.pallas_call(
        _wprep_kernel,
        grid=(head_tiles + tail_tiles,),
        in_specs=[pl.BlockSpec((pl.Element(tn), pl.Element(k)), rows)],
        out_specs=pl.BlockSpec((k, tn), lambda i: (0, i)),
        out_shape=jax.ShapeDtypeStruct((k, n - dt_width), BF16),
        compiler_params=pltpu.CompilerParams(dimension_semantics=("parallel",)),
        name="wprep",
    )(w_t)


def _inproj_kernel(x_ref, shift_ref, scale_ref, nw_ref, w_ref, wdt_ref, dtb_ref, cw_ref, cb_ref,
                   qkv_o, za_o, zm_o, xbc_o, g_o, dt_o, xraw_s, carry_s):
    ts = x_ref.shape[0]
    halo = carry_s.shape[1]

    @pl.when(pl.program_id(1) == 0)
    def _():
        carry_s[...] = jnp.zeros(carry_s.shape, F32)

    x = x_ref[...]
    ms = jnp.mean(x * x, axis=-1, keepdims=True)
    gain = nw_ref[...] * (1.0 + scale_ref[0])
    h = (x * lax.rsqrt(ms + EPS) * gain + shift_ref[0]).astype(BF16)

    def conv_silu_store(y, c0, slot):
        for jj in range(y.shape[1] // LANES):
            j = c0 // LANES + jj
            jl = slice(j * LANES, (j + 1) * LANES)
            yj = y[:, jj * LANES:(jj + 1) * LANES]
            xraw_s[slot, jj, 0:halo, :] = carry_s[j]
            xraw_s[slot, jj, halo:halo + ts, :] = yj
            carry_s[j] = yj[ts - halo:ts, :]
            acc = cb_ref[:, jl]
            for w in range(CONV_WIDTH):
                off = halo - (CONV_WIDTH - 1) + w
                acc = acc + cw_ref[w:w + 1, jl] * xraw_s[slot, jj, off:off + ts, :]
            xbc_o[:, jl] = _silu(acc).astype(BF16)

    step = xraw_s.shape[1] * LANES
    light, heavy, col = [], [], 0
    for o_ref, act in ((qkv_o, None), (za_o, _silu), (zm_o, _silu), (xbc_o, "conv"), (g_o, _sigmoid)):
        for c0 in range(0, o_ref.shape[1], step):
            (heavy if act == "conv" else light).append((o_ref, act, col + c0, c0))
        col += o_ref.shape[1]
    order = []
    while light or heavy:
        if light:
            order.append(light.pop(0))
        if heavy:
            order.append(heavy.pop(0))
    slot = 0
    for k, (o_ref, act, wc, c0) in enumerate(order):
        w_piece = w_ref[:, wc:wc + step]
        if k == 0:
            y = jnp.concatenate([_dot(h[:ts // 2], w_piece), _dot(h[ts // 2:], w_piece)], axis=0)
        else:
            y = _dot(h, w_piece)
        if act == "conv":
            conv_silu_store(y, c0, slot)
            slot = 1 - slot
        else:
            o_ref[:, c0:c0 + step] = (y if act is None else act(y)).astype(o_ref.dtype)
    dt_o[...] = _softplus(_dot_nt(h, wdt_ref[...]) + dtb_ref[...])


def _inproj_call(x2, shift, scale, norm_w, w_all, wdt, dtb, cw, cb, widths, batch, seq):
    t, d = x2.shape
    ts = min(ROWS_INPROJ, seq)
    ns = seq // ts
    row = lambda b, i: (b * ns + i, 0)
    per_b = pl.BlockSpec((1, 1, d), lambda b, i: (b, 0, 0))
    widths = list(widths) + [wdt.shape[0]]
    dtypes = [BF16] * (len(widths) - 1) + [F32]
    piece_blocks = 4
    return pl.pallas_call(
        _inproj_kernel,
        grid=(batch, ns),
        in_specs=[pl.BlockSpec((ts, d), row), per_b, per_b, _resident((1, d)),
                  _resident(w_all.shape), _resident(wdt.shape), _resident(dtb.shape),
                  _resident(cw.shape), _resident(cb.shape)],
        out_specs=[pl.BlockSpec((ts, n), row) for n in widths],
        out_shape=[jax.ShapeDtypeStruct((t, n), dt) for n, dt in zip(widths, dtypes)],
        scratch_shapes=[pltpu.VMEM((2, piece_blocks, ts + 8, LANES), F32),
                        pltpu.VMEM((cw.shape[1] // LANES, 8, LANES), F32)],
        compiler_params=pltpu.CompilerParams(
            dimension_semantics=("arbitrary", "arbitrary"), vmem_limit_bytes=VMEM_LIMIT),
        name="inproj",
    )(x2, shift, scale, norm_w, w_all, wdt, dtb, cw, cb)


def _segment_rms(x, bd, w):
    ms = _dot((x * x).astype(BF16), bd)
    return x * lax.rsqrt(ms + EPS) * w


def _branches_kernel(q_ref, kvc_ref, kvp_ref, za_ref, ga_ref, bias_ref, sink_ref,
                     qw_ref, kw_ref, bd_ref, wap_ref,
                     xbc_ref, dt_ref, zm_ref, alog_ref, dsk_ref, nw_ref, tri_ref,
                     gb_ref, x_ref, gate_ref, wsp_ref, wout_ref,
                     o_ref,
                     qn_s, klo_s, khi_s, vt_s, s_s, p_s, yt_s, y_s, ya_s, h_s, yn_s, yprev_s, merged_s):
    rows = q_ref.shape[0]
    nblk = rows // BLOCK
    nchunk = rows // CHUNK
    kvw = ATTN_KV_HEADS * HEAD_DIM
    sw = zm_ref.shape[1]
    gw = sw // SSM_GROUPS
    pairs_per_group = gw // LANES
    n_xs = sw // LANES
    first_tile = pl.program_id(1) == 0

    @pl.when(first_tile)
    def _():
        h_s[...] = jnp.zeros(h_s.shape, F32)
        yprev_s[...] = jnp.zeros(yprev_s.shape, BF16)
        ya_s[...] = jnp.zeros(ya_s.shape, BF16)

    d_out = o_ref.shape[1]
    piece_w = 2 * LANES

    def merge_piece(c0):
        cs = slice(c0, c0 + piece_w)
        yb = _dot(yprev_s[...], wsp_ref[:, cs])
        merged_s[:, cs] = ya_s[:, cs] + gb_ref[:, cs] * yb.astype(BF16)

    def out_piece(c0):
        cs = slice(c0, c0 + piece_w)
        o = _dot(merged_s[...], wout_ref[:, cs])
        o_ref[:, cs] = x_ref[:, cs] + gate_ref[0][:, cs] * o

    finish = ([functools.partial(merge_piece, c0) for c0 in range(0, d_out, piece_w)]
              + [functools.partial(out_piece, c0) for c0 in range(0, d_out, piece_w)])

    bd = bd_ref[...]
    qw = qw_ref[...] * (HEAD_DIM ** -0.5 * LOG2E)
    kw = kw_ref[...]
    low_lanes = lax.broadcasted_iota(jnp.int32, (1, LANES), 1) < HEAD_DIM

    for j in range(q_ref.shape[1] // LANES):
        sl = slice(j * LANES, (j + 1) * LANES)
        qn_s[:, sl] = _segment_rms(q_ref[:, sl].astype(F32), bd, qw).astype(BF16)
    for j in range(kvw // LANES):
        sl = slice(j * LANES, (j + 1) * LANES)
        even = slice(2 * j * LANES, (2 * j + 1) * LANES)
        odd = slice((2 * j + 1) * LANES, (2 * j + 2) * LANES)
        for dst, src in ((slice(0, BLOCK), kvp_ref), (slice(BLOCK, BLOCK + rows), kvc_ref)):
            kn = _segment_rms(src[:, sl].astype(F32), bd, kw)
            lo = jnp.where(low_lanes, kn, 0.0)
            hi = jnp.where(low_lanes, 0.0, kn)
            klo_s[dst, even] = lo.astype(BF16)
            khi_s[dst, odd] = hi.astype(BF16)
            khi_s[dst, even] = pltpu.roll(lo, HEAD_DIM, axis=1).astype(BF16)
            klo_s[dst, odd] = pltpu.roll(hi, HEAD_DIM, axis=1).astype(BF16)
    vt_s[:, 0:BLOCK] = kvp_ref[:, kvw:2 * kvw].astype(F32).T.astype(BF16)
    vt_s[:, BLOCK:] = kvc_ref[:, kvw:2 * kvw].astype(F32).T.astype(BF16)

    units = [(qb, h) for qb in range(nblk) for h in range(ATTN_KV_HEADS)]

    def scores(u):
        qb, h = units[u]
        r0 = qb * BLOCK
        hl = slice(h * LANES, (h + 1) * LANES)
        k2 = jnp.concatenate([klo_s[r0:r0 + 2 * BLOCK, hl], khi_s[r0:r0 + 2 * BLOCK, hl]], axis=0)
        q2 = jnp.concatenate([qn_s[r0:r0 + BLOCK, (2 * h + c) * LANES:(2 * h + c + 1) * LANES]
                              for c in range(2)], axis=0)
        entry = jnp.where(first_tile, ATTN_KV_HEADS, 0) + h if qb == 0 else h
        s_s[u % 2] = _dot_nt(k2, q2) + bias_ref[entry]

    def softmax_pv(u):
        qb, h = units[u]
        r0 = qb * BLOCK
        slot = u % 2
        inv = []
        for s in range(2):
            for c in range(2):
                blk = s_s[slot, s * 2 * BLOCK:(s + 1) * 2 * BLOCK, c * BLOCK:(c + 1) * BLOCK]
                sink = sink_ref[h, s:s + 1, c * BLOCK:(c + 1) * BLOCK] * LOG2E
                m = jnp.maximum(jnp.max(blk, axis=0, keepdims=True), sink)
                e = jnp.exp2(blk - m)
                den = jnp.sum(e, axis=0, keepdims=True) + jnp.exp2(sink - m)
                p_s[slot, :, (2 * s + c) * BLOCK:(2 * s + c + 1) * BLOCK] = e.astype(BF16)
                inv.append(1.0 / den)
        o_t = _dot(vt_s[h * HEAD_DIM:(h + 1) * HEAD_DIM, r0:r0 + 2 * BLOCK], p_s[slot])
        for s in range(2):
            for c in range(2):
                head = GROUP * h + 2 * c + s
                k = 2 * s + c
                yt_s[head * HEAD_DIM:(head + 1) * HEAD_DIM, r0:r0 + BLOCK] = (
                    o_t[:, k * BLOCK:(k + 1) * BLOCK] * inv[k])

    scores(0)
    attention = []
    for u in range(len(units)):
        def unit(u=u):
            if u + 1 < len(units):
                scores(u + 1)
            softmax_pv(u)
        attention.append(unit)

    def gate_y():
        y_s[...] = (yt_s[...].T * za_ref[...].astype(F32)).astype(BF16)

    def proj_piece(c0):
        cs = slice(c0, c0 + piece_w)
        ya = _dot(y_s[...], wap_ref[:, cs])
        ya_s[:, cs] = (ya * ga_ref[:, cs].astype(F32)).astype(BF16)

    first_proj = len(attention) + 1
    attention.append(gate_y)
    attention += [functools.partial(proj_piece, c0) for c0 in range(0, d_out, piece_w)]
    n_groups = nchunk * SSM_GROUPS
    base, extra = divmod(len(attention), n_groups)
    attn_counts = [base + (k < extra) for k in range(n_groups)]
    first_proj_group = next(k for k in range(n_groups) if sum(attn_counts[:k + 1]) > first_proj)
    assert len(finish) <= n_groups and first_proj_group >= len(finish) // 2 - 1

    def lane_block(j, r0):
        return xbc_ref[r0:r0 + CHUNK, j * LANES:(j + 1) * LANES]

    a_row = -jnp.exp(alog_ref[...])
    tri = tri_ref[...]
    li = lax.broadcasted_iota(jnp.int32, (CHUNK, CHUNK), 0)
    si = lax.broadcasted_iota(jnp.int32, (CHUNK, CHUNK), 1)
    causal = li >= si
    low_half = si < SSM_HEAD_DIM
    lo_mask = (lax.broadcasted_iota(jnp.int32, (1, LANES), 1) < SSM_HEAD_DIM).astype(BF16)
    hi_mask = 1 - lo_mask

    for c in range(nchunk):
        r0 = c * CHUNK
        dt = dt_ref[r0:r0 + CHUNK, :]
        dta = dt * a_row
        dta_hi = dta.astype(BF16)
        dta_lo = (dta - dta_hi.astype(F32)).astype(BF16)
        a2 = (_dot(tri, dta_hi) + _dot(tri, dta_lo)) * LOG2E
        a2_t = a2.T
        dt_t = dt.T
        row2_t = a2_t - jnp.log2(dt_t)
        a_end2_t = a2_t[:, CHUNK - 1:CHUNK]
        w_t = dt_t * jnp.exp2(a_end2_t - a2_t)
        e_end_t = jnp.exp2(a_end2_t)

        for g in range(SSM_GROUPS):
            if finish:
                finish.pop(0)()
            for _ in range(attn_counts[c * SSM_GROUPS + g]):
                attention.pop(0)()
            bm_gb = lane_block(n_xs + g, r0)
            cm_gb = lane_block(n_xs + SSM_GROUPS + g, r0)
            cb = _dot_nt(cm_gb, bm_gb)
            bm_gt = bm_gb.astype(F32).T
            gsl = slice(g * gw, (g + 1) * gw)
            hprev = h_s[g]
            y_off = _dot(cm_gb, hprev.astype(BF16))
            y_parts = []
            for p in range(pairs_per_group):
                pair = g * pairs_per_group + p
                lsl = slice(pair * LANES, (pair + 1) * LANES)
                xs_b = lane_block(pair, r0)
                xs_pair = xs_b.astype(F32)
                rhs = jnp.concatenate([xs_b * lo_mask, xs_b * hi_mask], axis=0)
                m_parts, b_parts, cols, e_ends = [], [], [], []
                for side in range(2):
                    hh = 2 * pair + side
                    col2 = jnp.broadcast_to(a2[:, hh:hh + 1], (CHUNK, CHUNK))
                    dec_dt = jnp.exp2(jnp.where(causal, col2 - row2_t[hh:hh + 1, :], NEG))
                    m_parts.append(cb * dec_dt)
                    b_parts.append(bm_gt * w_t[hh:hh + 1, :])
                    cols.append(col2)
                    e_ends.append(e_end_t[hh:hh + 1, :])
                lhs = jnp.concatenate(
                    [jnp.concatenate(m_parts, axis=1), jnp.concatenate(b_parts, axis=1)],
                    axis=0).astype(BF16)
                res = _dot(lhs, rhs)
                e_col = jnp.exp2(jnp.where(low_half, cols[0], cols[1]))
                y_pair = res[0:CHUNK] + e_col * y_off[:, p * LANES:(p + 1) * LANES]
                y_pair = y_pair + dsk_ref[:, lsl] * xs_pair
                y_parts.append(y_pair * zm_ref[r0:r0 + CHUNK, lsl].astype(F32))
                e_end = jnp.where(low_half, e_ends[0], e_ends[1])
                h_s[g, :, p * LANES:(p + 1) * LANES] = (
                    hprev[:, p * LANES:(p + 1) * LANES] * e_end + res[CHUNK:])
            yg = jnp.concatenate(y_parts, axis=1)
            ms = jnp.mean(yg * yg, axis=-1, keepdims=True)
            yn_s[r0:r0 + CHUNK, gsl] = (yg * lax.rsqrt(ms + EPS) * nw_ref[:, gsl]).astype(BF16)

    assert not finish and not attention
    yprev_s[...] = yn_s[...]


def _branches_call(qkv, za, g, bias, sink_rows, qw, kw, bd, wap,
                   xbc, dt, zm, alog, dsk, nw, tri, x2, gate, wsp, wout, batch, seq):
    t, d = x2.shape
    aw = ATTN_HEADS * HEAD_DIM
    kvw2 = 2 * ATTN_KV_HEADS * HEAD_DIM
    sw = zm.shape[1]
    rows = min(ROWS_BRANCHES, seq)
    nt = seq // rows
    bpt = rows // BLOCK
    tile = lambda b, i: b * nt + jnp.minimum(i, nt - 1)
    scan = lambda b, i: (tile(b, i), 0)
    done = lambda b, i: (b * nt + jnp.maximum(i - 1, 0), 0)
    return pl.pallas_call(
        _branches_kernel,
        grid=(batch, nt + 1),
        in_specs=[pl.BlockSpec((rows, aw), scan),
                  pl.BlockSpec((rows, kvw2), lambda b, i: (tile(b, i), aw // kvw2)),
                  pl.BlockSpec((BLOCK, kvw2),
                               lambda b, i: (jnp.maximum(tile(b, i) * bpt - 1, 0), aw // kvw2)),
                  pl.BlockSpec((rows, aw), scan),
                  pl.BlockSpec((rows, aw), scan),
                  _resident(bias.shape), _resident(sink_rows.shape),
                  _resident(qw.shape), _resident(kw.shape),
                  _resident(bd.shape), _resident(wap.shape),
                  pl.BlockSpec((rows, xbc.shape[1]), scan),
                  pl.BlockSpec((rows, dt.shape[1]), scan),
                  pl.BlockSpec((rows, sw), scan),
                  _resident(alog.shape), _resident(dsk.shape), _resident(nw.shape),
                  _resident(tri.shape),
                  pl.BlockSpec((rows, d), lambda b, i: (b * nt + jnp.maximum(i - 1, 0), 1)),
                  pl.BlockSpec((rows, d), done),
                  pl.BlockSpec((1, 1, d), lambda b, i: (b, 0, 0)),
                  _resident(wsp.shape), _resident(wout.shape)],
        out_specs=pl.BlockSpec((rows, d), done),
        out_shape=jax.ShapeDtypeStruct((t, d), F32),
        scratch_shapes=[pltpu.VMEM((rows, aw), BF16),
                        pltpu.VMEM((rows + BLOCK, ATTN_KV_HEADS * LANES), BF16),
                        pltpu.VMEM((rows + BLOCK, ATTN_KV_HEADS * LANES), BF16),
                        pltpu.VMEM((kvw2 // 2, rows + BLOCK), BF16),
                        pltpu.VMEM((2, 4 * BLOCK, 2 * BLOCK), F32),
                        pltpu.VMEM((2, 2 * BLOCK, 4 * BLOCK), BF16),
                        pltpu.VMEM((aw, rows), F32),
                        pltpu.VMEM((rows, aw), BF16),
                        pltpu.VMEM((rows, d), BF16),
                        pltpu.VMEM((SSM_GROUPS, SSM_STATE, sw // SSM_GROUPS), F32),
                        pltpu.VMEM((rows, sw), BF16),
                        pltpu.VMEM((rows, sw), BF16),
                        pltpu.VMEM((rows, d), BF16)],
        compiler_params=pltpu.CompilerParams(
            dimension_semantics=("arbitrary", "arbitrary"), vmem_limit_bytes=VMEM_LIMIT),
        name="branches",
    )(qkv, qkv, qkv, za, g, bias, sink_rows, qw, kw, bd, wap,
      xbc, dt, zm, alog, dsk, nw, tri, g, x2, gate, wsp, wout)


def _layer(x, c, w_ada, b_ada, norm_w, w_in, q_norm_w, k_norm_w, rel_bias, sinks,
           conv_w, conv_b, dt_bias, a_log, d_skip, ssm_norm_w, w_attn_proj, w_ssm_proj, w_out):
    batch, seq, d = x.shape
    aw = ATTN_HEADS * HEAD_DIM
    kvw = ATTN_KV_HEADS * HEAD_DIM
    sw = w_ssm_proj.shape[0]
    ssm_heads = sw // SSM_HEAD_DIM
    xbc_w = sw + 2 * SSM_GROUPS * SSM_STATE
    assert seq % BLOCK == 0 and seq % CHUNK == 0 and ssm_heads <= LANES

    c8 = jnp.zeros((8, d), F32).at[:batch].set(c)
    mod = _mod_call(c8, w_ada, b_ada.reshape(1, -1))[:batch]
    shift, scale, gate = (mod[:, k * d:(k + 1) * d].reshape(batch, 1, d) for k in range(3))

    o = np.cumsum([0, aw, kvw, kvw, aw, sw, xbc_w, ssm_heads, d, d])
    w_t = w_in.T
    w_all = _wprep_call(w_t, int(o[6]), ssm_heads)
    wdt = jnp.pad(w_t[o[6]:o[7]], ((0, LANES - ssm_heads), (0, 0))).astype(BF16)
    widths = [aw + 2 * kvw, aw, sw, xbc_w, 2 * d]
    pad_heads = lambda v: jnp.pad(v.astype(F32), (0, LANES - ssm_heads)).reshape(1, LANES)

    x2 = x.reshape(batch * seq, d)
    qkv, za, zm, xbc, g, dt = _inproj_call(
        x2, shift, scale, norm_w.reshape(1, d), w_all, wdt, pad_heads(dt_bias),
        conv_w.astype(F32), conv_b.reshape(1, -1).astype(F32), widths, batch, seq)

    bias = _bias_call(rel_bias.astype(F32))
    seg = np.arange(LANES) // HEAD_DIM
    bd = jnp.asarray((seg[:, None] == seg[None, :]).astype(np.float32) / HEAD_DIM, dtype=BF16)
    qw = jnp.tile(q_norm_w.astype(F32), LANES // HEAD_DIM).reshape(1, LANES)
    kw = jnp.tile(k_norm_w.astype(F32), LANES // HEAD_DIM).reshape(1, LANES)
    sink_rows = jnp.repeat(sinks.astype(F32).reshape(ATTN_KV_HEADS, 2, 2).transpose(0, 2, 1), BLOCK, axis=2)
    tri = jnp.asarray(np.tril(np.ones((CHUNK, CHUNK), np.float32)), dtype=BF16)
    out = _branches_call(qkv, za, g, bias, sink_rows, qw, kw, bd, w_attn_proj.astype(BF16),
                         xbc, dt, zm, pad_heads(a_log),
                         jnp.repeat(d_skip.astype(F32), SSM_HEAD_DIM).reshape(1, sw),
                         ssm_norm_w.reshape(1, sw).astype(F32), tri,
                         x2, gate, w_ssm_proj.astype(BF16), w_out.astype(BF16), batch, seq)
    return out.reshape(batch, seq, d)


def kernel(x, c, w_ada, b_ada, norm_w, w_in, q_norm_w, k_norm_w, rel_bias, sinks, conv_w, conv_b,
           dt_bias, a_log, d_skip, ssm_norm_w, w_attn_proj, w_ssm_proj, w_out):
    depth = w_in.shape[0]
    for i in range(depth):
        x = _layer(x, c, w_ada[i], b_ada[i], norm_w[i], w_in[i], q_norm_w[i], k_norm_w[i],
                   rel_bias, sinks[i], conv_w[i], conv_b[i], dt_bias[i], a_log[i], d_skip[i],
                   ssm_norm_w[i], w_attn_proj[i], w_ssm_proj[i], w_out[i])
    return x
```

```python
import functools
import math

import numpy as np
import jax
import jax.numpy as jnp
from jax import lax
from jax.experimental import pallas as pl
from jax.experimental.pallas import tpu as pltpu

F32 = jnp.float32
BF16 = jnp.bfloat16

HEAD_DIM = 64
ATTN_HEADS = 16
ATTN_KV_HEADS = 4
GROUP = ATTN_HEADS // ATTN_KV_HEADS
BLOCK = 128
REL_BUCKETS = 32
REL_MAX_DIST = 128
SSM_HEAD_DIM = 64
SSM_GROUPS = 4
SSM_STATE = 128
CONV_WIDTH = 4
CHUNK = 128
EPS = 1e-6
NEG = -1e30
LOG2E = 1.4426950408889634
LANES = 128
VMEM_LIMIT = 56 * 1024 * 1024

ROWS_INPROJ = 512
ROWS_BRANCHES = 256


def _sigmoid(x):
    return 0.5 + 0.5 * jnp.tanh(0.5 * x)


def _silu(x):
    half = 0.5 * x
    return half + half * jnp.tanh(half)


def _softplus(x):
    return jnp.maximum(x, 0.0) + jnp.log(1.0 + jnp.exp(-jnp.abs(x)))


def _dot(a, b):
    return jnp.dot(a, b, preferred_element_type=F32)


def _dot_nt(a, b):
    return lax.dot_general(a, b, (((1,), (1,)), ((), ())), preferred_element_type=F32)


def _resident(shape):
    nd = len(shape)
    return pl.BlockSpec(shape, lambda *_: (0,) * nd, pipeline_mode=pl.Buffered(1))


def _mod_kernel(c_ref, w_ref, b_ref, o_ref):
    s = _silu(c_ref[...])
    o_ref[...] = _dot(s.astype(BF16), w_ref[...].astype(BF16)) + b_ref[...]


def _mod_call(c8, w_ada, b_ada):
    d, n = w_ada.shape
    tn = 1024
    return pl.pallas_call(
        _mod_kernel,
        grid=(n // tn,),
        in_specs=[pl.BlockSpec((8, d), lambda j: (0, 0)),
                  pl.BlockSpec((d, tn), lambda j: (0, j)),
                  pl.BlockSpec((1, tn), lambda j: (0, j))],
        out_specs=pl.BlockSpec((8, tn), lambda j: (0, j)),
        out_shape=jax.ShapeDtypeStruct((8, n), F32),
        name="mod",
    )(c8, w_ada, b_ada)


def _bucket_table():
    qi = np.arange(BLOCK)[:, None]
    kj = np.arange(2 * BLOCK)[None, :]
    dist = qi + BLOCK - kj
    n = np.maximum(dist, 0)
    max_exact = REL_BUCKETS // 2
    nf = np.maximum(n, 1).astype(np.float32)
    large = max_exact + (np.log(nf / max_exact) / math.log(REL_MAX_DIST / max_exact)
                         * (REL_BUCKETS - max_exact)).astype(np.int32)
    large = np.minimum(large, REL_BUCKETS - 1)
    bucket = np.where(n < max_exact, n, large)
    valid = (dist >= 0) & (dist < BLOCK)
    return np.where(valid, bucket, -1).astype(np.int32)


def _bias_kernel(rb_ref, idx_ref, o_ref):
    kv = pl.program_id(0)
    idx = idx_ref[...]
    key_row = lax.broadcasted_iota(jnp.int32, idx.shape, 0)
    keep = idx >= 0
    keep_first = jnp.logical_and(keep, key_row >= BLOCK)
    for s in range(2):
        for c in range(2):
            acc = jnp.zeros(idx.shape, F32)
            for b in range(REL_BUCKETS):
                acc = jnp.where(idx == b, rb_ref[b, kv * GROUP + 2 * c + s], acc)
            acc = acc * LOG2E
            rows, cols = slice(s * 2 * BLOCK, (s + 1) * 2 * BLOCK), slice(c * BLOCK, (c + 1) * BLOCK)
            o_ref[0, 0, rows, cols] = jnp.where(keep, acc, NEG)
            o_ref[1, 0, rows, cols] = jnp.where(keep_first, acc, NEG)


def _bias_call(rel_bias):
    idx_t = jnp.asarray(np.ascontiguousarray(_bucket_table().T))
    table = pl.pallas_call(
        _bias_kernel,
        grid=(ATTN_KV_HEADS,),
        in_specs=[pl.BlockSpec(memory_space=pltpu.SMEM),
                  pl.BlockSpec((2 * BLOCK, BLOCK), lambda h: (0, 0))],
        out_specs=pl.BlockSpec((2, 1, 4 * BLOCK, 2 * BLOCK), lambda h: (0, h, 0, 0)),
        out_shape=jax.ShapeDtypeStruct((2, ATTN_KV_HEADS, 4 * BLOCK, 2 * BLOCK), F32),
        name="bias",
    )(rel_bias, idx_t)
    return table.reshape(2 * ATTN_KV_HEADS, 4 * BLOCK, 2 * BLOCK)


def _wprep_kernel(wt_ref, o_ref):
    o_ref[...] = wt_ref[...].T.astype(BF16)


def _wprep_call(w_t, dt_start, dt_width):
    n, k = w_t.shape
    tn = 512
    head_tiles, tail_tiles = dt_start // tn, (n - dt_start - dt_width) // tn
    assert head_tiles * tn == dt_start and tail_tiles * tn == n - dt_start - dt_width
    tail_start = dt_start + dt_width

    assert tail_start % 8 == 0

    def rows(i):
        start = jnp.where(i < head_tiles, i * tn, tail_start + (i - head_tiles) * tn)
        return (pl.multiple_of(start, 8), 0)

    return pl.pallas_call(
        _wprep_kernel,
        grid=(head_tiles + tail_tiles,),
        in_specs=[pl.BlockSpec((pl.Element(tn), pl.Element(k)), rows)],
        out_specs=pl.BlockSpec((k, tn), lambda i: (0, i)),
        out_shape=jax.ShapeDtypeStruct((k, n - dt_width), BF16),
        compiler_params=pltpu.CompilerParams(dimension_semantics=("parallel",)),
        name="wprep",
    )(w_t)


def _inproj_kernel(x_ref, shift_ref, scale_ref, nw_ref, w_ref, wdt_ref, dtb_ref, cw_ref, cb_ref,
                   qkv_o, za_o, zm_o, xbc_o, g_o, dt_o, xraw_s, carry_s):
    ts = x_ref.shape[0]
    halo = carry_s.shape[1]

    @pl.when(pl.program_id(1) == 0)
    def _():
        carry_s[...] = jnp.zeros(carry_s.shape, F32)

    x = x_ref[...]
    ms = jnp.mean(x * x, axis=-1, keepdims=True)
    gain = nw_ref[...] * (1.0 + scale_ref[0])
    h = (x * lax.rsqrt(ms + EPS) * gain + shift_ref[0]).astype(BF16)

    def conv_silu_store(y, c0, slot):
        for jj in range(y.shape[1] // LANES):
            j = c0 // LANES + jj
            jl = slice(j * LANES, (j + 1) * LANES)
            yj = y[:, jj * LANES:(jj + 1) * LANES]
            xraw_s[slot, jj, 0:halo, :] = carry_s[j]
            xraw_s[slot, jj, halo:halo + ts, :] = yj
            carry_s[j] = yj[ts - halo:ts, :]
            acc = cb_ref[:, jl]
            for w in range(CONV_WIDTH):
                off = halo - (CONV_WIDTH - 1) + w
                acc = acc + cw_ref[w:w + 1, jl] * xraw_s[slot, jj, off:off + ts, :]
            xbc_o[:, jl] = _silu(acc).astype(BF16)

    step = xraw_s.shape[1] * LANES
    light, heavy, col = [], [], 0
    for o_ref, act in ((qkv_o, None), (za_o, _silu), (zm_o, _silu), (xbc_o, "conv"), (g_o, _sigmoid)):
        for c0 in range(0, o_ref.shape[1], step):
            (heavy if act == "conv" else light).append((o_ref, act, col + c0, c0))
        col += o_ref.shape[1]
    gates = [p for p in light if p[1] is _sigmoid]
    others = [p for p in light if p[1] is not _sigmoid]
    order = []
    for piece in heavy:
        order += [piece, others.pop(0), gates.pop(0) if gates else others.pop(0)]
    order += others + gates
    slot = 0
    for k, (o_ref, act, wc, c0) in enumerate(order):
        w_piece = w_ref[:, wc:wc + step]
        if k == 0:
            y = jnp.concatenate([_dot(h[:ts // 2], w_piece), _dot(h[ts // 2:], w_piece)], axis=0)
        else:
            y = _dot(h, w_piece)
        if act == "conv":
            conv_silu_store(y, c0, slot)
            slot = 1 - slot
        else:
            o_ref[:, c0:c0 + step] = (y if act is None else act(y)).astype(o_ref.dtype)
    dt_o[...] = _softplus(_dot_nt(h, wdt_ref[...]) + dtb_ref[...])


def _inproj_call(x2, shift, scale, norm_w, w_all, wdt, dtb, cw, cb, widths, batch, seq):
    t, d = x2.shape
    ts = min(ROWS_INPROJ, seq)
    ns = seq // ts
    row = lambda b, i: (b * ns + i, 0)
    per_b = pl.BlockSpec((1, 1, d), lambda b, i: (b, 0, 0))
    widths = list(widths) + [wdt.shape[0]]
    dtypes = [BF16] * (len(widths) - 1) + [F32]
    piece_blocks = 4
    return pl.pallas_call(
        _inproj_kernel,
        grid=(batch, ns),
        in_specs=[pl.BlockSpec((ts, d), row), per_b, per_b, _resident((1, d)),
                  _resident(w_all.shape), _resident(wdt.shape), _resident(dtb.shape),
                  _resident(cw.shape), _resident(cb.shape)],
        out_specs=[pl.BlockSpec((ts, n), row) for n in widths],
        out_shape=[jax.ShapeDtypeStruct((t, n), dt) for n, dt in zip(widths, dtypes)],
        scratch_shapes=[pltpu.VMEM((2, piece_blocks, ts + 8, LANES), F32),
                        pltpu.VMEM((cw.shape[1] // LANES, 8, LANES), F32)],
        compiler_params=pltpu.CompilerParams(
            dimension_semantics=("arbitrary", "arbitrary"), vmem_limit_bytes=VMEM_LIMIT),
        name="inproj",
    )(x2, shift, scale, norm_w, w_all, wdt, dtb, cw, cb)


def _segment_rms(x, bd, w):
    ms = _dot((x * x).astype(BF16), bd)
    return x * lax.rsqrt(ms + EPS) * w


def _branches_kernel(q_ref, kvc_ref, kvp_ref, za_ref, ga_ref, bias_ref, sink_ref,
                     qw_ref, kw_ref, bd_ref, wap_ref,
                     xbc_ref, dt_ref, zm_ref, alog_ref, dsk_ref, nw_ref, tri_ref,
                     gb_ref, x_ref, gate_ref, wsp_ref, wout_ref,
                     o_ref,
                     qn_s, klo_s, khi_s, vt_s, s_s, p_s, yt_s, y_s, ya_s, h_s, yn_s, yprev_s, merged_s):
    rows = q_ref.shape[0]
    nblk = rows // BLOCK
    nchunk = rows // CHUNK
    kvw = ATTN_KV_HEADS * HEAD_DIM
    sw = zm_ref.shape[1]
    gw = sw // SSM_GROUPS
    pairs_per_group = gw // LANES
    n_xs = sw // LANES
    first_tile = pl.program_id(1) == 0

    @pl.when(first_tile)
    def _():
        h_s[...] = jnp.zeros(h_s.shape, F32)
        yprev_s[...] = jnp.zeros(yprev_s.shape, BF16)
        ya_s[...] = jnp.zeros(ya_s.shape, BF16)

    d_out = o_ref.shape[1]
    piece_w = 2 * LANES

    def merge_piece(c0):
        cs = slice(c0, c0 + piece_w)
        yb = _dot(yprev_s[...], wsp_ref[:, cs])
        merged_s[:, cs] = ya_s[:, cs] + gb_ref[:, cs] * yb.astype(BF16)

    def out_piece(c0):
        cs = slice(c0, c0 + piece_w)
        o = _dot(merged_s[...], wout_ref[:, cs])
        o_ref[:, cs] = x_ref[:, cs] + gate_ref[0][:, cs] * o

    finish = ([functools.partial(merge_piece, c0) for c0 in range(0, d_out, piece_w)]
              + [functools.partial(out_piece, c0) for c0 in range(0, d_out, piece_w)])

    bd = bd_ref[...]
    qw = qw_ref[...] * (HEAD_DIM ** -0.5 * LOG2E)
    kw = kw_ref[...]
    low_lanes = lax.broadcasted_iota(jnp.int32, (1, LANES), 1) < HEAD_DIM

    for j in range(q_ref.shape[1] // LANES):
        sl = slice(j * LANES, (j + 1) * LANES)
        qn_s[:, sl] = _segment_rms(q_ref[:, sl].astype(F32), bd, qw).astype(BF16)
    for j in range(kvw // LANES):
        sl = slice(j * LANES, (j + 1) * LANES)
        even = slice(2 * j * LANES, (2 * j + 1) * LANES)
        odd = slice((2 * j + 1) * LANES, (2 * j + 2) * LANES)
        for dst, src in ((slice(0, BLOCK), kvp_ref), (slice(BLOCK, BLOCK + rows), kvc_ref)):
            kn = _segment_rms(src[:, sl].astype(F32), bd, kw)
            lo = jnp.where(low_lanes, kn, 0.0)
            hi = jnp.where(low_lanes, 0.0, kn)
            klo_s[dst, even] = lo.astype(BF16)
            khi_s[dst, odd] = hi.astype(BF16)
            khi_s[dst, even] = pltpu.roll(lo, HEAD_DIM, axis=1).astype(BF16)
            klo_s[dst, odd] = pltpu.roll(hi, HEAD_DIM, axis=1).astype(BF16)
    vt_s[:, 0:BLOCK] = kvp_ref[:, kvw:2 * kvw].astype(F32).T.astype(BF16)
    vt_s[:, BLOCK:] = kvc_ref[:, kvw:2 * kvw].astype(F32).T.astype(BF16)

    units = [(qb, h) for qb in range(nblk) for h in range(ATTN_KV_HEADS)]

    def scores(u):
        qb, h = units[u]
        r0 = qb * BLOCK
        hl = slice(h * LANES, (h + 1) * LANES)
        k2 = jnp.concatenate([klo_s[r0:r0 + 2 * BLOCK, hl], khi_s[r0:r0 + 2 * BLOCK, hl]], axis=0)
        q2 = jnp.concatenate([qn_s[r0:r0 + BLOCK, (2 * h + c) * LANES:(2 * h + c + 1) * LANES]
                              for c in range(2)], axis=0)
        entry = jnp.where(first_tile, ATTN_KV_HEADS, 0) + h if qb == 0 else h
        s_s[u % 2] = _dot_nt(k2, q2) + bias_ref[entry]

    def softmax_pv(u):
        qb, h = units[u]
        r0 = qb * BLOCK
        slot = u % 2
        inv = []
        for s in range(2):
            for c in range(2):
                blk = s_s[slot, s * 2 * BLOCK:(s + 1) * 2 * BLOCK, c * BLOCK:(c + 1) * BLOCK]
                sink = sink_ref[h, s:s + 1, c * BLOCK:(c + 1) * BLOCK] * LOG2E
                m = jnp.maximum(jnp.max(blk, axis=0, keepdims=True), sink)
                e = jnp.exp2(blk - m)
                den = jnp.sum(e, axis=0, keepdims=True) + jnp.exp2(sink - m)
                p_s[slot, :, (2 * s + c) * BLOCK:(2 * s + c + 1) * BLOCK] = e.astype(BF16)
                inv.append(1.0 / den)
        o_t = _dot(vt_s[h * HEAD_DIM:(h + 1) * HEAD_DIM, r0:r0 + 2 * BLOCK], p_s[slot])
        for s in range(2):
            for c in range(2):
                head = GROUP * h + 2 * c + s
                k = 2 * s + c
                yt_s[head * HEAD_DIM:(head + 1) * HEAD_DIM, r0:r0 + BLOCK] = (
                    o_t[:, k * BLOCK:(k + 1) * BLOCK] * inv[k])

    scores(0)
    attention = []
    for u in range(len(units)):
        def unit(u=u):
            if u + 1 < len(units):
                scores(u + 1)
            softmax_pv(u)
        attention.append(unit)

    def gate_y():
        y_s[...] = (yt_s[...].T * za_ref[...].astype(F32)).astype(BF16)

    def proj_piece(c0):
        cs = slice(c0, c0 + piece_w)
        ya = _dot(y_s[...], wap_ref[:, cs])
        ya_s[:, cs] = (ya * ga_ref[:, cs].astype(F32)).astype(BF16)

    first_proj = len(attention) + 1
    attention.append(gate_y)
    attention += [functools.partial(proj_piece, c0) for c0 in range(0, d_out, piece_w)]
    n_groups = nchunk * SSM_GROUPS
    base, extra = divmod(len(attention), n_groups)
    attn_counts = [base + (k < extra) for k in range(n_groups)]
    first_proj_group = next(k for k in range(n_groups) if sum(attn_counts[:k + 1]) > first_proj)
    assert len(finish) <= n_groups and first_proj_group >= len(finish) // 2 - 1

    def lane_block(j, r0):
        return xbc_ref[r0:r0 + CHUNK, j * LANES:(j + 1) * LANES]

    a_row = -jnp.exp(alog_ref[...])
    tri = tri_ref[...]
    li = lax.broadcasted_iota(jnp.int32, (CHUNK, CHUNK), 0)
    si = lax.broadcasted_iota(jnp.int32, (CHUNK, CHUNK), 1)
    causal = li >= si
    low_half = si < SSM_HEAD_DIM
    lo_mask = (lax.broadcasted_iota(jnp.int32, (1, LANES), 1) < SSM_HEAD_DIM).astype(BF16)
    hi_mask = 1 - lo_mask

    for c in range(nchunk):
        r0 = c * CHUNK
        dt = dt_ref[r0:r0 + CHUNK, :]
        dta = dt * a_row
        dta_hi = dta.astype(BF16)
        dta_lo = (dta - dta_hi.astype(F32)).astype(BF16)
        a2 = (_dot(tri, dta_hi) + _dot(tri, dta_lo)) * LOG2E
        a2_t = a2.T
        dt_t = dt.T
        row2_t = a2_t - jnp.log2(dt_t)
        a_end2_t = a2_t[:, CHUNK - 1:CHUNK]
        w_t = dt_t * jnp.exp2(a_end2_t - a2_t)
        e_end_t = jnp.exp2(a_end2_t)

        for g in range(SSM_GROUPS):
            if finish:
                finish.pop(0)()
            for _ in range(attn_counts[c * SSM_GROUPS + g]):
                attention.pop(0)()
            bm_gb = lane_block(n_xs + g, r0)
            cm_gb = lane_block(n_xs + SSM_GROUPS + g, r0)
            cb = _dot_nt(cm_gb, bm_gb)
            bm_gt = bm_gb.astype(F32).T
            gsl = slice(g * gw, (g + 1) * gw)
            hprev = h_s[g]
            y_off = _dot(cm_gb, hprev.astype(BF16))
            y_parts = []
            for p in range(pairs_per_group):
                pair = g * pairs_per_group + p
                lsl = slice(pair * LANES, (pair + 1) * LANES)
                xs_b = lane_block(pair, r0)
                xs_pair = xs_b.astype(F32)
                rhs = jnp.concatenate([xs_b * lo_mask, xs_b * hi_mask], axis=0)
                m_parts, b_parts, cols, e_ends = [], [], [], []
                for side in range(2):
                    hh = 2 * pair + side
                    col2 = jnp.broadcast_to(a2[:, hh:hh + 1], (CHUNK, CHUNK))
                    dec_dt = jnp.exp2(jnp.where(causal, col2 - row2_t[hh:hh + 1, :], NEG))
                    m_parts.append(cb * dec_dt)
                    b_parts.append(bm_gt * w_t[hh:hh + 1, :])
                    cols.append(col2)
                    e_ends.append(e_end_t[hh:hh + 1, :])
                lhs = jnp.concatenate(
                    [jnp.concatenate(m_parts, axis=1), jnp.concatenate(b_parts, axis=1)],
                    axis=0).astype(BF16)
                res = _dot(lhs, rhs)
                e_col = jnp.exp2(jnp.where(low_half, cols[0], cols[1]))
                y_pair = res[0:CHUNK] + e_col * y_off[:, p * LANES:(p + 1) * LANES]
                y_pair = y_pair + dsk_ref[:, lsl] * xs_pair
                y_parts.append(y_pair * zm_ref[r0:r0 + CHUNK, lsl].astype(F32))
                e_end = jnp.where(low_half, e_ends[0], e_ends[1])
                h_s[g, :, p * LANES:(p + 1) * LANES] = (
                    hprev[:, p * LANES:(p + 1) * LANES] * e_end + res[CHUNK:])
            yg = jnp.concatenate(y_parts, axis=1)
            ms = jnp.mean(yg * yg, axis=-1, keepdims=True)
            yn_s[r0:r0 + CHUNK, gsl] = (yg * lax.rsqrt(ms + EPS) * nw_ref[:, gsl]).astype(BF16)

    assert not finish and not attention
    yprev_s[...] = yn_s[...]


def _branches_call(qkv, za, g, bias, sink_rows, qw, kw, bd, wap,
                   xbc, dt, zm, alog, dsk, nw, tri, x2, gate, wsp, wout, batch, seq):
    t, d = x2.shape
    aw = ATTN_HEADS * HEAD_DIM
    kvw2 = 2 * ATTN_KV_HEADS * HEAD_DIM
    sw = zm.shape[1]
    rows = min(ROWS_BRANCHES, seq)
    nt = seq // rows
    bpt = rows // BLOCK
    tile = lambda b, i: b * nt + jnp.minimum(i, nt - 1)
    scan = lambda b, i: (tile(b, i), 0)
    done = lambda b, i: (b * nt + jnp.maximum(i - 1, 0), 0)
    return pl.pallas_call(
        _branches_kernel,
        grid=(batch, nt + 1),
        in_specs=[pl.BlockSpec((rows, aw), scan),
                  pl.BlockSpec((rows, kvw2), lambda b, i: (tile(b, i), aw // kvw2)),
                  pl.BlockSpec((BLOCK, kvw2),
                               lambda b, i: (jnp.maximum(tile(b, i) * bpt - 1, 0), aw // kvw2)),
                  pl.BlockSpec((rows, aw), scan),
                  pl.BlockSpec((rows, aw), scan),
                  _resident(bias.shape), _resident(sink_rows.shape),
                  _resident(qw.shape), _resident(kw.shape),
                  _resident(bd.shape), _resident(wap.shape),
                  pl.BlockSpec((rows, xbc.shape[1]), scan),
                  pl.BlockSpec((rows, dt.shape[1]), scan),
                  pl.BlockSpec((rows, sw), scan),
                  _resident(alog.shape), _resident(dsk.shape), _resident(nw.shape),
                  _resident(tri.shape),
                  pl.BlockSpec((rows, d), lambda b, i: (b * nt + jnp.maximum(i - 1, 0), 1)),
                  pl.BlockSpec((rows, d), done),
                  pl.BlockSpec((1, 1, d), lambda b, i: (b, 0, 0)),
                  _resident(wsp.shape), _resident(wout.shape)],
        out_specs=pl.BlockSpec((rows, d), done),
        out_shape=jax.ShapeDtypeStruct((t, d), F32),
        scratch_shapes=[pltpu.VMEM((rows, aw), BF16),
                        pltpu.VMEM((rows + BLOCK, ATTN_KV_HEADS * LANES), BF16),
                        pltpu.VMEM((rows + BLOCK, ATTN_KV_HEADS * LANES), BF16),
                        pltpu.VMEM((kvw2 // 2, rows + BLOCK), BF16),
                        pltpu.VMEM((2, 4 * BLOCK, 2 * BLOCK), F32),
                        pltpu.VMEM((2, 2 * BLOCK, 4 * BLOCK), BF16),
                        pltpu.VMEM((aw, rows), F32),
                        pltpu.VMEM((rows, aw), BF16),
                        pltpu.VMEM((rows, d), BF16),
                        pltpu.VMEM((SSM_GROUPS, SSM_STATE, sw // SSM_GROUPS), F32),
                        pltpu.VMEM((rows, sw), BF16),
                        pltpu.VMEM((rows, sw), BF16),
                        pltpu.VMEM((rows, d), BF16)],
        compiler_params=pltpu.CompilerParams(
            dimension_semantics=("arbitrary", "arbitrary"), vmem_limit_bytes=VMEM_LIMIT),
        name="branches",
    )(qkv, qkv, qkv, za, g, bias, sink_rows, qw, kw, bd, wap,
      xbc, dt, zm, alog, dsk, nw, tri, g, x2, gate, wsp, wout)


def _layer(x, c, w_ada, b_ada, norm_w, w_in, q_norm_w, k_norm_w, rel_bias, sinks,
           conv_w, conv_b, dt_bias, a_log, d_skip, ssm_norm_w, w_attn_proj, w_ssm_proj, w_out):
    batch, seq, d = x.shape
    aw = ATTN_HEADS * HEAD_DIM
    kvw = ATTN_KV_HEADS * HEAD_DIM
    sw = w_ssm_proj.shape[0]
    ssm_heads = sw // SSM_HEAD_DIM
    xbc_w = sw + 2 * SSM_GROUPS * SSM_STATE
    assert seq % BLOCK == 0 and seq % CHUNK == 0 and ssm_heads <= LANES

    c8 = jnp.zeros((8, d), F32).at[:batch].set(c)
    mod = _mod_call(c8, w_ada, b_ada.reshape(1, -1))[:batch]
    shift, scale, gate = (mod[:, k * d:(k + 1) * d].reshape(batch, 1, d) for k in range(3))

    o = np.cumsum([0, aw, kvw, kvw, aw, sw, xbc_w, ssm_heads, d, d])
    w_t = w_in.T
    w_all = _wprep_call(w_t, int(o[6]), ssm_heads)
    wdt = jnp.pad(w_t[o[6]:o[7]], ((0, LANES - ssm_heads), (0, 0))).astype(BF16)
    widths = [aw + 2 * kvw, aw, sw, xbc_w, 2 * d]
    pad_heads = lambda v: jnp.pad(v.astype(F32), (0, LANES - ssm_heads)).reshape(1, LANES)

    x2 = x.reshape(batch * seq, d)
    qkv, za, zm, xbc, g, dt = _inproj_call(
        x2, shift, scale, norm_w.reshape(1, d), w_all, wdt, pad_heads(dt_bias),
        conv_w.astype(F32), conv_b.reshape(1, -1).astype(F32), widths, batch, seq)

    bias = _bias_call(rel_bias.astype(F32))
    seg = np.arange(LANES) // HEAD_DIM
    bd = jnp.asarray((seg[:, None] == seg[None, :]).astype(np.float32) / HEAD_DIM, dtype=BF16)
    qw = jnp.tile(q_norm_w.astype(F32), LANES // HEAD_DIM).reshape(1, LANES)
    kw = jnp.tile(k_norm_w.astype(F32), LANES // HEAD_DIM).reshape(1, LANES)
    sink_rows = jnp.repeat(sinks.astype(F32).reshape(ATTN_KV_HEADS, 2, 2).transpose(0, 2, 1), BLOCK, axis=2)
    tri = jnp.asarray(np.tril(np.ones((CHUNK, CHUNK), np.float32)), dtype=BF16)
    out = _branches_call(qkv, za, g, bias, sink_rows, qw, kw, bd, w_attn_proj.astype(BF16),
                         xbc, dt, zm, pad_heads(a_log),
                         jnp.repeat(d_skip.astype(F32), SSM_HEAD_DIM).reshape(1, sw),
                         ssm_norm_w.reshape(1, sw).astype(F32), tri,
                         x2, gate, w_ssm_proj.astype(BF16), w_out.astype(BF16), batch, seq)
    return out.reshape(batch, seq, d)


def kernel(x, c, w_ada, b_ada, norm_w, w_in, q_norm_w, k_norm_w, rel_bias, sinks, conv_w, conv_b,
           dt_bias, a_log, d_skip, ssm_norm_w, w_attn_proj, w_ssm_proj, w_out):
    depth = w_in.shape[0]
    for i in range(depth):
        x = _layer(x, c, w_ada[i], b_ada[i], norm_w[i], w_in[i], q_norm_w[i], k_norm_w[i],
                   rel_bias, sinks[i], conv_w[i], conv_b[i], dt_bias[i], a_log[i], d_skip[i],
                   ssm_norm_w[i], w_attn_proj[i], w_ssm_proj[i], w_out[i])
    return x
```

```python
import functools
import math

import numpy as np
import jax
import jax.numpy as jnp
from jax import lax
from jax.experimental import pallas as pl
from jax.experimental.pallas import tpu as pltpu

F32 = jnp.float32
BF16 = jnp.bfloat16

HEAD_DIM = 64
ATTN_HEADS = 16
ATTN_KV_HEADS = 4
GROUP = ATTN_HEADS // ATTN_KV_HEADS
BLOCK = 128
REL_BUCKETS = 32
REL_MAX_DIST = 128
SSM_HEAD_DIM = 64
SSM_GROUPS = 4
SSM_STATE = 128
CONV_WIDTH = 4
CHUNK = 128
EPS = 1e-6
NEG = -1e30
LOG2E = 1.4426950408889634
LANES = 128
VMEM_LIMIT = 56 * 1024 * 1024

ROWS_INPROJ = 512
ROWS_BRANCHES = 256


def _sigmoid(x):
    return 0.5 + 0.5 * jnp.tanh(0.5 * x)


def _silu(x):
    half = 0.5 * x
    return half + half * jnp.tanh(half)


def _softplus(x):
    return jnp.maximum(x, 0.0) + jnp.log(1.0 + jnp.exp(-jnp.abs(x)))


def _dot(a, b):
    return jnp.dot(a, b, preferred_element_type=F32)


def _dot_nt(a, b):
    return lax.dot_general(a, b, (((1,), (1,)), ((), ())), preferred_element_type=F32)


def _resident(shape):
    nd = len(shape)
    return pl.BlockSpec(shape, lambda *_: (0,) * nd, pipeline_mode=pl.Buffered(1))


def _param_table(rows):
    width = max(sum(v.shape[0] for _, v in r) for r in rows)
    layout, lines = {}, []
    for i, r in enumerate(rows):
        col, parts = 0, []
        for name, v in r:
            assert col % LANES == 0 and v.shape[0] % LANES == 0
            layout[name] = (i, col, v.shape[0])
            parts.append(v.astype(F32))
            col += v.shape[0]
        if col < width:
            parts.append(jnp.zeros((width - col,), F32))
        lines.append(jnp.concatenate(parts))
    return jnp.stack(lines), layout


def _param(p_ref, layout, name, lanes=None):
    r, c, n = layout[name]
    lanes = slice(0, n) if lanes is None else lanes
    return p_ref[r:r + 1, c + lanes.start:c + lanes.stop]


def _mod_kernel(c_ref, w_ref, b_ref, o_ref):
    s = _silu(c_ref[...])
    o_ref[...] = _dot(s.astype(BF16), w_ref[...].astype(BF16)) + b_ref[...]


def _mod_call(c8, w_ada, b_ada):
    d, n = w_ada.shape
    tn = 1024
    return pl.pallas_call(
        _mod_kernel,
        grid=(n // tn,),
        in_specs=[pl.BlockSpec((8, d), lambda j: (0, 0)),
                  pl.BlockSpec((d, tn), lambda j: (0, j)),
                  pl.BlockSpec((1, tn), lambda j: (0, j))],
        out_specs=pl.BlockSpec((8, tn), lambda j: (0, j)),
        out_shape=jax.ShapeDtypeStruct((8, n), F32),
        name="mod",
    )(c8, w_ada, b_ada)


def _bucket_table():
    qi = np.arange(BLOCK)[:, None]
    kj = np.arange(2 * BLOCK)[None, :]
    dist = qi + BLOCK - kj
    n = np.maximum(dist, 0)
    max_exact = REL_BUCKETS // 2
    nf = np.maximum(n, 1).astype(np.float32)
    large = max_exact + (np.log(nf / max_exact) / math.log(REL_MAX_DIST / max_exact)
                         * (REL_BUCKETS - max_exact)).astype(np.int32)
    large = np.minimum(large, REL_BUCKETS - 1)
    bucket = np.where(n < max_exact, n, large)
    valid = (dist >= 0) & (dist < BLOCK)
    return np.where(valid, bucket, -1).astype(np.int32)


def _bias_kernel(rb_ref, idx_ref, o_ref):
    kv = pl.program_id(0)
    idx = idx_ref[...]
    key_row = lax.broadcasted_iota(jnp.int32, idx.shape, 0)
    keep = idx >= 0
    keep_first = jnp.logical_and(keep, key_row >= BLOCK)
    for s in range(2):
        for c in range(2):
            acc = jnp.zeros(idx.shape, F32)
            for b in range(REL_BUCKETS):
                acc = jnp.where(idx == b, rb_ref[b, kv * GROUP + 2 * c + s], acc)
            acc = acc * LOG2E
            rows, cols = slice(s * 2 * BLOCK, (s + 1) * 2 * BLOCK), slice(c * BLOCK, (c + 1) * BLOCK)
            o_ref[0, 0, rows, cols] = jnp.where(keep, acc, NEG)
            o_ref[1, 0, rows, cols] = jnp.where(keep_first, acc, NEG)


def _bias_call(rel_bias):
    idx_t = jnp.asarray(np.ascontiguousarray(_bucket_table().T))
    table = pl.pallas_call(
        _bias_kernel,
        grid=(ATTN_KV_HEADS,),
        in_specs=[pl.BlockSpec(memory_space=pltpu.SMEM),
                  pl.BlockSpec((2 * BLOCK, BLOCK), lambda h: (0, 0))],
        out_specs=pl.BlockSpec((2, 1, 4 * BLOCK, 2 * BLOCK), lambda h: (0, h, 0, 0)),
        out_shape=jax.ShapeDtypeStruct((2, ATTN_KV_HEADS, 4 * BLOCK, 2 * BLOCK), F32),
        name="bias",
    )(rel_bias, idx_t)
    return table.reshape(2 * ATTN_KV_HEADS, 4 * BLOCK, 2 * BLOCK)


def _wprep_kernel(wt_ref, o_ref):
    o_ref[...] = wt_ref[...].T.astype(BF16)


def _wprep_call(w_t, dt_start, dt_width):
    n, k = w_t.shape
    tn = 512
    head_tiles, tail_tiles = dt_start // tn, (n - dt_start - dt_width) // tn
    assert head_tiles * tn == dt_start and tail_tiles * tn == n - dt_start - dt_width
    tail_start = dt_start + dt_width

    assert tail_start % 8 == 0

    def rows(i):
        start = jnp.where(i < head_tiles, i * tn, tail_start + (i - head_tiles) * tn)
        return (pl.multiple_of(start, 8), 0)

    return pl.pallas_call(
        _wprep_kernel,
        grid=(head_tiles + tail_tiles,),
        in_specs=[pl.BlockSpec((pl.Element(tn), pl.Element(k)), rows)],
        out_specs=pl.BlockSpec((k, tn), lambda i: (0, i)),
        out_shape=jax.ShapeDtypeStruct((k, n - dt_width), BF16),
        compiler_params=pltpu.CompilerParams(dimension_semantics=("parallel",)),
        name="wprep",
    )(w_t)


def _inproj_kernel(x_ref, shift_ref, scale_ref, p_ref, w_ref, wdt_ref,
                   qkv_o, za_o, zm_o, xbc_o, g_o, dt_o, xraw_s, carry_s, *, layout):
    ts = x_ref.shape[0]
    halo = carry_s.shape[1]

    @pl.when(pl.program_id(1) == 0)
    def _():
        carry_s[...] = jnp.zeros(carry_s.shape, F32)

    x = x_ref[...]
    ms = jnp.mean(x * x, axis=-1, keepdims=True)
    gain = _param(p_ref, layout, "norm_w") * (1.0 + scale_ref[0])
    h = (x * lax.rsqrt(ms + EPS) * gain + shift_ref[0]).astype(BF16)

    def conv_silu_store(y, c0, slot):
        for jj in range(y.shape[1] // LANES):
            j = c0 // LANES + jj
            jl = slice(j * LANES, (j + 1) * LANES)
            yj = y[:, jj * LANES:(jj + 1) * LANES]
            xraw_s[slot, jj, 0:halo, :] = carry_s[j]
            xraw_s[slot, jj, halo:halo + ts, :] = yj
            carry_s[j] = yj[ts - halo:ts, :]
            acc = _param(p_ref, layout, "conv_b", jl)
            for w in range(CONV_WIDTH):
                off = halo - (CONV_WIDTH - 1) + w
                acc = acc + _param(p_ref, layout, f"conv_w{w}", jl) * xraw_s[slot, jj, off:off + ts, :]
            xbc_o[:, jl] = _silu(acc).astype(BF16)

    step = xraw_s.shape[1] * LANES
    light, heavy, col = [], [], 0
    for o_ref, act in ((qkv_o, None), (za_o, _silu), (zm_o, _silu), (xbc_o, "conv"), (g_o, _sigmoid)):
        for c0 in range(0, o_ref.shape[1], step):
            (heavy if act == "conv" else light).append((o_ref, act, col + c0, c0))
        col += o_ref.shape[1]
    gates = [p for p in light if p[1] is _sigmoid]
    others = [p for p in light if p[1] is not _sigmoid]
    order = []
    for piece in heavy:
        order += [piece, others.pop(0), gates.pop(0) if gates else others.pop(0)]
    order += others + gates
    slot = 0
    for k, (o_ref, act, wc, c0) in enumerate(order):
        w_piece = w_ref[:, wc:wc + step]
        if k == 0:
            y = jnp.concatenate([_dot(h[:ts // 2], w_piece), _dot(h[ts // 2:], w_piece)], axis=0)
        else:
            y = _dot(h, w_piece)
        if act == "conv":
            conv_silu_store(y, c0, slot)
            slot = 1 - slot
        else:
            o_ref[:, c0:c0 + step] = (y if act is None else act(y)).astype(o_ref.dtype)
    dt_o[...] = _softplus(_dot_nt(h, wdt_ref[...]) + _param(p_ref, layout, "dt_bias"))


def _inproj_call(x2, shift, scale, params, layout, w_all, wdt, widths, batch, seq):
    t, d = x2.shape
    ts = min(ROWS_INPROJ, seq)
    ns = seq // ts
    row = lambda b, i: (b * ns + i, 0)
    per_b = pl.BlockSpec((1, 1, d), lambda b, i: (b, 0, 0))
    widths = list(widths) + [wdt.shape[0]]
    dtypes = [BF16] * (len(widths) - 1) + [F32]
    piece_blocks = 4
    return pl.pallas_call(
        functools.partial(_inproj_kernel, layout=layout),
        grid=(batch, ns),
        in_specs=[pl.BlockSpec((ts, d), row), per_b, per_b, _resident(params.shape),
                  _resident(w_all.shape), _resident(wdt.shape)],
        out_specs=[pl.BlockSpec((ts, n), row) for n in widths],
        out_shape=[jax.ShapeDtypeStruct((t, n), dt) for n, dt in zip(widths, dtypes)],
        scratch_shapes=[pltpu.VMEM((2, piece_blocks, ts + 8, LANES), F32),
                        pltpu.VMEM((widths[3] // LANES, 8, LANES), F32)],
        compiler_params=pltpu.CompilerParams(
            dimension_semantics=("arbitrary", "arbitrary"), vmem_limit_bytes=VMEM_LIMIT),
        name="inproj",
    )(x2, shift, scale, params, w_all, wdt)


def _segment_rms(x, bd, w):
    ms = _dot((x * x).astype(BF16), bd)
    return x * lax.rsqrt(ms + EPS) * w


def _branches_kernel(q_ref, kvc_ref, kvp_ref, za_ref, ga_ref, bias_ref, p_ref, bd_ref, wap_ref,
                     xbc_ref, dt_ref, zm_ref, tri_ref,
                     gb_ref, x_ref, gate_ref, wsp_ref, wout_ref,
                     o_ref,
                     qn_s, klo_s, khi_s, vt_s, s_s, p_s, yt_s, y_s, ya_s, h_s, yn_s, yprev_s, merged_s,
                     *, layout):
    rows = q_ref.shape[0]
    nblk = rows // BLOCK
    nchunk = rows // CHUNK
    kvw = ATTN_KV_HEADS * HEAD_DIM
    sw = zm_ref.shape[1]
    gw = sw // SSM_GROUPS
    pairs_per_group = gw // LANES
    n_xs = sw // LANES
    first_tile = pl.program_id(1) == 0

    @pl.when(first_tile)
    def _():
        h_s[...] = jnp.zeros(h_s.shape, F32)
        yprev_s[...] = jnp.zeros(yprev_s.shape, BF16)
        ya_s[...] = jnp.zeros(ya_s.shape, BF16)

    d_out = o_ref.shape[1]
    piece_w = 2 * LANES

    def merge_piece(c0):
        cs = slice(c0, c0 + piece_w)
        yb = _dot(yprev_s[...], wsp_ref[:, cs])
        merged_s[:, cs] = ya_s[:, cs] + gb_ref[:, cs] * yb.astype(BF16)

    def out_piece(c0):
        cs = slice(c0, c0 + piece_w)
        o = _dot(merged_s[...], wout_ref[:, cs])
        o_ref[:, cs] = x_ref[:, cs] + gate_ref[0][:, cs] * o

    finish = ([functools.partial(merge_piece, c0) for c0 in range(0, d_out, piece_w)]
              + [functools.partial(out_piece, c0) for c0 in range(0, d_out, piece_w)])

    bd = bd_ref[...]
    qw = _param(p_ref, layout, "q_norm_w") * (HEAD_DIM ** -0.5 * LOG2E)
    kw = _param(p_ref, layout, "k_norm_w")
    low_lanes = lax.broadcasted_iota(jnp.int32, (1, LANES), 1) < HEAD_DIM

    for j in range(q_ref.shape[1] // LANES):
        sl = slice(j * LANES, (j + 1) * LANES)
        qn_s[:, sl] = _segment_rms(q_ref[:, sl].astype(F32), bd, qw).astype(BF16)
    for j in range(kvw // LANES):
        sl = slice(j * LANES, (j + 1) * LANES)
        even = slice(2 * j * LANES, (2 * j + 1) * LANES)
        odd = slice((2 * j + 1) * LANES, (2 * j + 2) * LANES)
        for dst, src in ((slice(0, BLOCK), kvp_ref), (slice(BLOCK, BLOCK + rows), kvc_ref)):
            kn = _segment_rms(src[:, sl].astype(F32), bd, kw)
            lo = jnp.where(low_lanes, kn, 0.0)
            hi = jnp.where(low_lanes, 0.0, kn)
            klo_s[dst, even] = lo.astype(BF16)
            khi_s[dst, odd] = hi.astype(BF16)
            khi_s[dst, even] = pltpu.roll(lo, HEAD_DIM, axis=1).astype(BF16)
            klo_s[dst, odd] = pltpu.roll(hi, HEAD_DIM, axis=1).astype(BF16)
    vt_s[:, 0:BLOCK] = kvp_ref[:, kvw:2 * kvw].astype(F32).T.astype(BF16)
    vt_s[:, BLOCK:] = kvc_ref[:, kvw:2 * kvw].astype(F32).T.astype(BF16)

    units = [(qb, h) for qb in range(nblk) for h in range(ATTN_KV_HEADS)]

    def scores(u):
        qb, h = units[u]
        r0 = qb * BLOCK
        hl = slice(h * LANES, (h + 1) * LANES)
        k2 = jnp.concatenate([klo_s[r0:r0 + 2 * BLOCK, hl], khi_s[r0:r0 + 2 * BLOCK, hl]], axis=0)
        q2 = jnp.concatenate([qn_s[r0:r0 + BLOCK, (2 * h + c) * LANES:(2 * h + c + 1) * LANES]
                              for c in range(2)], axis=0)
        entry = jnp.where(first_tile, ATTN_KV_HEADS, 0) + h if qb == 0 else h
        s_s[u % 2] = _dot_nt(k2, q2) + bias_ref[entry]

    def softmax_pv(u):
        qb, h = units[u]
        r0 = qb * BLOCK
        slot = u % 2
        inv = []
        for s in range(2):
            for c in range(2):
                blk = s_s[slot, s * 2 * BLOCK:(s + 1) * 2 * BLOCK, c * BLOCK:(c + 1) * BLOCK]
                sc = ((h * 2 + s) * 2 + c) * BLOCK
                sink = _param(p_ref, layout, "sinks", slice(sc, sc + BLOCK)) * LOG2E
                m = jnp.maximum(jnp.max(blk, axis=0, keepdims=True), sink)
                e = jnp.exp2(blk - m)
                den = jnp.sum(e, axis=0, keepdims=True) + jnp.exp2(sink - m)
                p_s[slot, :, (2 * s + c) * BLOCK:(2 * s + c + 1) * BLOCK] = e.astype(BF16)
                inv.append(1.0 / den)
        o_t = _dot(vt_s[h * HEAD_DIM:(h + 1) * HEAD_DIM, r0:r0 + 2 * BLOCK], p_s[slot])
        for s in range(2):
            for c in range(2):
                head = GROUP * h + 2 * c + s
                k = 2 * s + c
                yt_s[head * HEAD_DIM:(head + 1) * HEAD_DIM, r0:r0 + BLOCK] = (
                    o_t[:, k * BLOCK:(k + 1) * BLOCK] * inv[k])

    scores(0)
    attention = []
    for u in range(len(units)):
        def unit(u=u):
            if u + 1 < len(units):
                scores(u + 1)
            softmax_pv(u)
        attention.append(unit)

    def gate_y():
        y_s[...] = (yt_s[...].T * za_ref[...].astype(F32)).astype(BF16)

    def proj_piece(c0):
        cs = slice(c0, c0 + piece_w)
        ya = _dot(y_s[...], wap_ref[:, cs])
        ya_s[:, cs] = (ya * ga_ref[:, cs].astype(F32)).astype(BF16)

    first_proj = len(attention) + 1
    attention.append(gate_y)
    attention += [functools.partial(proj_piece, c0) for c0 in range(0, d_out, piece_w)]
    n_groups = nchunk * SSM_GROUPS
    base, extra = divmod(len(attention), n_groups)
    attn_counts = [base + (k < extra) for k in range(n_groups)]
    first_proj_group = next(k for k in range(n_groups) if sum(attn_counts[:k + 1]) > first_proj)
    assert len(finish) <= n_groups and first_proj_group >= len(finish) // 2 - 1

    def lane_block(j, r0):
        return xbc_ref[r0:r0 + CHUNK, j * LANES:(j + 1) * LANES]

    a_row = -jnp.exp(_param(p_ref, layout, "a_log"))
    tri = tri_ref[...]
    li = lax.broadcasted_iota(jnp.int32, (CHUNK, CHUNK), 0)
    si = lax.broadcasted_iota(jnp.int32, (CHUNK, CHUNK), 1)
    causal = li >= si
    low_half = si < SSM_HEAD_DIM
    lo_mask = (lax.broadcasted_iota(jnp.int32, (1, LANES), 1) < SSM_HEAD_DIM).astype(BF16)
    hi_mask = 1 - lo_mask

    for c in range(nchunk):
        r0 = c * CHUNK
        dt = dt_ref[r0:r0 + CHUNK, :]
        dta = dt * a_row
        dta_hi = dta.astype(BF16)
        dta_lo = (dta - dta_hi.astype(F32)).astype(BF16)
        a2 = (_dot(tri, dta_hi) + _dot(tri, dta_lo)) * LOG2E
        a2_t = a2.T
        dt_t = dt.T
        row2_t = a2_t - jnp.log2(dt_t)
        a_end2_t = a2_t[:, CHUNK - 1:CHUNK]
        w_t = dt_t * jnp.exp2(a_end2_t - a2_t)
        e_end_t = jnp.exp2(a_end2_t)

        for g in range(SSM_GROUPS):
            if finish:
                finish.pop(0)()
            for _ in range(attn_counts[c * SSM_GROUPS + g]):
                attention.pop(0)()
            bm_gb = lane_block(n_xs + g, r0)
            cm_gb = lane_block(n_xs + SSM_GROUPS + g, r0)
            cb = _dot_nt(cm_gb, bm_gb)
            bm_gt = bm_gb.astype(F32).T
            gsl = slice(g * gw, (g + 1) * gw)
            hprev = h_s[g]
            y_off = _dot(cm_gb, hprev.astype(BF16))
            y_parts = []
            for p in range(pairs_per_group):
                pair = g * pairs_per_group + p
                lsl = slice(pair * LANES, (pair + 1) * LANES)
                xs_b = lane_block(pair, r0)
                xs_pair = xs_b.astype(F32)
                rhs = jnp.concatenate([xs_b * lo_mask, xs_b * hi_mask], axis=0)
                m_parts, b_parts, cols, e_ends = [], [], [], []
                for side in range(2):
                    hh = 2 * pair + side
                    col2 = jnp.broadcast_to(a2[:, hh:hh + 1], (CHUNK, CHUNK))
                    dec_dt = jnp.exp2(jnp.where(causal, col2 - row2_t[hh:hh + 1, :], NEG))
                    m_parts.append(cb * dec_dt)
                    b_parts.append(bm_gt * w_t[hh:hh + 1, :])
                    cols.append(col2)
                    e_ends.append(e_end_t[hh:hh + 1, :])
                lhs = jnp.concatenate(
                    [jnp.concatenate(m_parts, axis=1), jnp.concatenate(b_parts, axis=1)],
                    axis=0).astype(BF16)
                res = _dot(lhs, rhs)
                e_col = jnp.exp2(jnp.where(low_half, cols[0], cols[1]))
                y_pair = res[0:CHUNK] + e_col * y_off[:, p * LANES:(p + 1) * LANES]
                y_pair = y_pair + _param(p_ref, layout, "d_skip", lsl) * xs_pair
                y_parts.append(y_pair * zm_ref[r0:r0 + CHUNK, lsl].astype(F32))
                e_end = jnp.where(low_half, e_ends[0], e_ends[1])
                h_s[g, :, p * LANES:(p + 1) * LANES] = (
                    hprev[:, p * LANES:(p + 1) * LANES] * e_end + res[CHUNK:])
            yg = jnp.concatenate(y_parts, axis=1)
            ms = jnp.mean(yg * yg, axis=-1, keepdims=True)
            nw = _param(p_ref, layout, "ssm_norm_w", gsl)
            yn_s[r0:r0 + CHUNK, gsl] = (yg * lax.rsqrt(ms + EPS) * nw).astype(BF16)

    assert not finish and not attention
    yprev_s[...] = yn_s[...]


def _branches_call(qkv, za, g, bias, params, layout, bd, wap,
                   xbc, dt, zm, tri, x2, gate, wsp, wout, batch, seq):
    t, d = x2.shape
    aw = ATTN_HEADS * HEAD_DIM
    kvw2 = 2 * ATTN_KV_HEADS * HEAD_DIM
    sw = zm.shape[1]
    rows = min(ROWS_BRANCHES, seq)
    nt = seq // rows
    bpt = rows // BLOCK
    tile = lambda b, i: b * nt + jnp.minimum(i, nt - 1)
    scan = lambda b, i: (tile(b, i), 0)
    done = lambda b, i: (b * nt + jnp.maximum(i - 1, 0), 0)
    return pl.pallas_call(
        functools.partial(_branches_kernel, layout=layout),
        grid=(batch, nt + 1),
        in_specs=[pl.BlockSpec((rows, aw), scan),
                  pl.BlockSpec((rows, kvw2), lambda b, i: (tile(b, i), aw // kvw2)),
                  pl.BlockSpec((BLOCK, kvw2),
                               lambda b, i: (jnp.maximum(tile(b, i) * bpt - 1, 0), aw // kvw2)),
                  pl.BlockSpec((rows, aw), scan),
                  pl.BlockSpec((rows, aw), scan),
                  _resident(bias.shape), _resident(params.shape),
                  _resident(bd.shape), _resident(wap.shape),
                  pl.BlockSpec((rows, xbc.shape[1]), scan),
                  pl.BlockSpec((rows, dt.shape[1]), scan),
                  pl.BlockSpec((rows, sw), scan),
                  _resident(tri.shape),
                  pl.BlockSpec((rows, d), lambda b, i: (b * nt + jnp.maximum(i - 1, 0), 1)),
                  pl.BlockSpec((rows, d), done),
                  pl.BlockSpec((1, 1, d), lambda b, i: (b, 0, 0)),
                  _resident(wsp.shape), _resident(wout.shape)],
        out_specs=pl.BlockSpec((rows, d), done),
        out_shape=jax.ShapeDtypeStruct((t, d), F32),
        scratch_shapes=[pltpu.VMEM((rows, aw), BF16),
                        pltpu.VMEM((rows + BLOCK, ATTN_KV_HEADS * LANES), BF16),
                        pltpu.VMEM((rows + BLOCK, ATTN_KV_HEADS * LANES), BF16),
                        pltpu.VMEM((kvw2 // 2, rows + BLOCK), BF16),
                        pltpu.VMEM((2, 4 * BLOCK, 2 * BLOCK), F32),
                        pltpu.VMEM((2, 2 * BLOCK, 4 * BLOCK), BF16),
                        pltpu.VMEM((aw, rows), F32),
                        pltpu.VMEM((rows, aw), BF16),
                        pltpu.VMEM((rows, d), BF16),
                        pltpu.VMEM((SSM_GROUPS, SSM_STATE, sw // SSM_GROUPS), F32),
                        pltpu.VMEM((rows, sw), BF16),
                        pltpu.VMEM((rows, sw), BF16),
                        pltpu.VMEM((rows, d), BF16)],
        compiler_params=pltpu.CompilerParams(
            dimension_semantics=("arbitrary", "arbitrary"), vmem_limit_bytes=VMEM_LIMIT),
        name="branches",
    )(qkv, qkv, qkv, za, g, bias, params, bd, wap,
      xbc, dt, zm, tri, g, x2, gate, wsp, wout)


def _layer(x, c, w_ada, b_ada, norm_w, w_in, q_norm_w, k_norm_w, rel_bias, sinks,
           conv_w, conv_b, dt_bias, a_log, d_skip, ssm_norm_w, w_attn_proj, w_ssm_proj, w_out):
    batch, seq, d = x.shape
    aw = ATTN_HEADS * HEAD_DIM
    kvw = ATTN_KV_HEADS * HEAD_DIM
    sw = w_ssm_proj.shape[0]
    ssm_heads = sw // SSM_HEAD_DIM
    xbc_w = sw + 2 * SSM_GROUPS * SSM_STATE
    assert seq % BLOCK == 0 and seq % CHUNK == 0 and ssm_heads <= LANES

    c8 = jnp.zeros((8, d), F32).at[:batch].set(c)
    mod = _mod_call(c8, w_ada, b_ada.reshape(1, -1))[:batch]
    shift, scale, gate = (mod[:, k * d:(k + 1) * d].reshape(batch, 1, d) for k in range(3))

    o = np.cumsum([0, aw, kvw, kvw, aw, sw, xbc_w, ssm_heads, d, d])
    w_t = w_in.T
    w_all = _wprep_call(w_t, int(o[6]), ssm_heads)
    wdt = jnp.pad(w_t[o[6]:o[7]], ((0, LANES - ssm_heads), (0, 0))).astype(BF16)
    widths = [aw + 2 * kvw, aw, sw, xbc_w, 2 * d]

    pad_heads = lambda v: jnp.pad(v.astype(F32), (0, LANES - ssm_heads))
    pair = lambda v: jnp.tile(v.astype(F32), LANES // HEAD_DIM)
    sink_lanes = jnp.repeat(sinks.astype(F32).reshape(ATTN_KV_HEADS, 2, 2).transpose(0, 2, 1).reshape(-1), BLOCK)
    params, layout = _param_table(
        [[(f"conv_w{w}", conv_w[w])] for w in range(CONV_WIDTH)]
        + [[("conv_b", conv_b)],
           [("d_skip", jnp.repeat(d_skip.astype(F32), SSM_HEAD_DIM)), ("norm_w", norm_w)],
           [("ssm_norm_w", ssm_norm_w), ("q_norm_w", pair(q_norm_w)), ("k_norm_w", pair(k_norm_w)),
            ("dt_bias", pad_heads(dt_bias)), ("a_log", pad_heads(a_log))],
           [("sinks", sink_lanes)]])

    x2 = x.reshape(batch * seq, d)
    qkv, za, zm, xbc, g, dt = _inproj_call(x2, shift, scale, params, layout, w_all, wdt, widths, batch, seq)

    bias = _bias_call(rel_bias.astype(F32))
    seg = np.arange(LANES) // HEAD_DIM
    bd = jnp.asarray((seg[:, None] == seg[None, :]).astype(np.float32) / HEAD_DIM, dtype=BF16)
    tri = jnp.asarray(np.tril(np.ones((CHUNK, CHUNK), np.float32)), dtype=BF16)
    out = _branches_call(qkv, za, g, bias, params, layout, bd, w_attn_proj.astype(BF16),
                         xbc, dt, zm, tri,
                         x2, gate, w_ssm_proj.astype(BF16), w_out.astype(BF16), batch, seq)
    return out.reshape(batch, seq, d)


def kernel(x, c, w_ada, b_ada, norm_w, w_in, q_norm_w, k_norm_w, rel_bias, sinks, conv_w, conv_b,
           dt_bias, a_log, d_skip, ssm_norm_w, w_attn_proj, w_ssm_proj, w_out):
    depth = w_in.shape[0]
    for i in range(depth):
        x = _layer(x, c, w_ada[i], b_ada[i], norm_w[i], w_in[i], q_norm_w[i], k_norm_w[i],
                   rel_bias, sinks[i], conv_w[i], conv_b[i], dt_bias[i], a_log[i], d_skip[i],
                   ssm_norm_w[i], w_attn_proj[i], w_ssm_proj[i], w_out[i])
    return x
```

```python
import functools
import math

import numpy as np
import jax
import jax.numpy as jnp
from jax import lax
from jax.experimental import pallas as pl
from jax.experimental.pallas import tpu as pltpu

F32 = jnp.float32
BF16 = jnp.bfloat16

HEAD_DIM = 64
ATTN_HEADS = 16
ATTN_KV_HEADS = 4
GROUP = ATTN_HEADS // ATTN_KV_HEADS
BLOCK = 128
REL_BUCKETS = 32
REL_MAX_DIST = 128
SSM_HEAD_DIM = 64
SSM_GROUPS = 4
SSM_STATE = 128
CONV_WIDTH = 4
CHUNK = 128
EPS = 1e-6
NEG = -1e30
LOG2E = 1.4426950408889634
LANES = 128
VMEM_LIMIT = 56 * 1024 * 1024

ROWS_INPROJ = 512
ROWS_BRANCHES = 256


def _sigmoid(x):
    return 0.5 + 0.5 * jnp.tanh(0.5 * x)


def _silu(x):
    half = 0.5 * x
    return half + half * jnp.tanh(half)


def _softplus(x):
    return jnp.maximum(x, 0.0) + jnp.log(1.0 + jnp.exp(-jnp.abs(x)))


def _dot(a, b):
    return jnp.dot(a, b, preferred_element_type=F32)


def _dot_nt(a, b):
    return lax.dot_general(a, b, (((1,), (1,)), ((), ())), preferred_element_type=F32)


def _resident(shape):
    nd = len(shape)
    return pl.BlockSpec(shape, lambda *_: (0,) * nd, pipeline_mode=pl.Buffered(1))


def _param_table(rows):
    width = max(sum(v.shape[0] for _, v in r) for r in rows)
    layout, lines = {}, []
    for i, r in enumerate(rows):
        col, parts = 0, []
        for name, v in r:
            assert col % LANES == 0 and v.shape[0] % LANES == 0
            layout[name] = (i, col, v.shape[0])
            parts.append(v.astype(F32))
            col += v.shape[0]
        if col < width:
            parts.append(jnp.zeros((width - col,), F32))
        lines.append(jnp.concatenate(parts))
    return jnp.stack(lines), layout


def _param(p_ref, layout, name, lanes=None):
    r, c, n = layout[name]
    lanes = slice(0, n) if lanes is None else lanes
    return p_ref[r:r + 1, c + lanes.start:c + lanes.stop]


def _mod_kernel(c_ref, w_ref, b_ref, o_ref):
    s = _silu(c_ref[...])
    o_ref[...] = _dot(s.astype(BF16), w_ref[...].astype(BF16)) + b_ref[...]


def _mod_call(c8, w_ada, b_ada):
    d, n = w_ada.shape
    tn = 1024
    return pl.pallas_call(
        _mod_kernel,
        grid=(n // tn,),
        in_specs=[pl.BlockSpec((8, d), lambda j: (0, 0)),
                  pl.BlockSpec((d, tn), lambda j: (0, j)),
                  pl.BlockSpec((1, tn), lambda j: (0, j))],
        out_specs=pl.BlockSpec((8, tn), lambda j: (0, j)),
        out_shape=jax.ShapeDtypeStruct((8, n), F32),
        name="mod",
    )(c8, w_ada, b_ada)


def _bucket_table():
    qi = np.arange(BLOCK)[:, None]
    kj = np.arange(2 * BLOCK)[None, :]
    dist = qi + BLOCK - kj
    n = np.maximum(dist, 0)
    max_exact = REL_BUCKETS // 2
    nf = np.maximum(n, 1).astype(np.float32)
    large = max_exact + (np.log(nf / max_exact) / math.log(REL_MAX_DIST / max_exact)
                         * (REL_BUCKETS - max_exact)).astype(np.int32)
    large = np.minimum(large, REL_BUCKETS - 1)
    bucket = np.where(n < max_exact, n, large)
    valid = (dist >= 0) & (dist < BLOCK)
    return np.where(valid, bucket, -1).astype(np.int32)


def _bias_kernel(rb_ref, idx_ref, o_ref):
    kv = pl.program_id(0)
    idx = idx_ref[...]
    key_row = lax.broadcasted_iota(jnp.int32, idx.shape, 0)
    keep = idx >= 0
    keep_first = jnp.logical_and(keep, key_row >= BLOCK)
    for s in range(2):
        for c in range(2):
            acc = jnp.zeros(idx.shape, F32)
            for b in range(REL_BUCKETS):
                acc = jnp.where(idx == b, rb_ref[b, kv * GROUP + 2 * c + s], acc)
            acc = acc * LOG2E
            rows, cols = slice(s * 2 * BLOCK, (s + 1) * 2 * BLOCK), slice(c * BLOCK, (c + 1) * BLOCK)
            o_ref[0, 0, rows, cols] = jnp.where(keep, acc, NEG)
            o_ref[1, 0, rows, cols] = jnp.where(keep_first, acc, NEG)


def _bias_call(rel_bias):
    idx_t = jnp.asarray(np.ascontiguousarray(_bucket_table().T))
    table = pl.pallas_call(
        _bias_kernel,
        grid=(ATTN_KV_HEADS,),
        in_specs=[pl.BlockSpec(memory_space=pltpu.SMEM),
                  pl.BlockSpec((2 * BLOCK, BLOCK), lambda h: (0, 0))],
        out_specs=pl.BlockSpec((2, 1, 4 * BLOCK, 2 * BLOCK), lambda h: (0, h, 0, 0)),
        out_shape=jax.ShapeDtypeStruct((2, ATTN_KV_HEADS, 4 * BLOCK, 2 * BLOCK), F32),
        name="bias",
    )(rel_bias, idx_t)
    return table.reshape(2 * ATTN_KV_HEADS, 4 * BLOCK, 2 * BLOCK)


def _wprep_kernel(wt_ref, o_ref):
    o_ref[...] = wt_ref[...].T.astype(BF16)


def _wprep_call(w_t, dt_start, dt_width):
    n, k = w_t.shape
    tn = 512
    head_tiles, tail_tiles = dt_start // tn, (n - dt_start - dt_width) // tn
    assert head_tiles * tn == dt_start and tail_tiles * tn == n - dt_start - dt_width
    tail_start = dt_start + dt_width

    assert tail_start % 8 == 0

    def rows(i):
        start = jnp.where(i < head_tiles, i * tn, tail_start + (i - head_tiles) * tn)
        return (pl.multiple_of(start, 8), 0)

    return pl.pallas_call(
        _wprep_kernel,
        grid=(head_tiles + tail_tiles,),
        in_specs=[pl.BlockSpec((pl.Element(tn), pl.Element(k)), rows)],
        out_specs=pl.BlockSpec((k, tn), lambda i: (0, i)),
        out_shape=jax.ShapeDtypeStruct((k, n - dt_width), BF16),
        compiler_params=pltpu.CompilerParams(dimension_semantics=("parallel",)),
        name="wprep",
    )(w_t)


def _inproj_kernel(x_ref, shift_ref, scale_ref, p_ref, w_ref, wdt_ref,
                   qkv_o, za_o, zm_o, xbc_o, g_o, dt_o, xraw_s, carry_s, *, layout):
    ts = x_ref.shape[0]
    halo = carry_s.shape[1]

    @pl.when(pl.program_id(1) == 0)
    def _():
        carry_s[...] = jnp.zeros(carry_s.shape, F32)

    x = x_ref[...]
    ms = jnp.mean(x * x, axis=-1, keepdims=True)
    gain = _param(p_ref, layout, "norm_w") * (1.0 + scale_ref[0])
    h = (x * lax.rsqrt(ms + EPS) * gain + shift_ref[0]).astype(BF16)

    def conv_silu_store(y, c0, slot):
        for jj in range(y.shape[1] // LANES):
            j = c0 // LANES + jj
            jl = slice(j * LANES, (j + 1) * LANES)
            yj = y[:, jj * LANES:(jj + 1) * LANES]
            xraw_s[slot, jj, 0:halo, :] = carry_s[j]
            xraw_s[slot, jj, halo:halo + ts, :] = yj
            carry_s[j] = yj[ts - halo:ts, :]
            acc = _param(p_ref, layout, "conv_b", jl)
            for w in range(CONV_WIDTH):
                off = halo - (CONV_WIDTH - 1) + w
                acc = acc + _param(p_ref, layout, f"conv_w{w}", jl) * xraw_s[slot, jj, off:off + ts, :]
            xbc_o[:, jl] = _silu(acc).astype(BF16)

    step = xraw_s.shape[1] * LANES
    light, heavy, col = [], [], 0
    for o_ref, act in ((qkv_o, None), (za_o, _silu), (zm_o, _silu), (xbc_o, "conv"), (g_o, _sigmoid)):
        for c0 in range(0, o_ref.shape[1], step):
            (heavy if act == "conv" else light).append((o_ref, act, col + c0, c0))
        col += o_ref.shape[1]
    gates = [p for p in light if p[1] is _sigmoid]
    others = [p for p in light if p[1] is not _sigmoid]
    order = []
    for piece in heavy:
        order += [piece, others.pop(0), gates.pop(0) if gates else others.pop(0)]
    order += others + gates
    slot = 0
    for k, (o_ref, act, wc, c0) in enumerate(order):
        w_piece = w_ref[:, wc:wc + step]
        if k == 0:
            y = jnp.concatenate([_dot(h[:ts // 2], w_piece), _dot(h[ts // 2:], w_piece)], axis=0)
        else:
            y = _dot(h, w_piece)
        if act == "conv":
            conv_silu_store(y, c0, slot)
            slot = 1 - slot
        else:
            o_ref[:, c0:c0 + step] = (y if act is None else act(y)).astype(o_ref.dtype)
    dt_o[...] = _softplus(_dot_nt(h, wdt_ref[...]) + _param(p_ref, layout, "dt_bias"))


def _inproj_call(x2, shift, scale, params, layout, w_all, wdt, widths, batch, seq):
    t, d = x2.shape
    ts = min(ROWS_INPROJ, seq)
    ns = seq // ts
    row = lambda b, i: (b * ns + i, 0)
    per_b = pl.BlockSpec((1, 1, d), lambda b, i: (b, 0, 0))
    widths = list(widths) + [wdt.shape[0]]
    dtypes = [BF16] * (len(widths) - 1) + [F32]
    piece_blocks = 4
    return pl.pallas_call(
        functools.partial(_inproj_kernel, layout=layout),
        grid=(batch, ns),
        in_specs=[pl.BlockSpec((ts, d), row), per_b, per_b, _resident(params.shape),
                  _resident(w_all.shape), _resident(wdt.shape)],
        out_specs=[pl.BlockSpec((ts, n), row) for n in widths],
        out_shape=[jax.ShapeDtypeStruct((t, n), dt) for n, dt in zip(widths, dtypes)],
        scratch_shapes=[pltpu.VMEM((2, piece_blocks, ts + 8, LANES), F32),
                        pltpu.VMEM((widths[3] // LANES, 8, LANES), F32)],
        compiler_params=pltpu.CompilerParams(
            dimension_semantics=("arbitrary", "arbitrary"), vmem_limit_bytes=VMEM_LIMIT),
        name="inproj",
    )(x2, shift, scale, params, w_all, wdt)


def _segment_rms(x, bd, w):
    ms = _dot((x * x).astype(BF16), bd)
    return x * lax.rsqrt(ms + EPS) * w


def _branches_kernel(*refs, layout):
    i, last = pl.program_id(1), pl.num_programs(1) - 1
    step = functools.partial(_branches_step, *refs, layout=layout)
    pl.when(i == 0)(functools.partial(step, compute=True, finishing=False, first=True))
    pl.when(jnp.logical_and(i > 0, i < last))(functools.partial(step, compute=True, finishing=True, first=False))
    pl.when(i == last)(functools.partial(step, compute=False, finishing=True, first=False))


def _branches_step(q_ref, kvc_ref, kvp_ref, za_ref, ga_ref, bias_ref, p_ref, bd_ref, wap_ref,
                   xbc_ref, dt_ref, zm_ref, tri_ref,
                   gb_ref, x_ref, gate_ref, wsp_ref, wout_ref,
                   o_ref,
                   qn_s, klo_s, khi_s, vt_s, s_s, p_s, yt_s, y_s, ya_s, h_s, yn_s, yprev_s, merged_s,
                   *, layout, compute, finishing, first):
    rows = q_ref.shape[0]
    nblk = rows // BLOCK
    nchunk = rows // CHUNK
    kvw = ATTN_KV_HEADS * HEAD_DIM
    sw = zm_ref.shape[1]
    gw = sw // SSM_GROUPS
    pairs_per_group = gw // LANES
    n_xs = sw // LANES
    if first:
        h_s[...] = jnp.zeros(h_s.shape, F32)

    d_out = o_ref.shape[1]
    piece_w = 2 * LANES

    def merge_piece(c0):
        cs = slice(c0, c0 + piece_w)
        yb = _dot(yprev_s[...], wsp_ref[:, cs])
        merged_s[:, cs] = ya_s[:, cs] + gb_ref[:, cs] * yb.astype(BF16)

    def out_piece(c0):
        cs = slice(c0, c0 + piece_w)
        o = _dot(merged_s[...], wout_ref[:, cs])
        o_ref[:, cs] = x_ref[:, cs] + gate_ref[0][:, cs] * o

    finish = []
    if finishing:
        finish = ([functools.partial(merge_piece, c0) for c0 in range(0, d_out, piece_w)]
                  + [functools.partial(out_piece, c0) for c0 in range(0, d_out, piece_w)])
    if not compute:
        for piece in finish:
            piece()
        return

    bd = bd_ref[...]
    qw = _param(p_ref, layout, "q_norm_w") * (HEAD_DIM ** -0.5 * LOG2E)
    kw = _param(p_ref, layout, "k_norm_w")
    low_lanes = lax.broadcasted_iota(jnp.int32, (1, LANES), 1) < HEAD_DIM

    for j in range(q_ref.shape[1] // LANES):
        sl = slice(j * LANES, (j + 1) * LANES)
        qn_s[:, sl] = _segment_rms(q_ref[:, sl].astype(F32), bd, qw).astype(BF16)
    for j in range(kvw // LANES):
        sl = slice(j * LANES, (j + 1) * LANES)
        even = slice(2 * j * LANES, (2 * j + 1) * LANES)
        odd = slice((2 * j + 1) * LANES, (2 * j + 2) * LANES)
        for dst, src in ((slice(0, BLOCK), kvp_ref), (slice(BLOCK, BLOCK + rows), kvc_ref)):
            kn = _segment_rms(src[:, sl].astype(F32), bd, kw)
            lo = jnp.where(low_lanes, kn, 0.0)
            hi = jnp.where(low_lanes, 0.0, kn)
            klo_s[dst, even] = lo.astype(BF16)
            khi_s[dst, odd] = hi.astype(BF16)
            khi_s[dst, even] = pltpu.roll(lo, HEAD_DIM, axis=1).astype(BF16)
            klo_s[dst, odd] = pltpu.roll(hi, HEAD_DIM, axis=1).astype(BF16)
    vt_s[:, 0:BLOCK] = kvp_ref[:, kvw:2 * kvw].astype(F32).T.astype(BF16)
    vt_s[:, BLOCK:] = kvc_ref[:, kvw:2 * kvw].astype(F32).T.astype(BF16)

    units = [(qb, h) for qb in range(nblk) for h in range(ATTN_KV_HEADS)]

    def scores(u):
        qb, h = units[u]
        r0 = qb * BLOCK
        hl = slice(h * LANES, (h + 1) * LANES)
        k2 = jnp.concatenate([klo_s[r0:r0 + 2 * BLOCK, hl], khi_s[r0:r0 + 2 * BLOCK, hl]], axis=0)
        q2 = jnp.concatenate([qn_s[r0:r0 + BLOCK, (2 * h + c) * LANES:(2 * h + c + 1) * LANES]
                              for c in range(2)], axis=0)
        entry = ATTN_KV_HEADS + h if first and qb == 0 else h
        s_s[u % 2] = _dot_nt(k2, q2) + bias_ref[entry]

    def softmax_pv(u):
        qb, h = units[u]
        r0 = qb * BLOCK
        slot = u % 2
        inv = []
        for s in range(2):
            for c in range(2):
                blk = s_s[slot, s * 2 * BLOCK:(s + 1) * 2 * BLOCK, c * BLOCK:(c + 1) * BLOCK]
                sc = ((h * 2 + s) * 2 + c) * BLOCK
                sink = _param(p_ref, layout, "sinks", slice(sc, sc + BLOCK)) * LOG2E
                m = jnp.maximum(jnp.max(blk, axis=0, keepdims=True), sink)
                e = jnp.exp2(blk - m)
                den = jnp.sum(e, axis=0, keepdims=True) + jnp.exp2(sink - m)
                p_s[slot, :, (2 * s + c) * BLOCK:(2 * s + c + 1) * BLOCK] = e.astype(BF16)
                inv.append(1.0 / den)
        o_t = _dot(vt_s[h * HEAD_DIM:(h + 1) * HEAD_DIM, r0:r0 + 2 * BLOCK], p_s[slot])
        for s in range(2):
            for c in range(2):
                head = GROUP * h + 2 * c + s
                k = 2 * s + c
                yt_s[head * HEAD_DIM:(head + 1) * HEAD_DIM, r0:r0 + BLOCK] = (
                    o_t[:, k * BLOCK:(k + 1) * BLOCK] * inv[k])

    scores(0)
    attention = []
    for u in range(len(units)):
        def unit(u=u):
            if u + 1 < len(units):
                scores(u + 1)
            softmax_pv(u)
        attention.append(unit)

    def gate_y():
        y_s[...] = (yt_s[...].T * za_ref[...].astype(F32)).astype(BF16)

    def proj_piece(c0):
        cs = slice(c0, c0 + piece_w)
        ya = _dot(y_s[...], wap_ref[:, cs])
        ya_s[:, cs] = (ya * ga_ref[:, cs].astype(F32)).astype(BF16)

    first_proj = len(attention) + 1
    attention.append(gate_y)
    attention += [functools.partial(proj_piece, c0) for c0 in range(0, d_out, piece_w)]
    n_groups = nchunk * SSM_GROUPS
    base, extra = divmod(len(attention), n_groups)
    attn_counts = [base + (k < extra) for k in range(n_groups)]
    first_proj_group = next(k for k in range(n_groups) if sum(attn_counts[:k + 1]) > first_proj)
    assert len(finish) <= n_groups and first_proj_group >= len(finish) // 2 - 1

    def lane_block(j, r0):
        return xbc_ref[r0:r0 + CHUNK, j * LANES:(j + 1) * LANES]

    a_row = -jnp.exp(_param(p_ref, layout, "a_log"))
    tri = tri_ref[...]
    li = lax.broadcasted_iota(jnp.int32, (CHUNK, CHUNK), 0)
    si = lax.broadcasted_iota(jnp.int32, (CHUNK, CHUNK), 1)
    causal = li >= si
    low_half = si < SSM_HEAD_DIM
    lo_mask = (lax.broadcasted_iota(jnp.int32, (1, LANES), 1) < SSM_HEAD_DIM).astype(BF16)
    hi_mask = 1 - lo_mask

    for c in range(nchunk):
        r0 = c * CHUNK
        dt = dt_ref[r0:r0 + CHUNK, :]
        dta = dt * a_row
        dta_hi = dta.astype(BF16)
        dta_lo = (dta - dta_hi.astype(F32)).astype(BF16)
        a2 = (_dot(tri, dta_hi) + _dot(tri, dta_lo)) * LOG2E
        a2_t = a2.T
        dt_t = dt.T
        row2_t = a2_t - jnp.log2(dt_t)
        a_end2_t = a2_t[:, CHUNK - 1:CHUNK]
        w_t = dt_t * jnp.exp2(a_end2_t - a2_t)
        e_end_t = jnp.exp2(a_end2_t)

        for g in range(SSM_GROUPS):
            if finish:
                finish.pop(0)()
            for _ in range(attn_counts[c * SSM_GROUPS + g]):
                attention.pop(0)()
            bm_gb = lane_block(n_xs + g, r0)
            cm_gb = lane_block(n_xs + SSM_GROUPS + g, r0)
            cb = _dot_nt(cm_gb, bm_gb)
            bm_gt = bm_gb.astype(F32).T
            gsl = slice(g * gw, (g + 1) * gw)
            hprev = h_s[g]
            y_off = _dot(cm_gb, hprev.astype(BF16))
            y_parts = []
            for p in range(pairs_per_group):
                pair = g * pairs_per_group + p
                lsl = slice(pair * LANES, (pair + 1) * LANES)
                xs_b = lane_block(pair, r0)
                xs_pair = xs_b.astype(F32)
                rhs = jnp.concatenate([xs_b * lo_mask, xs_b * hi_mask], axis=0)
                m_parts, b_parts, cols, e_ends = [], [], [], []
                for side in range(2):
                    hh = 2 * pair + side
                    col2 = jnp.broadcast_to(a2[:, hh:hh + 1], (CHUNK, CHUNK))
                    dec_dt = jnp.exp2(jnp.where(causal, col2 - row2_t[hh:hh + 1, :], NEG))
                    m_parts.append(cb * dec_dt)
                    b_parts.append(bm_gt * w_t[hh:hh + 1, :])
                    cols.append(col2)
                    e_ends.append(e_end_t[hh:hh + 1, :])
                lhs = jnp.concatenate(
                    [jnp.concatenate(m_parts, axis=1), jnp.concatenate(b_parts, axis=1)],
                    axis=0).astype(BF16)
                res = _dot(lhs, rhs)
                e_col = jnp.exp2(jnp.where(low_half, cols[0], cols[1]))
                y_pair = res[0:CHUNK] + e_col * y_off[:, p * LANES:(p + 1) * LANES]
                y_pair = y_pair + _param(p_ref, layout, "d_skip", lsl) * xs_pair
                y_parts.append(y_pair * zm_ref[r0:r0 + CHUNK, lsl].astype(F32))
                e_end = jnp.where(low_half, e_ends[0], e_ends[1])
                h_s[g, :, p * LANES:(p + 1) * LANES] = (
                    hprev[:, p * LANES:(p + 1) * LANES] * e_end + res[CHUNK:])
            yg = jnp.concatenate(y_parts, axis=1)
            ms = jnp.mean(yg * yg, axis=-1, keepdims=True)
            nw = _param(p_ref, layout, "ssm_norm_w", gsl)
            yn_s[r0:r0 + CHUNK, gsl] = (yg * lax.rsqrt(ms + EPS) * nw).astype(BF16)

    assert not finish and not attention
    yprev_s[...] = yn_s[...]


def _branches_call(qkv, za, g, bias, params, layout, bd, wap,
                   xbc, dt, zm, tri, x2, gate, wsp, wout, batch, seq):
    t, d = x2.shape
    aw = ATTN_HEADS * HEAD_DIM
    kvw2 = 2 * ATTN_KV_HEADS * HEAD_DIM
    sw = zm.shape[1]
    rows = min(ROWS_BRANCHES, seq)
    nt = seq // rows
    bpt = rows // BLOCK
    tile = lambda b, i: b * nt + jnp.minimum(i, nt - 1)
    scan = lambda b, i: (tile(b, i), 0)
    done = lambda b, i: (b * nt + jnp.maximum(i - 1, 0), 0)
    return pl.pallas_call(
        functools.partial(_branches_kernel, layout=layout),
        grid=(batch, nt + 1),
        in_specs=[pl.BlockSpec((rows, aw), scan),
                  pl.BlockSpec((rows, kvw2), lambda b, i: (tile(b, i), aw // kvw2)),
                  pl.BlockSpec((BLOCK, kvw2),
                               lambda b, i: (jnp.maximum(tile(b, i) * bpt - 1, 0), aw // kvw2)),
                  pl.BlockSpec((rows, aw), scan),
                  pl.BlockSpec((rows, aw), scan),
                  _resident(bias.shape), _resident(params.shape),
                  _resident(bd.shape), _resident(wap.shape),
                  pl.BlockSpec((rows, xbc.shape[1]), scan),
                  pl.BlockSpec((rows, dt.shape[1]), scan),
                  pl.BlockSpec((rows, sw), scan),
                  _resident(tri.shape),
                  pl.BlockSpec((rows, d), lambda b, i: (b * nt + jnp.maximum(i - 1, 0), 1)),
                  pl.BlockSpec((rows, d), done),
                  pl.BlockSpec((1, 1, d), lambda b, i: (b, 0, 0)),
                  _resident(wsp.shape), _resident(wout.shape)],
        out_specs=pl.BlockSpec((rows, d), done),
        out_shape=jax.ShapeDtypeStruct((t, d), F32),
        scratch_shapes=[pltpu.VMEM((rows, aw), BF16),
                        pltpu.VMEM((rows + BLOCK, ATTN_KV_HEADS * LANES), BF16),
                        pltpu.VMEM((rows + BLOCK, ATTN_KV_HEADS * LANES), BF16),
                        pltpu.VMEM((kvw2 // 2, rows + BLOCK), BF16),
                        pltpu.VMEM((2, 4 * BLOCK, 2 * BLOCK), F32),
                        pltpu.VMEM((2, 2 * BLOCK, 4 * BLOCK), BF16),
                        pltpu.VMEM((aw, rows), F32),
                        pltpu.VMEM((rows, aw), BF16),
                        pltpu.VMEM((rows, d), BF16),
                        pltpu.VMEM((SSM_GROUPS, SSM_STATE, sw // SSM_GROUPS), F32),
                        pltpu.VMEM((rows, sw), BF16),
                        pltpu.VMEM((rows, sw), BF16),
                        pltpu.VMEM((rows, d), BF16)],
        compiler_params=pltpu.CompilerParams(
            dimension_semantics=("arbitrary", "arbitrary"), vmem_limit_bytes=VMEM_LIMIT),
        name="branches",
    )(qkv, qkv, qkv, za, g, bias, params, bd, wap,
      xbc, dt, zm, tri, g, x2, gate, wsp, wout)


def _layer(x, c, w_ada, b_ada, norm_w, w_in, q_norm_w, k_norm_w, rel_bias, sinks,
           conv_w, conv_b, dt_bias, a_log, d_skip, ssm_norm_w, w_attn_proj, w_ssm_proj, w_out):
    batch, seq, d = x.shape
    aw = ATTN_HEADS * HEAD_DIM
    kvw = ATTN_KV_HEADS * HEAD_DIM
    sw = w_ssm_proj.shape[0]
    ssm_heads = sw // SSM_HEAD_DIM
    xbc_w = sw + 2 * SSM_GROUPS * SSM_STATE
    assert seq % BLOCK == 0 and seq % CHUNK == 0 and ssm_heads <= LANES

    c8 = jnp.zeros((8, d), F32).at[:batch].set(c)
    mod = _mod_call(c8, w_ada, b_ada.reshape(1, -1))[:batch]
    shift, scale, gate = (mod[:, k * d:(k + 1) * d].reshape(batch, 1, d) for k in range(3))

    o = np.cumsum([0, aw, kvw, kvw, aw, sw, xbc_w, ssm_heads, d, d])
    w_t = w_in.T
    w_all = _wprep_call(w_t, int(o[6]), ssm_heads)
    wdt = jnp.pad(w_t[o[6]:o[7]], ((0, LANES - ssm_heads), (0, 0))).astype(BF16)
    widths = [aw + 2 * kvw, aw, sw, xbc_w, 2 * d]

    pad_heads = lambda v: jnp.pad(v.astype(F32), (0, LANES - ssm_heads))
    pair = lambda v: jnp.tile(v.astype(F32), LANES // HEAD_DIM)
    sink_lanes = jnp.repeat(sinks.astype(F32).reshape(ATTN_KV_HEADS, 2, 2).transpose(0, 2, 1).reshape(-1), BLOCK)
    params, layout = _param_table(
        [[(f"conv_w{w}", conv_w[w])] for w in range(CONV_WIDTH)]
        + [[("conv_b", conv_b)],
           [("d_skip", jnp.repeat(d_skip.astype(F32), SSM_HEAD_DIM)), ("norm_w", norm_w)],
           [("ssm_norm_w", ssm_norm_w), ("q_norm_w", pair(q_norm_w)), ("k_norm_w", pair(k_norm_w)),
            ("dt_bias", pad_heads(dt_bias)), ("a_log", pad_heads(a_log))],
           [("sinks", sink_lanes)]])

    x2 = x.reshape(batch * seq, d)
    qkv, za, zm, xbc, g, dt = _inproj_call(x2, shift, scale, params, layout, w_all, wdt, widths, batch, seq)

    bias = _bias_call(rel_bias.astype(F32))
    seg = np.arange(LANES) // HEAD_DIM
    bd = jnp.asarray((seg[:, None] == seg[None, :]).astype(np.float32) / HEAD_DIM, dtype=BF16)
    tri = jnp.asarray(np.tril(np.ones((CHUNK, CHUNK), np.float32)), dtype=BF16)
    out = _branches_call(qkv, za, g, bias, params, layout, bd, w_attn_proj.astype(BF16),
                         xbc, dt, zm, tri,
                         x2, gate, w_ssm_proj.astype(BF16), w_out.astype(BF16), batch, seq)
    return out.reshape(batch, seq, d)


def kernel(x, c, w_ada, b_ada, norm_w, w_in, q_norm_w, k_norm_w, rel_bias, sinks, conv_w, conv_b,
           dt_bias, a_log, d_skip, ssm_norm_w, w_attn_proj, w_ssm_proj, w_out):
    depth = w_in.shape[0]
    for i in range(depth):
        x = _layer(x, c, w_ada[i], b_ada[i], norm_w[i], w_in[i], q_norm_w[i], k_norm_w[i],
                   rel_bias, sinks[i], conv_w[i], conv_b[i], dt_bias[i], a_log[i], d_skip[i],
                   ssm_norm_w[i], w_attn_proj[i], w_ssm_proj[i], w_out[i])
    return x
```

```python
import functools
import math

import numpy as np
import jax
import jax.numpy as jnp
from jax import lax
from jax.experimental import pallas as pl
from jax.experimental.pallas import tpu as pltpu

F32 = jnp.float32
BF16 = jnp.bfloat16

HEAD_DIM = 64
ATTN_HEADS = 16
ATTN_KV_HEADS = 4
GROUP = ATTN_HEADS // ATTN_KV_HEADS
BLOCK = 128
REL_BUCKETS = 32
REL_MAX_DIST = 128
SSM_HEAD_DIM = 64
SSM_GROUPS = 4
SSM_STATE = 128
CONV_WIDTH = 4
CHUNK = 128
EPS = 1e-6
NEG = -1e30
LOG2E = 1.4426950408889634
LANES = 128
VMEM_LIMIT = 56 * 1024 * 1024

ROWS_INPROJ = 512
ROWS_BRANCHES = 256


def _sigmoid(x):
    return 0.5 + 0.5 * jnp.tanh(0.5 * x)


def _silu(x):
    half = 0.5 * x
    return half + half * jnp.tanh(half)


def _softplus(x):
    return jnp.maximum(x, 0.0) + jnp.log(1.0 + jnp.exp(-jnp.abs(x)))


def _dot(a, b):
    return jnp.dot(a, b, preferred_element_type=F32)


def _dot_nt(a, b):
    return lax.dot_general(a, b, (((1,), (1,)), ((), ())), preferred_element_type=F32)


def _resident(shape):
    nd = len(shape)
    return pl.BlockSpec(shape, lambda *_: (0,) * nd, pipeline_mode=pl.Buffered(1))


def _param_table(rows):
    width = max(sum(v.shape[0] for _, v in r) for r in rows)
    layout, lines = {}, []
    for i, r in enumerate(rows):
        col, parts = 0, []
        for name, v in r:
            assert col % LANES == 0 and v.shape[0] % LANES == 0
            layout[name] = (i, col, v.shape[0])
            parts.append(v.astype(F32))
            col += v.shape[0]
        if col < width:
            parts.append(jnp.zeros((width - col,), F32))
        lines.append(jnp.concatenate(parts))
    return jnp.stack(lines), layout


def _param(p_ref, layout, name, lanes=None):
    r, c, n = layout[name]
    lanes = slice(0, n) if lanes is None else lanes
    return p_ref[r:r + 1, c + lanes.start:c + lanes.stop]


def _mod_kernel(c_ref, w_ref, b_ref, o_ref):
    s = _silu(c_ref[...])
    o_ref[...] = _dot(s.astype(BF16), w_ref[...].astype(BF16)) + b_ref[...]


def _bucket_table():
    qi = np.arange(BLOCK)[:, None]
    kj = np.arange(2 * BLOCK)[None, :]
    dist = qi + BLOCK - kj
    n = np.maximum(dist, 0)
    max_exact = REL_BUCKETS // 2
    nf = np.maximum(n, 1).astype(np.float32)
    large = max_exact + (np.log(nf / max_exact) / math.log(REL_MAX_DIST / max_exact)
                         * (REL_BUCKETS - max_exact)).astype(np.int32)
    large = np.minimum(large, REL_BUCKETS - 1)
    bucket = np.where(n < max_exact, n, large)
    valid = (dist >= 0) & (dist < BLOCK)
    return np.where(valid, bucket, -1).astype(np.int32)


def _bias_kernel(rb_ref, idx_ref, o_ref):
    kv = pl.program_id(0)
    idx = idx_ref[...]
    key_row = lax.broadcasted_iota(jnp.int32, idx.shape, 0)
    keep = idx >= 0
    keep_first = jnp.logical_and(keep, key_row >= BLOCK)
    for s in range(2):
        for c in range(2):
            acc = jnp.zeros(idx.shape, F32)
            for b in range(REL_BUCKETS):
                acc = jnp.where(idx == b, rb_ref[b, kv * GROUP + 2 * c + s], acc)
            acc = acc * LOG2E
            rows, cols = slice(s * 2 * BLOCK, (s + 1) * 2 * BLOCK), slice(c * BLOCK, (c + 1) * BLOCK)
            o_ref[0, 0, rows, cols] = jnp.where(keep, acc, NEG)
            o_ref[1, 0, rows, cols] = jnp.where(keep_first, acc, NEG)


def _prep_kernel(rb_ref, idx_ref, c_ref, wada_ref, bada_ref, wt_ref, w_o, mod_o, bias_o):
    w_o[...] = wt_ref[...].T.astype(BF16)

    @pl.when(pl.program_id(0) < ATTN_KV_HEADS)
    def _():
        _mod_kernel(c_ref, wada_ref, bada_ref, mod_o)
        _bias_kernel(rb_ref, idx_ref, bias_o)


def _prep_call(c8, w_ada, b_ada, rel_bias, w_t, dt_start, dt_width):
    n, k = w_t.shape
    tn = 512
    head_tiles, tail_tiles = dt_start // tn, (n - dt_start - dt_width) // tn
    assert head_tiles * tn == dt_start and tail_tiles * tn == n - dt_start - dt_width
    tail_start = dt_start + dt_width
    assert tail_start % 8 == 0

    def rows(i):
        start = jnp.where(i < head_tiles, i * tn, tail_start + (i - head_tiles) * tn)
        return (pl.multiple_of(start, 8), 0)

    d, n_mod = w_ada.shape
    n_side = ATTN_KV_HEADS
    tm = n_mod // n_side
    assert tm * n_side == n_mod and tm % LANES == 0 and head_tiles + tail_tiles >= n_side
    side = lambda i: jnp.minimum(i, n_side - 1)
    idx_t = jnp.asarray(np.ascontiguousarray(_bucket_table().T))
    w_all, mod, table = pl.pallas_call(
        _prep_kernel,
        grid=(head_tiles + tail_tiles,),
        in_specs=[pl.BlockSpec(memory_space=pltpu.SMEM),
                  pl.BlockSpec((2 * BLOCK, BLOCK), lambda i: (0, 0)),
                  pl.BlockSpec((8, d), lambda i: (0, 0)),
                  pl.BlockSpec((d, tm), lambda i: (0, side(i))),
                  pl.BlockSpec((1, tm), lambda i: (0, side(i))),
                  pl.BlockSpec((pl.Element(tn), pl.Element(k)), rows)],
        out_specs=[pl.BlockSpec((k, tn), lambda i: (0, i)),
                   pl.BlockSpec((8, tm), lambda i: (0, side(i))),
                   pl.BlockSpec((2, 1, 4 * BLOCK, 2 * BLOCK), lambda i: (0, side(i), 0, 0))],
        out_shape=[jax.ShapeDtypeStruct((k, n - dt_width), BF16),
                   jax.ShapeDtypeStruct((8, n_mod), F32),
                   jax.ShapeDtypeStruct((2, ATTN_KV_HEADS, 4 * BLOCK, 2 * BLOCK), F32)],
        compiler_params=pltpu.CompilerParams(dimension_semantics=("arbitrary",)),
        name="prep",
    )(rel_bias, idx_t, c8, w_ada, b_ada, w_t)
    return w_all, mod, table.reshape(2 * ATTN_KV_HEADS, 4 * BLOCK, 2 * BLOCK)


def _inproj_kernel(x_ref, shift_ref, scale_ref, p_ref, w_ref, wdt_ref,
                   qkv_o, za_o, zm_o, xbc_o, g_o, dt_o, xraw_s, carry_s, *, layout):
    ts = x_ref.shape[0]
    halo = carry_s.shape[1]

    @pl.when(pl.program_id(1) == 0)
    def _():
        carry_s[...] = jnp.zeros(carry_s.shape, F32)

    x = x_ref[...]
    ms = jnp.mean(x * x, axis=-1, keepdims=True)
    gain = _param(p_ref, layout, "norm_w") * (1.0 + scale_ref[0])
    h = (x * lax.rsqrt(ms + EPS) * gain + shift_ref[0]).astype(BF16)

    def conv_silu_store(y, c0, slot):
        for jj in range(y.shape[1] // LANES):
            j = c0 // LANES + jj
            jl = slice(j * LANES, (j + 1) * LANES)
            yj = y[:, jj * LANES:(jj + 1) * LANES]
            xraw_s[slot, jj, 0:halo, :] = carry_s[j]
            xraw_s[slot, jj, halo:halo + ts, :] = yj
            carry_s[j] = yj[ts - halo:ts, :]
            acc = _param(p_ref, layout, "conv_b", jl)
            for w in range(CONV_WIDTH):
                off = halo - (CONV_WIDTH - 1) + w
                acc = acc + _param(p_ref, layout, f"conv_w{w}", jl) * xraw_s[slot, jj, off:off + ts, :]
            xbc_o[:, jl] = _silu(acc).astype(BF16)

    step = xraw_s.shape[1] * LANES
    light, heavy, col = [], [], 0
    for o_ref, act in ((qkv_o, None), (za_o, _silu), (zm_o, _silu), (xbc_o, "conv"), (g_o, _sigmoid)):
        for c0 in range(0, o_ref.shape[1], step):
            (heavy if act == "conv" else light).append((o_ref, act, col + c0, c0))
        col += o_ref.shape[1]
    gates = [p for p in light if p[1] is _sigmoid]
    others = [p for p in light if p[1] is not _sigmoid]
    order = []
    for piece in heavy:
        order += [piece, others.pop(0), gates.pop(0) if gates else others.pop(0)]
    order += others + gates
    slot = 0
    for k, (o_ref, act, wc, c0) in enumerate(order):
        w_piece = w_ref[:, wc:wc + step]
        if k == 0:
            y = jnp.concatenate([_dot(h[:ts // 2], w_piece), _dot(h[ts // 2:], w_piece)], axis=0)
        else:
            y = _dot(h, w_piece)
        if act == "conv":
            conv_silu_store(y, c0, slot)
            slot = 1 - slot
        else:
            o_ref[:, c0:c0 + step] = (y if act is None else act(y)).astype(o_ref.dtype)
    dt_o[...] = _softplus(_dot_nt(h, wdt_ref[...]) + _param(p_ref, layout, "dt_bias"))


def _inproj_call(x2, shift, scale, params, layout, w_all, wdt, widths, batch, seq):
    t, d = x2.shape
    ts = min(ROWS_INPROJ, seq)
    ns = seq // ts
    row = lambda b, i: (b * ns + i, 0)
    per_b = pl.BlockSpec((1, 1, d), lambda b, i: (b, 0, 0))
    widths = list(widths) + [wdt.shape[0]]
    dtypes = [BF16] * (len(widths) - 1) + [F32]
    piece_blocks = 4
    return pl.pallas_call(
        functools.partial(_inproj_kernel, layout=layout),
        grid=(batch, ns),
        in_specs=[pl.BlockSpec((ts, d), row), per_b, per_b, _resident(params.shape),
                  _resident(w_all.shape), _resident(wdt.shape)],
        out_specs=[pl.BlockSpec((ts, n), row) for n in widths],
        out_shape=[jax.ShapeDtypeStruct((t, n), dt) for n, dt in zip(widths, dtypes)],
        scratch_shapes=[pltpu.VMEM((2, piece_blocks, ts + 8, LANES), F32),
                        pltpu.VMEM((widths[3] // LANES, 8, LANES), F32)],
        compiler_params=pltpu.CompilerParams(
            dimension_semantics=("arbitrary", "arbitrary"), vmem_limit_bytes=VMEM_LIMIT),
        name="inproj",
    )(x2, shift, scale, params, w_all, wdt)


def _segment_rms(x, bd, w):
    ms = _dot((x * x).astype(BF16), bd)
    return x * lax.rsqrt(ms + EPS) * w


def _branches_kernel(*refs, layout):
    i, last = pl.program_id(1), pl.num_programs(1) - 1
    step = functools.partial(_branches_step, *refs, layout=layout)
    pl.when(i == 0)(functools.partial(step, compute=True, finishing=False, first=True))
    pl.when(jnp.logical_and(i > 0, i < last))(functools.partial(step, compute=True, finishing=True, first=False))
    pl.when(i == last)(functools.partial(step, compute=False, finishing=True, first=False))


def _branches_step(q_ref, kvc_ref, kvp_ref, za_ref, ga_ref, bias_ref, p_ref, bd_ref, wap_ref,
                   xbc_ref, dt_ref, zm_ref, tri_ref,
                   gb_ref, x_ref, gate_ref, wsp_ref, wout_ref,
                   o_ref,
                   qn_s, klo_s, khi_s, vt_s, s_s, p_s, yt_s, y_s, ya_s, h_s, yn_s, yprev_s, merged_s,
                   *, layout, compute, finishing, first):
    rows = q_ref.shape[0]
    nblk = rows // BLOCK
    nchunk = rows // CHUNK
    kvw = ATTN_KV_HEADS * HEAD_DIM
    sw = zm_ref.shape[1]
    gw = sw // SSM_GROUPS
    pairs_per_group = gw // LANES
    n_xs = sw // LANES
    if first:
        h_s[...] = jnp.zeros(h_s.shape, F32)

    d_out = o_ref.shape[1]
    piece_w = 2 * LANES

    def merge_piece(c0):
        cs = slice(c0, c0 + piece_w)
        yb = _dot(yprev_s[...], wsp_ref[:, cs])
        merged_s[:, cs] = ya_s[:, cs] + gb_ref[:, cs] * yb.astype(BF16)

    def out_piece(c0):
        cs = slice(c0, c0 + piece_w)
        o = _dot(merged_s[...], wout_ref[:, cs])
        o_ref[:, cs] = x_ref[:, cs] + gate_ref[0][:, cs] * o

    finish = []
    if finishing:
        finish = ([functools.partial(merge_piece, c0) for c0 in range(0, d_out, piece_w)]
                  + [functools.partial(out_piece, c0) for c0 in range(0, d_out, piece_w)])
    if not compute:
        for piece in finish:
            piece()
        return

    bd = bd_ref[...]
    qw = _param(p_ref, layout, "q_norm_w") * (HEAD_DIM ** -0.5 * LOG2E)
    kw = _param(p_ref, layout, "k_norm_w")
    low_lanes = lax.broadcasted_iota(jnp.int32, (1, LANES), 1) < HEAD_DIM

    for j in range(q_ref.shape[1] // LANES):
        sl = slice(j * LANES, (j + 1) * LANES)
        qn_s[:, sl] = _segment_rms(q_ref[:, sl].astype(F32), bd, qw).astype(BF16)
    for j in range(kvw // LANES):
        sl = slice(j * LANES, (j + 1) * LANES)
        even = slice(2 * j * LANES, (2 * j + 1) * LANES)
        odd = slice((2 * j + 1) * LANES, (2 * j + 2) * LANES)
        for dst, src in ((slice(0, BLOCK), kvp_ref), (slice(BLOCK, BLOCK + rows), kvc_ref)):
            kn = _segment_rms(src[:, sl].astype(F32), bd, kw)
            lo = jnp.where(low_lanes, kn, 0.0)
            hi = jnp.where(low_lanes, 0.0, kn)
            klo_s[dst, even] = lo.astype(BF16)
            khi_s[dst, odd] = hi.astype(BF16)
            khi_s[dst, even] = pltpu.roll(lo, HEAD_DIM, axis=1).astype(BF16)
            klo_s[dst, odd] = pltpu.roll(hi, HEAD_DIM, axis=1).astype(BF16)
    vt_s[:, 0:BLOCK] = kvp_ref[:, kvw:2 * kvw].astype(F32).T.astype(BF16)
    vt_s[:, BLOCK:] = kvc_ref[:, kvw:2 * kvw].astype(F32).T.astype(BF16)

    units = [(qb, h) for qb in range(nblk) for h in range(ATTN_KV_HEADS)]

    def scores(u):
        qb, h = units[u]
        r0 = qb * BLOCK
        hl = slice(h * LANES, (h + 1) * LANES)
        k2 = jnp.concatenate([klo_s[r0:r0 + 2 * BLOCK, hl], khi_s[r0:r0 + 2 * BLOCK, hl]], axis=0)
        q2 = jnp.concatenate([qn_s[r0:r0 + BLOCK, (2 * h + c) * LANES:(2 * h + c + 1) * LANES]
                              for c in range(2)], axis=0)
        entry = ATTN_KV_HEADS + h if first and qb == 0 else h
        s_s[u % 2] = _dot_nt(k2, q2) + bias_ref[entry]

    def softmax_pv(u):
        qb, h = units[u]
        r0 = qb * BLOCK
        slot = u % 2
        inv = []
        for s in range(2):
            for c in range(2):
                blk = s_s[slot, s * 2 * BLOCK:(s + 1) * 2 * BLOCK, c * BLOCK:(c + 1) * BLOCK]
                sc = ((h * 2 + s) * 2 + c) * BLOCK
                sink = _param(p_ref, layout, "sinks", slice(sc, sc + BLOCK)) * LOG2E
                m = jnp.maximum(jnp.max(blk, axis=0, keepdims=True), sink)
                e = jnp.exp2(blk - m)
                den = jnp.sum(e, axis=0, keepdims=True) + jnp.exp2(sink - m)
                p_s[slot, :, (2 * s + c) * BLOCK:(2 * s + c + 1) * BLOCK] = e.astype(BF16)
                inv.append(1.0 / den)
        o_t = _dot(vt_s[h * HEAD_DIM:(h + 1) * HEAD_DIM, r0:r0 + 2 * BLOCK], p_s[slot])
        for s in range(2):
            for c in range(2):
                head = GROUP * h + 2 * c + s
                k = 2 * s + c
                yt_s[head * HEAD_DIM:(head + 1) * HEAD_DIM, r0:r0 + BLOCK] = (
                    o_t[:, k * BLOCK:(k + 1) * BLOCK] * inv[k])

    scores(0)
    attention = []
    for u in range(len(units)):
        def unit(u=u):
            if u + 1 < len(units):
                scores(u + 1)
            softmax_pv(u)
        attention.append(unit)

    def gate_y():
        y_s[...] = (yt_s[...].T * za_ref[...].astype(F32)).astype(BF16)

    def proj_piece(c0):
        cs = slice(c0, c0 + piece_w)
        ya = _dot(y_s[...], wap_ref[:, cs])
        ya_s[:, cs] = (ya * ga_ref[:, cs].astype(F32)).astype(BF16)

    first_proj = len(attention) + 1
    attention.append(gate_y)
    attention += [functools.partial(proj_piece, c0) for c0 in range(0, d_out, piece_w)]
    n_groups = nchunk * SSM_GROUPS
    base, extra = divmod(len(attention), n_groups)
    attn_counts = [base + (k < extra) for k in range(n_groups)]
    first_proj_group = next(k for k in range(n_groups) if sum(attn_counts[:k + 1]) > first_proj)
    assert len(finish) <= n_groups and first_proj_group >= len(finish) // 2 - 1

    def lane_block(j, r0):
        return xbc_ref[r0:r0 + CHUNK, j * LANES:(j + 1) * LANES]

    a_row = -jnp.exp(_param(p_ref, layout, "a_log"))
    tri = tri_ref[...]
    li = lax.broadcasted_iota(jnp.int32, (CHUNK, CHUNK), 0)
    si = lax.broadcasted_iota(jnp.int32, (CHUNK, CHUNK), 1)
    causal = li >= si
    low_half = si < SSM_HEAD_DIM
    lo_mask = (lax.broadcasted_iota(jnp.int32, (1, LANES), 1) < SSM_HEAD_DIM).astype(BF16)
    hi_mask = 1 - lo_mask

    for c in range(nchunk):
        r0 = c * CHUNK
        dt = dt_ref[r0:r0 + CHUNK, :]
        dta = dt * a_row
        dta_hi = dta.astype(BF16)
        dta_lo = (dta - dta_hi.astype(F32)).astype(BF16)
        a2 = (_dot(tri, dta_hi) + _dot(tri, dta_lo)) * LOG2E
        a2_t = a2.T
        dt_t = dt.T
        row2_t = a2_t - jnp.log2(dt_t)
        a_end2_t = a2_t[:, CHUNK - 1:CHUNK]
        w_t = dt_t * jnp.exp2(a_end2_t - a2_t)
        e_end_t = jnp.exp2(a_end2_t)

        for g in range(SSM_GROUPS):
            if finish:
                finish.pop(0)()
            for _ in range(attn_counts[c * SSM_GROUPS + g]):
                attention.pop(0)()
            bm_gb = lane_block(n_xs + g, r0)
            cm_gb = lane_block(n_xs + SSM_GROUPS + g, r0)
            cb = _dot_nt(cm_gb, bm_gb)
            bm_gt = bm_gb.astype(F32).T
            gsl = slice(g * gw, (g + 1) * gw)
            hprev = h_s[g]
            y_off = _dot(cm_gb, hprev.astype(BF16))
            y_parts = []
            for p in range(pairs_per_group):
                pair = g * pairs_per_group + p
                lsl = slice(pair * LANES, (pair + 1) * LANES)
                xs_b = lane_block(pair, r0)
                xs_pair = xs_b.astype(F32)
                rhs = jnp.concatenate([xs_b * lo_mask, xs_b * hi_mask], axis=0)
                m_parts, b_parts, cols, e_ends = [], [], [], []
                for side in range(2):
                    hh = 2 * pair + side
                    col2 = jnp.broadcast_to(a2[:, hh:hh + 1], (CHUNK, CHUNK))
                    dec_dt = jnp.exp2(jnp.where(causal, col2 - row2_t[hh:hh + 1, :], NEG))
                    m_parts.append(cb * dec_dt)
                    b_parts.append(bm_gt * w_t[hh:hh + 1, :])
                    cols.append(col2)
                    e_ends.append(e_end_t[hh:hh + 1, :])
                lhs = jnp.concatenate(
                    [jnp.concatenate(m_parts, axis=1), jnp.concatenate(b_parts, axis=1)],
                    axis=0).astype(BF16)
                res = _dot(lhs, rhs)
                e_col = jnp.exp2(jnp.where(low_half, cols[0], cols[1]))
                y_pair = res[0:CHUNK] + e_col * y_off[:, p * LANES:(p + 1) * LANES]
                y_pair = y_pair + _param(p_ref, layout, "d_skip", lsl) * xs_pair
                y_parts.append(y_pair * zm_ref[r0:r0 + CHUNK, lsl].astype(F32))
                e_end = jnp.where(low_half, e_ends[0], e_ends[1])
                h_s[g, :, p * LANES:(p + 1) * LANES] = (
                    hprev[:, p * LANES:(p + 1) * LANES] * e_end + res[CHUNK:])
            yg = jnp.concatenate(y_parts, axis=1)
            ms = jnp.mean(yg * yg, axis=-1, keepdims=True)
            nw = _param(p_ref, layout, "ssm_norm_w", gsl)
            yn_s[r0:r0 + CHUNK, gsl] = (yg * lax.rsqrt(ms + EPS) * nw).astype(BF16)

    assert not finish and not attention
    yprev_s[...] = yn_s[...]


def _branches_call(qkv, za, g, bias, params, layout, bd, wap,
                   xbc, dt, zm, tri, x2, gate, wsp, wout, batch, seq):
    t, d = x2.shape
    aw = ATTN_HEADS * HEAD_DIM
    kvw2 = 2 * ATTN_KV_HEADS * HEAD_DIM
    sw = zm.shape[1]
    rows = min(ROWS_BRANCHES, seq)
    nt = seq // rows
    bpt = rows // BLOCK
    tile = lambda b, i: b * nt + jnp.minimum(i, nt - 1)
    scan = lambda b, i: (tile(b, i), 0)
    done = lambda b, i: (b * nt + jnp.maximum(i - 1, 0), 0)
    return pl.pallas_call(
        functools.partial(_branches_kernel, layout=layout),
        grid=(batch, nt + 1),
        in_specs=[pl.BlockSpec((rows, aw), scan),
                  pl.BlockSpec((rows, kvw2), lambda b, i: (tile(b, i), aw // kvw2)),
                  pl.BlockSpec((BLOCK, kvw2),
                               lambda b, i: (jnp.maximum(tile(b, i) * bpt - 1, 0), aw // kvw2)),
                  pl.BlockSpec((rows, aw), scan),
                  pl.BlockSpec((rows, aw), scan),
                  _resident(bias.shape), _resident(params.shape),
                  _resident(bd.shape), _resident(wap.shape),
                  pl.BlockSpec((rows, xbc.shape[1]), scan),
                  pl.BlockSpec((rows, dt.shape[1]), scan),
                  pl.BlockSpec((rows, sw), scan),
                  _resident(tri.shape),
                  pl.BlockSpec((rows, d), lambda b, i: (b * nt + jnp.maximum(i - 1, 0), 1)),
                  pl.BlockSpec((rows, d), done),
                  pl.BlockSpec((1, 1, d), lambda b, i: (b, 0, 0)),
                  _resident(wsp.shape), _resident(wout.shape)],
        out_specs=pl.BlockSpec((rows, d), done),
        out_shape=jax.ShapeDtypeStruct((t, d), F32),
        scratch_shapes=[pltpu.VMEM((rows, aw), BF16),
                        pltpu.VMEM((rows + BLOCK, ATTN_KV_HEADS * LANES), BF16),
                        pltpu.VMEM((rows + BLOCK, ATTN_KV_HEADS * LANES), BF16),
                        pltpu.VMEM((kvw2 // 2, rows + BLOCK), BF16),
                        pltpu.VMEM((2, 4 * BLOCK, 2 * BLOCK), F32),
                        pltpu.VMEM((2, 2 * BLOCK, 4 * BLOCK), BF16),
                        pltpu.VMEM((aw, rows), F32),
                        pltpu.VMEM((rows, aw), BF16),
                        pltpu.VMEM((rows, d), BF16),
                        pltpu.VMEM((SSM_GROUPS, SSM_STATE, sw // SSM_GROUPS), F32),
                        pltpu.VMEM((rows, sw), BF16),
                        pltpu.VMEM((rows, sw), BF16),
                        pltpu.VMEM((rows, d), BF16)],
        compiler_params=pltpu.CompilerParams(
            dimension_semantics=("arbitrary", "arbitrary"), vmem_limit_bytes=VMEM_LIMIT),
        name="branches",
    )(qkv, qkv, qkv, za, g, bias, params, bd, wap,
      xbc, dt, zm, tri, g, x2, gate, wsp, wout)


def _layer(x, c, w_ada, b_ada, norm_w, w_in, q_norm_w, k_norm_w, rel_bias, sinks,
           conv_w, conv_b, dt_bias, a_log, d_skip, ssm_norm_w, w_attn_proj, w_ssm_proj, w_out):
    batch, seq, d = x.shape
    aw = ATTN_HEADS * HEAD_DIM
    kvw = ATTN_KV_HEADS * HEAD_DIM
    sw = w_ssm_proj.shape[0]
    ssm_heads = sw // SSM_HEAD_DIM
    xbc_w = sw + 2 * SSM_GROUPS * SSM_STATE
    assert seq % BLOCK == 0 and seq % CHUNK == 0 and ssm_heads <= LANES

    o = np.cumsum([0, aw, kvw, kvw, aw, sw, xbc_w, ssm_heads, d, d])
    w_t = w_in.T
    c8 = jnp.zeros((8, d), F32).at[:batch].set(c)
    w_all, mod, bias = _prep_call(c8, w_ada, b_ada.reshape(1, -1), rel_bias.astype(F32), w_t, int(o[6]), ssm_heads)
    shift, scale, gate = (mod[:batch, k * d:(k + 1) * d].reshape(batch, 1, d) for k in range(3))
    wdt = jnp.pad(w_t[o[6]:o[7]], ((0, LANES - ssm_heads), (0, 0))).astype(BF16)
    widths = [aw + 2 * kvw, aw, sw, xbc_w, 2 * d]

    pad_heads = lambda v: jnp.pad(v.astype(F32), (0, LANES - ssm_heads))
    pair = lambda v: jnp.tile(v.astype(F32), LANES // HEAD_DIM)
    sink_lanes = jnp.repeat(sinks.astype(F32).reshape(ATTN_KV_HEADS, 2, 2).transpose(0, 2, 1).reshape(-1), BLOCK)
    params, layout = _param_table(
        [[(f"conv_w{w}", conv_w[w])] for w in range(CONV_WIDTH)]
        + [[("conv_b", conv_b)],
           [("d_skip", jnp.repeat(d_skip.astype(F32), SSM_HEAD_DIM)), ("norm_w", norm_w)],
           [("ssm_norm_w", ssm_norm_w), ("q_norm_w", pair(q_norm_w)), ("k_norm_w", pair(k_norm_w)),
            ("dt_bias", pad_heads(dt_bias)), ("a_log", pad_heads(a_log))],
           [("sinks", sink_lanes)]])

    x2 = x.reshape(batch * seq, d)
    qkv, za, zm, xbc, g, dt = _inproj_call(x2, shift, scale, params, layout, w_all, wdt, widths, batch, seq)

    seg = np.arange(LANES) // HEAD_DIM
    bd = jnp.asarray((seg[:, None] == seg[None, :]).astype(np.float32) / HEAD_DIM, dtype=BF16)
    tri = jnp.asarray(np.tril(np.ones((CHUNK, CHUNK), np.float32)), dtype=BF16)
    out = _branches_call(qkv, za, g, bias, params, layout, bd, w_attn_proj.astype(BF16),
                         xbc, dt, zm, tri,
                         x2, gate, w_ssm_proj.astype(BF16), w_out.astype(BF16), batch, seq)
    return out.reshape(batch, seq, d)


def kernel(x, c, w_ada, b_ada, norm_w, w_in, q_norm_w, k_norm_w, rel_bias, sinks, conv_w, conv_b,
           dt_bias, a_log, d_skip, ssm_norm_w, w_attn_proj, w_ssm_proj, w_out):
    depth = w_in.shape[0]
    for i in range(depth):
        x = _layer(x, c, w_ada[i], b_ada[i], norm_w[i], w_in[i], q_norm_w[i], k_norm_w[i],
                   rel_bias, sinks[i], conv_w[i], conv_b[i], dt_bias[i], a_log[i], d_skip[i],
                   ssm_norm_w[i], w_attn_proj[i], w_ssm_proj[i], w_out[i])
    return x
```

```python
import functools
import math

import numpy as np
import jax
import jax.numpy as jnp
from jax import lax
from jax.experimental import pallas as pl
from jax.experimental.pallas import tpu as pltpu

F32 = jnp.float32
BF16 = jnp.bfloat16

HEAD_DIM = 64
ATTN_HEADS = 16
ATTN_KV_HEADS = 4
GROUP = ATTN_HEADS // ATTN_KV_HEADS
BLOCK = 128
REL_BUCKETS = 32
REL_MAX_DIST = 128
SSM_HEAD_DIM = 64
SSM_GROUPS = 4
SSM_STATE = 128
CONV_WIDTH = 4
CHUNK = 128
EPS = 1e-6
NEG = -1e30
LOG2E = 1.4426950408889634
LANES = 128
VMEM_LIMIT = 56 * 1024 * 1024

ROWS_INPROJ = 512
ROWS_BRANCHES = 256


def _sigmoid(x):
    return 0.5 + 0.5 * jnp.tanh(0.5 * x)


def _silu(x):
    half = 0.5 * x
    return half + half * jnp.tanh(half)


def _softplus(x):
    return jnp.maximum(x, 0.0) + jnp.log(1.0 + jnp.exp(-jnp.abs(x)))


def _dot(a, b):
    return jnp.dot(a, b, preferred_element_type=F32)


def _dot_nt(a, b):
    return lax.dot_general(a, b, (((1,), (1,)), ((), ())), preferred_element_type=F32)


def _resident(shape):
    nd = len(shape)
    return pl.BlockSpec(shape, lambda *_: (0,) * nd, pipeline_mode=pl.Buffered(1))


def _param_table(rows):
    width = max(sum(v.shape[0] for _, v in r) for r in rows)
    layout, lines = {}, []
    for i, r in enumerate(rows):
        col, parts = 0, []
        for name, v in r:
            assert col % LANES == 0 and v.shape[0] % LANES == 0
            layout[name] = (i, col, v.shape[0])
            parts.append(v.astype(F32))
            col += v.shape[0]
        if col < width:
            parts.append(jnp.zeros((width - col,), F32))
        lines.append(jnp.concatenate(parts))
    return jnp.stack(lines), layout


def _param(p_ref, layout, name, lanes=None):
    r, c, n = layout[name]
    lanes = slice(0, n) if lanes is None else lanes
    return p_ref[r:r + 1, c + lanes.start:c + lanes.stop]


def _mod_kernel(c_ref, w_ref, b_ref, o_ref):
    s = _silu(c_ref[...])
    o_ref[...] = _dot(s.astype(BF16), w_ref[...].astype(BF16)) + b_ref[...]


def _bucket_table():
    qi = np.arange(BLOCK)[:, None]
    kj = np.arange(2 * BLOCK)[None, :]
    dist = qi + BLOCK - kj
    n = np.maximum(dist, 0)
    max_exact = REL_BUCKETS // 2
    nf = np.maximum(n, 1).astype(np.float32)
    large = max_exact + (np.log(nf / max_exact) / math.log(REL_MAX_DIST / max_exact)
                         * (REL_BUCKETS - max_exact)).astype(np.int32)
    large = np.minimum(large, REL_BUCKETS - 1)
    bucket = np.where(n < max_exact, n, large)
    valid = (dist >= 0) & (dist < BLOCK)
    return np.where(valid, bucket, -1).astype(np.int32)


def _bias_kernel(rb_ref, idx_ref, o_ref):
    kv = pl.program_id(0)
    idx = idx_ref[...]
    key_row = lax.broadcasted_iota(jnp.int32, idx.shape, 0)
    keep = idx >= 0
    keep_first = jnp.logical_and(keep, key_row >= BLOCK)
    for s in range(2):
        for c in range(2):
            acc = jnp.zeros(idx.shape, F32)
            for b in range(REL_BUCKETS):
                acc = jnp.where(idx == b, rb_ref[b, kv * GROUP + 2 * c + s], acc)
            acc = acc * LOG2E
            rows, cols = slice(s * 2 * BLOCK, (s + 1) * 2 * BLOCK), slice(c * BLOCK, (c + 1) * BLOCK)
            o_ref[0, 0, rows, cols] = jnp.where(keep, acc, NEG)
            o_ref[1, 0, rows, cols] = jnp.where(keep_first, acc, NEG)


def _prep_kernel(rb_ref, idx_ref, c_ref, wada_ref, bada_ref, wt_ref, w_o, mod_o, bias_o):
    w_o[...] = wt_ref[...].T.astype(BF16)

    @pl.when(pl.program_id(0) < ATTN_KV_HEADS)
    def _():
        _mod_kernel(c_ref, wada_ref, bada_ref, mod_o)
        _bias_kernel(rb_ref, idx_ref, bias_o)


def _prep_call(c8, w_ada, b_ada, rel_bias, w_t, dt_start, dt_width):
    n, k = w_t.shape
    tn = 512
    head_tiles, tail_tiles = dt_start // tn, (n - dt_start - dt_width) // tn
    assert head_tiles * tn == dt_start and tail_tiles * tn == n - dt_start - dt_width
    tail_start = dt_start + dt_width
    assert tail_start % 8 == 0

    def rows(i):
        start = jnp.where(i < head_tiles, i * tn, tail_start + (i - head_tiles) * tn)
        return (pl.multiple_of(start, 8), 0)

    d, n_mod = w_ada.shape
    n_side = ATTN_KV_HEADS
    tm = n_mod // n_side
    assert tm * n_side == n_mod and tm % LANES == 0 and head_tiles + tail_tiles >= n_side
    side = lambda i: jnp.minimum(i, n_side - 1)
    idx_t = jnp.asarray(np.ascontiguousarray(_bucket_table().T))
    w_all, mod, table = pl.pallas_call(
        _prep_kernel,
        grid=(head_tiles + tail_tiles,),
        in_specs=[pl.BlockSpec(memory_space=pltpu.SMEM),
                  pl.BlockSpec((2 * BLOCK, BLOCK), lambda i: (0, 0)),
                  pl.BlockSpec((8, d), lambda i: (0, 0)),
                  pl.BlockSpec((d, tm), lambda i: (0, side(i))),
                  pl.BlockSpec((1, tm), lambda i: (0, side(i))),
                  pl.BlockSpec((pl.Element(tn), pl.Element(k)), rows)],
        out_specs=[pl.BlockSpec((k, tn), lambda i: (0, i)),
                   pl.BlockSpec((8, tm), lambda i: (0, side(i))),
                   pl.BlockSpec((2, 1, 4 * BLOCK, 2 * BLOCK), lambda i: (0, side(i), 0, 0))],
        out_shape=[jax.ShapeDtypeStruct((k, n - dt_width), BF16),
                   jax.ShapeDtypeStruct((8, n_mod), F32),
                   jax.ShapeDtypeStruct((2, ATTN_KV_HEADS, 4 * BLOCK, 2 * BLOCK), F32)],
        compiler_params=pltpu.CompilerParams(dimension_semantics=("arbitrary",)),
        name="prep",
    )(rel_bias, idx_t, c8, w_ada, b_ada, w_t)
    return w_all, mod, table.reshape(2 * ATTN_KV_HEADS, 4 * BLOCK, 2 * BLOCK)


def _inproj_kernel(x_ref, mod_ref, p_ref, w_ref, wdt_ref,
                   qkv_o, za_o, zm_o, xbc_o, g_o, dt_o, xraw_s, carry_s, *, layout):
    ts = x_ref.shape[0]
    halo = carry_s.shape[1]

    @pl.when(pl.program_id(1) == 0)
    def _():
        carry_s[...] = jnp.zeros(carry_s.shape, F32)

    x = x_ref[...]
    ms = jnp.mean(x * x, axis=-1, keepdims=True)
    d = x.shape[1]
    mod = mod_ref[pl.ds(pl.program_id(0), 1), :]
    gain = _param(p_ref, layout, "norm_w") * (1.0 + mod[:, d:2 * d])
    h = (x * lax.rsqrt(ms + EPS) * gain + mod[:, 0:d]).astype(BF16)

    def conv_silu_store(y, c0, slot):
        for jj in range(y.shape[1] // LANES):
            j = c0 // LANES + jj
            jl = slice(j * LANES, (j + 1) * LANES)
            yj = y[:, jj * LANES:(jj + 1) * LANES]
            xraw_s[slot, jj, 0:halo, :] = carry_s[j]
            xraw_s[slot, jj, halo:halo + ts, :] = yj
            carry_s[j] = yj[ts - halo:ts, :]
            acc = _param(p_ref, layout, "conv_b", jl)
            for w in range(CONV_WIDTH):
                off = halo - (CONV_WIDTH - 1) + w
                acc = acc + _param(p_ref, layout, f"conv_w{w}", jl) * xraw_s[slot, jj, off:off + ts, :]
            xbc_o[:, jl] = _silu(acc).astype(BF16)

    step = xraw_s.shape[1] * LANES
    light, heavy, col = [], [], 0
    for o_ref, act in ((qkv_o, None), (za_o, _silu), (zm_o, _silu), (xbc_o, "conv"), (g_o, _sigmoid)):
        for c0 in range(0, o_ref.shape[1], step):
            (heavy if act == "conv" else light).append((o_ref, act, col + c0, c0))
        col += o_ref.shape[1]
    gates = [p for p in light if p[1] is _sigmoid]
    others = [p for p in light if p[1] is not _sigmoid]
    order = []
    for piece in heavy:
        order += [piece, others.pop(0), gates.pop(0) if gates else others.pop(0)]
    order += others + gates
    slot = 0
    for k, (o_ref, act, wc, c0) in enumerate(order):
        w_piece = w_ref[:, wc:wc + step]
        if k == 0:
            y = jnp.concatenate([_dot(h[:ts // 2], w_piece), _dot(h[ts // 2:], w_piece)], axis=0)
        else:
            y = _dot(h, w_piece)
        if act == "conv":
            conv_silu_store(y, c0, slot)
            slot = 1 - slot
        else:
            o_ref[:, c0:c0 + step] = (y if act is None else act(y)).astype(o_ref.dtype)
    dt_o[...] = _softplus(_dot_nt(h, wdt_ref[...]) + _param(p_ref, layout, "dt_bias"))


def _inproj_call(x2, mod, params, layout, w_all, wdt, widths, batch, seq):
    t, d = x2.shape
    ts = min(ROWS_INPROJ, seq)
    ns = seq // ts
    row = lambda b, i: (b * ns + i, 0)
    widths = list(widths) + [wdt.shape[0]]
    dtypes = [BF16] * (len(widths) - 1) + [F32]
    piece_blocks = 4
    return pl.pallas_call(
        functools.partial(_inproj_kernel, layout=layout),
        grid=(batch, ns),
        in_specs=[pl.BlockSpec((ts, d), row), _resident(mod.shape), _resident(params.shape),
                  _resident(w_all.shape), _resident(wdt.shape)],
        out_specs=[pl.BlockSpec((ts, n), row) for n in widths],
        out_shape=[jax.ShapeDtypeStruct((t, n), dt) for n, dt in zip(widths, dtypes)],
        scratch_shapes=[pltpu.VMEM((2, piece_blocks, ts + 8, LANES), F32),
                        pltpu.VMEM((widths[3] // LANES, 8, LANES), F32)],
        compiler_params=pltpu.CompilerParams(
            dimension_semantics=("arbitrary", "arbitrary"), vmem_limit_bytes=VMEM_LIMIT),
        name="inproj",
    )(x2, mod, params, w_all, wdt)


def _segment_rms(x, bd, w):
    ms = _dot((x * x).astype(BF16), bd)
    return x * lax.rsqrt(ms + EPS) * w


def _branches_kernel(*refs, layout):
    i, last = pl.program_id(1), pl.num_programs(1) - 1
    step = functools.partial(_branches_step, *refs, layout=layout)
    pl.when(i == 0)(functools.partial(step, compute=True, finishing=False, first=True))
    pl.when(jnp.logical_and(i > 0, i < last))(functools.partial(step, compute=True, finishing=True, first=False))
    pl.when(i == last)(functools.partial(step, compute=False, finishing=True, first=False))


def _branches_step(q_ref, kvc_ref, kvp_ref, za_ref, ga_ref, bias_ref, p_ref, bd_ref, wap_ref,
                   xbc_ref, dt_ref, zm_ref, tri_ref,
                   gb_ref, x_ref, mod_ref, wsp_ref, wout_ref,
                   o_ref,
                   qn_s, klo_s, khi_s, vt_s, s_s, p_s, yt_s, y_s, ya_s, h_s, yn_s, yprev_s, merged_s,
                   *, layout, compute, finishing, first):
    rows = q_ref.shape[0]
    nblk = rows // BLOCK
    nchunk = rows // CHUNK
    kvw = ATTN_KV_HEADS * HEAD_DIM
    sw = zm_ref.shape[1]
    gw = sw // SSM_GROUPS
    pairs_per_group = gw // LANES
    n_xs = sw // LANES
    if first:
        h_s[...] = jnp.zeros(h_s.shape, F32)

    d_out = o_ref.shape[1]
    piece_w = 2 * LANES

    def merge_piece(c0):
        cs = slice(c0, c0 + piece_w)
        yb = _dot(yprev_s[...], wsp_ref[:, cs])
        merged_s[:, cs] = ya_s[:, cs] + gb_ref[:, cs] * yb.astype(BF16)

    def out_piece(c0):
        cs = slice(c0, c0 + piece_w)
        o = _dot(merged_s[...], wout_ref[:, cs])
        gate = mod_ref[pl.ds(pl.program_id(0), 1), 2 * d_out + c0:2 * d_out + c0 + piece_w]
        o_ref[:, cs] = x_ref[:, cs] + gate * o

    finish = []
    if finishing:
        finish = ([functools.partial(merge_piece, c0) for c0 in range(0, d_out, piece_w)]
                  + [functools.partial(out_piece, c0) for c0 in range(0, d_out, piece_w)])
    if not compute:
        for piece in finish:
            piece()
        return

    bd = bd_ref[...]
    qw = _param(p_ref, layout, "q_norm_w") * (HEAD_DIM ** -0.5 * LOG2E)
    kw = _param(p_ref, layout, "k_norm_w")
    low_lanes = lax.broadcasted_iota(jnp.int32, (1, LANES), 1) < HEAD_DIM

    for j in range(q_ref.shape[1] // LANES):
        sl = slice(j * LANES, (j + 1) * LANES)
        qn_s[:, sl] = _segment_rms(q_ref[:, sl].astype(F32), bd, qw).astype(BF16)
    for j in range(kvw // LANES):
        sl = slice(j * LANES, (j + 1) * LANES)
        even = slice(2 * j * LANES, (2 * j + 1) * LANES)
        odd = slice((2 * j + 1) * LANES, (2 * j + 2) * LANES)
        for dst, src in ((slice(0, BLOCK), kvp_ref), (slice(BLOCK, BLOCK + rows), kvc_ref)):
            kn = _segment_rms(src[:, sl].astype(F32), bd, kw)
            lo = jnp.where(low_lanes, kn, 0.0)
            hi = jnp.where(low_lanes, 0.0, kn)
            klo_s[dst, even] = lo.astype(BF16)
            khi_s[dst, odd] = hi.astype(BF16)
            khi_s[dst, even] = pltpu.roll(lo, HEAD_DIM, axis=1).astype(BF16)
            klo_s[dst, odd] = pltpu.roll(hi, HEAD_DIM, axis=1).astype(BF16)
    vt_s[:, 0:BLOCK] = kvp_ref[:, kvw:2 * kvw].astype(F32).T.astype(BF16)
    vt_s[:, BLOCK:] = kvc_ref[:, kvw:2 * kvw].astype(F32).T.astype(BF16)

    units = [(qb, h) for qb in range(nblk) for h in range(ATTN_KV_HEADS)]

    def scores(u):
        qb, h = units[u]
        r0 = qb * BLOCK
        hl = slice(h * LANES, (h + 1) * LANES)
        k2 = jnp.concatenate([klo_s[r0:r0 + 2 * BLOCK, hl], khi_s[r0:r0 + 2 * BLOCK, hl]], axis=0)
        q2 = jnp.concatenate([qn_s[r0:r0 + BLOCK, (2 * h + c) * LANES:(2 * h + c + 1) * LANES]
                              for c in range(2)], axis=0)
        entry = ATTN_KV_HEADS + h if first and qb == 0 else h
        s_s[u % 2] = _dot_nt(k2, q2) + bias_ref[entry]

    def softmax_pv(u):
        qb, h = units[u]
        r0 = qb * BLOCK
        slot = u % 2
        inv = []
        for s in range(2):
            for c in range(2):
                blk = s_s[slot, s * 2 * BLOCK:(s + 1) * 2 * BLOCK, c * BLOCK:(c + 1) * BLOCK]
                sc = ((h * 2 + s) * 2 + c) * BLOCK
                sink = _param(p_ref, layout, "sinks", slice(sc, sc + BLOCK)) * LOG2E
                m = jnp.maximum(jnp.max(blk, axis=0, keepdims=True), sink)
                e = jnp.exp2(blk - m)
                den = jnp.sum(e, axis=0, keepdims=True) + jnp.exp2(sink - m)
                p_s[slot, :, (2 * s + c) * BLOCK:(2 * s + c + 1) * BLOCK] = e.astype(BF16)
                inv.append(1.0 / den)
        o_t = _dot(vt_s[h * HEAD_DIM:(h + 1) * HEAD_DIM, r0:r0 + 2 * BLOCK], p_s[slot])
        for s in range(2):
            for c in range(2):
                head = GROUP * h + 2 * c + s
                k = 2 * s + c
                yt_s[head * HEAD_DIM:(head + 1) * HEAD_DIM, r0:r0 + BLOCK] = (
                    o_t[:, k * BLOCK:(k + 1) * BLOCK] * inv[k])

    scores(0)
    attention = []
    for u in range(len(units)):
        def unit(u=u):
            if u + 1 < len(units):
                scores(u + 1)
            softmax_pv(u)
        attention.append(unit)

    def gate_y():
        y_s[...] = (yt_s[...].T * za_ref[...].astype(F32)).astype(BF16)

    def proj_piece(c0):
        cs = slice(c0, c0 + piece_w)
        ya = _dot(y_s[...], wap_ref[:, cs])
        ya_s[:, cs] = (ya * ga_ref[:, cs].astype(F32)).astype(BF16)

    first_proj = len(attention) + 1
    attention.append(gate_y)
    attention += [functools.partial(proj_piece, c0) for c0 in range(0, d_out, piece_w)]
    n_groups = nchunk * SSM_GROUPS
    base, extra = divmod(len(attention), n_groups)
    attn_counts = [base + (k < extra) for k in range(n_groups)]
    first_proj_group = next(k for k in range(n_groups) if sum(attn_counts[:k + 1]) > first_proj)
    assert len(finish) <= n_groups and first_proj_group >= len(finish) // 2 - 1

    def lane_block(j, r0):
        return xbc_ref[r0:r0 + CHUNK, j * LANES:(j + 1) * LANES]

    a_row = -jnp.exp(_param(p_ref, layout, "a_log"))
    tri = tri_ref[...]
    li = lax.broadcasted_iota(jnp.int32, (CHUNK, CHUNK), 0)
    si = lax.broadcasted_iota(jnp.int32, (CHUNK, CHUNK), 1)
    causal = li >= si
    low_half = si < SSM_HEAD_DIM
    lo_mask = (lax.broadcasted_iota(jnp.int32, (1, LANES), 1) < SSM_HEAD_DIM).astype(BF16)
    hi_mask = 1 - lo_mask

    for c in range(nchunk):
        r0 = c * CHUNK
        dt = dt_ref[r0:r0 + CHUNK, :]
        dta = dt * a_row
        dta_hi = dta.astype(BF16)
        dta_lo = (dta - dta_hi.astype(F32)).astype(BF16)
        a2 = (_dot(tri, dta_hi) + _dot(tri, dta_lo)) * LOG2E
        a2_t = a2.T
        dt_t = dt.T
        row2_t = a2_t - jnp.log2(dt_t)
        a_end2_t = a2_t[:, CHUNK - 1:CHUNK]
        w_t = dt_t * jnp.exp2(a_end2_t - a2_t)
        e_end_t = jnp.exp2(a_end2_t)

        for g in range(SSM_GROUPS):
            if finish:
                finish.pop(0)()
            for _ in range(attn_counts[c * SSM_GROUPS + g]):
                attention.pop(0)()
            bm_gb = lane_block(n_xs + g, r0)
            cm_gb = lane_block(n_xs + SSM_GROUPS + g, r0)
            cb = _dot_nt(cm_gb, bm_gb)
            bm_gt = bm_gb.astype(F32).T
            gsl = slice(g * gw, (g + 1) * gw)
            hprev = h_s[g]
            y_off = _dot(cm_gb, hprev.astype(BF16))
            y_parts = []
            for p in range(pairs_per_group):
                pair = g * pairs_per_group + p
                lsl = slice(pair * LANES, (pair + 1) * LANES)
                xs_b = lane_block(pair, r0)
                xs_pair = xs_b.astype(F32)
                rhs = jnp.concatenate([xs_b * lo_mask, xs_b * hi_mask], axis=0)
                m_parts, b_parts, cols, e_ends = [], [], [], []
                for side in range(2):
                    hh = 2 * pair + side
                    col2 = jnp.broadcast_to(a2[:, hh:hh + 1], (CHUNK, CHUNK))
                    dec_dt = jnp.exp2(jnp.where(causal, col2 - row2_t[hh:hh + 1, :], NEG))
                    m_parts.append(cb * dec_dt)
                    b_parts.append(bm_gt * w_t[hh:hh + 1, :])
                    cols.append(col2)
                    e_ends.append(e_end_t[hh:hh + 1, :])
                lhs = jnp.concatenate(
                    [jnp.concatenate(m_parts, axis=1), jnp.concatenate(b_parts, axis=1)],
                    axis=0).astype(BF16)
                res = _dot(lhs, rhs)
                e_col = jnp.exp2(jnp.where(low_half, cols[0], cols[1]))
                y_pair = res[0:CHUNK] + e_col * y_off[:, p * LANES:(p + 1) * LANES]
                y_pair = y_pair + _param(p_ref, layout, "d_skip", lsl) * xs_pair
                y_parts.append(y_pair * zm_ref[r0:r0 + CHUNK, lsl].astype(F32))
                e_end = jnp.where(low_half, e_ends[0], e_ends[1])
                h_s[g, :, p * LANES:(p + 1) * LANES] = (
                    hprev[:, p * LANES:(p + 1) * LANES] * e_end + res[CHUNK:])
            yg = jnp.concatenate(y_parts, axis=1)
            ms = jnp.mean(yg * yg, axis=-1, keepdims=True)
            nw = _param(p_ref, layout, "ssm_norm_w", gsl)
            yn_s[r0:r0 + CHUNK, gsl] = (yg * lax.rsqrt(ms + EPS) * nw).astype(BF16)

    assert not finish and not attention
    yprev_s[...] = yn_s[...]


def _branches_call(qkv, za, g, bias, params, layout, bd, wap,
                   xbc, dt, zm, tri, x2, mod, wsp, wout, batch, seq):
    t, d = x2.shape
    aw = ATTN_HEADS * HEAD_DIM
    kvw2 = 2 * ATTN_KV_HEADS * HEAD_DIM
    sw = zm.shape[1]
    rows = min(ROWS_BRANCHES, seq)
    nt = seq // rows
    bpt = rows // BLOCK
    tile = lambda b, i: b * nt + jnp.minimum(i, nt - 1)
    scan = lambda b, i: (tile(b, i), 0)
    done = lambda b, i: (b * nt + jnp.maximum(i - 1, 0), 0)
    return pl.pallas_call(
        functools.partial(_branches_kernel, layout=layout),
        grid=(batch, nt + 1),
        in_specs=[pl.BlockSpec((rows, aw), scan),
                  pl.BlockSpec((rows, kvw2), lambda b, i: (tile(b, i), aw // kvw2)),
                  pl.BlockSpec((BLOCK, kvw2),
                               lambda b, i: (jnp.maximum(tile(b, i) * bpt - 1, 0), aw // kvw2)),
                  pl.BlockSpec((rows, aw), scan),
                  pl.BlockSpec((rows, aw), scan),
                  _resident(bias.shape), _resident(params.shape),
                  _resident(bd.shape), _resident(wap.shape),
                  pl.BlockSpec((rows, xbc.shape[1]), scan),
                  pl.BlockSpec((rows, dt.shape[1]), scan),
                  pl.BlockSpec((rows, sw), scan),
                  _resident(tri.shape),
                  pl.BlockSpec((rows, d), lambda b, i: (b * nt + jnp.maximum(i - 1, 0), 1)),
                  pl.BlockSpec((rows, d), done),
                  _resident(mod.shape),
                  _resident(wsp.shape), _resident(wout.shape)],
        out_specs=pl.BlockSpec((rows, d), done),
        out_shape=jax.ShapeDtypeStruct((t, d), F32),
        scratch_shapes=[pltpu.VMEM((rows, aw), BF16),
                        pltpu.VMEM((rows + BLOCK, ATTN_KV_HEADS * LANES), BF16),
                        pltpu.VMEM((rows + BLOCK, ATTN_KV_HEADS * LANES), BF16),
                        pltpu.VMEM((kvw2 // 2, rows + BLOCK), BF16),
                        pltpu.VMEM((2, 4 * BLOCK, 2 * BLOCK), F32),
                        pltpu.VMEM((2, 2 * BLOCK, 4 * BLOCK), BF16),
                        pltpu.VMEM((aw, rows), F32),
                        pltpu.VMEM((rows, aw), BF16),
                        pltpu.VMEM((rows, d), BF16),
                        pltpu.VMEM((SSM_GROUPS, SSM_STATE, sw // SSM_GROUPS), F32),
                        pltpu.VMEM((rows, sw), BF16),
                        pltpu.VMEM((rows, sw), BF16),
                        pltpu.VMEM((rows, d), BF16)],
        compiler_params=pltpu.CompilerParams(
            dimension_semantics=("arbitrary", "arbitrary"), vmem_limit_bytes=VMEM_LIMIT),
        name="branches",
    )(qkv, qkv, qkv, za, g, bias, params, bd, wap,
      xbc, dt, zm, tri, g, x2, mod, wsp, wout)


def _layer(x, c, w_ada, b_ada, norm_w, w_in, q_norm_w, k_norm_w, rel_bias, sinks,
           conv_w, conv_b, dt_bias, a_log, d_skip, ssm_norm_w, w_attn_proj, w_ssm_proj, w_out):
    batch, seq, d = x.shape
    aw = ATTN_HEADS * HEAD_DIM
    kvw = ATTN_KV_HEADS * HEAD_DIM
    sw = w_ssm_proj.shape[0]
    ssm_heads = sw // SSM_HEAD_DIM
    xbc_w = sw + 2 * SSM_GROUPS * SSM_STATE
    assert seq % BLOCK == 0 and seq % CHUNK == 0 and ssm_heads <= LANES

    o = np.cumsum([0, aw, kvw, kvw, aw, sw, xbc_w, ssm_heads, d, d])
    w_t = w_in.T
    c8 = jnp.zeros((8, d), F32).at[:batch].set(c)
    w_all, mod, bias = _prep_call(c8, w_ada, b_ada.reshape(1, -1), rel_bias.astype(F32), w_t, int(o[6]), ssm_heads)
    wdt = jnp.pad(w_t[o[6]:o[7]], ((0, LANES - ssm_heads), (0, 0))).astype(BF16)
    widths = [aw + 2 * kvw, aw, sw, xbc_w, 2 * d]

    pad_heads = lambda v: jnp.pad(v.astype(F32), (0, LANES - ssm_heads))
    pair = lambda v: jnp.tile(v.astype(F32), LANES // HEAD_DIM)
    sink_lanes = jnp.repeat(sinks.astype(F32).reshape(ATTN_KV_HEADS, 2, 2).transpose(0, 2, 1).reshape(-1), BLOCK)
    params, layout = _param_table(
        [[(f"conv_w{w}", conv_w[w])] for w in range(CONV_WIDTH)]
        + [[("conv_b", conv_b)],
           [("d_skip", jnp.repeat(d_skip.astype(F32), SSM_HEAD_DIM)), ("norm_w", norm_w)],
           [("ssm_norm_w", ssm_norm_w), ("q_norm_w", pair(q_norm_w)), ("k_norm_w", pair(k_norm_w)),
            ("dt_bias", pad_heads(dt_bias)), ("a_log", pad_heads(a_log))],
           [("sinks", sink_lanes)]])

    x2 = x.reshape(batch * seq, d)
    qkv, za, zm, xbc, g, dt = _inproj_call(x2, mod, params, layout, w_all, wdt, widths, batch, seq)

    seg = np.arange(LANES) // HEAD_DIM
    bd = jnp.asarray((seg[:, None] == seg[None, :]).astype(np.float32) / HEAD_DIM, dtype=BF16)
    tri = jnp.asarray(np.tril(np.ones((CHUNK, CHUNK), np.float32)), dtype=BF16)
    out = _branches_call(qkv, za, g, bias, params, layout, bd, w_attn_proj.astype(BF16),
                         xbc, dt, zm, tri,
                         x2, mod, w_ssm_proj.astype(BF16), w_out.astype(BF16), batch, seq)
    return out.reshape(batch, seq, d)


def kernel(x, c, w_ada, b_ada, norm_w, w_in, q_norm_w, k_norm_w, rel_bias, sinks, conv_w, conv_b,
           dt_bias, a_log, d_skip, ssm_norm_w, w_attn_proj, w_ssm_proj, w_out):
    depth = w_in.shape[0]
    for i in range(depth):
        x = _layer(x, c, w_ada[i], b_ada[i], norm_w[i], w_in[i], q_norm_w[i], k_norm_w[i],
                   rel_bias, sinks[i], conv_w[i], conv_b[i], dt_bias[i], a_log[i], d_skip[i],
                   ssm_norm_w[i], w_attn_proj[i], w_ssm_proj[i], w_out[i])
    return x
```

```python
import functools
import math

import numpy as np
import jax
import jax.numpy as jnp
from jax import lax
from jax.experimental import pallas as pl
from jax.experimental.pallas import tpu as pltpu

F32 = jnp.float32
BF16 = jnp.bfloat16

HEAD_DIM = 64
ATTN_HEADS = 16
ATTN_KV_HEADS = 4
GROUP = ATTN_HEADS // ATTN_KV_HEADS
BLOCK = 128
REL_BUCKETS = 32
REL_MAX_DIST = 128
SSM_HEAD_DIM = 64
SSM_GROUPS = 4
SSM_STATE = 128
CONV_WIDTH = 4
CHUNK = 128
EPS = 1e-6
NEG = -1e30
LOG2E = 1.4426950408889634
LANES = 128
VMEM_LIMIT = 56 * 1024 * 1024

V_PAD = 16
V_ROWS = HEAD_DIM + V_PAD

ROWS_INPROJ = 512
ROWS_BRANCHES = 256


def _sigmoid(x):
    return 0.5 + 0.5 * jnp.tanh(0.5 * x)


def _silu(x):
    half = 0.5 * x
    return half + half * jnp.tanh(half)


def _softplus(x):
    return jnp.maximum(x, 0.0) + jnp.log(1.0 + jnp.exp(-jnp.abs(x)))


def _dot(a, b):
    return jnp.dot(a, b, preferred_element_type=F32)


def _dot_nt(a, b):
    return lax.dot_general(a, b, (((1,), (1,)), ((), ())), preferred_element_type=F32)


def _resident(shape):
    nd = len(shape)
    return pl.BlockSpec(shape, lambda *_: (0,) * nd, pipeline_mode=pl.Buffered(1))


def _param_table(rows):
    width = max(sum(v.shape[0] for _, v in r) for r in rows)
    layout, lines = {}, []
    for i, r in enumerate(rows):
        col, parts = 0, []
        for name, v in r:
            assert col % LANES == 0 and v.shape[0] % LANES == 0
            layout[name] = (i, col, v.shape[0])
            parts.append(v.astype(F32))
            col += v.shape[0]
        if col < width:
            parts.append(jnp.zeros((width - col,), F32))
        lines.append(jnp.concatenate(parts))
    return jnp.stack(lines), layout


def _param(p_ref, layout, name, lanes=None):
    r, c, n = layout[name]
    lanes = slice(0, n) if lanes is None else lanes
    return p_ref[r:r + 1, c + lanes.start:c + lanes.stop]


def _mod_kernel(c_ref, w_ref, b_ref, o_ref):
    s = _silu(c_ref[...])
    o_ref[...] = _dot(s.astype(BF16), w_ref[...].astype(BF16)) + b_ref[...]


def _bucket_table():
    qi = np.arange(BLOCK)[:, None]
    kj = np.arange(2 * BLOCK)[None, :]
    dist = qi + BLOCK - kj
    n = np.maximum(dist, 0)
    max_exact = REL_BUCKETS // 2
    nf = np.maximum(n, 1).astype(np.float32)
    large = max_exact + (np.log(nf / max_exact) / math.log(REL_MAX_DIST / max_exact)
                         * (REL_BUCKETS - max_exact)).astype(np.int32)
    large = np.minimum(large, REL_BUCKETS - 1)
    bucket = np.where(n < max_exact, n, large)
    valid = (dist >= 0) & (dist < BLOCK)
    return np.where(valid, bucket, -1).astype(np.int32)


def _bias_kernel(rb_ref, idx_ref, o_ref):
    kv = pl.program_id(0)
    idx = idx_ref[...]
    key_row = lax.broadcasted_iota(jnp.int32, idx.shape, 0)
    keep = idx >= 0
    keep_first = jnp.logical_and(keep, key_row >= BLOCK)
    for s in range(2):
        for c in range(2):
            acc = jnp.zeros(idx.shape, F32)
            for b in range(REL_BUCKETS):
                acc = jnp.where(idx == b, rb_ref[b, kv * GROUP + 2 * c + s], acc)
            acc = acc * LOG2E
            rows, cols = slice(s * 2 * BLOCK, (s + 1) * 2 * BLOCK), slice(c * BLOCK, (c + 1) * BLOCK)
            o_ref[0, 0, rows, cols] = jnp.where(keep, acc, NEG)
            o_ref[1, 0, rows, cols] = jnp.where(keep_first, acc, NEG)


def _prep_kernel(rb_ref, idx_ref, c_ref, wada_ref, bada_ref, wt_ref, w_o, mod_o, bias_o):
    w_o[...] = wt_ref[...].T.astype(BF16)

    @pl.when(pl.program_id(0) < ATTN_KV_HEADS)
    def _():
        _mod_kernel(c_ref, wada_ref, bada_ref, mod_o)
        _bias_kernel(rb_ref, idx_ref, bias_o)


def _prep_call(c8, w_ada, b_ada, rel_bias, w_t, dt_start, dt_width):
    n, k = w_t.shape
    tn = 512
    head_tiles, tail_tiles = dt_start // tn, (n - dt_start - dt_width) // tn
    assert head_tiles * tn == dt_start and tail_tiles * tn == n - dt_start - dt_width
    tail_start = dt_start + dt_width
    assert tail_start % 8 == 0

    def rows(i):
        start = jnp.where(i < head_tiles, i * tn, tail_start + (i - head_tiles) * tn)
        return (pl.multiple_of(start, 8), 0)

    d, n_mod = w_ada.shape
    n_side = ATTN_KV_HEADS
    tm = n_mod // n_side
    assert tm * n_side == n_mod and tm % LANES == 0 and head_tiles + tail_tiles >= n_side
    side = lambda i: jnp.minimum(i, n_side - 1)
    idx_t = jnp.asarray(np.ascontiguousarray(_bucket_table().T))
    w_all, mod, table = pl.pallas_call(
        _prep_kernel,
        grid=(head_tiles + tail_tiles,),
        in_specs=[pl.BlockSpec(memory_space=pltpu.SMEM),
                  pl.BlockSpec((2 * BLOCK, BLOCK), lambda i: (0, 0)),
                  pl.BlockSpec((8, d), lambda i: (0, 0)),
                  pl.BlockSpec((d, tm), lambda i: (0, side(i))),
                  pl.BlockSpec((1, tm), lambda i: (0, side(i))),
                  pl.BlockSpec((pl.Element(tn), pl.Element(k)), rows)],
        out_specs=[pl.BlockSpec((k, tn), lambda i: (0, i)),
                   pl.BlockSpec((8, tm), lambda i: (0, side(i))),
                   pl.BlockSpec((2, 1, 4 * BLOCK, 2 * BLOCK), lambda i: (0, side(i), 0, 0))],
        out_shape=[jax.ShapeDtypeStruct((k, n - dt_width), BF16),
                   jax.ShapeDtypeStruct((8, n_mod), F32),
                   jax.ShapeDtypeStruct((2, ATTN_KV_HEADS, 4 * BLOCK, 2 * BLOCK), F32)],
        compiler_params=pltpu.CompilerParams(dimension_semantics=("arbitrary",)),
        name="prep",
    )(rel_bias, idx_t, c8, w_ada, b_ada, w_t)
    return w_all, mod, table.reshape(2 * ATTN_KV_HEADS, 4 * BLOCK, 2 * BLOCK)


def _inproj_kernel(x_ref, shift_ref, scale_ref, p_ref, w_ref, wdt_ref,
                   qkv_o, za_o, zm_o, xbc_o, g_o, dt_o, xraw_s, carry_s, *, layout):
    ts = x_ref.shape[0]
    halo = carry_s.shape[1]

    @pl.when(pl.program_id(1) == 0)
    def _():
        carry_s[...] = jnp.zeros(carry_s.shape, F32)

    x = x_ref[...]
    ms = jnp.mean(x * x, axis=-1, keepdims=True)
    gain = _param(p_ref, layout, "norm_w") * (1.0 + scale_ref[0])
    h = (x * lax.rsqrt(ms + EPS) * gain + shift_ref[0]).astype(BF16)

    def conv_silu_store(y, c0, slot):
        for jj in range(y.shape[1] // LANES):
            j = c0 // LANES + jj
            jl = slice(j * LANES, (j + 1) * LANES)
            yj = y[:, jj * LANES:(jj + 1) * LANES]
            xraw_s[slot, jj, 0:halo, :] = carry_s[j]
            xraw_s[slot, jj, halo:halo + ts, :] = yj
            carry_s[j] = yj[ts - halo:ts, :]
            acc = _param(p_ref, layout, "conv_b", jl)
            for w in range(CONV_WIDTH):
                off = halo - (CONV_WIDTH - 1) + w
                acc = acc + _param(p_ref, layout, f"conv_w{w}", jl) * xraw_s[slot, jj, off:off + ts, :]
            xbc_o[:, jl] = _silu(acc).astype(BF16)

    step = xraw_s.shape[1] * LANES
    light, heavy, col = [], [], 0
    for o_ref, act in ((qkv_o, None), (za_o, _silu), (zm_o, _silu), (xbc_o, "conv"), (g_o, _sigmoid)):
        for c0 in range(0, o_ref.shape[1], step):
            (heavy if act == "conv" else light).append((o_ref, act, col + c0, c0))
        col += o_ref.shape[1]
    gates = [p for p in light if p[1] is _sigmoid]
    others = [p for p in light if p[1] is not _sigmoid]
    order = []
    for piece in heavy:
        order += [piece, others.pop(0), gates.pop(0) if gates else others.pop(0)]
    order += others + gates
    slot = 0
    for k, (o_ref, act, wc, c0) in enumerate(order):
        w_piece = w_ref[:, wc:wc + step]
        if k == 0:
            y = jnp.concatenate([_dot(h[:ts // 2], w_piece), _dot(h[ts // 2:], w_piece)], axis=0)
        else:
            y = _dot(h, w_piece)
        if act == "conv":
            conv_silu_store(y, c0, slot)
            slot = 1 - slot
        else:
            o_ref[:, c0:c0 + step] = (y if act is None else act(y)).astype(o_ref.dtype)
    dt_o[...] = _softplus(_dot_nt(h, wdt_ref[...]) + _param(p_ref, layout, "dt_bias"))


def _inproj_call(x2, shift, scale, params, layout, w_all, wdt, widths, batch, seq):
    t, d = x2.shape
    ts = min(ROWS_INPROJ, seq)
    ns = seq // ts
    row = lambda b, i: (b * ns + i, 0)
    per_b = pl.BlockSpec((1, 1, d), lambda b, i: (b, 0, 0))
    widths = list(widths) + [wdt.shape[0]]
    dtypes = [BF16] * (len(widths) - 1) + [F32]
    piece_blocks = 4
    return pl.pallas_call(
        functools.partial(_inproj_kernel, layout=layout),
        grid=(batch, ns),
        in_specs=[pl.BlockSpec((ts, d), row), per_b, per_b, _resident(params.shape),
                  _resident(w_all.shape), _resident(wdt.shape)],
        out_specs=[pl.BlockSpec((ts, n), row) for n in widths],
        out_shape=[jax.ShapeDtypeStruct((t, n), dt) for n, dt in zip(widths, dtypes)],
        scratch_shapes=[pltpu.VMEM((2, piece_blocks, ts + 8, LANES), F32),
                        pltpu.VMEM((widths[3] // LANES, 8, LANES), F32)],
        compiler_params=pltpu.CompilerParams(
            dimension_semantics=("arbitrary", "arbitrary"), vmem_limit_bytes=VMEM_LIMIT),
        name="inproj",
    )(x2, shift, scale, params, w_all, wdt)


def _segment_rms(x, bd, w):
    ms = _dot((x * x).astype(BF16), bd)
    return x * lax.rsqrt(ms + EPS) * w


def _branches_kernel(*refs, layout):
    i, last = pl.program_id(1), pl.num_programs(1) - 1
    step = functools.partial(_branches_step, *refs, layout=layout)
    pl.when(i == 0)(functools.partial(step, compute=True, finishing=False, first=True))
    pl.when(jnp.logical_and(i > 0, i < last))(functools.partial(step, compute=True, finishing=True, first=False))
    pl.when(i == last)(functools.partial(step, compute=False, finishing=True, first=False))


def _branches_step(q_ref, kvc_ref, kvp_ref, za_ref, ga_ref, bias_ref, p_ref, bd_ref, wap_ref,
                   xbc_ref, dt_ref, zm_ref, tri_ref,
                   gb_ref, x_ref, gate_ref, wsp_ref, wout_ref,
                   o_ref,
                   qn_s, klo_s, khi_s, vt_s, s_s, p_s, yt_s, y_s, ya_s, h_s, yn_s, yprev_s, merged_s,
                   *, layout, compute, finishing, first):
    rows = q_ref.shape[0]
    nblk = rows // BLOCK
    nchunk = rows // CHUNK
    kvw = ATTN_KV_HEADS * HEAD_DIM
    sw = zm_ref.shape[1]
    gw = sw // SSM_GROUPS
    pairs_per_group = gw // LANES
    n_xs = sw // LANES
    if first:
        h_s[...] = jnp.zeros(h_s.shape, F32)

    d_out = o_ref.shape[1]
    piece_w = 2 * LANES

    def merge_piece(c0):
        cs = slice(c0, c0 + piece_w)
        yb = _dot(yprev_s[...], wsp_ref[:, cs])
        merged_s[:, cs] = ya_s[:, cs] + gb_ref[:, cs] * yb.astype(BF16)

    def out_piece(c0):
        cs = slice(c0, c0 + piece_w)
        o = _dot(merged_s[...], wout_ref[:, cs])
        o_ref[:, cs] = x_ref[:, cs] + gate_ref[0][:, cs] * o

    finish = []
    if finishing:
        finish = ([functools.partial(merge_piece, c0) for c0 in range(0, d_out, piece_w)]
                  + [functools.partial(out_piece, c0) for c0 in range(0, d_out, piece_w)])
    if not compute:
        for piece in finish:
            piece()
        return

    bd = bd_ref[...]
    qw = _param(p_ref, layout, "q_norm_w") * (HEAD_DIM ** -0.5 * LOG2E)
    kw = _param(p_ref, layout, "k_norm_w")
    low_lanes = lax.broadcasted_iota(jnp.int32, (1, LANES), 1) < HEAD_DIM

    for j in range(q_ref.shape[1] // LANES):
        sl = slice(j * LANES, (j + 1) * LANES)
        qn_s[:, sl] = _segment_rms(q_ref[:, sl].astype(F32), bd, qw).astype(BF16)
    for j in range(kvw // LANES):
        sl = slice(j * LANES, (j + 1) * LANES)
        even = slice(2 * j * LANES, (2 * j + 1) * LANES)
        odd = slice((2 * j + 1) * LANES, (2 * j + 2) * LANES)
        for dst, src in ((slice(0, BLOCK), kvp_ref), (slice(BLOCK, BLOCK + rows), kvc_ref)):
            kn = _segment_rms(src[:, sl].astype(F32), bd, kw)
            lo = jnp.where(low_lanes, kn, 0.0)
            hi = jnp.where(low_lanes, 0.0, kn)
            klo_s[dst, even] = lo.astype(BF16)
            khi_s[dst, odd] = hi.astype(BF16)
            khi_s[dst, even] = pltpu.roll(lo, HEAD_DIM, axis=1).astype(BF16)
            klo_s[dst, odd] = pltpu.roll(hi, HEAD_DIM, axis=1).astype(BF16)
    for dst, src in ((slice(0, BLOCK), kvp_ref), (slice(BLOCK, BLOCK + rows), kvc_ref)):
        v_t = src[:, kvw:2 * kvw].astype(F32).T.astype(BF16)
        for h in range(ATTN_KV_HEADS):
            vt_s[h * V_ROWS:h * V_ROWS + HEAD_DIM, dst] = v_t[h * HEAD_DIM:(h + 1) * HEAD_DIM, :]
    if first:
        for h in range(ATTN_KV_HEADS):
            vt_s[h * V_ROWS + HEAD_DIM:(h + 1) * V_ROWS, :] = jnp.ones((V_PAD, vt_s.shape[1]), BF16)

    units = [(qb, h) for qb in range(nblk) for h in range(ATTN_KV_HEADS)]

    def scores(u):
        qb, h = units[u]
        r0 = qb * BLOCK
        hl = slice(h * LANES, (h + 1) * LANES)
        k2 = jnp.concatenate([klo_s[r0:r0 + 2 * BLOCK, hl], khi_s[r0:r0 + 2 * BLOCK, hl]], axis=0)
        q2 = jnp.concatenate([qn_s[r0:r0 + BLOCK, (2 * h + c) * LANES:(2 * h + c + 1) * LANES]
                              for c in range(2)], axis=0)
        entry = ATTN_KV_HEADS + h if first and qb == 0 else h
        s_s[u % 2] = _dot_nt(k2, q2) + bias_ref[entry]

    def softmax_pv(u):
        qb, h = units[u]
        r0 = qb * BLOCK
        slot = u % 2
        sink_terms = []
        for s in range(2):
            for c in range(2):
                blk = s_s[slot, s * 2 * BLOCK:(s + 1) * 2 * BLOCK, c * BLOCK:(c + 1) * BLOCK]
                sc = ((h * 2 + s) * 2 + c) * BLOCK
                sink = _param(p_ref, layout, "sinks", slice(sc, sc + BLOCK)) * LOG2E
                m = jnp.maximum(jnp.max(blk, axis=0, keepdims=True), sink)
                e = jnp.exp2(blk - m)
                p_s[slot, :, (2 * s + c) * BLOCK:(2 * s + c + 1) * BLOCK] = e.astype(BF16)
                sink_terms.append(jnp.exp2(sink - m))
        o_t = _dot(vt_s[h * V_ROWS:(h + 1) * V_ROWS, r0:r0 + 2 * BLOCK], p_s[slot])
        for s in range(2):
            for c in range(2):
                head = GROUP * h + 2 * c + s
                k = 2 * s + c
                cols = slice(k * BLOCK, (k + 1) * BLOCK)
                den = o_t[HEAD_DIM:HEAD_DIM + 1, cols] + sink_terms[k]
                yt_s[head * HEAD_DIM:(head + 1) * HEAD_DIM, r0:r0 + BLOCK] = o_t[0:HEAD_DIM, cols] * (1.0 / den)

    scores(0)
    attention = []
    for u in range(len(units)):
        def unit(u=u):
            if u + 1 < len(units):
                scores(u + 1)
            softmax_pv(u)
        attention.append(unit)

    def gate_y():
        y_s[...] = (yt_s[...].T * za_ref[...].astype(F32)).astype(BF16)

    def proj_piece(c0):
        cs = slice(c0, c0 + piece_w)
        ya = _dot(y_s[...], wap_ref[:, cs])
        ya_s[:, cs] = (ya * ga_ref[:, cs].astype(F32)).astype(BF16)

    first_proj = len(attention) + 1
    attention.append(gate_y)
    attention += [functools.partial(proj_piece, c0) for c0 in range(0, d_out, piece_w)]
    n_groups = nchunk * SSM_GROUPS
    base, extra = divmod(len(attention), n_groups)
    attn_counts = [base + (k < extra) for k in range(n_groups)]
    first_proj_group = next(k for k in range(n_groups) if sum(attn_counts[:k + 1]) > first_proj)
    assert len(finish) <= n_groups and first_proj_group >= len(finish) // 2 - 1

    def lane_block(j, r0):
        return xbc_ref[r0:r0 + CHUNK, j * LANES:(j + 1) * LANES]

    a_row = -jnp.exp(_param(p_ref, layout, "a_log"))
    tri = tri_ref[...]
    li = lax.broadcasted_iota(jnp.int32, (CHUNK, CHUNK), 0)
    si = lax.broadcasted_iota(jnp.int32, (CHUNK, CHUNK), 1)
    causal = li >= si
    low_half = si < SSM_HEAD_DIM
    lo_mask = (lax.broadcasted_iota(jnp.int32, (1, LANES), 1) < SSM_HEAD_DIM).astype(BF16)
    hi_mask = 1 - lo_mask

    for c in range(nchunk):
        r0 = c * CHUNK
        dt = dt_ref[r0:r0 + CHUNK, :]
        dta = dt * a_row
        dta_hi = dta.astype(BF16)
        dta_lo = (dta - dta_hi.astype(F32)).astype(BF16)
        a2 = (_dot(tri, dta_hi) + _dot(tri, dta_lo)) * LOG2E
        a2_t = a2.T
        dt_t = dt.T
        row2_t = a2_t - jnp.log2(dt_t)
        a_end2_t = a2_t[:, CHUNK - 1:CHUNK]
        w_t = dt_t * jnp.exp2(a_end2_t - a2_t)
        e_end_t = jnp.exp2(a_end2_t)

        for g in range(SSM_GROUPS):
            if finish:
                finish.pop(0)()
            for _ in range(attn_counts[c * SSM_GROUPS + g]):
                attention.pop(0)()
            bm_gb = lane_block(n_xs + g, r0)
            cm_gb = lane_block(n_xs + SSM_GROUPS + g, r0)
            cb = _dot_nt(cm_gb, bm_gb)
            bm_gt = bm_gb.astype(F32).T
            gsl = slice(g * gw, (g + 1) * gw)
            hprev = h_s[g]
            y_off = _dot(cm_gb, hprev.astype(BF16))
            y_parts = []
            for p in range(pairs_per_group):
                pair = g * pairs_per_group + p
                lsl = slice(pair * LANES, (pair + 1) * LANES)
                xs_b = lane_block(pair, r0)
                xs_pair = xs_b.astype(F32)
                rhs = jnp.concatenate([xs_b * lo_mask, xs_b * hi_mask], axis=0)
                m_parts, b_parts, cols, e_ends = [], [], [], []
                for side in range(2):
                    hh = 2 * pair + side
                    col2 = jnp.broadcast_to(a2[:, hh:hh + 1], (CHUNK, CHUNK))
                    dec_dt = jnp.exp2(jnp.where(causal, col2 - row2_t[hh:hh + 1, :], NEG))
                    m_parts.append(cb * dec_dt)
                    b_parts.append(bm_gt * w_t[hh:hh + 1, :])
                    cols.append(col2)
                    e_ends.append(e_end_t[hh:hh + 1, :])
                lhs = jnp.concatenate(
                    [jnp.concatenate(m_parts, axis=1), jnp.concatenate(b_parts, axis=1)],
                    axis=0).astype(BF16)
                res = _dot(lhs, rhs)
                e_col = jnp.exp2(jnp.where(low_half, cols[0], cols[1]))
                y_pair = res[0:CHUNK] + e_col * y_off[:, p * LANES:(p + 1) * LANES]
                y_pair = y_pair + _param(p_ref, layout, "d_skip", lsl) * xs_pair
                y_parts.append(y_pair * zm_ref[r0:r0 + CHUNK, lsl].astype(F32))
                e_end = jnp.where(low_half, e_ends[0], e_ends[1])
                h_s[g, :, p * LANES:(p + 1) * LANES] = (
                    hprev[:, p * LANES:(p + 1) * LANES] * e_end + res[CHUNK:])
            yg = jnp.concatenate(y_parts, axis=1)
            ms = jnp.mean(yg * yg, axis=-1, keepdims=True)
            nw = _param(p_ref, layout, "ssm_norm_w", gsl)
            yn_s[r0:r0 + CHUNK, gsl] = (yg * lax.rsqrt(ms + EPS) * nw).astype(BF16)

    assert not finish and not attention
    yprev_s[...] = yn_s[...]


def _branches_call(qkv, za, g, bias, params, layout, bd, wap,
                   xbc, dt, zm, tri, x2, gate, wsp, wout, batch, seq):
    t, d = x2.shape
    aw = ATTN_HEADS * HEAD_DIM
    kvw2 = 2 * ATTN_KV_HEADS * HEAD_DIM
    sw = zm.shape[1]
    rows = min(ROWS_BRANCHES, seq)
    nt = seq // rows
    bpt = rows // BLOCK
    tile = lambda b, i: b * nt + jnp.minimum(i, nt - 1)
    scan = lambda b, i: (tile(b, i), 0)
    done = lambda b, i: (b * nt + jnp.maximum(i - 1, 0), 0)
    return pl.pallas_call(
        functools.partial(_branches_kernel, layout=layout),
        grid=(batch, nt + 1),
        in_specs=[pl.BlockSpec((rows, aw), scan),
                  pl.BlockSpec((rows, kvw2), lambda b, i: (tile(b, i), aw // kvw2)),
                  pl.BlockSpec((BLOCK, kvw2),
                               lambda b, i: (jnp.maximum(tile(b, i) * bpt - 1, 0), aw // kvw2)),
                  pl.BlockSpec((rows, aw), scan),
                  pl.BlockSpec((rows, aw), scan),
                  _resident(bias.shape), _resident(params.shape),
                  _resident(bd.shape), _resident(wap.shape),
                  pl.BlockSpec((rows, xbc.shape[1]), scan),
                  pl.BlockSpec((rows, dt.shape[1]), scan),
                  pl.BlockSpec((rows, sw), scan),
                  _resident(tri.shape),
                  pl.BlockSpec((rows, d), lambda b, i: (b * nt + jnp.maximum(i - 1, 0), 1)),
                  pl.BlockSpec((rows, d), done),
                  pl.BlockSpec((1, 1, d), lambda b, i: (b, 0, 0)),
                  _resident(wsp.shape), _resident(wout.shape)],
        out_specs=pl.BlockSpec((rows, d), done),
        out_shape=jax.ShapeDtypeStruct((t, d), F32),
        scratch_shapes=[pltpu.VMEM((rows, aw), BF16),
                        pltpu.VMEM((rows + BLOCK, ATTN_KV_HEADS * LANES), BF16),
                        pltpu.VMEM((rows + BLOCK, ATTN_KV_HEADS * LANES), BF16),
                        pltpu.VMEM((ATTN_KV_HEADS * V_ROWS, rows + BLOCK), BF16),
                        pltpu.VMEM((2, 4 * BLOCK, 2 * BLOCK), F32),
                        pltpu.VMEM((2, 2 * BLOCK, 4 * BLOCK), BF16),
                        pltpu.VMEM((aw, rows), F32),
                        pltpu.VMEM((rows, aw), BF16),
                        pltpu.VMEM((rows, d), BF16),
                        pltpu.VMEM((SSM_GROUPS, SSM_STATE, sw // SSM_GROUPS), F32),
                        pltpu.VMEM((rows, sw), BF16),
                        pltpu.VMEM((rows, sw), BF16),
                        pltpu.VMEM((rows, d), BF16)],
        compiler_params=pltpu.CompilerParams(
            dimension_semantics=("arbitrary", "arbitrary"), vmem_limit_bytes=VMEM_LIMIT),
        name="branches",
    )(qkv, qkv, qkv, za, g, bias, params, bd, wap,
      xbc, dt, zm, tri, g, x2, gate, wsp, wout)


def _layer(x, c, w_ada, b_ada, norm_w, w_in, q_norm_w, k_norm_w, rel_bias, sinks,
           conv_w, conv_b, dt_bias, a_log, d_skip, ssm_norm_w, w_attn_proj, w_ssm_proj, w_out):
    batch, seq, d = x.shape
    aw = ATTN_HEADS * HEAD_DIM
    kvw = ATTN_KV_HEADS * HEAD_DIM
    sw = w_ssm_proj.shape[0]
    ssm_heads = sw // SSM_HEAD_DIM
    xbc_w = sw + 2 * SSM_GROUPS * SSM_STATE
    assert seq % BLOCK == 0 and seq % CHUNK == 0 and ssm_heads <= LANES

    o = np.cumsum([0, aw, kvw, kvw, aw, sw, xbc_w, ssm_heads, d, d])
    w_t = w_in.T
    c8 = jnp.zeros((8, d), F32).at[:batch].set(c)
    w_all, mod, bias = _prep_call(c8, w_ada, b_ada.reshape(1, -1), rel_bias.astype(F32), w_t, int(o[6]), ssm_heads)
    shift, scale, gate = (mod[:batch, k * d:(k + 1) * d].reshape(batch, 1, d) for k in range(3))
    wdt = jnp.pad(w_t[o[6]:o[7]], ((0, LANES - ssm_heads), (0, 0))).astype(BF16)
    widths = [aw + 2 * kvw, aw, sw, xbc_w, 2 * d]

    pad_heads = lambda v: jnp.pad(v.astype(F32), (0, LANES - ssm_heads))
    pair = lambda v: jnp.tile(v.astype(F32), LANES // HEAD_DIM)
    sink_lanes = jnp.repeat(sinks.astype(F32).reshape(ATTN_KV_HEADS, 2, 2).transpose(0, 2, 1).reshape(-1), BLOCK)
    params, layout = _param_table(
        [[(f"conv_w{w}", conv_w[w])] for w in range(CONV_WIDTH)]
        + [[("conv_b", conv_b)],
           [("d_skip", jnp.repeat(d_skip.astype(F32), SSM_HEAD_DIM)), ("norm_w", norm_w)],
           [("ssm_norm_w", ssm_norm_w), ("q_norm_w", pair(q_norm_w)), ("k_norm_w", pair(k_norm_w)),
            ("dt_bias", pad_heads(dt_bias)), ("a_log", pad_heads(a_log))],
           [("sinks", sink_lanes)]])

    x2 = x.reshape(batch * seq, d)
    qkv, za, zm, xbc, g, dt = _inproj_call(x2, shift, scale, params, layout, w_all, wdt, widths, batch, seq)

    seg = np.arange(LANES) // HEAD_DIM
    bd = jnp.asarray((seg[:, None] == seg[None, :]).astype(np.float32) / HEAD_DIM, dtype=BF16)
    tri = jnp.asarray(np.tril(np.ones((CHUNK, CHUNK), np.float32)), dtype=BF16)
    out = _branches_call(qkv, za, g, bias, params, layout, bd, w_attn_proj.astype(BF16),
                         xbc, dt, zm, tri,
                         x2, gate, w_ssm_proj.astype(BF16), w_out.astype(BF16), batch, seq)
    return out.reshape(batch, seq, d)


def kernel(x, c, w_ada, b_ada, norm_w, w_in, q_norm_w, k_norm_w, rel_bias, sinks, conv_w, conv_b,
           dt_bias, a_log, d_skip, ssm_norm_w, w_attn_proj, w_ssm_proj, w_out):
    depth = w_in.shape[0]
    for i in range(depth):
        x = _layer(x, c, w_ada[i], b_ada[i], norm_w[i], w_in[i], q_norm_w[i], k_norm_w[i],
                   rel_bias, sinks[i], conv_w[i], conv_b[i], dt_bias[i], a_log[i], d_skip[i],
                   ssm_norm_w[i], w_attn_proj[i], w_ssm_proj[i], w_out[i])
    return x
```

```python
import functools
import math

import numpy as np
import jax
import jax.numpy as jnp
from jax import lax
from jax.experimental import pallas as pl
from jax.experimental.pallas import tpu as pltpu

F32 = jnp.float32
BF16 = jnp.bfloat16

HEAD_DIM = 64
ATTN_HEADS = 16
ATTN_KV_HEADS = 4
GROUP = ATTN_HEADS // ATTN_KV_HEADS
BLOCK = 128
REL_BUCKETS = 32
REL_MAX_DIST = 128
SSM_HEAD_DIM = 64
SSM_GROUPS = 4
SSM_STATE = 128
CONV_WIDTH = 4
CHUNK = 128
EPS = 1e-6
NEG = -1e30
LOG2E = 1.4426950408889634
LANES = 128
VMEM_LIMIT = 56 * 1024 * 1024

V_PAD = 16
V_ROWS = HEAD_DIM + V_PAD

ROWS_INPROJ = 512
ROWS_BRANCHES = 256


def _sigmoid(x):
    return 0.5 + 0.5 * jnp.tanh(0.5 * x)


def _silu(x):
    half = 0.5 * x
    return half + half * jnp.tanh(half)


def _softplus(x):
    return jnp.maximum(x, 0.0) + jnp.log(1.0 + jnp.exp(-jnp.abs(x)))


def _dot(a, b):
    return jnp.dot(a, b, preferred_element_type=F32)


def _dot_nt(a, b):
    return lax.dot_general(a, b, (((1,), (1,)), ((), ())), preferred_element_type=F32)


def _resident(shape):
    nd = len(shape)
    return pl.BlockSpec(shape, lambda *_: (0,) * nd, pipeline_mode=pl.Buffered(1))


def _param_table(rows):
    width = max(sum(v.shape[0] for _, v in r) for r in rows)
    layout, lines = {}, []
    for i, r in enumerate(rows):
        col, parts = 0, []
        for name, v in r:
            assert col % LANES == 0 and v.shape[0] % LANES == 0
            layout[name] = (i, col, v.shape[0])
            parts.append(v.astype(F32))
            col += v.shape[0]
        if col < width:
            parts.append(jnp.zeros((width - col,), F32))
        lines.append(jnp.concatenate(parts))
    return jnp.stack(lines), layout


def _param(p_ref, layout, name, lanes=None):
    r, c, n = layout[name]
    lanes = slice(0, n) if lanes is None else lanes
    return p_ref[r:r + 1, c + lanes.start:c + lanes.stop]


def _mod_kernel(c_ref, w_ref, b_ref, o_ref):
    s = _silu(c_ref[...])
    o_ref[...] = _dot(s.astype(BF16), w_ref[...].astype(BF16)) + b_ref[...]


def _bucket_table():
    qi = np.arange(BLOCK)[:, None]
    kj = np.arange(2 * BLOCK)[None, :]
    dist = qi + BLOCK - kj
    n = np.maximum(dist, 0)
    max_exact = REL_BUCKETS // 2
    nf = np.maximum(n, 1).astype(np.float32)
    large = max_exact + (np.log(nf / max_exact) / math.log(REL_MAX_DIST / max_exact)
                         * (REL_BUCKETS - max_exact)).astype(np.int32)
    large = np.minimum(large, REL_BUCKETS - 1)
    bucket = np.where(n < max_exact, n, large)
    valid = (dist >= 0) & (dist < BLOCK)
    return np.where(valid, bucket, -1).astype(np.int32)


def _bias_kernel(rb_ref, idx_ref, o_ref):
    kv = pl.program_id(0)
    idx = idx_ref[...]
    key_row = lax.broadcasted_iota(jnp.int32, idx.shape, 0)
    keep = idx >= 0
    keep_first = jnp.logical_and(keep, key_row >= BLOCK)
    for s in range(2):
        for c in range(2):
            acc = jnp.zeros(idx.shape, F32)
            for b in range(REL_BUCKETS):
                acc = jnp.where(idx == b, rb_ref[b, kv * GROUP + 2 * c + s], acc)
            acc = acc * LOG2E
            rows, cols = slice(s * 2 * BLOCK, (s + 1) * 2 * BLOCK), slice(c * BLOCK, (c + 1) * BLOCK)
            o_ref[0, 0, rows, cols] = jnp.where(keep, acc, NEG)
            o_ref[1, 0, rows, cols] = jnp.where(keep_first, acc, NEG)


def _prep_kernel(rb_ref, idx_ref, c_ref, wada_ref, bada_ref, wt_ref, w_o, mod_o, bias_o):
    w_o[...] = wt_ref[...].T.astype(BF16)

    @pl.when(pl.program_id(0) < ATTN_KV_HEADS)
    def _():
        _mod_kernel(c_ref, wada_ref, bada_ref, mod_o)
        _bias_kernel(rb_ref, idx_ref, bias_o)


def _prep_call(c8, w_ada, b_ada, rel_bias, w_t, dt_start, dt_width):
    n, k = w_t.shape
    tn = 512
    head_tiles, tail_tiles = dt_start // tn, (n - dt_start - dt_width) // tn
    assert head_tiles * tn == dt_start and tail_tiles * tn == n - dt_start - dt_width
    tail_start = dt_start + dt_width
    assert tail_start % 8 == 0

    def rows(i):
        start = jnp.where(i < head_tiles, i * tn, tail_start + (i - head_tiles) * tn)
        return (pl.multiple_of(start, 8), 0)

    d, n_mod = w_ada.shape
    n_side = ATTN_KV_HEADS
    tm = n_mod // n_side
    assert tm * n_side == n_mod and tm % LANES == 0 and head_tiles + tail_tiles >= n_side
    side = lambda i: jnp.minimum(i, n_side - 1)
    idx_t = jnp.asarray(np.ascontiguousarray(_bucket_table().T))
    w_all, mod, table = pl.pallas_call(
        _prep_kernel,
        grid=(head_tiles + tail_tiles,),
        in_specs=[pl.BlockSpec(memory_space=pltpu.SMEM),
                  pl.BlockSpec((2 * BLOCK, BLOCK), lambda i: (0, 0)),
                  pl.BlockSpec((8, d), lambda i: (0, 0)),
                  pl.BlockSpec((d, tm), lambda i: (0, side(i))),
                  pl.BlockSpec((1, tm), lambda i: (0, side(i))),
                  pl.BlockSpec((pl.Element(tn), pl.Element(k)), rows)],
        out_specs=[pl.BlockSpec((k, tn), lambda i: (0, i)),
                   pl.BlockSpec((8, tm), lambda i: (0, side(i))),
                   pl.BlockSpec((2, 1, 4 * BLOCK, 2 * BLOCK), lambda i: (0, side(i), 0, 0))],
        out_shape=[jax.ShapeDtypeStruct((k, n - dt_width), BF16),
                   jax.ShapeDtypeStruct((8, n_mod), F32),
                   jax.ShapeDtypeStruct((2, ATTN_KV_HEADS, 4 * BLOCK, 2 * BLOCK), F32)],
        compiler_params=pltpu.CompilerParams(dimension_semantics=("arbitrary",)),
        name="prep",
    )(rel_bias, idx_t, c8, w_ada, b_ada, w_t)
    return w_all, mod, table.reshape(2 * ATTN_KV_HEADS, 4 * BLOCK, 2 * BLOCK)


def _inproj_kernel(x_ref, shift_ref, scale_ref, p_ref, w_ref, wdt_ref,
                   qkv_o, za_o, zm_o, xbc_o, g_o, dt_o, xraw_s, carry_s, *, layout):
    ts = x_ref.shape[0]
    halo = carry_s.shape[1]

    @pl.when(pl.program_id(1) == 0)
    def _():
        carry_s[...] = jnp.zeros(carry_s.shape, F32)

    x = x_ref[...]
    ms = jnp.mean(x * x, axis=-1, keepdims=True)
    gain = _param(p_ref, layout, "norm_w") * (1.0 + scale_ref[0])
    h = (x * lax.rsqrt(ms + EPS) * gain + shift_ref[0]).astype(BF16)

    def conv_silu_store(y, c0, slot):
        for jj in range(y.shape[1] // LANES):
            j = c0 // LANES + jj
            jl = slice(j * LANES, (j + 1) * LANES)
            yj = y[:, jj * LANES:(jj + 1) * LANES]
            xraw_s[slot, jj, 0:halo, :] = carry_s[j]
            xraw_s[slot, jj, halo:halo + ts, :] = yj
            carry_s[j] = yj[ts - halo:ts, :]
            acc = _param(p_ref, layout, "conv_b", jl)
            for w in range(CONV_WIDTH):
                off = halo - (CONV_WIDTH - 1) + w
                acc = acc + _param(p_ref, layout, f"conv_w{w}", jl) * xraw_s[slot, jj, off:off + ts, :]
            xbc_o[:, jl] = _silu(acc).astype(BF16)

    step = xraw_s.shape[1] * LANES
    light, heavy, col = [], [], 0
    for o_ref, act in ((qkv_o, None), (za_o, _silu), (zm_o, _silu), (xbc_o, "conv"), (g_o, _sigmoid)):
        for c0 in range(0, o_ref.shape[1], step):
            (heavy if act == "conv" else light).append((o_ref, act, col + c0, c0))
        col += o_ref.shape[1]
    gates = [p for p in light if p[1] is _sigmoid]
    others = [p for p in light if p[1] is not _sigmoid]
    order = []
    for piece in heavy:
        order += [piece, others.pop(0), gates.pop(0) if gates else others.pop(0)]
    order += others + gates
    slot = 0
    for k, (o_ref, act, wc, c0) in enumerate(order):
        w_piece = w_ref[:, wc:wc + step]
        if k == 0:
            y = jnp.concatenate([_dot(h[:ts // 2], w_piece), _dot(h[ts // 2:], w_piece)], axis=0)
        else:
            y = _dot(h, w_piece)
        if act == "conv":
            conv_silu_store(y, c0, slot)
            slot = 1 - slot
        else:
            o_ref[:, c0:c0 + step] = (y if act is None else act(y)).astype(o_ref.dtype)
    dt_o[...] = _softplus(_dot_nt(h, wdt_ref[...]) + _param(p_ref, layout, "dt_bias"))


def _inproj_call(x2, shift, scale, params, layout, w_all, wdt, widths, batch, seq):
    t, d = x2.shape
    ts = min(ROWS_INPROJ, seq)
    ns = seq // ts
    row = lambda b, i: (b * ns + i, 0)
    per_b = pl.BlockSpec((1, 1, d), lambda b, i: (b, 0, 0))
    widths = list(widths) + [wdt.shape[0]]
    dtypes = [BF16] * (len(widths) - 1) + [F32]
    piece_blocks = 4
    return pl.pallas_call(
        functools.partial(_inproj_kernel, layout=layout),
        grid=(batch, ns),
        in_specs=[pl.BlockSpec((ts, d), row), per_b, per_b, _resident(params.shape),
                  _resident(w_all.shape), _resident(wdt.shape)],
        out_specs=[pl.BlockSpec((ts, n), row) for n in widths],
        out_shape=[jax.ShapeDtypeStruct((t, n), dt) for n, dt in zip(widths, dtypes)],
        scratch_shapes=[pltpu.VMEM((2, piece_blocks, ts + 8, LANES), F32),
                        pltpu.VMEM((widths[3] // LANES, 8, LANES), F32)],
        compiler_params=pltpu.CompilerParams(
            dimension_semantics=("arbitrary", "arbitrary"), vmem_limit_bytes=VMEM_LIMIT),
        name="inproj",
    )(x2, shift, scale, params, w_all, wdt)


def _segment_rms(x, bd, w):
    ms = _dot((x * x).astype(BF16), bd)
    return x * lax.rsqrt(ms + EPS) * w


def _branches_kernel(*refs, layout):
    i, last = pl.program_id(1), pl.num_programs(1) - 1
    step = functools.partial(_branches_step, *refs, layout=layout)
    pl.when(i == 0)(functools.partial(step, compute=True, finishing=False, first=True))
    pl.when(jnp.logical_and(i > 0, i < last))(functools.partial(step, compute=True, finishing=True, first=False))
    pl.when(i == last)(functools.partial(step, compute=False, finishing=True, first=False))


def _branches_step(q_ref, kvc_ref, kvp_ref, za_ref, ga_ref, bias_ref, p_ref, bd_ref, wap_ref,
                   xbc_ref, dt_ref, zm_ref, tri_ref,
                   gb_ref, x_ref, gate_ref, wsp_ref, wout_ref,
                   o_ref,
                   qn_s, klo_s, khi_s, vt_s, s_s, p_s, yt_s, y_s, ya_s, h_s, yn_s, yprev_s, merged_s,
                   *, layout, compute, finishing, first):
    rows = q_ref.shape[0]
    nblk = rows // BLOCK
    nchunk = rows // CHUNK
    kvw = ATTN_KV_HEADS * HEAD_DIM
    sw = zm_ref.shape[1]
    gw = sw // SSM_GROUPS
    pairs_per_group = gw // LANES
    n_xs = sw // LANES
    if first:
        h_s[...] = jnp.zeros(h_s.shape, F32)

    d_out = o_ref.shape[1]
    piece_w = 2 * LANES

    def merge_piece(c0):
        cs = slice(c0, c0 + piece_w)
        yb = _dot(yprev_s[...], wsp_ref[:, cs])
        merged_s[:, cs] = ya_s[:, cs] + gb_ref[:, cs] * yb.astype(BF16)

    def out_piece(c0):
        cs = slice(c0, c0 + piece_w)
        o = _dot(merged_s[...], wout_ref[:, cs])
        o_ref[:, cs] = x_ref[:, cs] + gate_ref[0][:, cs] * o

    finish = []
    if finishing:
        finish = ([functools.partial(merge_piece, c0) for c0 in range(0, d_out, piece_w)]
                  + [functools.partial(out_piece, c0) for c0 in range(0, d_out, piece_w)])
    if not compute:
        for piece in finish:
            piece()
        return

    bd = bd_ref[...]
    qw = _param(p_ref, layout, "q_norm_w") * (HEAD_DIM ** -0.5 * LOG2E)
    kw = _param(p_ref, layout, "k_norm_w")
    low_lanes = lax.broadcasted_iota(jnp.int32, (1, LANES), 1) < HEAD_DIM

    for j in range(q_ref.shape[1] // LANES):
        sl = slice(j * LANES, (j + 1) * LANES)
        qn_s[:, sl] = _segment_rms(q_ref[:, sl].astype(F32), bd, qw).astype(BF16)
    for j in range(kvw // LANES):
        sl = slice(j * LANES, (j + 1) * LANES)
        even = slice(2 * j * LANES, (2 * j + 1) * LANES)
        odd = slice((2 * j + 1) * LANES, (2 * j + 2) * LANES)
        for dst, src in ((slice(0, BLOCK), kvp_ref), (slice(BLOCK, BLOCK + rows), kvc_ref)):
            kn = _segment_rms(src[:, sl].astype(F32), bd, kw)
            lo = jnp.where(low_lanes, kn, 0.0)
            hi = jnp.where(low_lanes, 0.0, kn)
            klo_s[dst, even] = lo.astype(BF16)
            khi_s[dst, odd] = hi.astype(BF16)
            khi_s[dst, even] = pltpu.roll(lo, HEAD_DIM, axis=1).astype(BF16)
            klo_s[dst, odd] = pltpu.roll(hi, HEAD_DIM, axis=1).astype(BF16)
    for dst, src in ((slice(0, BLOCK), kvp_ref), (slice(BLOCK, BLOCK + rows), kvc_ref)):
        v_t = src[:, kvw:2 * kvw].astype(F32).T.astype(BF16)
        for h in range(ATTN_KV_HEADS):
            vt_s[h * V_ROWS:h * V_ROWS + HEAD_DIM, dst] = v_t[h * HEAD_DIM:(h + 1) * HEAD_DIM, :]
    if first:
        for h in range(ATTN_KV_HEADS):
            vt_s[h * V_ROWS + HEAD_DIM:(h + 1) * V_ROWS, :] = jnp.ones((V_PAD, vt_s.shape[1]), BF16)

    units = [(qb, h) for qb in range(nblk) for h in range(ATTN_KV_HEADS)]

    def scores(u):
        qb, h = units[u]
        r0 = qb * BLOCK
        hl = slice(h * LANES, (h + 1) * LANES)
        k2 = jnp.concatenate([klo_s[r0:r0 + 2 * BLOCK, hl], khi_s[r0:r0 + 2 * BLOCK, hl]], axis=0)
        q2 = jnp.concatenate([qn_s[r0:r0 + BLOCK, (2 * h + c) * LANES:(2 * h + c + 1) * LANES]
                              for c in range(2)], axis=0)
        entry = ATTN_KV_HEADS + h if first and qb == 0 else h
        s_s[u % 2] = _dot_nt(k2, q2) + bias_ref[entry]

    def softmax_pv(u):
        qb, h = units[u]
        r0 = qb * BLOCK
        slot = u % 2
        sink_terms = []
        for s in range(2):
            for c in range(2):
                blk = s_s[slot, s * 2 * BLOCK:(s + 1) * 2 * BLOCK, c * BLOCK:(c + 1) * BLOCK]
                sc = ((h * 2 + s) * 2 + c) * BLOCK
                sink = _param(p_ref, layout, "sinks", slice(sc, sc + BLOCK)) * LOG2E
                m = jnp.maximum(jnp.max(blk, axis=0, keepdims=True), sink)
                e = jnp.exp2(blk - m)
                p_s[slot, :, (2 * s + c) * BLOCK:(2 * s + c + 1) * BLOCK] = e.astype(BF16)
                sink_terms.append(jnp.exp2(sink - m))
        o_t = _dot(vt_s[h * V_ROWS:(h + 1) * V_ROWS, r0:r0 + 2 * BLOCK], p_s[slot])
        for s in range(2):
            for c in range(2):
                head = GROUP * h + 2 * c + s
                k = 2 * s + c
                cols = slice(k * BLOCK, (k + 1) * BLOCK)
                den = o_t[HEAD_DIM:HEAD_DIM + 1, cols] + sink_terms[k]
                yt_s[head * HEAD_DIM:(head + 1) * HEAD_DIM, r0:r0 + BLOCK] = o_t[0:HEAD_DIM, cols] * (1.0 / den)

    scores(0)
    attention = []
    for u in range(len(units)):
        def unit(u=u):
            if u + 1 < len(units):
                scores(u + 1)
            softmax_pv(u)
        attention.append(unit)

    def gate_y():
        y_s[...] = (yt_s[...].T * za_ref[...].astype(F32)).astype(BF16)

    def proj_piece(c0):
        cs = slice(c0, c0 + piece_w)
        ya = _dot(y_s[...], wap_ref[:, cs])
        ya_s[:, cs] = (ya * ga_ref[:, cs].astype(F32)).astype(BF16)

    first_proj = len(attention) + 1
    attention.append(gate_y)
    attention += [functools.partial(proj_piece, c0) for c0 in range(0, d_out, piece_w)]
    n_groups = nchunk * SSM_GROUPS
    base, extra = divmod(len(attention), n_groups)
    attn_counts = [base + (k < extra) for k in range(n_groups)]
    first_proj_group = next(k for k in range(n_groups) if sum(attn_counts[:k + 1]) > first_proj)
    assert len(finish) <= n_groups and first_proj_group >= len(finish) // 2 - 1

    def lane_block(j, r0):
        return xbc_ref[r0:r0 + CHUNK, j * LANES:(j + 1) * LANES]

    a_row = -jnp.exp(_param(p_ref, layout, "a_log"))
    tri = tri_ref[...]
    li = lax.broadcasted_iota(jnp.int32, (CHUNK, CHUNK), 0)
    si = lax.broadcasted_iota(jnp.int32, (CHUNK, CHUNK), 1)
    causal = li >= si
    low_half = si < SSM_HEAD_DIM
    lo_mask = (lax.broadcasted_iota(jnp.int32, (1, LANES), 1) < SSM_HEAD_DIM).astype(BF16)
    hi_mask = 1 - lo_mask

    for c in range(nchunk):
        r0 = c * CHUNK
        dt = dt_ref[r0:r0 + CHUNK, :]
        dta = dt * a_row
        dta_hi = dta.astype(BF16)
        dta_lo = (dta - dta_hi.astype(F32)).astype(BF16)
        a2 = (_dot(tri, dta_hi) + _dot(tri, dta_lo)) * LOG2E
        a2_t = a2.T
        dt_t = dt.T
        row2_t = a2_t - jnp.log2(dt_t)
        a_end2_t = a2_t[:, CHUNK - 1:CHUNK]
        w_t = dt_t * jnp.exp2(a_end2_t - a2_t)
        e_end_t = jnp.exp2(a_end2_t)

        for g in range(SSM_GROUPS):
            if finish:
                finish.pop(0)()
            n_items = attn_counts[c * SSM_GROUPS + g]
            bm_gb = lane_block(n_xs + g, r0)
            cm_gb = lane_block(n_xs + SSM_GROUPS + g, r0)
            cb = _dot_nt(cm_gb, bm_gb)
            bm_gt = bm_gb.astype(F32).T
            gsl = slice(g * gw, (g + 1) * gw)
            hprev = h_s[g]
            y_off = _dot(cm_gb, hprev.astype(BF16))
            y_parts = []
            for p in range(pairs_per_group):
                if (n_items == 1 and p == pairs_per_group // 2) or (n_items >= 2 and p in (1, pairs_per_group - 1)):
                    for _ in range(1 if n_items <= 2 else (n_items + 1) // 2):
                        if attention:
                            attention.pop(0)()
                pair = g * pairs_per_group + p
                lsl = slice(pair * LANES, (pair + 1) * LANES)
                xs_b = lane_block(pair, r0)
                xs_pair = xs_b.astype(F32)
                rhs = jnp.concatenate([xs_b * lo_mask, xs_b * hi_mask], axis=0)
                m_parts, b_parts, cols, e_ends = [], [], [], []
                for side in range(2):
                    hh = 2 * pair + side
                    col2 = jnp.broadcast_to(a2[:, hh:hh + 1], (CHUNK, CHUNK))
                    dec_dt = jnp.exp2(jnp.where(causal, col2 - row2_t[hh:hh + 1, :], NEG))
                    m_parts.append(cb * dec_dt)
                    b_parts.append(bm_gt * w_t[hh:hh + 1, :])
                    cols.append(col2)
                    e_ends.append(e_end_t[hh:hh + 1, :])
                lhs = jnp.concatenate(
                    [jnp.concatenate(m_parts, axis=1), jnp.concatenate(b_parts, axis=1)],
                    axis=0).astype(BF16)
                res = _dot(lhs, rhs)
                e_col = jnp.exp2(jnp.where(low_half, cols[0], cols[1]))
                y_pair = res[0:CHUNK] + e_col * y_off[:, p * LANES:(p + 1) * LANES]
                y_pair = y_pair + _param(p_ref, layout, "d_skip", lsl) * xs_pair
                y_parts.append(y_pair * zm_ref[r0:r0 + CHUNK, lsl].astype(F32))
                e_end = jnp.where(low_half, e_ends[0], e_ends[1])
                h_s[g, :, p * LANES:(p + 1) * LANES] = (
                    hprev[:, p * LANES:(p + 1) * LANES] * e_end + res[CHUNK:])
            yg = jnp.concatenate(y_parts, axis=1)
            ms = jnp.mean(yg * yg, axis=-1, keepdims=True)
            nw = _param(p_ref, layout, "ssm_norm_w", gsl)
            yn_s[r0:r0 + CHUNK, gsl] = (yg * lax.rsqrt(ms + EPS) * nw).astype(BF16)

    assert not finish and not attention
    yprev_s[...] = yn_s[...]


def _branches_call(qkv, za, g, bias, params, layout, bd, wap,
                   xbc, dt, zm, tri, x2, gate, wsp, wout, batch, seq):
    t, d = x2.shape
    aw = ATTN_HEADS * HEAD_DIM
    kvw2 = 2 * ATTN_KV_HEADS * HEAD_DIM
    sw = zm.shape[1]
    rows = min(ROWS_BRANCHES, seq)
    nt = seq // rows
    bpt = rows // BLOCK
    tile = lambda b, i: b * nt + jnp.minimum(i, nt - 1)
    scan = lambda b, i: (tile(b, i), 0)
    done = lambda b, i: (b * nt + jnp.maximum(i - 1, 0), 0)
    return pl.pallas_call(
        functools.partial(_branches_kernel, layout=layout),
        grid=(batch, nt + 1),
        in_specs=[pl.BlockSpec((rows, aw), scan),
                  pl.BlockSpec((rows, kvw2), lambda b, i: (tile(b, i), aw // kvw2)),
                  pl.BlockSpec((BLOCK, kvw2),
                               lambda b, i: (jnp.maximum(tile(b, i) * bpt - 1, 0), aw // kvw2)),
                  pl.BlockSpec((rows, aw), scan),
                  pl.BlockSpec((rows, aw), scan),
                  _resident(bias.shape), _resident(params.shape),
                  _resident(bd.shape), _resident(wap.shape),
                  pl.BlockSpec((rows, xbc.shape[1]), scan),
                  pl.BlockSpec((rows, dt.shape[1]), scan),
                  pl.BlockSpec((rows, sw), scan),
                  _resident(tri.shape),
                  pl.BlockSpec((rows, d), lambda b, i: (b * nt + jnp.maximum(i - 1, 0), 1)),
                  pl.BlockSpec((rows, d), done),
                  pl.BlockSpec((1, 1, d), lambda b, i: (b, 0, 0)),
                  _resident(wsp.shape), _resident(wout.shape)],
        out_specs=pl.BlockSpec((rows, d), done),
        out_shape=jax.ShapeDtypeStruct((t, d), F32),
        scratch_shapes=[pltpu.VMEM((rows, aw), BF16),
                        pltpu.VMEM((rows + BLOCK, ATTN_KV_HEADS * LANES), BF16),
                        pltpu.VMEM((rows + BLOCK, ATTN_KV_HEADS * LANES), BF16),
                        pltpu.VMEM((ATTN_KV_HEADS * V_ROWS, rows + BLOCK), BF16),
                        pltpu.VMEM((2, 4 * BLOCK, 2 * BLOCK), F32),
                        pltpu.VMEM((2, 2 * BLOCK, 4 * BLOCK), BF16),
                        pltpu.VMEM((aw, rows), F32),
                        pltpu.VMEM((rows, aw), BF16),
                        pltpu.VMEM((rows, d), BF16),
                        pltpu.VMEM((SSM_GROUPS, SSM_STATE, sw // SSM_GROUPS), F32),
                        pltpu.VMEM((rows, sw), BF16),
                        pltpu.VMEM((rows, sw), BF16),
                        pltpu.VMEM((rows, d), BF16)],
        compiler_params=pltpu.CompilerParams(
            dimension_semantics=("arbitrary", "arbitrary"), vmem_limit_bytes=VMEM_LIMIT),
        name="branches",
    )(qkv, qkv, qkv, za, g, bias, params, bd, wap,
      xbc, dt, zm, tri, g, x2, gate, wsp, wout)


def _layer(x, c, w_ada, b_ada, norm_w, w_in, q_norm_w, k_norm_w, rel_bias, sinks,
           conv_w, conv_b, dt_bias, a_log, d_skip, ssm_norm_w, w_attn_proj, w_ssm_proj, w_out):
    batch, seq, d = x.shape
    aw = ATTN_HEADS * HEAD_DIM
    kvw = ATTN_KV_HEADS * HEAD_DIM
    sw = w_ssm_proj.shape[0]
    ssm_heads = sw // SSM_HEAD_DIM
    xbc_w = sw + 2 * SSM_GROUPS * SSM_STATE
    assert seq % BLOCK == 0 and seq % CHUNK == 0 and ssm_heads <= LANES

    o = np.cumsum([0, aw, kvw, kvw, aw, sw, xbc_w, ssm_heads, d, d])
    w_t = w_in.T
    c8 = jnp.zeros((8, d), F32).at[:batch].set(c)
    w_all, mod, bias = _prep_call(c8, w_ada, b_ada.reshape(1, -1), rel_bias.astype(F32), w_t, int(o[6]), ssm_heads)
    shift, scale, gate = (mod[:batch, k * d:(k + 1) * d].reshape(batch, 1, d) for k in range(3))
    wdt = jnp.pad(w_t[o[6]:o[7]], ((0, LANES - ssm_heads), (0, 0))).astype(BF16)
    widths = [aw + 2 * kvw, aw, sw, xbc_w, 2 * d]

    pad_heads = lambda v: jnp.pad(v.astype(F32), (0, LANES - ssm_heads))
    pair = lambda v: jnp.tile(v.astype(F32), LANES // HEAD_DIM)
    sink_lanes = jnp.repeat(sinks.astype(F32).reshape(ATTN_KV_HEADS, 2, 2).transpose(0, 2, 1).reshape(-1), BLOCK)
    params, layout = _param_table(
        [[(f"conv_w{w}", conv_w[w])] for w in range(CONV_WIDTH)]
        + [[("conv_b", conv_b)],
           [("d_skip", jnp.repeat(d_skip.astype(F32), SSM_HEAD_DIM)), ("norm_w", norm_w)],
           [("ssm_norm_w", ssm_norm_w), ("q_norm_w", pair(q_norm_w)), ("k_norm_w", pair(k_norm_w)),
            ("dt_bias", pad_heads(dt_bias)), ("a_log", pad_heads(a_log))],
           [("sinks", sink_lanes)]])

    x2 = x.reshape(batch * seq, d)
    qkv, za, zm, xbc, g, dt = _inproj_call(x2, shift, scale, params, layout, w_all, wdt, widths, batch, seq)

    seg = np.arange(LANES) // HEAD_DIM
    bd = jnp.asarray((seg[:, None] == seg[None, :]).astype(np.float32) / HEAD_DIM, dtype=BF16)
    tri = jnp.asarray(np.tril(np.ones((CHUNK, CHUNK), np.float32)), dtype=BF16)
    out = _branches_call(qkv, za, g, bias, params, layout, bd, w_attn_proj.astype(BF16),
                         xbc, dt, zm, tri,
                         x2, gate, w_ssm_proj.astype(BF16), w_out.astype(BF16), batch, seq)
    return out.reshape(batch, seq, d)


def kernel(x, c, w_ada, b_ada, norm_w, w_in, q_norm_w, k_norm_w, rel_bias, sinks, conv_w, conv_b,
           dt_bias, a_log, d_skip, ssm_norm_w, w_attn_proj, w_ssm_proj, w_out):
    depth = w_in.shape[0]
    for i in range(depth):
        x = _layer(x, c, w_ada[i], b_ada[i], norm_w[i], w_in[i], q_norm_w[i], k_norm_w[i],
                   rel_bias, sinks[i], conv_w[i], conv_b[i], dt_bias[i], a_log[i], d_skip[i],
                   ssm_norm_w[i], w_attn_proj[i], w_ssm_proj[i], w_out[i])
    return x
```

```python
import functools
import math

import numpy as np
import jax
import jax.numpy as jnp
from jax import lax
from jax.experimental import pallas as pl
from jax.experimental.pallas import tpu as pltpu

F32 = jnp.float32
BF16 = jnp.bfloat16

HEAD_DIM = 64
ATTN_HEADS = 16
ATTN_KV_HEADS = 4
GROUP = ATTN_HEADS // ATTN_KV_HEADS
BLOCK = 128
REL_BUCKETS = 32
REL_MAX_DIST = 128
SSM_HEAD_DIM = 64
SSM_GROUPS = 4
SSM_STATE = 128
CONV_WIDTH = 4
CHUNK = 128
EPS = 1e-6
NEG = -1e30
LOG2E = 1.4426950408889634
LANES = 128
VMEM_LIMIT = 56 * 1024 * 1024

V_PAD = 16
V_ROWS = HEAD_DIM + V_PAD

ROWS_INPROJ = 512
ROWS_BRANCHES = 256


def _sigmoid(x):
    return 0.5 + 0.5 * jnp.tanh(0.5 * x)


def _silu(x):
    half = 0.5 * x
    return half + half * jnp.tanh(half)


def _softplus(x):
    return jnp.maximum(x, 0.0) + jnp.log(1.0 + jnp.exp(-jnp.abs(x)))


def _dot(a, b):
    return jnp.dot(a, b, preferred_element_type=F32)


def _dot_nt(a, b):
    return lax.dot_general(a, b, (((1,), (1,)), ((), ())), preferred_element_type=F32)


def _resident(shape):
    nd = len(shape)
    return pl.BlockSpec(shape, lambda *_: (0,) * nd, pipeline_mode=pl.Buffered(1))


def _param_table(rows):
    width = max(sum(v.shape[0] for _, v in r) for r in rows)
    layout, lines = {}, []
    for i, r in enumerate(rows):
        col, parts = 0, []
        for name, v in r:
            assert col % LANES == 0 and v.shape[0] % LANES == 0
            layout[name] = (i, col, v.shape[0])
            parts.append(v.astype(F32))
            col += v.shape[0]
        if col < width:
            parts.append(jnp.zeros((width - col,), F32))
        lines.append(jnp.concatenate(parts))
    return jnp.stack(lines), layout


def _param(p_ref, layout, name, lanes=None):
    r, c, n = layout[name]
    lanes = slice(0, n) if lanes is None else lanes
    return p_ref[r:r + 1, c + lanes.start:c + lanes.stop]


def _mod_kernel(c_ref, w_ref, b_ref, o_ref):
    s = _silu(c_ref[...])
    o_ref[...] = _dot(s.astype(BF16), w_ref[...].astype(BF16)) + b_ref[...]


def _bucket_table():
    qi = np.arange(BLOCK)[:, None]
    kj = np.arange(2 * BLOCK)[None, :]
    dist = qi + BLOCK - kj
    n = np.maximum(dist, 0)
    max_exact = REL_BUCKETS // 2
    nf = np.maximum(n, 1).astype(np.float32)
    large = max_exact + (np.log(nf / max_exact) / math.log(REL_MAX_DIST / max_exact)
                         * (REL_BUCKETS - max_exact)).astype(np.int32)
    large = np.minimum(large, REL_BUCKETS - 1)
    bucket = np.where(n < max_exact, n, large)
    valid = (dist >= 0) & (dist < BLOCK)
    return np.where(valid, bucket, -1).astype(np.int32)


def _bias_kernel(rb_ref, idx_ref, o_ref):
    kv = pl.program_id(0)
    idx = idx_ref[...]
    key_row = lax.broadcasted_iota(jnp.int32, idx.shape, 0)
    keep = idx >= 0
    keep_first = jnp.logical_and(keep, key_row >= BLOCK)
    for s in range(2):
        for c in range(2):
            acc = jnp.zeros(idx.shape, F32)
            for b in range(REL_BUCKETS):
                acc = jnp.where(idx == b, rb_ref[b, kv * GROUP + 2 * c + s], acc)
            acc = acc * LOG2E
            rows, cols = slice(s * 2 * BLOCK, (s + 1) * 2 * BLOCK), slice(c * BLOCK, (c + 1) * BLOCK)
            o_ref[0, 0, rows, cols] = jnp.where(keep, acc, NEG)
            o_ref[1, 0, rows, cols] = jnp.where(keep_first, acc, NEG)


def _prep_kernel(rb_ref, idx_ref, c_ref, wada_ref, bada_ref, wt_ref, w_o, mod_o, bias_o):
    w_o[...] = wt_ref[...].T.astype(BF16)

    @pl.when(pl.program_id(0) < ATTN_KV_HEADS)
    def _():
        _mod_kernel(c_ref, wada_ref, bada_ref, mod_o)
        _bias_kernel(rb_ref, idx_ref, bias_o)


def _prep_call(c8, w_ada, b_ada, rel_bias, w_t, dt_start, dt_width):
    n, k = w_t.shape
    tn = 512
    head_tiles, tail_tiles = dt_start // tn, (n - dt_start - dt_width) // tn
    assert head_tiles * tn == dt_start and tail_tiles * tn == n - dt_start - dt_width
    tail_start = dt_start + dt_width
    assert tail_start % 8 == 0

    def rows(i):
        start = jnp.where(i < head_tiles, i * tn, tail_start + (i - head_tiles) * tn)
        return (pl.multiple_of(start, 8), 0)

    d, n_mod = w_ada.shape
    n_side = ATTN_KV_HEADS
    tm = n_mod // n_side
    assert tm * n_side == n_mod and tm % LANES == 0 and head_tiles + tail_tiles >= n_side
    side = lambda i: jnp.minimum(i, n_side - 1)
    idx_t = jnp.asarray(np.ascontiguousarray(_bucket_table().T))
    w_all, mod, table = pl.pallas_call(
        _prep_kernel,
        grid=(head_tiles + tail_tiles,),
        in_specs=[pl.BlockSpec(memory_space=pltpu.SMEM),
                  pl.BlockSpec((2 * BLOCK, BLOCK), lambda i: (0, 0)),
                  pl.BlockSpec((8, d), lambda i: (0, 0)),
                  pl.BlockSpec((d, tm), lambda i: (0, side(i))),
                  pl.BlockSpec((1, tm), lambda i: (0, side(i))),
                  pl.BlockSpec((pl.Element(tn), pl.Element(k)), rows)],
        out_specs=[pl.BlockSpec((k, tn), lambda i: (0, i)),
                   pl.BlockSpec((8, tm), lambda i: (0, side(i))),
                   pl.BlockSpec((2, 1, 4 * BLOCK, 2 * BLOCK), lambda i: (0, side(i), 0, 0))],
        out_shape=[jax.ShapeDtypeStruct((k, n - dt_width), BF16),
                   jax.ShapeDtypeStruct((8, n_mod), F32),
                   jax.ShapeDtypeStruct((2, ATTN_KV_HEADS, 4 * BLOCK, 2 * BLOCK), F32)],
        compiler_params=pltpu.CompilerParams(dimension_semantics=("arbitrary",)),
        name="prep",
    )(rel_bias, idx_t, c8, w_ada, b_ada, w_t)
    return w_all, mod, table.reshape(2 * ATTN_KV_HEADS, 4 * BLOCK, 2 * BLOCK)


def _inproj_kernel(x_ref, shift_ref, scale_ref, p_ref, w_ref, wdt_ref,
                   qkv_o, za_o, zm_o, xbc_o, g_o, dt_o, xraw_s, carry_s, *, layout):
    ts = x_ref.shape[0]
    halo = carry_s.shape[1]

    @pl.when(pl.program_id(1) == 0)
    def _():
        carry_s[...] = jnp.zeros(carry_s.shape, F32)

    x = x_ref[...]
    ms = jnp.mean(x * x, axis=-1, keepdims=True)
    gain = _param(p_ref, layout, "norm_w") * (1.0 + scale_ref[0])
    h = (x * lax.rsqrt(ms + EPS) * gain + shift_ref[0]).astype(BF16)

    def conv_silu_store(y, c0, slot):
        for jj in range(y.shape[1] // LANES):
            j = c0 // LANES + jj
            jl = slice(j * LANES, (j + 1) * LANES)
            yj = y[:, jj * LANES:(jj + 1) * LANES]
            xraw_s[slot, jj, 0:halo, :] = carry_s[j]
            xraw_s[slot, jj, halo:halo + ts, :] = yj
            carry_s[j] = yj[ts - halo:ts, :]
            acc = _param(p_ref, layout, "conv_b", jl)
            for w in range(CONV_WIDTH):
                off = halo - (CONV_WIDTH - 1) + w
                acc = acc + _param(p_ref, layout, f"conv_w{w}", jl) * xraw_s[slot, jj, off:off + ts, :]
            xbc_o[:, jl] = _silu(acc).astype(BF16)

    step = xraw_s.shape[1] * LANES
    light, heavy, col = [], [], 0
    for o_ref, act in ((qkv_o, None), (za_o, _silu), (zm_o, _silu), (xbc_o, "conv"), (g_o, _sigmoid)):
        for c0 in range(0, o_ref.shape[1], step):
            (heavy if act == "conv" else light).append((o_ref, act, col + c0, c0))
        col += o_ref.shape[1]
    gates = [p for p in light if p[1] is _sigmoid]
    others = [p for p in light if p[1] is not _sigmoid]
    order = []
    for piece in heavy:
        order += [piece, others.pop(0), gates.pop(0) if gates else others.pop(0)]
    order += others + gates
    slot = 0
    for k, (o_ref, act, wc, c0) in enumerate(order):
        w_piece = w_ref[:, wc:wc + step]
        if k == 0:
            y = jnp.concatenate([_dot(h[:ts // 2], w_piece), _dot(h[ts // 2:], w_piece)], axis=0)
        else:
            y = _dot(h, w_piece)
        if act == "conv":
            conv_silu_store(y, c0, slot)
            slot = 1 - slot
        else:
            o_ref[:, c0:c0 + step] = (y if act is None else act(y)).astype(o_ref.dtype)
    dt_o[...] = _softplus(_dot_nt(h, wdt_ref[...]) + _param(p_ref, layout, "dt_bias"))


def _inproj_call(x2, shift, scale, params, layout, w_all, wdt, widths, batch, seq):
    t, d = x2.shape
    ts = min(ROWS_INPROJ, seq)
    ns = seq // ts
    row = lambda b, i: (b * ns + i, 0)
    per_b = pl.BlockSpec((1, 1, d), lambda b, i: (b, 0, 0))
    widths = list(widths) + [wdt.shape[0]]
    dtypes = [BF16] * (len(widths) - 1) + [F32]
    piece_blocks = 4
    return pl.pallas_call(
        functools.partial(_inproj_kernel, layout=layout),
        grid=(batch, ns),
        in_specs=[pl.BlockSpec((ts, d), row), per_b, per_b, _resident(params.shape),
                  _resident(w_all.shape), _resident(wdt.shape)],
        out_specs=[pl.BlockSpec((ts, n), row) for n in widths],
        out_shape=[jax.ShapeDtypeStruct((t, n), dt) for n, dt in zip(widths, dtypes)],
        scratch_shapes=[pltpu.VMEM((2, piece_blocks, ts + 8, LANES), F32),
                        pltpu.VMEM((widths[3] // LANES, 8, LANES), F32)],
        compiler_params=pltpu.CompilerParams(
            dimension_semantics=("arbitrary", "arbitrary"), vmem_limit_bytes=VMEM_LIMIT),
        name="inproj",
    )(x2, shift, scale, params, w_all, wdt)


def _segment_rms(x, bd, w):
    ms = _dot((x * x).astype(BF16), bd)
    return x * lax.rsqrt(ms + EPS) * w


def _branches_kernel(*refs, layout):
    i, last = pl.program_id(1), pl.num_programs(1) - 1
    step = functools.partial(_branches_step, *refs, layout=layout)
    pl.when(i == 0)(functools.partial(step, compute=True, finishing=False, first=True))
    pl.when(jnp.logical_and(i > 0, i < last))(functools.partial(step, compute=True, finishing=True, first=False))
    pl.when(i == last)(functools.partial(step, compute=False, finishing=True, first=False))


def _branches_step(q_ref, kvc_ref, kvp_ref, za_ref, ga_ref, bias_ref, p_ref, bd_ref, wap_ref,
                   xbc_ref, dt_ref, zm_ref, tri_ref,
                   gb_ref, x_ref, gate_ref, wsp_ref, wout_ref,
                   o_ref,
                   qn_s, klo_s, khi_s, vt_s, s_s, p_s, yt_s, y_s, ya_s, h_s, yn_s, yprev_s, merged_s,
                   *, layout, compute, finishing, first):
    rows = q_ref.shape[0]
    nblk = rows // BLOCK
    nchunk = rows // CHUNK
    kvw = ATTN_KV_HEADS * HEAD_DIM
    sw = zm_ref.shape[1]
    gw = sw // SSM_GROUPS
    pairs_per_group = gw // LANES
    n_xs = sw // LANES
    if first:
        h_s[...] = jnp.zeros(h_s.shape, F32)

    d_out = o_ref.shape[1]
    piece_w = 2 * LANES

    def merge_piece(c0):
        cs = slice(c0, c0 + piece_w)
        yb = _dot(yprev_s[...], wsp_ref[:, cs])
        merged_s[:, cs] = ya_s[:, cs] + gb_ref[:, cs] * yb.astype(BF16)

    def out_piece(c0):
        cs = slice(c0, c0 + piece_w)
        o = _dot(merged_s[...], wout_ref[:, cs])
        o_ref[:, cs] = x_ref[:, cs] + gate_ref[0][:, cs] * o

    finish = []
    if finishing:
        finish = ([functools.partial(merge_piece, c0) for c0 in range(0, d_out, piece_w)]
                  + [functools.partial(out_piece, c0) for c0 in range(0, d_out, piece_w)])
    if not compute:
        for piece in finish:
            piece()
        return

    bd = bd_ref[...]
    qw = _param(p_ref, layout, "q_norm_w") * (HEAD_DIM ** -0.5 * LOG2E)
    kw = _param(p_ref, layout, "k_norm_w")
    low_lanes = lax.broadcasted_iota(jnp.int32, (1, LANES), 1) < HEAD_DIM

    for j in range(q_ref.shape[1] // LANES):
        sl = slice(j * LANES, (j + 1) * LANES)
        qn_s[:, sl] = _segment_rms(q_ref[:, sl].astype(F32), bd, qw).astype(BF16)
    for j in range(kvw // LANES):
        sl = slice(j * LANES, (j + 1) * LANES)
        even = slice(2 * j * LANES, (2 * j + 1) * LANES)
        odd = slice((2 * j + 1) * LANES, (2 * j + 2) * LANES)
        for dst, src in ((slice(0, BLOCK), kvp_ref), (slice(BLOCK, BLOCK + rows), kvc_ref)):
            kn = _segment_rms(src[:, sl].astype(F32), bd, kw)
            lo = jnp.where(low_lanes, kn, 0.0)
            hi = jnp.where(low_lanes, 0.0, kn)
            klo_s[dst, even] = lo.astype(BF16)
            khi_s[dst, odd] = hi.astype(BF16)
            khi_s[dst, even] = pltpu.roll(lo, HEAD_DIM, axis=1).astype(BF16)
            klo_s[dst, odd] = pltpu.roll(hi, HEAD_DIM, axis=1).astype(BF16)
    for dst, src in ((slice(0, BLOCK), kvp_ref), (slice(BLOCK, BLOCK + rows), kvc_ref)):
        v_t = src[:, kvw:2 * kvw].astype(F32).T.astype(BF16)
        for h in range(ATTN_KV_HEADS):
            vt_s[h * V_ROWS:h * V_ROWS + HEAD_DIM, dst] = v_t[h * HEAD_DIM:(h + 1) * HEAD_DIM, :]
    if first:
        for h in range(ATTN_KV_HEADS):
            vt_s[h * V_ROWS + HEAD_DIM:(h + 1) * V_ROWS, :] = jnp.ones((V_PAD, vt_s.shape[1]), BF16)

    units = [(qb, h) for qb in range(nblk) for h in range(ATTN_KV_HEADS)]

    def scores(u):
        qb, h = units[u]
        r0 = qb * BLOCK
        hl = slice(h * LANES, (h + 1) * LANES)
        k2 = jnp.concatenate([klo_s[r0:r0 + 2 * BLOCK, hl], khi_s[r0:r0 + 2 * BLOCK, hl]], axis=0)
        q2 = jnp.concatenate([qn_s[r0:r0 + BLOCK, (2 * h + c) * LANES:(2 * h + c + 1) * LANES]
                              for c in range(2)], axis=0)
        entry = ATTN_KV_HEADS + h if first and qb == 0 else h
        s_s[u % 2] = _dot_nt(k2, q2) + bias_ref[entry]

    sink_terms = {}

    def softmax_block(u, s, c):
        qb, h = units[u]
        slot = u % 2
        blk = s_s[slot, s * 2 * BLOCK:(s + 1) * 2 * BLOCK, c * BLOCK:(c + 1) * BLOCK]
        sc = ((h * 2 + s) * 2 + c) * BLOCK
        sink = _param(p_ref, layout, "sinks", slice(sc, sc + BLOCK)) * LOG2E
        m = jnp.maximum(jnp.max(blk, axis=0, keepdims=True), sink)
        e = jnp.exp2(blk - m)
        p_s[slot, :, (2 * s + c) * BLOCK:(2 * s + c + 1) * BLOCK] = e.astype(BF16)
        sink_terms[u, 2 * s + c] = jnp.exp2(sink - m)

    def pv(u):
        qb, h = units[u]
        r0 = qb * BLOCK
        slot = u % 2
        o_t = _dot(vt_s[h * V_ROWS:(h + 1) * V_ROWS, r0:r0 + 2 * BLOCK], p_s[slot])
        for s in range(2):
            for c in range(2):
                head = GROUP * h + 2 * c + s
                k = 2 * s + c
                cols = slice(k * BLOCK, (k + 1) * BLOCK)
                den = o_t[HEAD_DIM:HEAD_DIM + 1, cols] + sink_terms[u, k]
                yt_s[head * HEAD_DIM:(head + 1) * HEAD_DIM, r0:r0 + BLOCK] = o_t[0:HEAD_DIM, cols] * (1.0 / den)

    scores(0)
    attention = []
    for u in range(len(units)):
        if u + 1 < len(units):
            attention.append(functools.partial(scores, u + 1))
        attention += [functools.partial(softmax_block, u, s, c) for s in range(2) for c in range(2)]
        attention.append(functools.partial(pv, u))

    def gate_y():
        y_s[...] = (yt_s[...].T * za_ref[...].astype(F32)).astype(BF16)

    def proj_piece(c0):
        cs = slice(c0, c0 + piece_w)
        ya = _dot(y_s[...], wap_ref[:, cs])
        ya_s[:, cs] = (ya * ga_ref[:, cs].astype(F32)).astype(BF16)

    first_proj = len(attention) + 1
    attention.append(gate_y)
    attention += [functools.partial(proj_piece, c0) for c0 in range(0, d_out, piece_w)]
    n_groups = nchunk * SSM_GROUPS
    n_slots = n_groups * pairs_per_group
    base, extra = divmod(len(attention), n_slots)
    attn_counts = [base + (k < extra) for k in range(n_slots)]
    first_proj_slot = next(k for k in range(n_slots) if sum(attn_counts[:k + 1]) > first_proj)
    assert len(finish) <= n_groups and first_proj_slot // pairs_per_group >= len(finish) // 2 - 1

    def lane_block(j, r0):
        return xbc_ref[r0:r0 + CHUNK, j * LANES:(j + 1) * LANES]

    a_row = -jnp.exp(_param(p_ref, layout, "a_log"))
    tri = tri_ref[...]
    li = lax.broadcasted_iota(jnp.int32, (CHUNK, CHUNK), 0)
    si = lax.broadcasted_iota(jnp.int32, (CHUNK, CHUNK), 1)
    causal = li >= si
    low_half = si < SSM_HEAD_DIM
    lo_mask = (lax.broadcasted_iota(jnp.int32, (1, LANES), 1) < SSM_HEAD_DIM).astype(BF16)
    hi_mask = 1 - lo_mask

    for c in range(nchunk):
        r0 = c * CHUNK
        dt = dt_ref[r0:r0 + CHUNK, :]
        dta = dt * a_row
        dta_hi = dta.astype(BF16)
        dta_lo = (dta - dta_hi.astype(F32)).astype(BF16)
        a2 = (_dot(tri, dta_hi) + _dot(tri, dta_lo)) * LOG2E
        a2_t = a2.T
        dt_t = dt.T
        row2_t = a2_t - jnp.log2(dt_t)
        a_end2_t = a2_t[:, CHUNK - 1:CHUNK]
        w_t = dt_t * jnp.exp2(a_end2_t - a2_t)
        e_end_t = jnp.exp2(a_end2_t)

        for g in range(SSM_GROUPS):
            if finish:
                finish.pop(0)()
            bm_gb = lane_block(n_xs + g, r0)
            cm_gb = lane_block(n_xs + SSM_GROUPS + g, r0)
            cb = _dot_nt(cm_gb, bm_gb)
            bm_gt = bm_gb.astype(F32).T
            gsl = slice(g * gw, (g + 1) * gw)
            hprev = h_s[g]
            y_off = _dot(cm_gb, hprev.astype(BF16))
            y_parts = []
            for p in range(pairs_per_group):
                for _ in range(attn_counts[(c * SSM_GROUPS + g) * pairs_per_group + p]):
                    attention.pop(0)()
                pair = g * pairs_per_group + p
                lsl = slice(pair * LANES, (pair + 1) * LANES)
                xs_b = lane_block(pair, r0)
                xs_pair = xs_b.astype(F32)
                rhs = jnp.concatenate([xs_b * lo_mask, xs_b * hi_mask], axis=0)
                m_parts, b_parts, cols, e_ends = [], [], [], []
                for side in range(2):
                    hh = 2 * pair + side
                    col2 = jnp.broadcast_to(a2[:, hh:hh + 1], (CHUNK, CHUNK))
                    dec_dt = jnp.exp2(jnp.where(causal, col2 - row2_t[hh:hh + 1, :], NEG))
                    m_parts.append(cb * dec_dt)
                    b_parts.append(bm_gt * w_t[hh:hh + 1, :])
                    cols.append(col2)
                    e_ends.append(e_end_t[hh:hh + 1, :])
                lhs = jnp.concatenate(
                    [jnp.concatenate(m_parts, axis=1), jnp.concatenate(b_parts, axis=1)],
                    axis=0).astype(BF16)
                res = _dot(lhs, rhs)
                e_col = jnp.exp2(jnp.where(low_half, cols[0], cols[1]))
                y_pair = res[0:CHUNK] + e_col * y_off[:, p * LANES:(p + 1) * LANES]
                y_pair = y_pair + _param(p_ref, layout, "d_skip", lsl) * xs_pair
                y_parts.append(y_pair * zm_ref[r0:r0 + CHUNK, lsl].astype(F32))
                e_end = jnp.where(low_half, e_ends[0], e_ends[1])
                h_s[g, :, p * LANES:(p + 1) * LANES] = (
                    hprev[:, p * LANES:(p + 1) * LANES] * e_end + res[CHUNK:])
            yg = jnp.concatenate(y_parts, axis=1)
            ms = jnp.mean(yg * yg, axis=-1, keepdims=True)
            nw = _param(p_ref, layout, "ssm_norm_w", gsl)
            yn_s[r0:r0 + CHUNK, gsl] = (yg * lax.rsqrt(ms + EPS) * nw).astype(BF16)

    assert not finish and not attention
    yprev_s[...] = yn_s[...]


def _branches_call(qkv, za, g, bias, params, layout, bd, wap,
                   xbc, dt, zm, tri, x2, gate, wsp, wout, batch, seq):
    t, d = x2.shape
    aw = ATTN_HEADS * HEAD_DIM
    kvw2 = 2 * ATTN_KV_HEADS * HEAD_DIM
    sw = zm.shape[1]
    rows = min(ROWS_BRANCHES, seq)
    nt = seq // rows
    bpt = rows // BLOCK
    tile = lambda b, i: b * nt + jnp.minimum(i, nt - 1)
    scan = lambda b, i: (tile(b, i), 0)
    done = lambda b, i: (b * nt + jnp.maximum(i - 1, 0), 0)
    return pl.pallas_call(
        functools.partial(_branches_kernel, layout=layout),
        grid=(batch, nt + 1),
        in_specs=[pl.BlockSpec((rows, aw), scan),
                  pl.BlockSpec((rows, kvw2), lambda b, i: (tile(b, i), aw // kvw2)),
                  pl.BlockSpec((BLOCK, kvw2),
                               lambda b, i: (jnp.maximum(tile(b, i) * bpt - 1, 0), aw // kvw2)),
                  pl.BlockSpec((rows, aw), scan),
                  pl.BlockSpec((rows, aw), scan),
                  _resident(bias.shape), _resident(params.shape),
                  _resident(bd.shape), _resident(wap.shape),
                  pl.BlockSpec((rows, xbc.shape[1]), scan),
                  pl.BlockSpec((rows, dt.shape[1]), scan),
                  pl.BlockSpec((rows, sw), scan),
                  _resident(tri.shape),
                  pl.BlockSpec((rows, d), lambda b, i: (b * nt + jnp.maximum(i - 1, 0), 1)),
                  pl.BlockSpec((rows, d), done),
                  pl.BlockSpec((1, 1, d), lambda b, i: (b, 0, 0)),
                  _resident(wsp.shape), _resident(wout.shape)],
        out_specs=pl.BlockSpec((rows, d), done),
        out_shape=jax.ShapeDtypeStruct((t, d), F32),
        scratch_shapes=[pltpu.VMEM((rows, aw), BF16),
                        pltpu.VMEM((rows + BLOCK, ATTN_KV_HEADS * LANES), BF16),
                        pltpu.VMEM((rows + BLOCK, ATTN_KV_HEADS * LANES), BF16),
                        pltpu.VMEM((ATTN_KV_HEADS * V_ROWS, rows + BLOCK), BF16),
                        pltpu.VMEM((2, 4 * BLOCK, 2 * BLOCK), F32),
                        pltpu.VMEM((2, 2 * BLOCK, 4 * BLOCK), BF16),
                        pltpu.VMEM((aw, rows), F32),
                        pltpu.VMEM((rows, aw), BF16),
                        pltpu.VMEM((rows, d), BF16),
                        pltpu.VMEM((SSM_GROUPS, SSM_STATE, sw // SSM_GROUPS), F32),
                        pltpu.VMEM((rows, sw), BF16),
                        pltpu.VMEM((rows, sw), BF16),
                        pltpu.VMEM((rows, d), BF16)],
        compiler_params=pltpu.CompilerParams(
            dimension_semantics=("arbitrary", "arbitrary"), vmem_limit_bytes=VMEM_LIMIT),
        name="branches",
    )(qkv, qkv, qkv, za, g, bias, params, bd, wap,
      xbc, dt, zm, tri, g, x2, gate, wsp, wout)


def _layer(x, c, w_ada, b_ada, norm_w, w_in, q_norm_w, k_norm_w, rel_bias, sinks,
           conv_w, conv_b, dt_bias, a_log, d_skip, ssm_norm_w, w_attn_proj, w_ssm_proj, w_out):
    batch, seq, d = x.shape
    aw = ATTN_HEADS * HEAD_DIM
    kvw = ATTN_KV_HEADS * HEAD_DIM
    sw = w_ssm_proj.shape[0]
    ssm_heads = sw // SSM_HEAD_DIM
    xbc_w = sw + 2 * SSM_GROUPS * SSM_STATE
    assert seq % BLOCK == 0 and seq % CHUNK == 0 and ssm_heads <= LANES

    o = np.cumsum([0, aw, kvw, kvw, aw, sw, xbc_w, ssm_heads, d, d])
    w_t = w_in.T
    c8 = jnp.zeros((8, d), F32).at[:batch].set(c)
    w_all, mod, bias = _prep_call(c8, w_ada, b_ada.reshape(1, -1), rel_bias.astype(F32), w_t, int(o[6]), ssm_heads)
    shift, scale, gate = (mod[:batch, k * d:(k + 1) * d].reshape(batch, 1, d) for k in range(3))
    wdt = jnp.pad(w_t[o[6]:o[7]], ((0, LANES - ssm_heads), (0, 0))).astype(BF16)
    widths = [aw + 2 * kvw, aw, sw, xbc_w, 2 * d]

    pad_heads = lambda v: jnp.pad(v.astype(F32), (0, LANES - ssm_heads))
    pair = lambda v: jnp.tile(v.astype(F32), LANES // HEAD_DIM)
    sink_lanes = jnp.repeat(sinks.astype(F32).reshape(ATTN_KV_HEADS, 2, 2).transpose(0, 2, 1).reshape(-1), BLOCK)
    params, layout = _param_table(
        [[(f"conv_w{w}", conv_w[w])] for w in range(CONV_WIDTH)]
        + [[("conv_b", conv_b)],
           [("d_skip", jnp.repeat(d_skip.astype(F32), SSM_HEAD_DIM)), ("norm_w", norm_w)],
           [("ssm_norm_w", ssm_norm_w), ("q_norm_w", pair(q_norm_w)), ("k_norm_w", pair(k_norm_w)),
            ("dt_bias", pad_heads(dt_bias)), ("a_log", pad_heads(a_log))],
           [("sinks", sink_lanes)]])

    x2 = x.reshape(batch * seq, d)
    qkv, za, zm, xbc, g, dt = _inproj_call(x2, shift, scale, params, layout, w_all, wdt, widths, batch, seq)

    seg = np.arange(LANES) // HEAD_DIM
    bd = jnp.asarray((seg[:, None] == seg[None, :]).astype(np.float32) / HEAD_DIM, dtype=BF16)
    tri = jnp.asarray(np.tril(np.ones((CHUNK, CHUNK), np.float32)), dtype=BF16)
    out = _branches_call(qkv, za, g, bias, params, layout, bd, w_attn_proj.astype(BF16),
                         xbc, dt, zm, tri,
                         x2, gate, w_ssm_proj.astype(BF16), w_out.astype(BF16), batch, seq)
    return out.reshape(batch, seq, d)


def kernel(x, c, w_ada, b_ada, norm_w, w_in, q_norm_w, k_norm_w, rel_bias, sinks, conv_w, conv_b,
           dt_bias, a_log, d_skip, ssm_norm_w, w_attn_proj, w_ssm_proj, w_out):
    depth = w_in.shape[0]
    for i in range(depth):
        x = _layer(x, c, w_ada[i], b_ada[i], norm_w[i], w_in[i], q_norm_w[i], k_norm_w[i],
                   rel_bias, sinks[i], conv_w[i], conv_b[i], dt_bias[i], a_log[i], d_skip[i],
                   ssm_norm_w[i], w_attn_proj[i], w_ssm_proj[i], w_out[i])
    return x
```

```python
import functools
import math

import numpy as np
import jax
import jax.numpy as jnp
from jax import lax
from jax.experimental import pallas as pl
from jax.experimental.pallas import tpu as pltpu

F32 = jnp.float32
BF16 = jnp.bfloat16

HEAD_DIM = 64
ATTN_HEADS = 16
ATTN_KV_HEADS = 4
GROUP = ATTN_HEADS // ATTN_KV_HEADS
BLOCK = 128
REL_BUCKETS = 32
REL_MAX_DIST = 128
SSM_HEAD_DIM = 64
SSM_GROUPS = 4
SSM_STATE = 128
CONV_WIDTH = 4
CHUNK = 128
EPS = 1e-6
NEG = -1e30
LOG2E = 1.4426950408889634
LANES = 128
VMEM_LIMIT = 56 * 1024 * 1024

V_PAD = 16
V_ROWS = HEAD_DIM + V_PAD

ROWS_INPROJ = 512
ROWS_BRANCHES = 256


def _sigmoid(x):
    return 0.5 + 0.5 * jnp.tanh(0.5 * x)


def _silu(x):
    half = 0.5 * x
    return half + half * jnp.tanh(half)


def _softplus(x):
    return jnp.maximum(x, 0.0) + jnp.log(1.0 + jnp.exp(-jnp.abs(x)))


def _dot(a, b):
    return jnp.dot(a, b, preferred_element_type=F32)


def _dot_nt(a, b):
    return lax.dot_general(a, b, (((1,), (1,)), ((), ())), preferred_element_type=F32)


def _resident(shape):
    nd = len(shape)
    return pl.BlockSpec(shape, lambda *_: (0,) * nd, pipeline_mode=pl.Buffered(1))


def _param_table(rows):
    width = max(sum(v.shape[0] for _, v in r) for r in rows)
    layout, lines = {}, []
    for i, r in enumerate(rows):
        col, parts = 0, []
        for name, v in r:
            assert col % LANES == 0 and v.shape[0] % LANES == 0
            layout[name] = (i, col, v.shape[0])
            parts.append(v.astype(F32))
            col += v.shape[0]
        if col < width:
            parts.append(jnp.zeros((width - col,), F32))
        lines.append(jnp.concatenate(parts))
    return jnp.stack(lines), layout


def _param(p_ref, layout, name, lanes=None):
    r, c, n = layout[name]
    lanes = slice(0, n) if lanes is None else lanes
    return p_ref[r:r + 1, c + lanes.start:c + lanes.stop]


def _mod_kernel(c_ref, w_ref, b_ref, o_ref):
    s = _silu(c_ref[...])
    o_ref[...] = _dot(s.astype(BF16), w_ref[...].astype(BF16)) + b_ref[...]


def _bucket_table():
    qi = np.arange(BLOCK)[:, None]
    kj = np.arange(2 * BLOCK)[None, :]
    dist = qi + BLOCK - kj
    n = np.maximum(dist, 0)
    max_exact = REL_BUCKETS // 2
    nf = np.maximum(n, 1).astype(np.float32)
    large = max_exact + (np.log(nf / max_exact) / math.log(REL_MAX_DIST / max_exact)
                         * (REL_BUCKETS - max_exact)).astype(np.int32)
    large = np.minimum(large, REL_BUCKETS - 1)
    bucket = np.where(n < max_exact, n, large)
    valid = (dist >= 0) & (dist < BLOCK)
    return np.where(valid, bucket, -1).astype(np.int32)


def _bias_kernel(rb_ref, idx_ref, o_ref):
    kv = pl.program_id(0)
    idx = idx_ref[...]
    key_row = lax.broadcasted_iota(jnp.int32, idx.shape, 0)
    keep = idx >= 0
    keep_first = jnp.logical_and(keep, key_row >= BLOCK)
    for s in range(2):
        for c in range(2):
            acc = jnp.zeros(idx.shape, F32)
            for b in range(REL_BUCKETS):
                acc = jnp.where(idx == b, rb_ref[b, kv * GROUP + 2 * c + s], acc)
            acc = acc * LOG2E
            rows, cols = slice(s * 2 * BLOCK, (s + 1) * 2 * BLOCK), slice(c * BLOCK, (c + 1) * BLOCK)
            o_ref[0, 0, rows, cols] = jnp.where(keep, acc, NEG)
            o_ref[1, 0, rows, cols] = jnp.where(keep_first, acc, NEG)


def _prep_kernel(rb_ref, idx_ref, c_ref, wada_ref, bada_ref, wt_ref, w_o, mod_o, bias_o):
    w_o[...] = wt_ref[...].T.astype(BF16)

    @pl.when(pl.program_id(0) < ATTN_KV_HEADS)
    def _():
        _mod_kernel(c_ref, wada_ref, bada_ref, mod_o)
        _bias_kernel(rb_ref, idx_ref, bias_o)


def _prep_call(c8, w_ada, b_ada, rel_bias, w_t, dt_start, dt_width):
    n, k = w_t.shape
    tn = 512
    head_tiles, tail_tiles = dt_start // tn, (n - dt_start - dt_width) // tn
    assert head_tiles * tn == dt_start and tail_tiles * tn == n - dt_start - dt_width
    tail_start = dt_start + dt_width
    assert tail_start % 8 == 0

    def rows(i):
        start = jnp.where(i < head_tiles, i * tn, tail_start + (i - head_tiles) * tn)
        return (pl.multiple_of(start, 8), 0)

    d, n_mod = w_ada.shape
    n_side = ATTN_KV_HEADS
    tm = n_mod // n_side
    assert tm * n_side == n_mod and tm % LANES == 0 and head_tiles + tail_tiles >= n_side
    side = lambda i: jnp.minimum(i, n_side - 1)
    idx_t = jnp.asarray(np.ascontiguousarray(_bucket_table().T))
    w_all, mod, table = pl.pallas_call(
        _prep_kernel,
        grid=(head_tiles + tail_tiles,),
        in_specs=[pl.BlockSpec(memory_space=pltpu.SMEM),
                  pl.BlockSpec((2 * BLOCK, BLOCK), lambda i: (0, 0)),
                  pl.BlockSpec((8, d), lambda i: (0, 0)),
                  pl.BlockSpec((d, tm), lambda i: (0, side(i))),
                  pl.BlockSpec((1, tm), lambda i: (0, side(i))),
                  pl.BlockSpec((pl.Element(tn), pl.Element(k)), rows)],
        out_specs=[pl.BlockSpec((None, k, tn), lambda i: (i, 0, 0)),
                   pl.BlockSpec((8, tm), lambda i: (0, side(i))),
                   pl.BlockSpec((2, 1, 4 * BLOCK, 2 * BLOCK), lambda i: (0, side(i), 0, 0))],
        out_shape=[jax.ShapeDtypeStruct((head_tiles + tail_tiles, k, tn), BF16),
                   jax.ShapeDtypeStruct((8, n_mod), F32),
                   jax.ShapeDtypeStruct((2, ATTN_KV_HEADS, 4 * BLOCK, 2 * BLOCK), F32)],
        compiler_params=pltpu.CompilerParams(dimension_semantics=("arbitrary",)),
        name="prep",
    )(rel_bias, idx_t, c8, w_ada, b_ada, w_t)
    return w_all, mod, table.reshape(2 * ATTN_KV_HEADS, 4 * BLOCK, 2 * BLOCK)


def _inproj_kernel(x_ref, shift_ref, scale_ref, p_ref, w_ref, wdt_ref,
                   qkv_o, za_o, zm_o, xbc_o, g_o, dt_o, xraw_s, carry_s, *, layout):
    ts = x_ref.shape[0]
    halo = carry_s.shape[1]

    @pl.when(pl.program_id(1) == 0)
    def _():
        carry_s[...] = jnp.zeros(carry_s.shape, F32)

    x = x_ref[...]
    ms = jnp.mean(x * x, axis=-1, keepdims=True)
    gain = _param(p_ref, layout, "norm_w") * (1.0 + scale_ref[0])
    h = (x * lax.rsqrt(ms + EPS) * gain + shift_ref[0]).astype(BF16)

    def conv_silu_store(y, c0, slot):
        for jj in range(y.shape[1] // LANES):
            j = c0 // LANES + jj
            jl = slice(j * LANES, (j + 1) * LANES)
            yj = y[:, jj * LANES:(jj + 1) * LANES]
            xraw_s[slot, jj, 0:halo, :] = carry_s[j]
            xraw_s[slot, jj, halo:halo + ts, :] = yj
            carry_s[j] = yj[ts - halo:ts, :]
            acc = _param(p_ref, layout, "conv_b", jl)
            for w in range(CONV_WIDTH):
                off = halo - (CONV_WIDTH - 1) + w
                acc = acc + _param(p_ref, layout, f"conv_w{w}", jl) * xraw_s[slot, jj, off:off + ts, :]
            xbc_o[:, jl] = _silu(acc).astype(BF16)

    step = xraw_s.shape[1] * LANES
    assert w_ref.shape[2] == step
    light, heavy, col = [], [], 0
    for o_ref, act in ((qkv_o, None), (za_o, _silu), (zm_o, _silu), (xbc_o, "conv"), (g_o, _sigmoid)):
        for c0 in range(0, o_ref.shape[1], step):
            (heavy if act == "conv" else light).append((o_ref, act, col + c0, c0))
        col += o_ref.shape[1]
    gates = [p for p in light if p[1] is _sigmoid]
    others = [p for p in light if p[1] is not _sigmoid]
    order = []
    for piece in heavy:
        order += [piece, others.pop(0), gates.pop(0) if gates else others.pop(0)]
    order += others + gates
    slot = 0
    for k, (o_ref, act, wc, c0) in enumerate(order):
        w_piece = w_ref[wc // step]
        if k == 0:
            y = jnp.concatenate([_dot(h[:ts // 2], w_piece), _dot(h[ts // 2:], w_piece)], axis=0)
        else:
            y = _dot(h, w_piece)
        if act == "conv":
            conv_silu_store(y, c0, slot)
            slot = 1 - slot
        else:
            o_ref[:, c0:c0 + step] = (y if act is None else act(y)).astype(o_ref.dtype)
    dt_o[...] = _softplus(_dot_nt(h, wdt_ref[...]) + _param(p_ref, layout, "dt_bias"))


def _inproj_call(x2, shift, scale, params, layout, w_all, wdt, widths, batch, seq):
    t, d = x2.shape
    ts = min(ROWS_INPROJ, seq)
    ns = seq // ts
    row = lambda b, i: (b * ns + i, 0)
    per_b = pl.BlockSpec((1, 1, d), lambda b, i: (b, 0, 0))
    widths = list(widths) + [wdt.shape[0]]
    dtypes = [BF16] * (len(widths) - 1) + [F32]
    piece_blocks = 4
    return pl.pallas_call(
        functools.partial(_inproj_kernel, layout=layout),
        grid=(batch, ns),
        in_specs=[pl.BlockSpec((ts, d), row), per_b, per_b, _resident(params.shape),
                  _resident(w_all.shape), _resident(wdt.shape)],
        out_specs=[pl.BlockSpec((ts, n), row) for n in widths],
        out_shape=[jax.ShapeDtypeStruct((t, n), dt) for n, dt in zip(widths, dtypes)],
        scratch_shapes=[pltpu.VMEM((2, piece_blocks, ts + 8, LANES), F32),
                        pltpu.VMEM((widths[3] // LANES, 8, LANES), F32)],
        compiler_params=pltpu.CompilerParams(
            dimension_semantics=("arbitrary", "arbitrary"), vmem_limit_bytes=VMEM_LIMIT),
        name="inproj",
    )(x2, shift, scale, params, w_all, wdt)


def _segment_rms(x, bd, w):
    ms = _dot((x * x).astype(BF16), bd)
    return x * lax.rsqrt(ms + EPS) * w


def _branches_kernel(*refs, layout):
    i, last = pl.program_id(1), pl.num_programs(1) - 1
    step = functools.partial(_branches_step, *refs, layout=layout)
    pl.when(i == 0)(functools.partial(step, compute=True, finishing=False, first=True))
    pl.when(jnp.logical_and(i > 0, i < last))(functools.partial(step, compute=True, finishing=True, first=False))
    pl.when(i == last)(functools.partial(step, compute=False, finishing=True, first=False))


def _branches_step(q_ref, kvc_ref, kvp_ref, za_ref, ga_ref, bias_ref, p_ref, bd_ref, wap_ref,
                   xbc_ref, dt_ref, zm_ref, tri_ref,
                   gb_ref, x_ref, gate_ref, wsp_ref, wout_ref,
                   o_ref,
                   qn_s, klo_s, khi_s, vt_s, s_s, p_s, yt_s, y_s, ya_s, h_s, yn_s, yprev_s, merged_s,
                   *, layout, compute, finishing, first):
    rows = q_ref.shape[0]
    nblk = rows // BLOCK
    nchunk = rows // CHUNK
    kvw = ATTN_KV_HEADS * HEAD_DIM
    sw = zm_ref.shape[1]
    gw = sw // SSM_GROUPS
    pairs_per_group = gw // LANES
    n_xs = sw // LANES
    if first:
        h_s[...] = jnp.zeros(h_s.shape, F32)

    d_out = o_ref.shape[1]
    piece_w = 2 * LANES

    def merge_piece(c0):
        cs = slice(c0, c0 + piece_w)
        yb = _dot(yprev_s[...], wsp_ref[:, cs])
        merged_s[:, cs] = ya_s[:, cs] + gb_ref[:, cs] * yb.astype(BF16)

    def out_piece(c0):
        cs = slice(c0, c0 + piece_w)
        o = _dot(merged_s[...], wout_ref[:, cs])
        o_ref[:, cs] = x_ref[:, cs] + gate_ref[0][:, cs] * o

    finish = []
    if finishing:
        finish = ([functools.partial(merge_piece, c0) for c0 in range(0, d_out, piece_w)]
                  + [functools.partial(out_piece, c0) for c0 in range(0, d_out, piece_w)])
    if not compute:
        for piece in finish:
            piece()
        return

    bd = bd_ref[...]
    qw = _param(p_ref, layout, "q_norm_w") * (HEAD_DIM ** -0.5 * LOG2E)
    kw = _param(p_ref, layout, "k_norm_w")
    low_lanes = lax.broadcasted_iota(jnp.int32, (1, LANES), 1) < HEAD_DIM

    for j in range(q_ref.shape[1] // LANES):
        sl = slice(j * LANES, (j + 1) * LANES)
        qn_s[:, sl] = _segment_rms(q_ref[:, sl].astype(F32), bd, qw).astype(BF16)
    for j in range(kvw // LANES):
        sl = slice(j * LANES, (j + 1) * LANES)
        even = slice(2 * j * LANES, (2 * j + 1) * LANES)
        odd = slice((2 * j + 1) * LANES, (2 * j + 2) * LANES)
        for dst, src in ((slice(0, BLOCK), kvp_ref), (slice(BLOCK, BLOCK + rows), kvc_ref)):
            kn = _segment_rms(src[:, sl].astype(F32), bd, kw)
            lo = jnp.where(low_lanes, kn, 0.0)
            hi = jnp.where(low_lanes, 0.0, kn)
            klo_s[dst, even] = lo.astype(BF16)
            khi_s[dst, odd] = hi.astype(BF16)
            khi_s[dst, even] = pltpu.roll(lo, HEAD_DIM, axis=1).astype(BF16)
            klo_s[dst, odd] = pltpu.roll(hi, HEAD_DIM, axis=1).astype(BF16)
    for dst, src in ((slice(0, BLOCK), kvp_ref), (slice(BLOCK, BLOCK + rows), kvc_ref)):
        v_t = src[:, kvw:2 * kvw].astype(F32).T.astype(BF16)
        for h in range(ATTN_KV_HEADS):
            vt_s[h * V_ROWS:h * V_ROWS + HEAD_DIM, dst] = v_t[h * HEAD_DIM:(h + 1) * HEAD_DIM, :]
    if first:
        for h in range(ATTN_KV_HEADS):
            vt_s[h * V_ROWS + HEAD_DIM:(h + 1) * V_ROWS, :] = jnp.ones((V_PAD, vt_s.shape[1]), BF16)

    units = [(qb, h) for qb in range(nblk) for h in range(ATTN_KV_HEADS)]

    def scores(u):
        qb, h = units[u]
        r0 = qb * BLOCK
        hl = slice(h * LANES, (h + 1) * LANES)
        k2 = jnp.concatenate([klo_s[r0:r0 + 2 * BLOCK, hl], khi_s[r0:r0 + 2 * BLOCK, hl]], axis=0)
        q2 = jnp.concatenate([qn_s[r0:r0 + BLOCK, (2 * h + c) * LANES:(2 * h + c + 1) * LANES]
                              for c in range(2)], axis=0)
        entry = ATTN_KV_HEADS + h if first and qb == 0 else h
        s_s[u % 2] = _dot_nt(k2, q2) + bias_ref[entry]

    def softmax_pv(u):
        qb, h = units[u]
        r0 = qb * BLOCK
        slot = u % 2
        sink_terms = []
        for s in range(2):
            for c in range(2):
                blk = s_s[slot, s * 2 * BLOCK:(s + 1) * 2 * BLOCK, c * BLOCK:(c + 1) * BLOCK]
                sc = ((h * 2 + s) * 2 + c) * BLOCK
                sink = _param(p_ref, layout, "sinks", slice(sc, sc + BLOCK)) * LOG2E
                m = jnp.maximum(jnp.max(blk, axis=0, keepdims=True), sink)
                e = jnp.exp2(blk - m)
                p_s[slot, :, (2 * s + c) * BLOCK:(2 * s + c + 1) * BLOCK] = e.astype(BF16)
                sink_terms.append(jnp.exp2(sink - m))
        o_t = _dot(vt_s[h * V_ROWS:(h + 1) * V_ROWS, r0:r0 + 2 * BLOCK], p_s[slot])
        for s in range(2):
            for c in range(2):
                head = GROUP * h + 2 * c + s
                k = 2 * s + c
                cols = slice(k * BLOCK, (k + 1) * BLOCK)
                den = o_t[HEAD_DIM:HEAD_DIM + 1, cols] + sink_terms[k]
                yt_s[head * HEAD_DIM:(head + 1) * HEAD_DIM, r0:r0 + BLOCK] = o_t[0:HEAD_DIM, cols] * (1.0 / den)

    scores(0)
    attention = []
    for u in range(len(units)):
        def unit(u=u):
            if u + 1 < len(units):
                scores(u + 1)
            softmax_pv(u)
        attention.append(unit)

    def gate_y():
        y_s[...] = (yt_s[...].T * za_ref[...].astype(F32)).astype(BF16)

    def proj_piece(c0):
        cs = slice(c0, c0 + piece_w)
        ya = _dot(y_s[...], wap_ref[:, cs])
        ya_s[:, cs] = (ya * ga_ref[:, cs].astype(F32)).astype(BF16)

    first_proj = len(attention) + 1
    attention.append(gate_y)
    attention += [functools.partial(proj_piece, c0) for c0 in range(0, d_out, piece_w)]
    n_groups = nchunk * SSM_GROUPS
    base, extra = divmod(len(attention), n_groups)
    attn_counts = [base + (k < extra) for k in range(n_groups)]
    first_proj_group = next(k for k in range(n_groups) if sum(attn_counts[:k + 1]) > first_proj)
    assert len(finish) <= n_groups and first_proj_group >= len(finish) // 2 - 1

    def lane_block(j, r0):
        return xbc_ref[r0:r0 + CHUNK, j * LANES:(j + 1) * LANES]

    a_row = -jnp.exp(_param(p_ref, layout, "a_log"))
    tri = tri_ref[...]
    li = lax.broadcasted_iota(jnp.int32, (CHUNK, CHUNK), 0)
    si = lax.broadcasted_iota(jnp.int32, (CHUNK, CHUNK), 1)
    causal = li >= si
    low_half = si < SSM_HEAD_DIM
    lo_mask = (lax.broadcasted_iota(jnp.int32, (1, LANES), 1) < SSM_HEAD_DIM).astype(BF16)
    hi_mask = 1 - lo_mask

    for c in range(nchunk):
        r0 = c * CHUNK
        dt = dt_ref[r0:r0 + CHUNK, :]
        dta = dt * a_row
        dta_hi = dta.astype(BF16)
        dta_lo = (dta - dta_hi.astype(F32)).astype(BF16)
        a2 = (_dot(tri, dta_hi) + _dot(tri, dta_lo)) * LOG2E
        a2_t = a2.T
        dt_t = dt.T
        row2_t = a2_t - jnp.log2(dt_t)
        a_end2_t = a2_t[:, CHUNK - 1:CHUNK]
        w_t = dt_t * jnp.exp2(a_end2_t - a2_t)
        e_end_t = jnp.exp2(a_end2_t)

        for g in range(SSM_GROUPS):
            if finish:
                finish.pop(0)()
            n_items = attn_counts[c * SSM_GROUPS + g]
            bm_gb = lane_block(n_xs + g, r0)
            cm_gb = lane_block(n_xs + SSM_GROUPS + g, r0)
            cb = _dot_nt(cm_gb, bm_gb)
            bm_gt = bm_gb.astype(F32).T
            gsl = slice(g * gw, (g + 1) * gw)
            hprev = h_s[g]
            y_off = _dot(cm_gb, hprev.astype(BF16))
            y_parts = []
            for p in range(pairs_per_group):
                if (n_items == 1 and p == pairs_per_group // 2) or (n_items >= 2 and p in (1, pairs_per_group - 1)):
                    for _ in range(1 if n_items <= 2 else (n_items + 1) // 2):
                        if attention:
                            attention.pop(0)()
                pair = g * pairs_per_group + p
                lsl = slice(pair * LANES, (pair + 1) * LANES)
                xs_b = lane_block(pair, r0)
                xs_pair = xs_b.astype(F32)
                rhs = jnp.concatenate([xs_b * lo_mask, xs_b * hi_mask], axis=0)
                m_parts, b_parts, cols, e_ends = [], [], [], []
                for side in range(2):
                    hh = 2 * pair + side
                    col2 = jnp.broadcast_to(a2[:, hh:hh + 1], (CHUNK, CHUNK))
                    dec_dt = jnp.exp2(jnp.where(causal, col2 - row2_t[hh:hh + 1, :], NEG))
                    m_parts.append(cb * dec_dt)
                    b_parts.append(bm_gt * w_t[hh:hh + 1, :])
                    cols.append(col2)
                    e_ends.append(e_end_t[hh:hh + 1, :])
                lhs = jnp.concatenate(
                    [jnp.concatenate(m_parts, axis=1), jnp.concatenate(b_parts, axis=1)],
                    axis=0).astype(BF16)
                res = _dot(lhs, rhs)
                e_col = jnp.exp2(jnp.where(low_half, cols[0], cols[1]))
                y_pair = res[0:CHUNK] + e_col * y_off[:, p * LANES:(p + 1) * LANES]
                y_pair = y_pair + _param(p_ref, layout, "d_skip", lsl) * xs_pair
                y_parts.append(y_pair * zm_ref[r0:r0 + CHUNK, lsl].astype(F32))
                e_end = jnp.where(low_half, e_ends[0], e_ends[1])
                h_s[g, :, p * LANES:(p + 1) * LANES] = (
                    hprev[:, p * LANES:(p + 1) * LANES] * e_end + res[CHUNK:])
            yg = jnp.concatenate(y_parts, axis=1)
            ms = jnp.mean(yg * yg, axis=-1, keepdims=True)
            nw = _param(p_ref, layout, "ssm_norm_w", gsl)
            yn_s[r0:r0 + CHUNK, gsl] = (yg * lax.rsqrt(ms + EPS) * nw).astype(BF16)

    assert not finish and not attention
    yprev_s[...] = yn_s[...]


def _branches_call(qkv, za, g, bias, params, layout, bd, wap,
                   xbc, dt, zm, tri, x2, gate, wsp, wout, batch, seq):
    t, d = x2.shape
    aw = ATTN_HEADS * HEAD_DIM
    kvw2 = 2 * ATTN_KV_HEADS * HEAD_DIM
    sw = zm.shape[1]
    rows = min(ROWS_BRANCHES, seq)
    nt = seq // rows
    bpt = rows // BLOCK
    tile = lambda b, i: b * nt + jnp.minimum(i, nt - 1)
    scan = lambda b, i: (tile(b, i), 0)
    done = lambda b, i: (b * nt + jnp.maximum(i - 1, 0), 0)
    return pl.pallas_call(
        functools.partial(_branches_kernel, layout=layout),
        grid=(batch, nt + 1),
        in_specs=[pl.BlockSpec((rows, aw), scan),
                  pl.BlockSpec((rows, kvw2), lambda b, i: (tile(b, i), aw // kvw2)),
                  pl.BlockSpec((BLOCK, kvw2),
                               lambda b, i: (jnp.maximum(tile(b, i) * bpt - 1, 0), aw // kvw2)),
                  pl.BlockSpec((rows, aw), scan),
                  pl.BlockSpec((rows, aw), scan),
                  _resident(bias.shape), _resident(params.shape),
                  _resident(bd.shape), _resident(wap.shape),
                  pl.BlockSpec((rows, xbc.shape[1]), scan),
                  pl.BlockSpec((rows, dt.shape[1]), scan),
                  pl.BlockSpec((rows, sw), scan),
                  _resident(tri.shape),
                  pl.BlockSpec((rows, d), lambda b, i: (b * nt + jnp.maximum(i - 1, 0), 1)),
                  pl.BlockSpec((rows, d), done),
                  pl.BlockSpec((1, 1, d), lambda b, i: (b, 0, 0)),
                  _resident(wsp.shape), _resident(wout.shape)],
        out_specs=pl.BlockSpec((rows, d), done),
        out_shape=jax.ShapeDtypeStruct((t, d), F32),
        scratch_shapes=[pltpu.VMEM((rows, aw), BF16),
                        pltpu.VMEM((rows + BLOCK, ATTN_KV_HEADS * LANES), BF16),
                        pltpu.VMEM((rows + BLOCK, ATTN_KV_HEADS * LANES), BF16),
                        pltpu.VMEM((ATTN_KV_HEADS * V_ROWS, rows + BLOCK), BF16),
                        pltpu.VMEM((2, 4 * BLOCK, 2 * BLOCK), F32),
                        pltpu.VMEM((2, 2 * BLOCK, 4 * BLOCK), BF16),
                        pltpu.VMEM((aw, rows), F32),
                        pltpu.VMEM((rows, aw), BF16),
                        pltpu.VMEM((rows, d), BF16),
                        pltpu.VMEM((SSM_GROUPS, SSM_STATE, sw // SSM_GROUPS), F32),
                        pltpu.VMEM((rows, sw), BF16),
                        pltpu.VMEM((rows, sw), BF16),
                        pltpu.VMEM((rows, d), BF16)],
        compiler_params=pltpu.CompilerParams(
            dimension_semantics=("arbitrary", "arbitrary"), vmem_limit_bytes=VMEM_LIMIT),
        name="branches",
    )(qkv, qkv, qkv, za, g, bias, params, bd, wap,
      xbc, dt, zm, tri, g, x2, gate, wsp, wout)


def _layer(x, c, w_ada, b_ada, norm_w, w_in, q_norm_w, k_norm_w, rel_bias, sinks,
           conv_w, conv_b, dt_bias, a_log, d_skip, ssm_norm_w, w_attn_proj, w_ssm_proj, w_out):
    batch, seq, d = x.shape
    aw = ATTN_HEADS * HEAD_DIM
    kvw = ATTN_KV_HEADS * HEAD_DIM
    sw = w_ssm_proj.shape[0]
    ssm_heads = sw // SSM_HEAD_DIM
    xbc_w = sw + 2 * SSM_GROUPS * SSM_STATE
    assert seq % BLOCK == 0 and seq % CHUNK == 0 and ssm_heads <= LANES

    o = np.cumsum([0, aw, kvw, kvw, aw, sw, xbc_w, ssm_heads, d, d])
    w_t = w_in.T
    c8 = jnp.zeros((8, d), F32).at[:batch].set(c)
    w_all, mod, bias = _prep_call(c8, w_ada, b_ada.reshape(1, -1), rel_bias.astype(F32), w_t, int(o[6]), ssm_heads)
    shift, scale, gate = (mod[:batch, k * d:(k + 1) * d].reshape(batch, 1, d) for k in range(3))
    wdt = jnp.pad(w_t[o[6]:o[7]], ((0, LANES - ssm_heads), (0, 0))).astype(BF16)
    widths = [aw + 2 * kvw, aw, sw, xbc_w, 2 * d]

    pad_heads = lambda v: jnp.pad(v.astype(F32), (0, LANES - ssm_heads))
    pair = lambda v: jnp.tile(v.astype(F32), LANES // HEAD_DIM)
    sink_lanes = jnp.repeat(sinks.astype(F32).reshape(ATTN_KV_HEADS, 2, 2).transpose(0, 2, 1).reshape(-1), BLOCK)
    params, layout = _param_table(
        [[(f"conv_w{w}", conv_w[w])] for w in range(CONV_WIDTH)]
        + [[("conv_b", conv_b)],
           [("d_skip", jnp.repeat(d_skip.astype(F32), SSM_HEAD_DIM)), ("norm_w", norm_w)],
           [("ssm_norm_w", ssm_norm_w), ("q_norm_w", pair(q_norm_w)), ("k_norm_w", pair(k_norm_w)),
            ("dt_bias", pad_heads(dt_bias)), ("a_log", pad_heads(a_log))],
           [("sinks", sink_lanes)]])

    x2 = x.reshape(batch * seq, d)
    qkv, za, zm, xbc, g, dt = _inproj_call(x2, shift, scale, params, layout, w_all, wdt, widths, batch, seq)

    seg = np.arange(LANES) // HEAD_DIM
    bd = jnp.asarray((seg[:, None] == seg[None, :]).astype(np.float32) / HEAD_DIM, dtype=BF16)
    tri = jnp.asarray(np.tril(np.ones((CHUNK, CHUNK), np.float32)), dtype=BF16)
    out = _branches_call(qkv, za, g, bias, params, layout, bd, w_attn_proj.astype(BF16),
                         xbc, dt, zm, tri,
                         x2, gate, w_ssm_proj.astype(BF16), w_out.astype(BF16), batch, seq)
    return out.reshape(batch, seq, d)


def kernel(x, c, w_ada, b_ada, norm_w, w_in, q_norm_w, k_norm_w, rel_bias, sinks, conv_w, conv_b,
           dt_bias, a_log, d_skip, ssm_norm_w, w_attn_proj, w_ssm_proj, w_out):
    depth = w_in.shape[0]
    for i in range(depth):
        x = _layer(x, c, w_ada[i], b_ada[i], norm_w[i], w_in[i], q_norm_w[i], k_norm_w[i],
                   rel_bias, sinks[i], conv_w[i], conv_b[i], dt_bias[i], a_log[i], d_skip[i],
                   ssm_norm_w[i], w_attn_proj[i], w_ssm_proj[i], w_out[i])
    return x
```

```python
import functools
import math

import numpy as np
import jax
import jax.numpy as jnp
from jax import lax
from jax.experimental import pallas as pl
from jax.experimental.pallas import tpu as pltpu

F32 = jnp.float32
BF16 = jnp.bfloat16

HEAD_DIM = 64
ATTN_HEADS = 16
ATTN_KV_HEADS = 4
GROUP = ATTN_HEADS // ATTN_KV_HEADS
BLOCK = 128
REL_BUCKETS = 32
REL_MAX_DIST = 128
SSM_HEAD_DIM = 64
SSM_GROUPS = 4
SSM_STATE = 128
CONV_WIDTH = 4
CHUNK = 128
EPS = 1e-6
NEG = -1e30
LOG2E = 1.4426950408889634
LANES = 128
VMEM_LIMIT = 56 * 1024 * 1024

V_PAD = 16
V_ROWS = HEAD_DIM + V_PAD

ROWS_INPROJ = 512
ROWS_BRANCHES = 256


def _sigmoid(x):
    return 0.5 + 0.5 * jnp.tanh(0.5 * x)


def _silu(x):
    half = 0.5 * x
    return half + half * jnp.tanh(half)


def _softplus(x):
    return jnp.maximum(x, 0.0) + jnp.log(1.0 + jnp.exp(-jnp.abs(x)))


def _dot(a, b):
    return jnp.dot(a, b, preferred_element_type=F32)


def _dot_nt(a, b):
    return lax.dot_general(a, b, (((1,), (1,)), ((), ())), preferred_element_type=F32)


def _resident(shape):
    nd = len(shape)
    return pl.BlockSpec(shape, lambda *_: (0,) * nd, pipeline_mode=pl.Buffered(1))


def _param_table(rows):
    width = max(sum(v.shape[0] for _, v in r) for r in rows)
    layout, lines = {}, []
    for i, r in enumerate(rows):
        col, parts = 0, []
        for name, v in r:
            assert col % LANES == 0 and v.shape[0] % LANES == 0
            layout[name] = (i, col, v.shape[0])
            parts.append(v.astype(F32))
            col += v.shape[0]
        if col < width:
            parts.append(jnp.zeros((width - col,), F32))
        lines.append(jnp.concatenate(parts))
    return jnp.stack(lines), layout


def _param(p_ref, layout, name, lanes=None):
    r, c, n = layout[name]
    lanes = slice(0, n) if lanes is None else lanes
    return p_ref[r:r + 1, c + lanes.start:c + lanes.stop]


def _mod_kernel(c_ref, w_ref, b_ref, o_ref):
    s = _silu(c_ref[...])
    o_ref[...] = _dot(s.astype(BF16), w_ref[...].astype(BF16)) + b_ref[...]


def _bucket_table():
    qi = np.arange(BLOCK)[:, None]
    kj = np.arange(2 * BLOCK)[None, :]
    dist = qi + BLOCK - kj
    n = np.maximum(dist, 0)
    max_exact = REL_BUCKETS // 2
    nf = np.maximum(n, 1).astype(np.float32)
    large = max_exact + (np.log(nf / max_exact) / math.log(REL_MAX_DIST / max_exact)
                         * (REL_BUCKETS - max_exact)).astype(np.int32)
    large = np.minimum(large, REL_BUCKETS - 1)
    bucket = np.where(n < max_exact, n, large)
    valid = (dist >= 0) & (dist < BLOCK)
    return np.where(valid, bucket, -1).astype(np.int32)


def _bias_kernel(rb_ref, idx_ref, o_ref):
    kv = pl.program_id(0)
    idx = idx_ref[...]
    key_row = lax.broadcasted_iota(jnp.int32, idx.shape, 0)
    keep = idx >= 0
    keep_first = jnp.logical_and(keep, key_row >= BLOCK)
    for s in range(2):
        for c in range(2):
            acc = jnp.zeros(idx.shape, F32)
            for b in range(REL_BUCKETS):
                acc = jnp.where(idx == b, rb_ref[b, kv * GROUP + 2 * c + s], acc)
            acc = acc * LOG2E
            rows, cols = slice(s * 2 * BLOCK, (s + 1) * 2 * BLOCK), slice(c * BLOCK, (c + 1) * BLOCK)
            o_ref[0, 0, rows, cols] = jnp.where(keep, acc, NEG)
            o_ref[1, 0, rows, cols] = jnp.where(keep_first, acc, NEG)


def _prep_kernel(rb_ref, idx_ref, c_ref, wada_ref, bada_ref, wt_ref, w_o, mod_o, bias_o):
    w_o[...] = wt_ref[...].T.astype(BF16)

    @pl.when(pl.program_id(0) < ATTN_KV_HEADS)
    def _():
        _mod_kernel(c_ref, wada_ref, bada_ref, mod_o)
        _bias_kernel(rb_ref, idx_ref, bias_o)


def _prep_call(c8, w_ada, b_ada, rel_bias, w_t, dt_start, dt_width):
    n, k = w_t.shape
    tn = 512
    head_tiles, tail_tiles = dt_start // tn, (n - dt_start - dt_width) // tn
    assert head_tiles * tn == dt_start and tail_tiles * tn == n - dt_start - dt_width
    tail_start = dt_start + dt_width
    assert tail_start % 8 == 0

    def rows(i):
        start = jnp.where(i < head_tiles, i * tn, tail_start + (i - head_tiles) * tn)
        return (pl.multiple_of(start, 8), 0)

    d, n_mod = w_ada.shape
    n_side = ATTN_KV_HEADS
    tm = n_mod // n_side
    assert tm * n_side == n_mod and tm % LANES == 0 and head_tiles + tail_tiles >= n_side
    side = lambda i: jnp.minimum(i, n_side - 1)
    idx_t = jnp.asarray(np.ascontiguousarray(_bucket_table().T))
    w_all, mod, table = pl.pallas_call(
        _prep_kernel,
        grid=(head_tiles + tail_tiles,),
        in_specs=[pl.BlockSpec(memory_space=pltpu.SMEM),
                  pl.BlockSpec((2 * BLOCK, BLOCK), lambda i: (0, 0)),
                  pl.BlockSpec((8, d), lambda i: (0, 0)),
                  pl.BlockSpec((d, tm), lambda i: (0, side(i))),
                  pl.BlockSpec((1, tm), lambda i: (0, side(i))),
                  pl.BlockSpec((pl.Element(tn), pl.Element(k)), rows)],
        out_specs=[pl.BlockSpec((k, tn), lambda i: (0, i)),
                   pl.BlockSpec((8, tm), lambda i: (0, side(i))),
                   pl.BlockSpec((2, 1, 4 * BLOCK, 2 * BLOCK), lambda i: (0, side(i), 0, 0))],
        out_shape=[jax.ShapeDtypeStruct((k, n - dt_width), BF16),
                   jax.ShapeDtypeStruct((8, n_mod), F32),
                   jax.ShapeDtypeStruct((2, ATTN_KV_HEADS, 4 * BLOCK, 2 * BLOCK), F32)],
        compiler_params=pltpu.CompilerParams(dimension_semantics=("arbitrary",)),
        name="prep",
    )(rel_bias, idx_t, c8, w_ada, b_ada, w_t)
    return w_all, mod, table.reshape(2 * ATTN_KV_HEADS, 4 * BLOCK, 2 * BLOCK)


def _inproj_kernel(x_ref, xn_ref, shift_ref, scale_ref, p_ref, w_ref, wdt_ref,
                   qkv_o, za_o, zm_o, xbc_o, g_o, dt_o, xraw_s, carry_s, h_s, *, layout):
    ts = x_ref.shape[0]
    halo = carry_s.shape[1]
    slot_h = pl.program_id(1) % 2
    gain = _param(p_ref, layout, "norm_w") * (1.0 + scale_ref[0])

    def normalise(x):
        ms = jnp.mean(x * x, axis=-1, keepdims=True)
        return (x * lax.rsqrt(ms + EPS) * gain + shift_ref[0]).astype(BF16)

    @pl.when(pl.program_id(1) == 0)
    def _():
        carry_s[...] = jnp.zeros(carry_s.shape, F32)
        h_s[0] = normalise(x_ref[...])

    n_ahead = 4
    ahead_rows = ts // n_ahead

    def conv_silu_store(y, c0, slot):
        for jj in range(y.shape[1] // LANES):
            j = c0 // LANES + jj
            jl = slice(j * LANES, (j + 1) * LANES)
            yj = y[:, jj * LANES:(jj + 1) * LANES]
            xraw_s[slot, jj, 0:halo, :] = carry_s[j]
            xraw_s[slot, jj, halo:halo + ts, :] = yj
            carry_s[j] = yj[ts - halo:ts, :]
            acc = _param(p_ref, layout, "conv_b", jl)
            for w in range(CONV_WIDTH):
                off = halo - (CONV_WIDTH - 1) + w
                acc = acc + _param(p_ref, layout, f"conv_w{w}", jl) * xraw_s[slot, jj, off:off + ts, :]
            xbc_o[:, jl] = _silu(acc).astype(BF16)

    step = xraw_s.shape[1] * LANES
    light, heavy, col = [], [], 0
    for o_ref, act in ((qkv_o, None), (za_o, _silu), (zm_o, _silu), (xbc_o, "conv"), (g_o, _sigmoid)):
        for c0 in range(0, o_ref.shape[1], step):
            (heavy if act == "conv" else light).append((o_ref, act, col + c0, c0))
        col += o_ref.shape[1]
    gates = [p for p in light if p[1] is _sigmoid]
    others = [p for p in light if p[1] is not _sigmoid]
    order = []
    for piece in heavy:
        order += [piece, others.pop(0), gates.pop(0) if gates else others.pop(0)]
    order += others + gates
    ahead_at = {}
    for j in range(n_ahead):
        ahead_at.setdefault(min(3 * j + 2, len(order) - 1), []).append(j)
    slot = 0
    for k, (o_ref, act, wc, c0) in enumerate(order):
        y = _dot(h_s[slot_h], w_ref[:, wc:wc + step])
        if act == "conv":
            conv_silu_store(y, c0, slot)
            slot = 1 - slot
        else:
            o_ref[:, c0:c0 + step] = (y if act is None else act(y)).astype(o_ref.dtype)
        for j in ahead_at.get(k, ()):
            rs = slice(j * ahead_rows, (j + 1) * ahead_rows)
            h_s[1 - slot_h, rs, :] = normalise(xn_ref[rs, :])
    dt_o[...] = _softplus(_dot_nt(h_s[slot_h], wdt_ref[...]) + _param(p_ref, layout, "dt_bias"))


def _inproj_call(x2, shift, scale, params, layout, w_all, wdt, widths, batch, seq):
    t, d = x2.shape
    ts = min(ROWS_INPROJ, seq)
    ns = seq // ts
    row = lambda b, i: (b * ns + i, 0)
    row_next = lambda b, i: (b * ns + jnp.minimum(i + 1, ns - 1), 0)
    per_b = pl.BlockSpec((1, 1, d), lambda b, i: (b, 0, 0))
    widths = list(widths) + [wdt.shape[0]]
    dtypes = [BF16] * (len(widths) - 1) + [F32]
    piece_blocks = 4
    return pl.pallas_call(
        functools.partial(_inproj_kernel, layout=layout),
        grid=(batch, ns),
        in_specs=[pl.BlockSpec((ts, d), row), pl.BlockSpec((ts, d), row_next), per_b, per_b,
                  _resident(params.shape),
                  _resident(w_all.shape), _resident(wdt.shape)],
        out_specs=[pl.BlockSpec((ts, n), row) for n in widths],
        out_shape=[jax.ShapeDtypeStruct((t, n), dt) for n, dt in zip(widths, dtypes)],
        scratch_shapes=[pltpu.VMEM((2, piece_blocks, ts + 8, LANES), F32),
                        pltpu.VMEM((widths[3] // LANES, 8, LANES), F32),
                        pltpu.VMEM((2, ts, d), BF16)],
        compiler_params=pltpu.CompilerParams(
            dimension_semantics=("arbitrary", "arbitrary"), vmem_limit_bytes=VMEM_LIMIT),
        name="inproj",
    )(x2, x2, shift, scale, params, w_all, wdt)


def _segment_rms(x, bd, w):
    ms = _dot((x * x).astype(BF16), bd)
    return x * lax.rsqrt(ms + EPS) * w


def _branches_kernel(*refs, layout):
    i, last = pl.program_id(1), pl.num_programs(1) - 1
    step = functools.partial(_branches_step, *refs, layout=layout)
    pl.when(i == 0)(functools.partial(step, compute=True, finishing=False, first=True))
    pl.when(jnp.logical_and(i > 0, i < last))(functools.partial(step, compute=True, finishing=True, first=False))
    pl.when(i == last)(functools.partial(step, compute=False, finishing=True, first=False))


def _branches_step(q_ref, kvc_ref, kvp_ref, za_ref, ga_ref, bias_ref, p_ref, bd_ref, wap_ref,
                   xbc_ref, dt_ref, zm_ref, tri_ref,
                   gb_ref, x_ref, gate_ref, wsp_ref, wout_ref,
                   o_ref,
                   qn_s, klo_s, khi_s, vt_s, s_s, p_s, yt_s, y_s, ya_s, h_s, yn_s, yprev_s, merged_s,
                   *, layout, compute, finishing, first):
    rows = q_ref.shape[0]
    nblk = rows // BLOCK
    nchunk = rows // CHUNK
    kvw = ATTN_KV_HEADS * HEAD_DIM
    sw = zm_ref.shape[1]
    gw = sw // SSM_GROUPS
    pairs_per_group = gw // LANES
    n_xs = sw // LANES
    if first:
        h_s[...] = jnp.zeros(h_s.shape, F32)

    d_out = o_ref.shape[1]
    piece_w = 2 * LANES

    def merge_piece(c0):
        cs = slice(c0, c0 + piece_w)
        yb = _dot(yprev_s[...], wsp_ref[:, cs])
        merged_s[:, cs] = ya_s[:, cs] + gb_ref[:, cs] * yb.astype(BF16)

    def out_piece(c0):
        cs = slice(c0, c0 + piece_w)
        o = _dot(merged_s[...], wout_ref[:, cs])
        o_ref[:, cs] = x_ref[:, cs] + gate_ref[0][:, cs] * o

    finish = []
    if finishing:
        finish = ([functools.partial(merge_piece, c0) for c0 in range(0, d_out, piece_w)]
                  + [functools.partial(out_piece, c0) for c0 in range(0, d_out, piece_w)])
    if not compute:
        for piece in finish:
            piece()
        return

    bd = bd_ref[...]
    qw = _param(p_ref, layout, "q_norm_w") * (HEAD_DIM ** -0.5 * LOG2E)
    kw = _param(p_ref, layout, "k_norm_w")
    low_lanes = lax.broadcasted_iota(jnp.int32, (1, LANES), 1) < HEAD_DIM

    for j in range(q_ref.shape[1] // LANES):
        sl = slice(j * LANES, (j + 1) * LANES)
        qn_s[:, sl] = _segment_rms(q_ref[:, sl].astype(F32), bd, qw).astype(BF16)
    for j in range(kvw // LANES):
        sl = slice(j * LANES, (j + 1) * LANES)
        even = slice(2 * j * LANES, (2 * j + 1) * LANES)
        odd = slice((2 * j + 1) * LANES, (2 * j + 2) * LANES)
        for dst, src in ((slice(0, BLOCK), kvp_ref), (slice(BLOCK, BLOCK + rows), kvc_ref)):
            kn = _segment_rms(src[:, sl].astype(F32), bd, kw)
            lo = jnp.where(low_lanes, kn, 0.0)
            hi = jnp.where(low_lanes, 0.0, kn)
            klo_s[dst, even] = lo.astype(BF16)
            khi_s[dst, odd] = hi.astype(BF16)
            khi_s[dst, even] = pltpu.roll(lo, HEAD_DIM, axis=1).astype(BF16)
            klo_s[dst, odd] = pltpu.roll(hi, HEAD_DIM, axis=1).astype(BF16)
    for dst, src in ((slice(0, BLOCK), kvp_ref), (slice(BLOCK, BLOCK + rows), kvc_ref)):
        v_t = src[:, kvw:2 * kvw].astype(F32).T.astype(BF16)
        for h in range(ATTN_KV_HEADS):
            vt_s[h * V_ROWS:h * V_ROWS + HEAD_DIM, dst] = v_t[h * HEAD_DIM:(h + 1) * HEAD_DIM, :]
    if first:
        for h in range(ATTN_KV_HEADS):
            vt_s[h * V_ROWS + HEAD_DIM:(h + 1) * V_ROWS, :] = jnp.ones((V_PAD, vt_s.shape[1]), BF16)

    units = [(qb, h) for qb in range(nblk) for h in range(ATTN_KV_HEADS)]

    def scores(u):
        qb, h = units[u]
        r0 = qb * BLOCK
        hl = slice(h * LANES, (h + 1) * LANES)
        k2 = jnp.concatenate([klo_s[r0:r0 + 2 * BLOCK, hl], khi_s[r0:r0 + 2 * BLOCK, hl]], axis=0)
        q2 = jnp.concatenate([qn_s[r0:r0 + BLOCK, (2 * h + c) * LANES:(2 * h + c + 1) * LANES]
                              for c in range(2)], axis=0)
        entry = ATTN_KV_HEADS + h if first and qb == 0 else h
        s_s[u % 2] = _dot_nt(k2, q2) + bias_ref[entry]

    def softmax_pv(u):
        qb, h = units[u]
        r0 = qb * BLOCK
        slot = u % 2
        sink_terms = []
        for s in range(2):
            for c in range(2):
                blk = s_s[slot, s * 2 * BLOCK:(s + 1) * 2 * BLOCK, c * BLOCK:(c + 1) * BLOCK]
                sc = ((h * 2 + s) * 2 + c) * BLOCK
                sink = _param(p_ref, layout, "sinks", slice(sc, sc + BLOCK)) * LOG2E
                m = jnp.maximum(jnp.max(blk, axis=0, keepdims=True), sink)
                e = jnp.exp2(blk - m)
                p_s[slot, :, (2 * s + c) * BLOCK:(2 * s + c + 1) * BLOCK] = e.astype(BF16)
                sink_terms.append(jnp.exp2(sink - m))
        o_t = _dot(vt_s[h * V_ROWS:(h + 1) * V_ROWS, r0:r0 + 2 * BLOCK], p_s[slot])
        for s in range(2):
            for c in range(2):
                head = GROUP * h + 2 * c + s
                k = 2 * s + c
                cols = slice(k * BLOCK, (k + 1) * BLOCK)
                den = o_t[HEAD_DIM:HEAD_DIM + 1, cols] + sink_terms[k]
                yt_s[head * HEAD_DIM:(head + 1) * HEAD_DIM, r0:r0 + BLOCK] = o_t[0:HEAD_DIM, cols] * (1.0 / den)

    scores(0)
    attention = []
    for u in range(len(units)):
        def unit(u=u):
            if u + 1 < len(units):
                scores(u + 1)
            softmax_pv(u)
        attention.append(unit)

    def gate_y():
        y_s[...] = (yt_s[...].T * za_ref[...].astype(F32)).astype(BF16)

    def proj_piece(c0):
        cs = slice(c0, c0 + piece_w)
        ya = _dot(y_s[...], wap_ref[:, cs])
        ya_s[:, cs] = (ya * ga_ref[:, cs].astype(F32)).astype(BF16)

    first_proj = len(attention) + 1
    attention.append(gate_y)
    attention += [functools.partial(proj_piece, c0) for c0 in range(0, d_out, piece_w)]
    n_groups = nchunk * SSM_GROUPS
    base, extra = divmod(len(attention), n_groups)
    attn_counts = [base + (k < extra) for k in range(n_groups)]
    first_proj_group = next(k for k in range(n_groups) if sum(attn_counts[:k + 1]) > first_proj)
    assert len(finish) <= n_groups and first_proj_group >= len(finish) // 2 - 1

    def lane_block(j, r0):
        return xbc_ref[r0:r0 + CHUNK, j * LANES:(j + 1) * LANES]

    a_row = -jnp.exp(_param(p_ref, layout, "a_log"))
    tri = tri_ref[...]
    li = lax.broadcasted_iota(jnp.int32, (CHUNK, CHUNK), 0)
    si = lax.broadcasted_iota(jnp.int32, (CHUNK, CHUNK), 1)
    causal = li >= si
    low_half = si < SSM_HEAD_DIM
    lo_mask = (lax.broadcasted_iota(jnp.int32, (1, LANES), 1) < SSM_HEAD_DIM).astype(BF16)
    hi_mask = 1 - lo_mask

    for c in range(nchunk):
        r0 = c * CHUNK
        dt = dt_ref[r0:r0 + CHUNK, :]
        dta = dt * a_row
        dta_hi = dta.astype(BF16)
        dta_lo = (dta - dta_hi.astype(F32)).astype(BF16)
        a2 = (_dot(tri, dta_hi) + _dot(tri, dta_lo)) * LOG2E
        a2_t = a2.T
        dt_t = dt.T
        row2_t = a2_t - jnp.log2(dt_t)
        a_end2_t = a2_t[:, CHUNK - 1:CHUNK]
        w_t = dt_t * jnp.exp2(a_end2_t - a2_t)
        e_end_t = jnp.exp2(a_end2_t)

        for g in range(SSM_GROUPS):
            if finish:
                finish.pop(0)()
            n_items = attn_counts[c * SSM_GROUPS + g]
            bm_gb = lane_block(n_xs + g, r0)
            cm_gb = lane_block(n_xs + SSM_GROUPS + g, r0)
            cb = _dot_nt(cm_gb, bm_gb)
            bm_gt = bm_gb.astype(F32).T
            gsl = slice(g * gw, (g + 1) * gw)
            hprev = h_s[g]
            y_off = _dot(cm_gb, hprev.astype(BF16))
            y_parts = []
            for p in range(pairs_per_group):
                if (n_items == 1 and p == pairs_per_group // 2) or (n_items >= 2 and p in (1, pairs_per_group - 1)):
                    for _ in range(1 if n_items <= 2 else (n_items + 1) // 2):
                        if attention:
                            attention.pop(0)()
                pair = g * pairs_per_group + p
                lsl = slice(pair * LANES, (pair + 1) * LANES)
                xs_b = lane_block(pair, r0)
                xs_pair = xs_b.astype(F32)
                rhs = jnp.concatenate([xs_b * lo_mask, xs_b * hi_mask], axis=0)
                m_parts, b_parts, cols, e_ends = [], [], [], []
                for side in range(2):
                    hh = 2 * pair + side
                    col2 = jnp.broadcast_to(a2[:, hh:hh + 1], (CHUNK, CHUNK))
                    dec_dt = jnp.exp2(jnp.where(causal, col2 - row2_t[hh:hh + 1, :], NEG))
                    m_parts.append(cb * dec_dt)
                    b_parts.append(bm_gt * w_t[hh:hh + 1, :])
                    cols.append(col2)
                    e_ends.append(e_end_t[hh:hh + 1, :])
                lhs = jnp.concatenate(
                    [jnp.concatenate(m_parts, axis=1), jnp.concatenate(b_parts, axis=1)],
                    axis=0).astype(BF16)
                res = _dot(lhs, rhs)
                e_col = jnp.exp2(jnp.where(low_half, cols[0], cols[1]))
                y_pair = res[0:CHUNK] + e_col * y_off[:, p * LANES:(p + 1) * LANES]
                y_pair = y_pair + _param(p_ref, layout, "d_skip", lsl) * xs_pair
                y_parts.append(y_pair * zm_ref[r0:r0 + CHUNK, lsl].astype(F32))
                e_end = jnp.where(low_half, e_ends[0], e_ends[1])
                h_s[g, :, p * LANES:(p + 1) * LANES] = (
                    hprev[:, p * LANES:(p + 1) * LANES] * e_end + res[CHUNK:])
            yg = jnp.concatenate(y_parts, axis=1)
            ms = jnp.mean(yg * yg, axis=-1, keepdims=True)
            nw = _param(p_ref, layout, "ssm_norm_w", gsl)
            yn_s[r0:r0 + CHUNK, gsl] = (yg * lax.rsqrt(ms + EPS) * nw).astype(BF16)

    assert not finish and not attention
    yprev_s[...] = yn_s[...]


def _branches_call(qkv, za, g, bias, params, layout, bd, wap,
                   xbc, dt, zm, tri, x2, gate, wsp, wout, batch, seq):
    t, d = x2.shape
    aw = ATTN_HEADS * HEAD_DIM
    kvw2 = 2 * ATTN_KV_HEADS * HEAD_DIM
    sw = zm.shape[1]
    rows = min(ROWS_BRANCHES, seq)
    nt = seq // rows
    bpt = rows // BLOCK
    tile = lambda b, i: b * nt + jnp.minimum(i, nt - 1)
    scan = lambda b, i: (tile(b, i), 0)
    done = lambda b, i: (b * nt + jnp.maximum(i - 1, 0), 0)
    return pl.pallas_call(
        functools.partial(_branches_kernel, layout=layout),
        grid=(batch, nt + 1),
        in_specs=[pl.BlockSpec((rows, aw), scan),
                  pl.BlockSpec((rows, kvw2), lambda b, i: (tile(b, i), aw // kvw2)),
                  pl.BlockSpec((BLOCK, kvw2),
                               lambda b, i: (jnp.maximum(tile(b, i) * bpt - 1, 0), aw // kvw2)),
                  pl.BlockSpec((rows, aw), scan),
                  pl.BlockSpec((rows, aw), scan),
                  _resident(bias.shape), _resident(params.shape),
                  _resident(bd.shape), _resident(wap.shape),
                  pl.BlockSpec((rows, xbc.shape[1]), scan),
                  pl.BlockSpec((rows, dt.shape[1]), scan),
                  pl.BlockSpec((rows, sw), scan),
                  _resident(tri.shape),
                  pl.BlockSpec((rows, d), lambda b, i: (b * nt + jnp.maximum(i - 1, 0), 1)),
                  pl.BlockSpec((rows, d), done),
                  pl.BlockSpec((1, 1, d), lambda b, i: (b, 0, 0)),
                  _resident(wsp.shape), _resident(wout.shape)],
        out_specs=pl.BlockSpec((rows, d), done),
        out_shape=jax.ShapeDtypeStruct((t, d), F32),
        scratch_shapes=[pltpu.VMEM((rows, aw), BF16),
                        pltpu.VMEM((rows + BLOCK, ATTN_KV_HEADS * LANES), BF16),
                        pltpu.VMEM((rows + BLOCK, ATTN_KV_HEADS * LANES), BF16),
                        pltpu.VMEM((ATTN_KV_HEADS * V_ROWS, rows + BLOCK), BF16),
                        pltpu.VMEM((2, 4 * BLOCK, 2 * BLOCK), F32),
                        pltpu.VMEM((2, 2 * BLOCK, 4 * BLOCK), BF16),
                        pltpu.VMEM((aw, rows), F32),
                        pltpu.VMEM((rows, aw), BF16),
                        pltpu.VMEM((rows, d), BF16),
                        pltpu.VMEM((SSM_GROUPS, SSM_STATE, sw // SSM_GROUPS), F32),
                        pltpu.VMEM((rows, sw), BF16),
                        pltpu.VMEM((rows, sw), BF16),
                        pltpu.VMEM((rows, d), BF16)],
        compiler_params=pltpu.CompilerParams(
            dimension_semantics=("arbitrary", "arbitrary"), vmem_limit_bytes=VMEM_LIMIT),
        name="branches",
    )(qkv, qkv, qkv, za, g, bias, params, bd, wap,
      xbc, dt, zm, tri, g, x2, gate, wsp, wout)


def _layer(x, c, w_ada, b_ada, norm_w, w_in, q_norm_w, k_norm_w, rel_bias, sinks,
           conv_w, conv_b, dt_bias, a_log, d_skip, ssm_norm_w, w_attn_proj, w_ssm_proj, w_out):
    batch, seq, d = x.shape
    aw = ATTN_HEADS * HEAD_DIM
    kvw = ATTN_KV_HEADS * HEAD_DIM
    sw = w_ssm_proj.shape[0]
    ssm_heads = sw // SSM_HEAD_DIM
    xbc_w = sw + 2 * SSM_GROUPS * SSM_STATE
    assert seq % BLOCK == 0 and seq % CHUNK == 0 and ssm_heads <= LANES

    o = np.cumsum([0, aw, kvw, kvw, aw, sw, xbc_w, ssm_heads, d, d])
    w_t = w_in.T
    c8 = jnp.zeros((8, d), F32).at[:batch].set(c)
    w_all, mod, bias = _prep_call(c8, w_ada, b_ada.reshape(1, -1), rel_bias.astype(F32), w_t, int(o[6]), ssm_heads)
    shift, scale, gate = (mod[:batch, k * d:(k + 1) * d].reshape(batch, 1, d) for k in range(3))
    wdt = jnp.pad(w_t[o[6]:o[7]], ((0, LANES - ssm_heads), (0, 0))).astype(BF16)
    widths = [aw + 2 * kvw, aw, sw, xbc_w, 2 * d]

    pad_heads = lambda v: jnp.pad(v.astype(F32), (0, LANES - ssm_heads))
    pair = lambda v: jnp.tile(v.astype(F32), LANES // HEAD_DIM)
    sink_lanes = jnp.repeat(sinks.astype(F32).reshape(ATTN_KV_HEADS, 2, 2).transpose(0, 2, 1).reshape(-1), BLOCK)
    params, layout = _param_table(
        [[(f"conv_w{w}", conv_w[w])] for w in range(CONV_WIDTH)]
        + [[("conv_b", conv_b)],
           [("d_skip", jnp.repeat(d_skip.astype(F32), SSM_HEAD_DIM)), ("norm_w", norm_w)],
           [("ssm_norm_w", ssm_norm_w), ("q_norm_w", pair(q_norm_w)), ("k_norm_w", pair(k_norm_w)),
            ("dt_bias", pad_heads(dt_bias)), ("a_log", pad_heads(a_log))],
           [("sinks", sink_lanes)]])

    x2 = x.reshape(batch * seq, d)
    qkv, za, zm, xbc, g, dt = _inproj_call(x2, shift, scale, params, layout, w_all, wdt, widths, batch, seq)

    seg = np.arange(LANES) // HEAD_DIM
    bd = jnp.asarray((seg[:, None] == seg[None, :]).astype(np.float32) / HEAD_DIM, dtype=BF16)
    tri = jnp.asarray(np.tril(np.ones((CHUNK, CHUNK), np.float32)), dtype=BF16)
    out = _branches_call(qkv, za, g, bias, params, layout, bd, w_attn_proj.astype(BF16),
                         xbc, dt, zm, tri,
                         x2, gate, w_ssm_proj.astype(BF16), w_out.astype(BF16), batch, seq)
    return out.reshape(batch, seq, d)


def kernel(x, c, w_ada, b_ada, norm_w, w_in, q_norm_w, k_norm_w, rel_bias, sinks, conv_w, conv_b,
           dt_bias, a_log, d_skip, ssm_norm_w, w_attn_proj, w_ssm_proj, w_out):
    depth = w_in.shape[0]
    for i in range(depth):
        x = _layer(x, c, w_ada[i], b_ada[i], norm_w[i], w_in[i], q_norm_w[i], k_norm_w[i],
                   rel_bias, sinks[i], conv_w[i], conv_b[i], dt_bias[i], a_log[i], d_skip[i],
                   ssm_norm_w[i], w_attn_proj[i], w_ssm_proj[i], w_out[i])
    return x
```

```python
import functools
import math

import numpy as np
import jax
import jax.numpy as jnp
from jax import lax
from jax.experimental import pallas as pl
from jax.experimental.pallas import tpu as pltpu

F32 = jnp.float32
BF16 = jnp.bfloat16

HEAD_DIM = 64
ATTN_HEADS = 16
ATTN_KV_HEADS = 4
GROUP = ATTN_HEADS // ATTN_KV_HEADS
BLOCK = 128
REL_BUCKETS = 32
REL_MAX_DIST = 128
SSM_HEAD_DIM = 64
SSM_GROUPS = 4
SSM_STATE = 128
CONV_WIDTH = 4
CHUNK = 128
EPS = 1e-6
NEG = -1e30
LOG2E = 1.4426950408889634
LANES = 128
VMEM_LIMIT = 56 * 1024 * 1024

V_PAD = 16
V_ROWS = HEAD_DIM + V_PAD

ROWS_INPROJ = 512
ROWS_BRANCHES = 256


def _sigmoid(x):
    return 0.5 + 0.5 * jnp.tanh(0.5 * x)


def _silu(x):
    half = 0.5 * x
    return half + half * jnp.tanh(half)


def _softplus(x):
    return jnp.maximum(x, 0.0) + jnp.log(1.0 + jnp.exp(-jnp.abs(x)))


def _dot(a, b):
    return jnp.dot(a, b, preferred_element_type=F32)


def _dot_nt(a, b):
    return lax.dot_general(a, b, (((1,), (1,)), ((), ())), preferred_element_type=F32)


def _resident(shape):
    nd = len(shape)
    return pl.BlockSpec(shape, lambda *_: (0,) * nd, pipeline_mode=pl.Buffered(1))


def _param_table(rows):
    width = max(sum(v.shape[0] for _, v in r) for r in rows)
    layout, lines = {}, []
    for i, r in enumerate(rows):
        col, parts = 0, []
        for name, v in r:
            assert col % LANES == 0 and v.shape[0] % LANES == 0
            layout[name] = (i, col, v.shape[0])
            parts.append(v.astype(F32))
            col += v.shape[0]
        if col < width:
            parts.append(jnp.zeros((width - col,), F32))
        lines.append(jnp.concatenate(parts))
    return jnp.stack(lines), layout


def _param(p_ref, layout, name, lanes=None):
    r, c, n = layout[name]
    lanes = slice(0, n) if lanes is None else lanes
    return p_ref[r:r + 1, c + lanes.start:c + lanes.stop]


def _mod_kernel(c_ref, w_ref, b_ref, o_ref):
    s = _silu(c_ref[...])
    o_ref[...] = _dot(s.astype(BF16), w_ref[...].astype(BF16)) + b_ref[...]


def _bucket_table():
    qi = np.arange(BLOCK)[:, None]
    kj = np.arange(2 * BLOCK)[None, :]
    dist = qi + BLOCK - kj
    n = np.maximum(dist, 0)
    max_exact = REL_BUCKETS // 2
    nf = np.maximum(n, 1).astype(np.float32)
    large = max_exact + (np.log(nf / max_exact) / math.log(REL_MAX_DIST / max_exact)
                         * (REL_BUCKETS - max_exact)).astype(np.int32)
    large = np.minimum(large, REL_BUCKETS - 1)
    bucket = np.where(n < max_exact, n, large)
    valid = (dist >= 0) & (dist < BLOCK)
    return np.where(valid, bucket, -1).astype(np.int32)


def _bias_kernel(rb_ref, idx_ref, o_ref):
    kv = pl.program_id(0)
    idx = idx_ref[...]
    key_row = lax.broadcasted_iota(jnp.int32, idx.shape, 0)
    keep = idx >= 0
    keep_first = jnp.logical_and(keep, key_row >= BLOCK)
    for s in range(2):
        for c in range(2):
            acc = jnp.zeros(idx.shape, F32)
            for b in range(REL_BUCKETS):
                acc = jnp.where(idx == b, rb_ref[b, kv * GROUP + 2 * c + s], acc)
            acc = acc * LOG2E
            rows, cols = slice(s * 2 * BLOCK, (s + 1) * 2 * BLOCK), slice(c * BLOCK, (c + 1) * BLOCK)
            o_ref[0, 0, rows, cols] = jnp.where(keep, acc, NEG)
            o_ref[1, 0, rows, cols] = jnp.where(keep_first, acc, NEG)


def _prep_kernel(rb_ref, idx_ref, c_ref, wada_ref, bada_ref, wt_ref, w_o, mod_o, bias_o):
    w_o[...] = wt_ref[...].T.astype(BF16)

    @pl.when(pl.program_id(0) < ATTN_KV_HEADS)
    def _():
        _mod_kernel(c_ref, wada_ref, bada_ref, mod_o)
        _bias_kernel(rb_ref, idx_ref, bias_o)


def _prep_call(c8, w_ada, b_ada, rel_bias, w_t, dt_start, dt_width):
    n, k = w_t.shape
    tn = 512
    head_tiles, tail_tiles = dt_start // tn, (n - dt_start - dt_width) // tn
    assert head_tiles * tn == dt_start and tail_tiles * tn == n - dt_start - dt_width
    tail_start = dt_start + dt_width
    assert tail_start % 8 == 0

    def rows(i):
        start = jnp.where(i < head_tiles, i * tn, tail_start + (i - head_tiles) * tn)
        return (pl.multiple_of(start, 8), 0)

    d, n_mod = w_ada.shape
    n_side = ATTN_KV_HEADS
    tm = n_mod // n_side
    assert tm * n_side == n_mod and tm % LANES == 0 and head_tiles + tail_tiles >= n_side
    side = lambda i: jnp.minimum(i, n_side - 1)
    idx_t = jnp.asarray(np.ascontiguousarray(_bucket_table().T))
    w_all, mod, table = pl.pallas_call(
        _prep_kernel,
        grid=(head_tiles + tail_tiles,),
        in_specs=[pl.BlockSpec(memory_space=pltpu.SMEM),
                  pl.BlockSpec((2 * BLOCK, BLOCK), lambda i: (0, 0)),
                  pl.BlockSpec((8, d), lambda i: (0, 0)),
                  pl.BlockSpec((d, tm), lambda i: (0, side(i))),
                  pl.BlockSpec((1, tm), lambda i: (0, side(i))),
                  pl.BlockSpec((pl.Element(tn), pl.Element(k)), rows)],
        out_specs=[pl.BlockSpec((k, tn), lambda i: (0, i)),
                   pl.BlockSpec((8, tm), lambda i: (0, side(i))),
                   pl.BlockSpec((2, 1, 4 * BLOCK, 2 * BLOCK), lambda i: (0, side(i), 0, 0))],
        out_shape=[jax.ShapeDtypeStruct((k, n - dt_width), BF16),
                   jax.ShapeDtypeStruct((8, n_mod), F32),
                   jax.ShapeDtypeStruct((2, ATTN_KV_HEADS, 4 * BLOCK, 2 * BLOCK), F32)],
        compiler_params=pltpu.CompilerParams(dimension_semantics=("arbitrary",)),
        name="prep",
    )(rel_bias, idx_t, c8, w_ada, b_ada, w_t)
    return w_all, mod, table.reshape(2 * ATTN_KV_HEADS, 4 * BLOCK, 2 * BLOCK)


def _inproj_kernel(x_ref, shift_ref, scale_ref, p_ref, w_ref, wdt_ref,
                   qkv_o, za_o, zm_o, xbc_o, g_o, dt_o, xraw_s, carry_s, *, layout):
    ts = x_ref.shape[0]
    halo = carry_s.shape[1]

    @pl.when(pl.program_id(1) == 0)
    def _():
        carry_s[...] = jnp.zeros(carry_s.shape, F32)

    x = x_ref[...]
    ms = jnp.mean(x * x, axis=-1, keepdims=True)
    gain = _param(p_ref, layout, "norm_w") * (1.0 + scale_ref[0])
    h = (x * lax.rsqrt(ms + EPS) * gain + shift_ref[0]).astype(BF16)

    def conv_silu_store(y, c0, slot):
        for jj in range(y.shape[1] // LANES):
            j = c0 // LANES + jj
            jl = slice(j * LANES, (j + 1) * LANES)
            yj = y[:, jj * LANES:(jj + 1) * LANES]
            xraw_s[slot, jj, 0:halo, :] = carry_s[j]
            xraw_s[slot, jj, halo:halo + ts, :] = yj
            carry_s[j] = yj[ts - halo:ts, :]
            acc = _param(p_ref, layout, "conv_b", jl)
            for w in range(CONV_WIDTH):
                off = halo - (CONV_WIDTH - 1) + w
                acc = acc + _param(p_ref, layout, f"conv_w{w}", jl) * xraw_s[slot, jj, off:off + ts, :]
            xbc_o[:, jl] = _silu(acc).astype(BF16)

    step = xraw_s.shape[1] * LANES
    light, heavy, col = [], [], 0
    for o_ref, act in ((qkv_o, None), (za_o, _silu), (zm_o, _silu), (xbc_o, "conv"), (g_o, _sigmoid)):
        for c0 in range(0, o_ref.shape[1], step):
            (heavy if act == "conv" else light).append((o_ref, act, col + c0, c0))
        col += o_ref.shape[1]
    gates = [p for p in light if p[1] is _sigmoid]
    others = [p for p in light if p[1] is not _sigmoid]
    if others[0][1] is None:
        others.append(others.pop(0))
    order = []
    for piece in heavy:
        order += [piece, others.pop(0), gates.pop(0) if gates else others.pop(0)]
    order += others + gates
    slot = 0
    for k, (o_ref, act, wc, c0) in enumerate(order):
        w_piece = w_ref[:, wc:wc + step]
        if k == 0:
            y = jnp.concatenate([_dot(h[:ts // 2], w_piece), _dot(h[ts // 2:], w_piece)], axis=0)
        else:
            y = _dot(h, w_piece)
        if act == "conv":
            conv_silu_store(y, c0, slot)
            slot = 1 - slot
        else:
            o_ref[:, c0:c0 + step] = (y if act is None else act(y)).astype(o_ref.dtype)
    dt_o[...] = _softplus(_dot_nt(h, wdt_ref[...]) + _param(p_ref, layout, "dt_bias"))


def _inproj_call(x2, shift, scale, params, layout, w_all, wdt, widths, batch, seq):
    t, d = x2.shape
    ts = min(ROWS_INPROJ, seq)
    ns = seq // ts
    row = lambda b, i: (b * ns + i, 0)
    per_b = pl.BlockSpec((1, 1, d), lambda b, i: (b, 0, 0))
    widths = list(widths) + [wdt.shape[0]]
    dtypes = [BF16] * (len(widths) - 1) + [F32]
    piece_blocks = 4
    return pl.pallas_call(
        functools.partial(_inproj_kernel, layout=layout),
        grid=(batch, ns),
        in_specs=[pl.BlockSpec((ts, d), row), per_b, per_b, _resident(params.shape),
                  _resident(w_all.shape), _resident(wdt.shape)],
        out_specs=[pl.BlockSpec((ts, n), row) for n in widths],
        out_shape=[jax.ShapeDtypeStruct((t, n), dt) for n, dt in zip(widths, dtypes)],
        scratch_shapes=[pltpu.VMEM((2, piece_blocks, ts + 8, LANES), F32),
                        pltpu.VMEM((widths[3] // LANES, 8, LANES), F32)],
        compiler_params=pltpu.CompilerParams(
            dimension_semantics=("arbitrary", "arbitrary"), vmem_limit_bytes=VMEM_LIMIT),
        name="inproj",
    )(x2, shift, scale, params, w_all, wdt)


def _segment_rms(x, bd, w):
    ms = _dot((x * x).astype(BF16), bd)
    return x * lax.rsqrt(ms + EPS) * w


def _branches_kernel(*refs, layout):
    i, last = pl.program_id(1), pl.num_programs(1) - 1
    step = functools.partial(_branches_step, *refs, layout=layout)
    pl.when(i == 0)(functools.partial(step, compute=True, finishing=False, first=True))
    pl.when(jnp.logical_and(i > 0, i < last))(functools.partial(step, compute=True, finishing=True, first=False))
    pl.when(i == last)(functools.partial(step, compute=False, finishing=True, first=False))


def _branches_step(q_ref, kvc_ref, kvp_ref, za_ref, ga_ref, bias_ref, p_ref, bd_ref, wap_ref,
                   xbc_ref, dt_ref, zm_ref, tri_ref,
                   gb_ref, x_ref, gate_ref, wsp_ref, wout_ref,
                   o_ref,
                   qn_s, klo_s, khi_s, vt_s, s_s, p_s, yt_s, y_s, ya_s, h_s, yn_s, yprev_s, merged_s,
                   *, layout, compute, finishing, first):
    rows = q_ref.shape[0]
    nblk = rows // BLOCK
    nchunk = rows // CHUNK
    kvw = ATTN_KV_HEADS * HEAD_DIM
    sw = zm_ref.shape[1]
    gw = sw // SSM_GROUPS
    pairs_per_group = gw // LANES
    n_xs = sw // LANES
    if first:
        h_s[...] = jnp.zeros(h_s.shape, F32)

    d_out = o_ref.shape[1]
    piece_w = 2 * LANES

    def merge_piece(c0):
        cs = slice(c0, c0 + piece_w)
        yb = _dot(yprev_s[...], wsp_ref[:, cs])
        merged_s[:, cs] = ya_s[:, cs] + gb_ref[:, cs] * yb.astype(BF16)

    def out_piece(c0):
        cs = slice(c0, c0 + piece_w)
        o = _dot(merged_s[...], wout_ref[:, cs])
        o_ref[:, cs] = x_ref[:, cs] + gate_ref[0][:, cs] * o

    finish = []
    if finishing:
        finish = ([functools.partial(merge_piece, c0) for c0 in range(0, d_out, piece_w)]
                  + [functools.partial(out_piece, c0) for c0 in range(0, d_out, piece_w)])
    if not compute:
        for piece in finish:
            piece()
        return

    bd = bd_ref[...]
    qw = _param(p_ref, layout, "q_norm_w") * (HEAD_DIM ** -0.5 * LOG2E)
    kw = _param(p_ref, layout, "k_norm_w")
    low_lanes = lax.broadcasted_iota(jnp.int32, (1, LANES), 1) < HEAD_DIM

    for j in range(q_ref.shape[1] // LANES):
        sl = slice(j * LANES, (j + 1) * LANES)
        qn_s[:, sl] = _segment_rms(q_ref[:, sl].astype(F32), bd, qw).astype(BF16)
    for j in range(kvw // LANES):
        sl = slice(j * LANES, (j + 1) * LANES)
        even = slice(2 * j * LANES, (2 * j + 1) * LANES)
        odd = slice((2 * j + 1) * LANES, (2 * j + 2) * LANES)
        for dst, src in ((slice(0, BLOCK), kvp_ref), (slice(BLOCK, BLOCK + rows), kvc_ref)):
            kn = _segment_rms(src[:, sl].astype(F32), bd, kw)
            lo = jnp.where(low_lanes, kn, 0.0)
            hi = jnp.where(low_lanes, 0.0, kn)
            klo_s[dst, even] = lo.astype(BF16)
            khi_s[dst, odd] = hi.astype(BF16)
            khi_s[dst, even] = pltpu.roll(lo, HEAD_DIM, axis=1).astype(BF16)
            klo_s[dst, odd] = pltpu.roll(hi, HEAD_DIM, axis=1).astype(BF16)
    for dst, src in ((slice(0, BLOCK), kvp_ref), (slice(BLOCK, BLOCK + rows), kvc_ref)):
        v_t = src[:, kvw:2 * kvw].astype(F32).T.astype(BF16)
        for h in range(ATTN_KV_HEADS):
            vt_s[h * V_ROWS:h * V_ROWS + HEAD_DIM, dst] = v_t[h * HEAD_DIM:(h + 1) * HEAD_DIM, :]
    if first:
        for h in range(ATTN_KV_HEADS):
            vt_s[h * V_ROWS + HEAD_DIM:(h + 1) * V_ROWS, :] = jnp.ones((V_PAD, vt_s.shape[1]), BF16)

    units = [(qb, h) for qb in range(nblk) for h in range(ATTN_KV_HEADS)]

    def scores(u):
        qb, h = units[u]
        r0 = qb * BLOCK
        hl = slice(h * LANES, (h + 1) * LANES)
        k2 = jnp.concatenate([klo_s[r0:r0 + 2 * BLOCK, hl], khi_s[r0:r0 + 2 * BLOCK, hl]], axis=0)
        q2 = jnp.concatenate([qn_s[r0:r0 + BLOCK, (2 * h + c) * LANES:(2 * h + c + 1) * LANES]
                              for c in range(2)], axis=0)
        entry = ATTN_KV_HEADS + h if first and qb == 0 else h
        s_s[u % 2] = _dot_nt(k2, q2) + bias_ref[entry]

    def softmax_pv(u):
        qb, h = units[u]
        r0 = qb * BLOCK
        slot = u % 2
        sink_terms = []
        for s in range(2):
            for c in range(2):
                blk = s_s[slot, s * 2 * BLOCK:(s + 1) * 2 * BLOCK, c * BLOCK:(c + 1) * BLOCK]
                sc = ((h * 2 + s) * 2 + c) * BLOCK
                sink = _param(p_ref, layout, "sinks", slice(sc, sc + BLOCK)) * LOG2E
                m = jnp.maximum(jnp.max(blk, axis=0, keepdims=True), sink)
                e = jnp.exp2(blk - m)
                p_s[slot, :, (2 * s + c) * BLOCK:(2 * s + c + 1) * BLOCK] = e.astype(BF16)
                sink_terms.append(jnp.exp2(sink - m))
        o_t = _dot(vt_s[h * V_ROWS:(h + 1) * V_ROWS, r0:r0 + 2 * BLOCK], p_s[slot])
        for s in range(2):
            for c in range(2):
                head = GROUP * h + 2 * c + s
                k = 2 * s + c
                cols = slice(k * BLOCK, (k + 1) * BLOCK)
                den = o_t[HEAD_DIM:HEAD_DIM + 1, cols] + sink_terms[k]
                yt_s[head * HEAD_DIM:(head + 1) * HEAD_DIM, r0:r0 + BLOCK] = o_t[0:HEAD_DIM, cols] * (1.0 / den)

    scores(0)
    attention = []
    for u in range(len(units)):
        def unit(u=u):
            if u + 1 < len(units):
                scores(u + 1)
            softmax_pv(u)
        attention.append(unit)

    def gate_y():
        y_s[...] = (yt_s[...].T * za_ref[...].astype(F32)).astype(BF16)

    def proj_piece(c0):
        cs = slice(c0, c0 + piece_w)
        ya = _dot(y_s[...], wap_ref[:, cs])
        ya_s[:, cs] = (ya * ga_ref[:, cs].astype(F32)).astype(BF16)

    first_proj = len(attention) + 1
    attention.append(gate_y)
    attention += [functools.partial(proj_piece, c0) for c0 in range(0, d_out, piece_w)]
    n_groups = nchunk * SSM_GROUPS
    base, extra = divmod(len(attention), n_groups)
    attn_counts = [base + (k < extra) for k in range(n_groups)]
    first_proj_group = next(k for k in range(n_groups) if sum(attn_counts[:k + 1]) > first_proj)
    assert len(finish) <= n_groups and first_proj_group >= len(finish) // 2 - 1

    def lane_block(j, r0):
        return xbc_ref[r0:r0 + CHUNK, j * LANES:(j + 1) * LANES]

    a_row = -jnp.exp(_param(p_ref, layout, "a_log"))
    tri = tri_ref[...]
    li = lax.broadcasted_iota(jnp.int32, (CHUNK, CHUNK), 0)
    si = lax.broadcasted_iota(jnp.int32, (CHUNK, CHUNK), 1)
    causal = li >= si
    low_half = si < SSM_HEAD_DIM
    lo_mask = (lax.broadcasted_iota(jnp.int32, (1, LANES), 1) < SSM_HEAD_DIM).astype(BF16)
    hi_mask = 1 - lo_mask

    for c in range(nchunk):
        r0 = c * CHUNK
        dt = dt_ref[r0:r0 + CHUNK, :]
        dta = dt * a_row
        dta_hi = dta.astype(BF16)
        dta_lo = (dta - dta_hi.astype(F32)).astype(BF16)
        a2 = (_dot(tri, dta_hi) + _dot(tri, dta_lo)) * LOG2E
        a2_t = a2.T
        dt_t = dt.T
        row2_t = a2_t - jnp.log2(dt_t)
        a_end2_t = a2_t[:, CHUNK - 1:CHUNK]
        w_t = dt_t * jnp.exp2(a_end2_t - a2_t)
        e_end_t = jnp.exp2(a_end2_t)

        for g in range(SSM_GROUPS):
            if finish:
                finish.pop(0)()
            n_items = attn_counts[c * SSM_GROUPS + g]
            bm_gb = lane_block(n_xs + g, r0)
            cm_gb = lane_block(n_xs + SSM_GROUPS + g, r0)
            cb = _dot_nt(cm_gb, bm_gb)
            bm_gt = bm_gb.astype(F32).T
            gsl = slice(g * gw, (g + 1) * gw)
            hprev = h_s[g]
            y_off = _dot(cm_gb, hprev.astype(BF16))
            y_parts = []
            for p in range(pairs_per_group):
                if (n_items == 1 and p == pairs_per_group // 2) or (n_items >= 2 and p in (1, pairs_per_group - 1)):
                    for _ in range(1 if n_items <= 2 else (n_items + 1) // 2):
                        if attention:
                            attention.pop(0)()
                pair = g * pairs_per_group + p
                lsl = slice(pair * LANES, (pair + 1) * LANES)
                xs_b = lane_block(pair, r0)
                xs_pair = xs_b.astype(F32)
                rhs = jnp.concatenate([xs_b * lo_mask, xs_b * hi_mask], axis=0)
                m_parts, b_parts, cols, e_ends = [], [], [], []
                for side in range(2):
                    hh = 2 * pair + side
                    col2 = jnp.broadcast_to(a2[:, hh:hh + 1], (CHUNK, CHUNK))
                    dec_dt = jnp.exp2(jnp.where(causal, col2 - row2_t[hh:hh + 1, :], NEG))
                    m_parts.append(cb * dec_dt)
                    b_parts.append(bm_gt * w_t[hh:hh + 1, :])
                    cols.append(col2)
                    e_ends.append(e_end_t[hh:hh + 1, :])
                lhs = jnp.concatenate(
                    [jnp.concatenate(m_parts, axis=1), jnp.concatenate(b_parts, axis=1)],
                    axis=0).astype(BF16)
                res = _dot(lhs, rhs)
                e_col = jnp.exp2(jnp.where(low_half, cols[0], cols[1]))
                y_pair = res[0:CHUNK] + e_col * y_off[:, p * LANES:(p + 1) * LANES]
                y_pair = y_pair + _param(p_ref, layout, "d_skip", lsl) * xs_pair
                y_parts.append(y_pair * zm_ref[r0:r0 + CHUNK, lsl].astype(F32))
                e_end = jnp.where(low_half, e_ends[0], e_ends[1])
                h_s[g, :, p * LANES:(p + 1) * LANES] = (
                    hprev[:, p * LANES:(p + 1) * LANES] * e_end + res[CHUNK:])
            yg = jnp.concatenate(y_parts, axis=1)
            ms = jnp.mean(yg * yg, axis=-1, keepdims=True)
            nw = _param(p_ref, layout, "ssm_norm_w", gsl)
            yn_s[r0:r0 + CHUNK, gsl] = (yg * lax.rsqrt(ms + EPS) * nw).astype(BF16)

    assert not finish and not attention
    yprev_s[...] = yn_s[...]


def _branches_call(qkv, za, g, bias, params, layout, bd, wap,
                   xbc, dt, zm, tri, x2, gate, wsp, wout, batch, seq):
    t, d = x2.shape
    aw = ATTN_HEADS * HEAD_DIM
    kvw2 = 2 * ATTN_KV_HEADS * HEAD_DIM
    sw = zm.shape[1]
    rows = min(ROWS_BRANCHES, seq)
    nt = seq // rows
    bpt = rows // BLOCK
    tile = lambda b, i: b * nt + jnp.minimum(i, nt - 1)
    scan = lambda b, i: (tile(b, i), 0)
    done = lambda b, i: (b * nt + jnp.maximum(i - 1, 0), 0)
    return pl.pallas_call(
        functools.partial(_branches_kernel, layout=layout),
        grid=(batch, nt + 1),
        in_specs=[pl.BlockSpec((rows, aw), scan),
                  pl.BlockSpec((rows, kvw2), lambda b, i: (tile(b, i), aw // kvw2)),
                  pl.BlockSpec((BLOCK, kvw2),
                               lambda b, i: (jnp.maximum(tile(b, i) * bpt - 1, 0), aw // kvw2)),
                  pl.BlockSpec((rows, aw), scan),
                  pl.BlockSpec((rows, aw), scan),
                  _resident(bias.shape), _resident(params.shape),
                  _resident(bd.shape), _resident(wap.shape),
                  pl.BlockSpec((rows, xbc.shape[1]), scan),
                  pl.BlockSpec((rows, dt.shape[1]), scan),
                  pl.BlockSpec((rows, sw), scan),
                  _resident(tri.shape),
                  pl.BlockSpec((rows, d), lambda b, i: (b * nt + jnp.maximum(i - 1, 0), 1)),
                  pl.BlockSpec((rows, d), done),
                  pl.BlockSpec((1, 1, d), lambda b, i: (b, 0, 0)),
                  _resident(wsp.shape), _resident(wout.shape)],
        out_specs=pl.BlockSpec((rows, d), done),
        out_shape=jax.ShapeDtypeStruct((t, d), F32),
        scratch_shapes=[pltpu.VMEM((rows, aw), BF16),
                        pltpu.VMEM((rows + BLOCK, ATTN_KV_HEADS * LANES), BF16),
                        pltpu.VMEM((rows + BLOCK, ATTN_KV_HEADS * LANES), BF16),
                        pltpu.VMEM((ATTN_KV_HEADS * V_ROWS, rows + BLOCK), BF16),
                        pltpu.VMEM((2, 4 * BLOCK, 2 * BLOCK), F32),
                        pltpu.VMEM((2, 2 * BLOCK, 4 * BLOCK), BF16),
                        pltpu.VMEM((aw, rows), F32),
                        pltpu.VMEM((rows, aw), BF16),
                        pltpu.VMEM((rows, d), BF16),
                        pltpu.VMEM((SSM_GROUPS, SSM_STATE, sw // SSM_GROUPS), F32),
                        pltpu.VMEM((rows, sw), BF16),
                        pltpu.VMEM((rows, sw), BF16),
                        pltpu.VMEM((rows, d), BF16)],
        compiler_params=pltpu.CompilerParams(
            dimension_semantics=("arbitrary", "arbitrary"), vmem_limit_bytes=VMEM_LIMIT),
        name="branches",
    )(qkv, qkv, qkv, za, g, bias, params, bd, wap,
      xbc, dt, zm, tri, g, x2, gate, wsp, wout)


def _layer(x, c, w_ada, b_ada, norm_w, w_in, q_norm_w, k_norm_w, rel_bias, sinks,
           conv_w, conv_b, dt_bias, a_log, d_skip, ssm_norm_w, w_attn_proj, w_ssm_proj, w_out):
    batch, seq, d = x.shape
    aw = ATTN_HEADS * HEAD_DIM
    kvw = ATTN_KV_HEADS * HEAD_DIM
    sw = w_ssm_proj.shape[0]
    ssm_heads = sw // SSM_HEAD_DIM
    xbc_w = sw + 2 * SSM_GROUPS * SSM_STATE
    assert seq % BLOCK == 0 and seq % CHUNK == 0 and ssm_heads <= LANES

    o = np.cumsum([0, aw, kvw, kvw, aw, sw, xbc_w, ssm_heads, d, d])
    w_t = w_in.T
    c8 = jnp.zeros((8, d), F32).at[:batch].set(c)
    w_all, mod, bias = _prep_call(c8, w_ada, b_ada.reshape(1, -1), rel_bias.astype(F32), w_t, int(o[6]), ssm_heads)
    shift, scale, gate = (mod[:batch, k * d:(k + 1) * d].reshape(batch, 1, d) for k in range(3))
    wdt = jnp.pad(w_t[o[6]:o[7]], ((0, LANES - ssm_heads), (0, 0))).astype(BF16)
    widths = [aw + 2 * kvw, aw, sw, xbc_w, 2 * d]

    pad_heads = lambda v: jnp.pad(v.astype(F32), (0, LANES - ssm_heads))
    pair = lambda v: jnp.tile(v.astype(F32), LANES // HEAD_DIM)
    sink_lanes = jnp.repeat(sinks.astype(F32).reshape(ATTN_KV_HEADS, 2, 2).transpose(0, 2, 1).reshape(-1), BLOCK)
    params, layout = _param_table(
        [[(f"conv_w{w}", conv_w[w])] for w in range(CONV_WIDTH)]
        + [[("conv_b", conv_b)],
           [("d_skip", jnp.repeat(d_skip.astype(F32), SSM_HEAD_DIM)), ("norm_w", norm_w)],
           [("ssm_norm_w", ssm_norm_w), ("q_norm_w", pair(q_norm_w)), ("k_norm_w", pair(k_norm_w)),
            ("dt_bias", pad_heads(dt_bias)), ("a_log", pad_heads(a_log))],
           [("sinks", sink_lanes)]])

    x2 = x.reshape(batch * seq, d)
    qkv, za, zm, xbc, g, dt = _inproj_call(x2, shift, scale, params, layout, w_all, wdt, widths, batch, seq)

    seg = np.arange(LANES) // HEAD_DIM
    bd = jnp.asarray((seg[:, None] == seg[None, :]).astype(np.float32) / HEAD_DIM, dtype=BF16)
    tri = jnp.asarray(np.tril(np.ones((CHUNK, CHUNK), np.float32)), dtype=BF16)
    out = _branches_call(qkv, za, g, bias, params, layout, bd, w_attn_proj.astype(BF16),
                         xbc, dt, zm, tri,
                         x2, gate, w_ssm_proj.astype(BF16), w_out.astype(BF16), batch, seq)
    return out.reshape(batch, seq, d)


def kernel(x, c, w_ada, b_ada, norm_w, w_in, q_norm_w, k_norm_w, rel_bias, sinks, conv_w, conv_b,
           dt_bias, a_log, d_skip, ssm_norm_w, w_attn_proj, w_ssm_proj, w_out):
    depth = w_in.shape[0]
    for i in range(depth):
        x = _layer(x, c, w_ada[i], b_ada[i], norm_w[i], w_in[i], q_norm_w[i], k_norm_w[i],
                   rel_bias, sinks[i], conv_w[i], conv_b[i], dt_bias[i], a_log[i], d_skip[i],
                   ssm_norm_w[i], w_attn_proj[i], w_ssm_proj[i], w_out[i])
    return x
```

```python
import functools
import math

import numpy as np
import jax
import jax.numpy as jnp
from jax import lax
from jax.experimental import pallas as pl
from jax.experimental.pallas import tpu as pltpu

F32 = jnp.float32
BF16 = jnp.bfloat16

HEAD_DIM = 64
ATTN_HEADS = 16
ATTN_KV_HEADS = 4
GROUP = ATTN_HEADS // ATTN_KV_HEADS
BLOCK = 128
REL_BUCKETS = 32
REL_MAX_DIST = 128
SSM_HEAD_DIM = 64
SSM_GROUPS = 4
SSM_STATE = 128
CONV_WIDTH = 4
CHUNK = 128
EPS = 1e-6
NEG = -1e30
LOG2E = 1.4426950408889634
LANES = 128
VMEM_LIMIT = 56 * 1024 * 1024

V_PAD = 16
V_ROWS = HEAD_DIM + V_PAD

ROWS_INPROJ = 512
X_RING = 3
ROWS_BRANCHES = 256


def _sigmoid(x):
    return 0.5 + 0.5 * jnp.tanh(0.5 * x)


def _silu(x):
    half = 0.5 * x
    return half + half * jnp.tanh(half)


def _softplus(x):
    return jnp.maximum(x, 0.0) + jnp.log(1.0 + jnp.exp(-jnp.abs(x)))


def _dot(a, b):
    return jnp.dot(a, b, preferred_element_type=F32)


def _dot_nt(a, b):
    return lax.dot_general(a, b, (((1,), (1,)), ((), ())), preferred_element_type=F32)


def _resident(shape):
    nd = len(shape)
    return pl.BlockSpec(shape, lambda *_: (0,) * nd, pipeline_mode=pl.Buffered(1))


def _param_table(rows):
    width = max(sum(v.shape[0] for _, v in r) for r in rows)
    layout, lines = {}, []
    for i, r in enumerate(rows):
        col, parts = 0, []
        for name, v in r:
            assert col % LANES == 0 and v.shape[0] % LANES == 0
            layout[name] = (i, col, v.shape[0])
            parts.append(v.astype(F32))
            col += v.shape[0]
        if col < width:
            parts.append(jnp.zeros((width - col,), F32))
        lines.append(jnp.concatenate(parts))
    return jnp.stack(lines), layout


def _param(p_ref, layout, name, lanes=None):
    r, c, n = layout[name]
    lanes = slice(0, n) if lanes is None else lanes
    return p_ref[r:r + 1, c + lanes.start:c + lanes.stop]


def _mod_kernel(c_ref, w_ref, b_ref, o_ref):
    s = _silu(c_ref[...])
    o_ref[...] = _dot(s.astype(BF16), w_ref[...].astype(BF16)) + b_ref[...]


def _bucket_table():
    qi = np.arange(BLOCK)[:, None]
    kj = np.arange(2 * BLOCK)[None, :]
    dist = qi + BLOCK - kj
    n = np.maximum(dist, 0)
    max_exact = REL_BUCKETS // 2
    nf = np.maximum(n, 1).astype(np.float32)
    large = max_exact + (np.log(nf / max_exact) / math.log(REL_MAX_DIST / max_exact)
                         * (REL_BUCKETS - max_exact)).astype(np.int32)
    large = np.minimum(large, REL_BUCKETS - 1)
    bucket = np.where(n < max_exact, n, large)
    valid = (dist >= 0) & (dist < BLOCK)
    return np.where(valid, bucket, -1).astype(np.int32)


def _bias_kernel(rb_ref, idx_ref, o_ref):
    kv = pl.program_id(0)
    idx = idx_ref[...]
    key_row = lax.broadcasted_iota(jnp.int32, idx.shape, 0)
    keep = idx >= 0
    keep_first = jnp.logical_and(keep, key_row >= BLOCK)
    for s in range(2):
        for c in range(2):
            acc = jnp.zeros(idx.shape, F32)
            for b in range(REL_BUCKETS):
                acc = jnp.where(idx == b, rb_ref[b, kv * GROUP + 2 * c + s], acc)
            acc = acc * LOG2E
            rows, cols = slice(s * 2 * BLOCK, (s + 1) * 2 * BLOCK), slice(c * BLOCK, (c + 1) * BLOCK)
            o_ref[0, 0, rows, cols] = jnp.where(keep, acc, NEG)
            o_ref[1, 0, rows, cols] = jnp.where(keep_first, acc, NEG)


def _prep_kernel(rb_ref, idx_ref, c_ref, wada_ref, bada_ref, wt_ref, w_o, mod_o, bias_o):
    w_o[...] = wt_ref[...].T.astype(BF16)

    @pl.when(pl.program_id(0) < ATTN_KV_HEADS)
    def _():
        _mod_kernel(c_ref, wada_ref, bada_ref, mod_o)
        _bias_kernel(rb_ref, idx_ref, bias_o)


def _prep_call(c8, w_ada, b_ada, rel_bias, w_t, dt_start, dt_width):
    n, k = w_t.shape
    tn = 512
    head_tiles, tail_tiles = dt_start // tn, (n - dt_start - dt_width) // tn
    assert head_tiles * tn == dt_start and tail_tiles * tn == n - dt_start - dt_width
    tail_start = dt_start + dt_width
    assert tail_start % 8 == 0

    def rows(i):
        start = jnp.where(i < head_tiles, i * tn, tail_start + (i - head_tiles) * tn)
        return (pl.multiple_of(start, 8), 0)

    d, n_mod = w_ada.shape
    n_side = ATTN_KV_HEADS
    tm = n_mod // n_side
    assert tm * n_side == n_mod and tm % LANES == 0 and head_tiles + tail_tiles >= n_side
    side = lambda i: jnp.minimum(i, n_side - 1)
    idx_t = jnp.asarray(np.ascontiguousarray(_bucket_table().T))
    w_all, mod, table = pl.pallas_call(
        _prep_kernel,
        grid=(head_tiles + tail_tiles,),
        in_specs=[pl.BlockSpec(memory_space=pltpu.SMEM),
                  pl.BlockSpec((2 * BLOCK, BLOCK), lambda i: (0, 0)),
                  pl.BlockSpec((8, d), lambda i: (0, 0)),
                  pl.BlockSpec((d, tm), lambda i: (0, side(i))),
                  pl.BlockSpec((1, tm), lambda i: (0, side(i))),
                  pl.BlockSpec((pl.Element(tn), pl.Element(k)), rows)],
        out_specs=[pl.BlockSpec((k, tn), lambda i: (0, i)),
                   pl.BlockSpec((8, tm), lambda i: (0, side(i))),
                   pl.BlockSpec((2, 1, 4 * BLOCK, 2 * BLOCK), lambda i: (0, side(i), 0, 0))],
        out_shape=[jax.ShapeDtypeStruct((k, n - dt_width), BF16),
                   jax.ShapeDtypeStruct((8, n_mod), F32),
                   jax.ShapeDtypeStruct((2, ATTN_KV_HEADS, 4 * BLOCK, 2 * BLOCK), F32)],
        compiler_params=pltpu.CompilerParams(dimension_semantics=("arbitrary",)),
        name="prep",
    )(rel_bias, idx_t, c8, w_ada, b_ada, w_t)
    return w_all, mod, table.reshape(2 * ATTN_KV_HEADS, 4 * BLOCK, 2 * BLOCK)


def _inproj_kernel(x_hbm, shift_ref, scale_ref, p_ref, w_ref, wdt_ref,
                   qkv_o, za_o, zm_o, xbc_o, g_o, dt_o, xraw_s, carry_s, xbuf_s, x_sem, *, layout):
    ts = xbuf_s.shape[1]
    halo = carry_s.shape[1]
    ns = pl.num_programs(1)
    n_steps = pl.num_programs(0) * ns
    s = pl.program_id(0) * ns + pl.program_id(1)

    def x_copy(tile):
        slot = tile % X_RING
        start = pl.multiple_of(tile * ts, ts)
        return pltpu.make_async_copy(x_hbm.at[pl.ds(start, ts), :], xbuf_s.at[slot], x_sem.at[slot])

    @pl.when(s == 0)
    def _():
        for k in range(X_RING - 1):
            @pl.when(k < n_steps)
            def _():
                x_copy(k).start()

    @pl.when(s + X_RING - 1 < n_steps)
    def _():
        x_copy(s + X_RING - 1).start()

    @pl.when(pl.program_id(1) == 0)
    def _():
        carry_s[...] = jnp.zeros(carry_s.shape, F32)

    x_copy(s).wait()
    x = xbuf_s[s % X_RING]
    ms = jnp.mean(x * x, axis=-1, keepdims=True)
    gain = _param(p_ref, layout, "norm_w") * (1.0 + scale_ref[0])
    h = (x * lax.rsqrt(ms + EPS) * gain + shift_ref[0]).astype(BF16)

    def conv_silu_store(y, c0, slot):
        for jj in range(y.shape[1] // LANES):
            j = c0 // LANES + jj
            jl = slice(j * LANES, (j + 1) * LANES)
            yj = y[:, jj * LANES:(jj + 1) * LANES]
            xraw_s[slot, jj, 0:halo, :] = carry_s[j]
            xraw_s[slot, jj, halo:halo + ts, :] = yj
            carry_s[j] = yj[ts - halo:ts, :]
            acc = _param(p_ref, layout, "conv_b", jl)
            for w in range(CONV_WIDTH):
                off = halo - (CONV_WIDTH - 1) + w
                acc = acc + _param(p_ref, layout, f"conv_w{w}", jl) * xraw_s[slot, jj, off:off + ts, :]
            xbc_o[:, jl] = _silu(acc).astype(BF16)

    step = xraw_s.shape[1] * LANES
    light, heavy, col = [], [], 0
    for o_ref, act in ((qkv_o, None), (za_o, _silu), (zm_o, _silu), (xbc_o, "conv"), (g_o, _sigmoid)):
        for c0 in range(0, o_ref.shape[1], step):
            (heavy if act == "conv" else light).append((o_ref, act, col + c0, c0))
        col += o_ref.shape[1]
    gates = [p for p in light if p[1] is _sigmoid]
    others = [p for p in light if p[1] is not _sigmoid]
    order = []
    for piece in heavy:
        order += [piece, others.pop(0), gates.pop(0) if gates else others.pop(0)]
    order += others + gates
    slot = 0
    for k, (o_ref, act, wc, c0) in enumerate(order):
        w_piece = w_ref[:, wc:wc + step]
        if k == 0:
            y = jnp.concatenate([_dot(h[:ts // 2], w_piece), _dot(h[ts // 2:], w_piece)], axis=0)
        else:
            y = _dot(h, w_piece)
        if act == "conv":
            conv_silu_store(y, c0, slot)
            slot = 1 - slot
        else:
            o_ref[:, c0:c0 + step] = (y if act is None else act(y)).astype(o_ref.dtype)
    dt_o[...] = _softplus(_dot_nt(h, wdt_ref[...]) + _param(p_ref, layout, "dt_bias"))


def _inproj_call(x2, shift, scale, params, layout, w_all, wdt, widths, batch, seq):
    t, d = x2.shape
    ts = min(ROWS_INPROJ, seq)
    ns = seq // ts
    row = lambda b, i: (b * ns + i, 0)
    per_b = pl.BlockSpec((1, 1, d), lambda b, i: (b, 0, 0))
    widths = list(widths) + [wdt.shape[0]]
    dtypes = [BF16] * (len(widths) - 1) + [F32]
    piece_blocks = 4
    return pl.pallas_call(
        functools.partial(_inproj_kernel, layout=layout),
        grid=(batch, ns),
        in_specs=[pl.BlockSpec(memory_space=pl.ANY), per_b, per_b, _resident(params.shape),
                  _resident(w_all.shape), _resident(wdt.shape)],
        out_specs=[pl.BlockSpec((ts, n), row) for n in widths],
        out_shape=[jax.ShapeDtypeStruct((t, n), dt) for n, dt in zip(widths, dtypes)],
        scratch_shapes=[pltpu.VMEM((2, piece_blocks, ts + 8, LANES), F32),
                        pltpu.VMEM((widths[3] // LANES, 8, LANES), F32),
                        pltpu.VMEM((X_RING, ts, d), F32),
                        pltpu.SemaphoreType.DMA((X_RING,))],
        compiler_params=pltpu.CompilerParams(
            dimension_semantics=("arbitrary", "arbitrary"), vmem_limit_bytes=VMEM_LIMIT),
        name="inproj",
    )(x2, shift, scale, params, w_all, wdt)


def _segment_rms(x, bd, w):
    ms = _dot((x * x).astype(BF16), bd)
    return x * lax.rsqrt(ms + EPS) * w


def _branches_kernel(*refs, layout):
    i, last = pl.program_id(1), pl.num_programs(1) - 1
    step = functools.partial(_branches_step, *refs, layout=layout)
    pl.when(i == 0)(functools.partial(step, compute=True, finishing=False, first=True))
    pl.when(jnp.logical_and(i > 0, i < last))(functools.partial(step, compute=True, finishing=True, first=False))
    pl.when(i == last)(functools.partial(step, compute=False, finishing=True, first=False))


def _branches_step(q_ref, kvc_ref, kvp_ref, za_ref, ga_ref, bias_ref, p_ref, bd_ref, wap_ref,
                   xbc_ref, dt_ref, zm_ref, tri_ref,
                   gb_ref, x_ref, gate_ref, wsp_ref, wout_ref,
                   o_ref,
                   qn_s, klo_s, khi_s, vt_s, s_s, p_s, yt_s, y_s, ya_s, h_s, yn_s, yprev_s, merged_s,
                   *, layout, compute, finishing, first):
    rows = q_ref.shape[0]
    nblk = rows // BLOCK
    nchunk = rows // CHUNK
    kvw = ATTN_KV_HEADS * HEAD_DIM
    sw = zm_ref.shape[1]
    gw = sw // SSM_GROUPS
    pairs_per_group = gw // LANES
    n_xs = sw // LANES
    if first:
        h_s[...] = jnp.zeros(h_s.shape, F32)

    d_out = o_ref.shape[1]
    piece_w = 2 * LANES

    def merge_piece(c0):
        cs = slice(c0, c0 + piece_w)
        yb = _dot(yprev_s[...], wsp_ref[:, cs])
        merged_s[:, cs] = ya_s[:, cs] + gb_ref[:, cs] * yb.astype(BF16)

    def out_piece(c0):
        cs = slice(c0, c0 + piece_w)
        o = _dot(merged_s[...], wout_ref[:, cs])
        o_ref[:, cs] = x_ref[:, cs] + gate_ref[0][:, cs] * o

    finish = []
    if finishing:
        finish = ([functools.partial(merge_piece, c0) for c0 in range(0, d_out, piece_w)]
                  + [functools.partial(out_piece, c0) for c0 in range(0, d_out, piece_w)])
    if not compute:
        for piece in finish:
            piece()
        return

    bd = bd_ref[...]
    qw = _param(p_ref, layout, "q_norm_w") * (HEAD_DIM ** -0.5 * LOG2E)
    kw = _param(p_ref, layout, "k_norm_w")
    low_lanes = lax.broadcasted_iota(jnp.int32, (1, LANES), 1) < HEAD_DIM

    for j in range(q_ref.shape[1] // LANES):
        sl = slice(j * LANES, (j + 1) * LANES)
        qn_s[:, sl] = _segment_rms(q_ref[:, sl].astype(F32), bd, qw).astype(BF16)
    for j in range(kvw // LANES):
        sl = slice(j * LANES, (j + 1) * LANES)
        even = slice(2 * j * LANES, (2 * j + 1) * LANES)
        odd = slice((2 * j + 1) * LANES, (2 * j + 2) * LANES)
        for dst, src in ((slice(0, BLOCK), kvp_ref), (slice(BLOCK, BLOCK + rows), kvc_ref)):
            kn = _segment_rms(src[:, sl].astype(F32), bd, kw)
            lo = jnp.where(low_lanes, kn, 0.0)
            hi = jnp.where(low_lanes, 0.0, kn)
            klo_s[dst, even] = lo.astype(BF16)
            khi_s[dst, odd] = hi.astype(BF16)
            khi_s[dst, even] = pltpu.roll(lo, HEAD_DIM, axis=1).astype(BF16)
            klo_s[dst, odd] = pltpu.roll(hi, HEAD_DIM, axis=1).astype(BF16)
    for dst, src in ((slice(0, BLOCK), kvp_ref), (slice(BLOCK, BLOCK + rows), kvc_ref)):
        v_t = src[:, kvw:2 * kvw].astype(F32).T.astype(BF16)
        for h in range(ATTN_KV_HEADS):
            vt_s[h * V_ROWS:h * V_ROWS + HEAD_DIM, dst] = v_t[h * HEAD_DIM:(h + 1) * HEAD_DIM, :]
    if first:
        for h in range(ATTN_KV_HEADS):
            vt_s[h * V_ROWS + HEAD_DIM:(h + 1) * V_ROWS, :] = jnp.ones((V_PAD, vt_s.shape[1]), BF16)

    units = [(qb, h) for qb in range(nblk) for h in range(ATTN_KV_HEADS)]

    def scores(u):
        qb, h = units[u]
        r0 = qb * BLOCK
        hl = slice(h * LANES, (h + 1) * LANES)
        k2 = jnp.concatenate([klo_s[r0:r0 + 2 * BLOCK, hl], khi_s[r0:r0 + 2 * BLOCK, hl]], axis=0)
        q2 = jnp.concatenate([qn_s[r0:r0 + BLOCK, (2 * h + c) * LANES:(2 * h + c + 1) * LANES]
                              for c in range(2)], axis=0)
        entry = ATTN_KV_HEADS + h if first and qb == 0 else h
        s_s[u % 2] = _dot_nt(k2, q2) + bias_ref[entry]

    def softmax_pv(u):
        qb, h = units[u]
        r0 = qb * BLOCK
        slot = u % 2
        sink_terms = []
        for s in range(2):
            for c in range(2):
                blk = s_s[slot, s * 2 * BLOCK:(s + 1) * 2 * BLOCK, c * BLOCK:(c + 1) * BLOCK]
                sc = ((h * 2 + s) * 2 + c) * BLOCK
                sink = _param(p_ref, layout, "sinks", slice(sc, sc + BLOCK)) * LOG2E
                m = jnp.maximum(jnp.max(blk, axis=0, keepdims=True), sink)
                e = jnp.exp2(blk - m)
                p_s[slot, :, (2 * s + c) * BLOCK:(2 * s + c + 1) * BLOCK] = e.astype(BF16)
                sink_terms.append(jnp.exp2(sink - m))
        o_t = _dot(vt_s[h * V_ROWS:(h + 1) * V_ROWS, r0:r0 + 2 * BLOCK], p_s[slot])
        for s in range(2):
            for c in range(2):
                head = GROUP * h + 2 * c + s
                k = 2 * s + c
                cols = slice(k * BLOCK, (k + 1) * BLOCK)
                den = o_t[HEAD_DIM:HEAD_DIM + 1, cols] + sink_terms[k]
                yt_s[head * HEAD_DIM:(head + 1) * HEAD_DIM, r0:r0 + BLOCK] = o_t[0:HEAD_DIM, cols] * (1.0 / den)

    scores(0)
    attention = []
    for u in range(len(units)):
        def unit(u=u):
            if u + 1 < len(units):
                scores(u + 1)
            softmax_pv(u)
        attention.append(unit)

    def gate_y():
        y_s[...] = (yt_s[...].T * za_ref[...].astype(F32)).astype(BF16)

    def proj_piece(c0):
        cs = slice(c0, c0 + piece_w)
        ya = _dot(y_s[...], wap_ref[:, cs])
        ya_s[:, cs] = (ya * ga_ref[:, cs].astype(F32)).astype(BF16)

    first_proj = len(attention) + 1
    attention.append(gate_y)
    attention += [functools.partial(proj_piece, c0) for c0 in range(0, d_out, piece_w)]
    n_groups = nchunk * SSM_GROUPS
    base, extra = divmod(len(attention), n_groups)
    attn_counts = [base + (k < extra) for k in range(n_groups)]
    first_proj_group = next(k for k in range(n_groups) if sum(attn_counts[:k + 1]) > first_proj)
    assert len(finish) <= n_groups and first_proj_group >= len(finish) // 2 - 1

    def lane_block(j, r0):
        return xbc_ref[r0:r0 + CHUNK, j * LANES:(j + 1) * LANES]

    a_row = -jnp.exp(_param(p_ref, layout, "a_log"))
    tri = tri_ref[...]
    li = lax.broadcasted_iota(jnp.int32, (CHUNK, CHUNK), 0)
    si = lax.broadcasted_iota(jnp.int32, (CHUNK, CHUNK), 1)
    causal = li >= si
    low_half = si < SSM_HEAD_DIM
    lo_mask = (lax.broadcasted_iota(jnp.int32, (1, LANES), 1) < SSM_HEAD_DIM).astype(BF16)
    hi_mask = 1 - lo_mask

    for c in range(nchunk):
        r0 = c * CHUNK
        dt = dt_ref[r0:r0 + CHUNK, :]
        dta = dt * a_row
        dta_hi = dta.astype(BF16)
        dta_lo = (dta - dta_hi.astype(F32)).astype(BF16)
        a2 = (_dot(tri, dta_hi) + _dot(tri, dta_lo)) * LOG2E
        a2_t = a2.T
        dt_t = dt.T
        row2_t = a2_t - jnp.log2(dt_t)
        a_end2_t = a2_t[:, CHUNK - 1:CHUNK]
        w_t = dt_t * jnp.exp2(a_end2_t - a2_t)
        e_end_t = jnp.exp2(a_end2_t)

        for g in range(SSM_GROUPS):
            if finish:
                finish.pop(0)()
            n_items = attn_counts[c * SSM_GROUPS + g]
            bm_gb = lane_block(n_xs + g, r0)
            cm_gb = lane_block(n_xs + SSM_GROUPS + g, r0)
            cb = _dot_nt(cm_gb, bm_gb)
            bm_gt = bm_gb.astype(F32).T
            gsl = slice(g * gw, (g + 1) * gw)
            hprev = h_s[g]
            y_off = _dot(cm_gb, hprev.astype(BF16))
            y_parts = []
            for p in range(pairs_per_group):
                if (n_items == 1 and p == pairs_per_group // 2) or (n_items >= 2 and p in (1, pairs_per_group - 1)):
                    for _ in range(1 if n_items <= 2 else (n_items + 1) // 2):
                        if attention:
                            attention.pop(0)()
                pair = g * pairs_per_group + p
                lsl = slice(pair * LANES, (pair + 1) * LANES)
                xs_b = lane_block(pair, r0)
                xs_pair = xs_b.astype(F32)
                rhs = jnp.concatenate([xs_b * lo_mask, xs_b * hi_mask], axis=0)
                m_parts, b_parts, cols, e_ends = [], [], [], []
                for side in range(2):
                    hh = 2 * pair + side
                    col2 = jnp.broadcast_to(a2[:, hh:hh + 1], (CHUNK, CHUNK))
                    dec_dt = jnp.exp2(jnp.where(causal, col2 - row2_t[hh:hh + 1, :], NEG))
                    m_parts.append(cb * dec_dt)
                    b_parts.append(bm_gt * w_t[hh:hh + 1, :])
                    cols.append(col2)
                    e_ends.append(e_end_t[hh:hh + 1, :])
                lhs = jnp.concatenate(
                    [jnp.concatenate(m_parts, axis=1), jnp.concatenate(b_parts, axis=1)],
                    axis=0).astype(BF16)
                res = _dot(lhs, rhs)
                e_col = jnp.exp2(jnp.where(low_half, cols[0], cols[1]))
                y_pair = res[0:CHUNK] + e_col * y_off[:, p * LANES:(p + 1) * LANES]
                y_pair = y_pair + _param(p_ref, layout, "d_skip", lsl) * xs_pair
                y_parts.append(y_pair * zm_ref[r0:r0 + CHUNK, lsl].astype(F32))
                e_end = jnp.where(low_half, e_ends[0], e_ends[1])
                h_s[g, :, p * LANES:(p + 1) * LANES] = (
                    hprev[:, p * LANES:(p + 1) * LANES] * e_end + res[CHUNK:])
            yg = jnp.concatenate(y_parts, axis=1)
            ms = jnp.mean(yg * yg, axis=-1, keepdims=True)
            nw = _param(p_ref, layout, "ssm_norm_w", gsl)
            yn_s[r0:r0 + CHUNK, gsl] = (yg * lax.rsqrt(ms + EPS) * nw).astype(BF16)

    assert not finish and not attention
    yprev_s[...] = yn_s[...]


def _branches_call(qkv, za, g, bias, params, layout, bd, wap,
                   xbc, dt, zm, tri, x2, gate, wsp, wout, batch, seq):
    t, d = x2.shape
    aw = ATTN_HEADS * HEAD_DIM
    kvw2 = 2 * ATTN_KV_HEADS * HEAD_DIM
    sw = zm.shape[1]
    rows = min(ROWS_BRANCHES, seq)
    nt = seq // rows
    bpt = rows // BLOCK
    tile = lambda b, i: b * nt + jnp.minimum(i, nt - 1)
    scan = lambda b, i: (tile(b, i), 0)
    done = lambda b, i: (b * nt + jnp.maximum(i - 1, 0), 0)
    return pl.pallas_call(
        functools.partial(_branches_kernel, layout=layout),
        grid=(batch, nt + 1),
        in_specs=[pl.BlockSpec((rows, aw), scan),
                  pl.BlockSpec((rows, kvw2), lambda b, i: (tile(b, i), aw // kvw2)),
                  pl.BlockSpec((BLOCK, kvw2),
                               lambda b, i: (jnp.maximum(tile(b, i) * bpt - 1, 0), aw // kvw2)),
                  pl.BlockSpec((rows, aw), scan),
                  pl.BlockSpec((rows, aw), scan),
                  _resident(bias.shape), _resident(params.shape),
                  _resident(bd.shape), _resident(wap.shape),
                  pl.BlockSpec((rows, xbc.shape[1]), scan),
                  pl.BlockSpec((rows, dt.shape[1]), scan),
                  pl.BlockSpec((rows, sw), scan),
                  _resident(tri.shape),
                  pl.BlockSpec((rows, d), lambda b, i: (b * nt + jnp.maximum(i - 1, 0), 1)),
                  pl.BlockSpec((rows, d), done),
                  pl.BlockSpec((1, 1, d), lambda b, i: (b, 0, 0)),
                  _resident(wsp.shape), _resident(wout.shape)],
        out_specs=pl.BlockSpec((rows, d), done),
        out_shape=jax.ShapeDtypeStruct((t, d), F32),
        scratch_shapes=[pltpu.VMEM((rows, aw), BF16),
                        pltpu.VMEM((rows + BLOCK, ATTN_KV_HEADS * LANES), BF16),
                        pltpu.VMEM((rows + BLOCK, ATTN_KV_HEADS * LANES), BF16),
                        pltpu.VMEM((ATTN_KV_HEADS * V_ROWS, rows + BLOCK), BF16),
                        pltpu.VMEM((2, 4 * BLOCK, 2 * BLOCK), F32),
                        pltpu.VMEM((2, 2 * BLOCK, 4 * BLOCK), BF16),
                        pltpu.VMEM((aw, rows), F32),
                        pltpu.VMEM((rows, aw), BF16),
                        pltpu.VMEM((rows, d), BF16),
                        pltpu.VMEM((SSM_GROUPS, SSM_STATE, sw // SSM_GROUPS), F32),
                        pltpu.VMEM((rows, sw), BF16),
                        pltpu.VMEM((rows, sw), BF16),
                        pltpu.VMEM((rows, d), BF16)],
        compiler_params=pltpu.CompilerParams(
            dimension_semantics=("arbitrary", "arbitrary"), vmem_limit_bytes=VMEM_LIMIT),
        name="branches",
    )(qkv, qkv, qkv, za, g, bias, params, bd, wap,
      xbc, dt, zm, tri, g, x2, gate, wsp, wout)


def _layer(x, c, w_ada, b_ada, norm_w, w_in, q_norm_w, k_norm_w, rel_bias, sinks,
           conv_w, conv_b, dt_bias, a_log, d_skip, ssm_norm_w, w_attn_proj, w_ssm_proj, w_out):
    batch, seq, d = x.shape
    aw = ATTN_HEADS * HEAD_DIM
    kvw = ATTN_KV_HEADS * HEAD_DIM
    sw = w_ssm_proj.shape[0]
    ssm_heads = sw // SSM_HEAD_DIM
    xbc_w = sw + 2 * SSM_GROUPS * SSM_STATE
    assert seq % BLOCK == 0 and seq % CHUNK == 0 and ssm_heads <= LANES

    o = np.cumsum([0, aw, kvw, kvw, aw, sw, xbc_w, ssm_heads, d, d])
    w_t = w_in.T
    c8 = jnp.zeros((8, d), F32).at[:batch].set(c)
    w_all, mod, bias = _prep_call(c8, w_ada, b_ada.reshape(1, -1), rel_bias.astype(F32), w_t, int(o[6]), ssm_heads)
    shift, scale, gate = (mod[:batch, k * d:(k + 1) * d].reshape(batch, 1, d) for k in range(3))
    wdt = jnp.pad(w_t[o[6]:o[7]], ((0, LANES - ssm_heads), (0, 0))).astype(BF16)
    widths = [aw + 2 * kvw, aw, sw, xbc_w, 2 * d]

    pad_heads = lambda v: jnp.pad(v.astype(F32), (0, LANES - ssm_heads))
    pair = lambda v: jnp.tile(v.astype(F32), LANES // HEAD_DIM)
    sink_lanes = jnp.repeat(sinks.astype(F32).reshape(ATTN_KV_HEADS, 2, 2).transpose(0, 2, 1).reshape(-1), BLOCK)
    params, layout = _param_table(
        [[(f"conv_w{w}", conv_w[w])] for w in range(CONV_WIDTH)]
        + [[("conv_b", conv_b)],
           [("d_skip", jnp.repeat(d_skip.astype(F32), SSM_HEAD_DIM)), ("norm_w", norm_w)],
           [("ssm_norm_w", ssm_norm_w), ("q_norm_w", pair(q_norm_w)), ("k_norm_w", pair(k_norm_w)),
            ("dt_bias", pad_heads(dt_bias)), ("a_log", pad_heads(a_log))],
           [("sinks", sink_lanes)]])

    x2 = x.reshape(batch * seq, d)
    qkv, za, zm, xbc, g, dt = _inproj_call(x2, shift, scale, params, layout, w_all, wdt, widths, batch, seq)

    seg = np.arange(LANES) // HEAD_DIM
    bd = jnp.asarray((seg[:, None] == seg[None, :]).astype(np.float32) / HEAD_DIM, dtype=BF16)
    tri = jnp.asarray(np.tril(np.ones((CHUNK, CHUNK), np.float32)), dtype=BF16)
    out = _branches_call(qkv, za, g, bias, params, layout, bd, w_attn_proj.astype(BF16),
                         xbc, dt, zm, tri,
                         x2, gate, w_ssm_proj.astype(BF16), w_out.astype(BF16), batch, seq)
    return out.reshape(batch, seq, d)


def kernel(x, c, w_ada, b_ada, norm_w, w_in, q_norm_w, k_norm_w, rel_bias, sinks, conv_w, conv_b,
           dt_bias, a_log, d_skip, ssm_norm_w, w_attn_proj, w_ssm_proj, w_out):
    depth = w_in.shape[0]
    for i in range(depth):
        x = _layer(x, c, w_ada[i], b_ada[i], norm_w[i], w_in[i], q_norm_w[i], k_norm_w[i],
                   rel_bias, sinks[i], conv_w[i], conv_b[i], dt_bias[i], a_log[i], d_skip[i],
                   ssm_norm_w[i], w_attn_proj[i], w_ssm_proj[i], w_out[i])
    return x
```
